```python
import math
import jax, jax.numpy as jnp
from jax import lax
import numpy as np

D_MODEL = 1024
BATCH = 2
SEQ = 8192
DEPTH = 2
DEC_BATCH = 128
DEC_SEQ = 1
PAST_LEN = 8192
PAGE_SIZE = 128

N_A = DEPTH // 2
N_B = DEPTH - N_A
M_HEADS = 4
M_DK = 128
M_DV = 256
M_CHUNK = 64
M_IN = 2 * M_HEADS * M_DK + 2 * M_HEADS * M_DV + 2 * M_HEADS
D_FF = 2816
CONV_W = 3
MLA_HEADS = 16
Q_LORA = 384
KV_LORA = 256
QK_NOPE = 64
QK_ROPE = 32
V_HEAD = 64
ROPE_THETA = 10000.0
Q_BLOCK = 128
ATT_SCALE = (QK_NOPE + QK_ROPE) ** -0.5
EPS = 1e-6

kernel_name = 'yoco_mlstm_mla_convffn_adaln_step'


def rms_norm(x, g):
    xf = x.astype(jnp.float32)
    y = xf * lax.rsqrt(jnp.mean(xf * xf, axis=-1, keepdims=True) + EPS)
    return (y * g.astype(jnp.float32)).astype(x.dtype)


def ada_mod(c, w, b, n):
    mod = jax.nn.silu(c) @ w + b
    return jnp.split(mod[:, None, :], n, axis=-1)


def apply_rope(x, pos):
    half = x.shape[-1] // 2
    freq = ROPE_THETA ** (-jnp.arange(half, dtype=jnp.float32) / half)
    ang = pos.astype(jnp.float32)[:, None] * freq[None, :]
    cos = jnp.cos(ang)[None, :, None, :]
    sin = jnp.sin(ang)[None, :, None, :]
    xf = x.astype(jnp.float32)
    x1, x2 = xf[..., :half], xf[..., half:]
    return jnp.concatenate([x1 * cos - x2 * sin, x1 * sin + x2 * cos], axis=-1).astype(x.dtype)


def mlstm_chunkwise(q, k, v, log_i, log_f):
    bsz, nh, s, _ = q.shape
    nc = s // M_CHUNK
    qc = q.reshape(bsz, nh, nc, M_CHUNK, M_DK)
    kc = k.reshape(bsz, nh, nc, M_CHUNK, M_DK)
    vc = v.reshape(bsz, nh, nc, M_CHUNK, M_DV)
    li = log_i.reshape(bsz, nh, nc, M_CHUNK)
    b = jnp.cumsum(log_f.reshape(bsz, nh, nc, M_CHUNK), axis=-1)
    g = b[..., -1]
    a = g[..., None] - b + li
    m_loc = jnp.max(a, axis=-1)
    wa = jnp.exp(a - m_loc[..., None])
    c_loc = jnp.einsum('bhcld,bhcle->bhcde', kc * wa[..., None], vc)
    n_loc = jnp.einsum('bhcl,bhcld->bhcd', wa, kc)

    def chunk_step(carry, inp):
        c_st, n_st, m_st = carry
        g_c, m_l, c_l, n_l = inp
        m_new = jnp.maximum(g_c + m_st, m_l)
        fw = jnp.exp(g_c + m_st - m_new)
        lw = jnp.exp(m_l - m_new)
        c_new = fw[..., None, None] * c_st + lw[..., None, None] * c_l
        n_new = fw[..., None] * n_st + lw[..., None] * n_l
        return (c_new, n_new, m_new), (c_st, n_st, m_st)

    init = (jnp.zeros((bsz, nh, M_DK, M_DV), jnp.float32),
            jnp.zeros((bsz, nh, M_DK), jnp.float32),
            jnp.zeros((bsz, nh), jnp.float32))
    xs = (jnp.moveaxis(g, 2, 0), jnp.moveaxis(m_loc, 2, 0),
          jnp.moveaxis(c_loc, 2, 0), jnp.moveaxis(n_loc, 2, 0))
    final, (c_prev, n_prev, m_prev) = lax.scan(chunk_step, init, xs)
    c_prev = jnp.moveaxis(c_prev, 0, 2)
    n_prev = jnp.moveaxis(n_prev, 0, 2)
    m_prev = jnp.moveaxis(m_prev, 0, 2)
    w0 = b + m_prev[..., None]
    causal = jnp.tril(jnp.ones((M_CHUNK, M_CHUNK), bool))
    dmat = jnp.where(causal, b[..., :, None] - b[..., None, :] + li[..., None, :], -jnp.inf)
    m_s = jnp.maximum(w0, jnp.max(dmat, axis=-1))
    w_inter = jnp.exp(w0 - m_s)
    s_qk = jnp.einsum('bhcid,bhcjd->bhcij', qc, kc) * jnp.exp(dmat - m_s[..., None])
    num = (w_inter[..., None] * jnp.einsum('bhcid,bhcde->bhcie', qc, c_prev)
           + jnp.einsum('bhcij,bhcje->bhcie', s_qk, vc))
    den = w_inter * jnp.einsum('bhcid,bhcd->bhci', qc, n_prev) + jnp.sum(s_qk, axis=-1)
    h = num / jnp.maximum(jnp.abs(den), jnp.exp(-m_s))[..., None]
    return h.reshape(bsz, nh, s, M_DV), final


def mlstm_step(carry, inp):
    c_st, n_st, m_st = carry
    q, k, v, li, lf = inp
    m_new = jnp.maximum(lf + m_st, li)
    fw = jnp.exp(lf + m_st - m_new)
    iw = jnp.exp(li - m_new)
    c_new = fw[..., None, None] * c_st + iw[..., None, None] * (k[..., :, None] * v[..., None, :])
    n_new = fw[..., None] * n_st + iw[..., None] * k
    num = jnp.einsum('bhd,bhde->bhe', q, c_new)
    den = jnp.einsum('bhd,bhd->bh', q, n_new)
    h = num / jnp.maximum(jnp.abs(den), jnp.exp(-m_new))[..., None]
    return (c_new, n_new, m_new), h


def mlstm_recurrent(q, k, v, log_i, log_f, state):
    init = tuple(s_.astype(jnp.float32) for s_ in state)
    xs = (jnp.moveaxis(q, 2, 0), jnp.moveaxis(k, 2, 0), jnp.moveaxis(v, 2, 0),
          jnp.moveaxis(log_i, 2, 0), jnp.moveaxis(log_f, 2, 0))
    final, hs = lax.scan(mlstm_step, init, xs)
    return jnp.moveaxis(hs, 0, 2), final


def mlstm_mixer(h, w_in, b_gates, g_head, w_out, state):
    bsz, s, _ = h.shape
    qd, vd = M_HEADS * M_DK, M_HEADS * M_DV
    proj = h @ w_in
    q, k, v, o, gates = jnp.split(proj, [qd, 2 * qd, 2 * qd + vd, 2 * qd + 2 * vd], axis=-1)

    def heads(t, d):
        return t.reshape(bsz, s, M_HEADS, d).transpose(0, 2, 1, 3).astype(jnp.float32)

    q = heads(q, M_DK) * (M_DK ** -0.5)
    k = heads(k, M_DK)
    v = heads(v, M_DV)
    gates = (gates + b_gates).astype(jnp.float32).transpose(0, 2, 1)
    log_i = gates[:, :M_HEADS]
    log_f = jax.nn.log_sigmoid(gates[:, M_HEADS:])
    if state is None:
        hh, new_state = mlstm_chunkwise(q, k, v, log_i, log_f)
    else:
        hh, new_state = mlstm_recurrent(q, k, v, log_i, log_f, state)
    hh = rms_norm(hh.transpose(0, 2, 1, 3), g_head)
    hh = hh.reshape(bsz, s, vd).astype(h.dtype) * jax.nn.sigmoid(o)
    return hh @ w_out, new_state


def conv_ffn(h, buf, w_up, w_conv, b_conv, w_down):
    s = h.shape[1]
    a, gt = jnp.split(h @ w_up, 2, axis=-1)
    ext = jnp.concatenate([buf.astype(a.dtype), a], axis=1)
    conv = b_conv
    for j in range(CONV_W):
        conv = conv + ext[:, j:j + s] * w_conv[j]
    out = (jax.nn.silu(conv) * gt) @ w_down
    return out, ext[:, -(CONV_W - 1):]


def shared_latent(x, c, pos, g_kv_in, w_ada_kv, b_ada_kv, w_dkv, g_kv):
    shift, scale = ada_mod(c, w_ada_kv, b_ada_kv, 2)
    hn = rms_norm(x, g_kv_in) * (1 + scale) + shift
    lat = hn @ w_dkv
    ckv = rms_norm(lat[..., :KV_LORA], g_kv)
    kpe = apply_rope(lat[:, :, None, KV_LORA:], pos)[:, :, 0, :]
    return ckv, kpe


def mla_queries(h, pos, w_dq, g_q, w_uq):
    bsz, s, _ = h.shape
    q = (rms_norm(h @ w_dq, g_q) @ w_uq).reshape(bsz, s, MLA_HEADS, QK_NOPE + QK_ROPE)
    return q[..., :QK_NOPE], apply_rope(q[..., QK_NOPE:], pos)


def mla_attend_prompt(q_nope, q_pe, ckv, kpe, w_uk, w_uv):
    bsz, s, nh, _ = q_nope.shape
    nb = s // Q_BLOCK
    k_nope = jnp.einsum('bkc,chd->bkhd', ckv, w_uk)
    v = jnp.einsum('bkc,chd->bkhd', ckv, w_uv)
    k_pos = jnp.arange(s)

    def to_blocks(t):
        return jnp.moveaxis(t.reshape(bsz, nb, Q_BLOCK, nh, t.shape[-1]), 1, 0)

    def block(args):
        qn, qp, blk = args
        sc = jnp.einsum('bqhd,bkhd->bhqk', qn, k_nope) + jnp.einsum('bqhr,bkr->bhqk', qp, kpe)
        q_pos = blk * Q_BLOCK + jnp.arange(Q_BLOCK)
        sc = jnp.where(k_pos[None, :] <= q_pos[:, None], sc.astype(jnp.float32) * ATT_SCALE, -jnp.inf)
        p = jax.nn.softmax(sc, axis=-1).astype(v.dtype)
        return jnp.einsum('bhqk,bkhd->bqhd', p, v)

    o = lax.map(block, (to_blocks(q_nope), to_blocks(q_pe), jnp.arange(nb)))
    return jnp.moveaxis(o, 0, 1).reshape(bsz, s, nh * V_HEAD)


def mla_attend_sample(q_nope, q_pe, ckv_past, kpe_past, ckv_new, kpe_new, w_uk, w_uv):
    bsz, t, nh, _ = q_nope.shape
    n_past = ckv_past.shape[1]
    q_lat = jnp.einsum('bqhd,chd->bqhc', q_nope, w_uk)
    s_past = jnp.einsum('bqhc,bkc->bhqk', q_lat, ckv_past) + jnp.einsum('bqhr,bkr->bhqk', q_pe, kpe_past)
    s_new = jnp.einsum('bqhc,bkc->bhqk', q_lat, ckv_new) + jnp.einsum('bqhr,bkr->bhqk', q_pe, kpe_new)
    causal = jnp.tril(jnp.ones((t, t), bool))
    s_new = jnp.where(causal, s_new.astype(jnp.float32) * ATT_SCALE, -jnp.inf)
    sc = jnp.concatenate([s_past.astype(jnp.float32) * ATT_SCALE, s_new], axis=-1)
    p = jax.nn.softmax(sc, axis=-1).astype(ckv_new.dtype)
    o_lat = (jnp.einsum('bhqk,bkc->bqhc', p[..., :n_past], ckv_past)
             + jnp.einsum('bhqk,bkc->bqhc', p[..., n_past:], ckv_new))
    o = jnp.einsum('bqhc,chd->bqhd', o_lat, w_uv)
    return o.reshape(bsz, t, nh * V_HEAD)


def trunk(x, c, pos, conv_bufs, m_states, kv_past,
          g_norm1, g_norm2, w_ada, b_ada, w_up, w_conv, b_conv, w_down,
          w_m_in, b_m_gates, g_m_head, w_m_out,
          g_kv_in, w_ada_kv, b_ada_kv, w_dkv, g_kv, w_uk, w_uv,
          w_dq, g_q, w_uq, w_o, g_final):
    new_c, new_n, new_m, new_conv = [], [], [], []
    ckv = kpe = None
    for layer in range(DEPTH):
        sh1, sc1, gt1, sh2, sc2, gt2 = ada_mod(c, w_ada[layer], b_ada[layer], 6)
        if layer == N_A:
            ckv, kpe = shared_latent(x, c, pos, g_kv_in, w_ada_kv, b_ada_kv, w_dkv, g_kv)
        hn = rms_norm(x, g_norm1[layer]) * (1 + sc1) + sh1
        if layer < N_A:
            st = None if m_states is None else (m_states[0][layer], m_states[1][layer], m_states[2][layer])
            mix, (c_st, n_st, m_st) = mlstm_mixer(hn, w_m_in[layer], b_m_gates[layer],
                                                  g_m_head[layer], w_m_out[layer], st)
            new_c.append(c_st)
            new_n.append(n_st)
            new_m.append(m_st)
        else:
            j = layer - N_A
            q_nope, q_pe = mla_queries(hn, pos, w_dq[j], g_q[j], w_uq[j])
            if kv_past is None:
                att = mla_attend_prompt(q_nope, q_pe, ckv, kpe, w_uk, w_uv)
            else:
                att = mla_attend_sample(q_nope, q_pe, kv_past[0], kv_past[1], ckv, kpe, w_uk, w_uv)
            mix = att @ w_o[j]
        x = x + gt1 * mix
        hn = rms_norm(x, g_norm2[layer]) * (1 + sc2) + sh2
        f, buf = conv_ffn(hn, conv_bufs[layer], w_up[layer], w_conv[layer], b_conv[layer], w_down[layer])
        new_conv.append(buf)
        x = x + gt2 * f
    y = rms_norm(x, g_final)
    return y, jnp.stack(new_c), jnp.stack(new_n), jnp.stack(new_m), jnp.stack(new_conv), ckv, kpe


def setup_inputs(seed: int = 0) -> dict:
    key = jax.random.key(seed)
    ks = iter(jax.random.split(key, 48))

    def nrm(shape, scale=1.0):
        return scale * jax.random.normal(next(ks), shape, jnp.float32)

    D = D_MODEL
    n_pages = PAST_LEN // PAGE_SIZE
    n_used = DEC_BATCH * n_pages
    n_phys = n_used + n_used // 4
    x_prompt = nrm((BATCH, SEQ, D))
    x_sample = nrm((DEC_BATCH, DEC_SEQ, D))
    state_mlstm_C = nrm((N_A, DEC_BATCH, M_HEADS, M_DK, M_DV))
    state_mlstm_n = nrm((N_A, DEC_BATCH, M_HEADS, M_DK))
    state_mlstm_m = nrm((N_A, DEC_BATCH, M_HEADS))
    state_conv = nrm((DEPTH, DEC_BATCH, CONV_W - 1, D_FF))
    cache_ckv = nrm((n_phys, PAGE_SIZE, KV_LORA))
    cache_kpe = nrm((n_phys, PAGE_SIZE, QK_ROPE))
    page_table = jax.random.permutation(next(ks), n_phys)[:n_used].reshape(DEC_BATCH, n_pages).astype(jnp.int32)
    c_prompt = nrm((BATCH, D))
    c_sample = nrm((DEC_BATCH, D))
    g_norm1 = 1.0 + nrm((DEPTH, D), 0.05)
    g_norm2 = 1.0 + nrm((DEPTH, D), 0.05)
    w_ada = nrm((DEPTH, D, 6 * D), 0.5 * D ** -0.5)
    b_ada = nrm((DEPTH, 6 * D), 0.01)
    w_up = nrm((DEPTH, D, 2 * D_FF), D ** -0.5)
    w_conv = nrm((DEPTH, CONV_W, D_FF), CONV_W ** -0.5)
    b_conv = nrm((DEPTH, D_FF), 0.01)
    w_down = nrm((DEPTH, D_FF, D), D_FF ** -0.5)
    w_m_in = nrm((N_A, D, M_IN), D ** -0.5)
    b_m_gates = jnp.concatenate([nrm((N_A, M_HEADS), 0.1), 3.0 + nrm((N_A, M_HEADS), 0.1)], axis=-1)
    g_m_head = 1.0 + nrm((N_A, M_HEADS, M_DV), 0.05)
    w_m_out = nrm((N_A, M_HEADS * M_DV, D), (M_HEADS * M_DV) ** -0.5)
    g_kv_in = 1.0 + nrm((D,), 0.05)
    w_ada_kv = nrm((D, 2 * D), 0.5 * D ** -0.5)
    b_ada_kv = nrm((2 * D,), 0.01)
    w_dkv = nrm((D, KV_LORA + QK_ROPE), D ** -0.5)
    g_kv = 1.0 + nrm((KV_LORA,), 0.05)
    w_uk = nrm((KV_LORA, MLA_HEADS, QK_NOPE), KV_LORA ** -0.5)
    w_uv = nrm((KV_LORA, MLA_HEADS, V_HEAD), KV_LORA ** -0.5)
    w_dq = nrm((N_B, D, Q_LORA), D ** -0.5)
    g_q = 1.0 + nrm((N_B, Q_LORA), 0.05)
    w_uq = nrm((N_B, Q_LORA, MLA_HEADS * (QK_NOPE + QK_ROPE)), Q_LORA ** -0.5)
    w_o = nrm((N_B, MLA_HEADS * V_HEAD, D), (MLA_HEADS * V_HEAD) ** -0.5)
    g_final = 1.0 + nrm((D,), 0.05)
    return {'x_prompt': x_prompt, 'x_sample': x_sample,
            'state_mlstm_C': state_mlstm_C, 'state_mlstm_n': state_mlstm_n, 'state_mlstm_m': state_mlstm_m,
            'state_conv': state_conv, 'cache_ckv': cache_ckv, 'cache_kpe': cache_kpe,
            'page_table': page_table, 'c_prompt': c_prompt, 'c_sample': c_sample,
            'g_norm1': g_norm1, 'g_norm2': g_norm2, 'w_ada': w_ada, 'b_ada': b_ada,
            'w_up': w_up, 'w_conv': w_conv, 'b_conv': b_conv, 'w_down': w_down,
            'w_m_in': w_m_in, 'b_m_gates': b_m_gates, 'g_m_head': g_m_head, 'w_m_out': w_m_out,
            'g_kv_in': g_kv_in, 'w_ada_kv': w_ada_kv, 'b_ada_kv': b_ada_kv, 'w_dkv': w_dkv, 'g_kv': g_kv,
            'w_uk': w_uk, 'w_uv': w_uv, 'w_dq': w_dq, 'g_q': g_q, 'w_uq': w_uq, 'w_o': w_o,
            'g_final': g_final}


def reference(x_prompt, x_sample, state_mlstm_C, state_mlstm_n, state_mlstm_m, state_conv,
              cache_ckv, cache_kpe, page_table, c_prompt, c_sample,
              g_norm1, g_norm2, w_ada, b_ada, w_up, w_conv, b_conv, w_down,
              w_m_in, b_m_gates, g_m_head, w_m_out,
              g_kv_in, w_ada_kv, b_ada_kv, w_dkv, g_kv, w_uk, w_uv,
              w_dq, g_q, w_uq, w_o, g_final):
    weights = (g_norm1, g_norm2, w_ada, b_ada, w_up, w_conv, b_conv, w_down,
               w_m_in, b_m_gates, g_m_head, w_m_out,
               g_kv_in, w_ada_kv, b_ada_kv, w_dkv, g_kv, w_uk, w_uv,
               w_dq, g_q, w_uq, w_o, g_final)
    bp, s, _ = x_prompt.shape
    bs, t, _ = x_sample.shape
    past_len = page_table.shape[1] * cache_ckv.shape[1]

    conv0 = jnp.zeros((DEPTH, bp, CONV_W - 1, D_FF), x_prompt.dtype)
    y_p, c_p, n_p, m_p, conv_p, ckv_p, kpe_p = trunk(
        x_prompt, c_prompt, jnp.arange(s), conv0, None, None, *weights)

    ckv_past = cache_ckv[page_table].reshape(bs, past_len, KV_LORA)
    kpe_past = cache_kpe[page_table].reshape(bs, past_len, QK_ROPE)
    y_s, c_s, n_s, m_s, conv_s, ckv_s, kpe_s = trunk(
        x_sample, c_sample, past_len + jnp.arange(t), state_conv,
        (state_mlstm_C, state_mlstm_n, state_mlstm_m), (ckv_past, kpe_past), *weights)

    return (y_p, y_s, c_p, n_p, m_p, conv_p, ckv_p, kpe_p, c_s, n_s, m_s, conv_s, ckv_s, kpe_s)
```

```python
import functools
import math

import jax
import jax.numpy as jnp
from jax import lax
from jax.experimental import pallas as pl
from jax.experimental.pallas import tpu as pltpu

F32 = jnp.float32
BF16 = jnp.bfloat16

NORM_EPS = 1e-6
ROPE_THETA = 10000.0
LANES = 128
VMEM_LIMIT = 56 * 1024 * 1024

ROW_TILE = 512
MLSTM_CHUNK = 256
ATTN_TILE = 512
FFN_CHUNK = 256
PAGES_PER_STEP = 16
STEP_BATCH = 8

NT_DIMS = (((1,), (1,)), ((), ()))
TN_DIMS = (((0,), (0,)), ((), ()))


def _params(*sem):
    return pltpu.CompilerParams(dimension_semantics=sem, vmem_limit_bytes=VMEM_LIMIT)


def _const_spec(shape):
    nd = len(shape)
    return pl.BlockSpec(shape, lambda *_: (0,) * nd, pipeline_mode=pl.Buffered(1))


def _rms(x, g):
    return x * lax.rsqrt(jnp.mean(x * x, axis=-1, keepdims=True) + NORM_EPS) * g


def _norm_mod(x, g, sh, sc):
    return _rms(x, g) * (1.0 + sc) + sh


def _log_sigmoid(x):
    return jnp.minimum(x, 0.0) - jnp.log(1.0 + jnp.exp(-jnp.abs(x)))


def _rope3(t, c, sa, sb):
    return t * c + pltpu.roll(t, LANES - 16, axis=1) * sa + pltpu.roll(t, 16, axis=1) * sb


def _row_spec(tm, width):
    return pl.BlockSpec((None, tm, width), lambda g, i: (g, i, 0))


def _mod_spec(per_row, tm, width):
    if per_row:
        return pl.BlockSpec((None, tm, width), lambda g, i: (g, i, 0))
    return pl.BlockSpec((None, 1, width), lambda g, i: (g, 0, 0))


def _ada_kernel(c_ref, w_ref, b_ref, o_ref):
    c = c_ref[...]
    a = (c * jax.nn.sigmoid(c)).astype(BF16)
    o_ref[...] = jnp.dot(a, w_ref[...].astype(BF16), preferred_element_type=F32) + b_ref[...]


def ada_mod(c, w, b, tn=1024):
    m, d = c.shape
    nl, _, n = w.shape
    tn = min(tn, n)
    return pl.pallas_call(
        _ada_kernel,
        grid=(nl, n // tn),
        in_specs=[pl.BlockSpec((m, d), lambda l, j: (0, 0)),
                  pl.BlockSpec((None, d, tn), lambda l, j: (l, 0, j)),
                  pl.BlockSpec((None, 1, tn), lambda l, j: (l, 0, j))],
        out_specs=pl.BlockSpec((None, m, tn), lambda l, j: (l, 0, j)),
        out_shape=jax.ShapeDtypeStruct((nl, m, n), F32),
        compiler_params=_params("parallel", "parallel"),
        name="ada_mod",
    )(c, w, b)


def _proj_kernel(x_ref, g_ref, sh_ref, sc_ref, *refs, scales):
    n = len(scales)
    hn = _norm_mod(x_ref[...], g_ref[...], sh_ref[...], sc_ref[...]).astype(BF16)
    for w_ref, o_ref, s in zip(refs[:n], refs[n:], scales):
        acc = jnp.dot(hn, w_ref[...], preferred_element_type=F32)
        if s != 1.0:
            acc = acc * s
        o_ref[...] = acc.astype(o_ref.dtype)


def norm_mod_proj(x, g, sh, sc, ws, out_dtypes, scales, name):
    gq, r, d = x.shape
    tm = min(ROW_TILE, r)
    per_row = sh.shape[1] != 1
    in_specs = [_row_spec(tm, d), _const_spec((1, d)),
                _mod_spec(per_row, tm, d), _mod_spec(per_row, tm, d)]
    in_specs += [_const_spec(w.shape) for w in ws]
    return pl.pallas_call(
        functools.partial(_proj_kernel, scales=tuple(scales)),
        grid=(gq, r // tm),
        in_specs=in_specs,
        out_specs=[_row_spec(tm, w.shape[1]) for w in ws],
        out_shape=[jax.ShapeDtypeStruct((gq, r, w.shape[1]), dt) for w, dt in zip(ws, out_dtypes)],
        compiler_params=_params("parallel", "parallel"),
        name=name,
    )(x, g, sh, sc, *ws)


def _resid_kernel(a_ref, w_ref, x_ref, gt_ref, o_ref):
    mix = jnp.dot(a_ref[...].astype(BF16), w_ref[...], preferred_element_type=F32)
    o_ref[...] = x_ref[...] + gt_ref[...] * mix


def resid_proj(a, w, x, gt, name):
    gq, r, d = x.shape
    k = a.shape[-1]
    tm = min(ROW_TILE, r)
    per_row = gt.shape[1] != 1
    return pl.pallas_call(
        _resid_kernel,
        grid=(gq, r // tm),
        in_specs=[_row_spec(tm, k), _const_spec(w.shape), _row_spec(tm, d), _mod_spec(per_row, tm, d)],
        out_specs=_row_spec(tm, d),
        out_shape=jax.ShapeDtypeStruct((gq, r, d), F32),
        compiler_params=_params("parallel", "parallel"),
        name=name,
    )(a, w, x, gt)


def _mlstm_chunk_kernel(q_ref, k_ref, v_ref, o_ref, gr_ref, gc_ref, bgc_ref, bgr_ref, gh_ref,
                        hh_ref, c_ref, n_ref, m_ref, *, heads, dk, dv):
    ci = pl.program_id(1)
    chunk = q_ref.shape[0]

    @pl.when(ci == 0)
    def _():
        c_ref[...] = jnp.zeros_like(c_ref)
        n_ref[...] = jnp.zeros_like(n_ref)
        m_ref[...] = jnp.zeros_like(m_ref)

    gates_r = gr_ref[...] + bgc_ref[...]
    gates_c = gc_ref[...] + bgr_ref[...]
    row = lax.broadcasted_iota(jnp.int32, (chunk, chunk), 0)
    col = lax.broadcasted_iota(jnp.int32, (chunk, chunk), 1)
    causal = col <= row

    for h in range(heads):
        q = q_ref[:, h * dk:(h + 1) * dk]
        k = k_ref[:, h * dk:(h + 1) * dk]
        v = v_ref[:, h * dv:(h + 1) * dv]
        li_r = gates_r[h:h + 1, :]
        lf_r = _log_sigmoid(gates_r[heads + h:heads + h + 1, :])
        li_c = gates_c[:, h:h + 1]
        lf_c = _log_sigmoid(gates_c[:, heads + h:heads + h + 1])

        b_c = jnp.sum(jnp.where(causal, lf_r, 0.0), axis=1, keepdims=True)
        b_r = jnp.sum(jnp.where(row <= col, lf_c, 0.0), axis=0, keepdims=True)
        g = jnp.sum(lf_r, axis=1, keepdims=True)

        m_prev = m_ref[h]
        c_prev = c_ref[h]
        n_prev = n_ref[h]

        a_c = g - b_c + li_c
        m_loc = jnp.max(a_c, axis=0, keepdims=True)
        kw = k * jnp.exp(a_c - m_loc)
        c_loc = lax.dot_general(kw.astype(BF16), v, TN_DIMS, preferred_element_type=F32)
        n_loc = jnp.sum(kw, axis=0, keepdims=True)

        dmat = jnp.where(causal, b_c - b_r + li_r, -jnp.inf)
        w0 = b_c + m_prev
        m_s = jnp.maximum(w0, jnp.max(dmat, axis=1, keepdims=True))
        w_inter = jnp.exp(w0 - m_s)
        s = lax.dot_general(q, k.astype(BF16), NT_DIMS, preferred_element_type=F32) * jnp.exp(dmat - m_s)
        num = (w_inter * jnp.dot(q, c_prev.astype(BF16), preferred_element_type=F32)
               + jnp.dot(s.astype(BF16), v, preferred_element_type=F32))
        den = (w_inter * jnp.sum(q.astype(F32) * n_prev, axis=1, keepdims=True)
               + jnp.sum(s, axis=1, keepdims=True))
        hval = num / jnp.maximum(jnp.abs(den), jnp.exp(-m_s))
        hn = _rms(hval, gh_ref[h:h + 1, :])
        gate = jax.nn.sigmoid(o_ref[:, h * dv:(h + 1) * dv])
        hh_ref[:, h * dv:(h + 1) * dv] = (hn * gate).astype(hh_ref.dtype)

        m_new = jnp.maximum(g + m_prev, m_loc)
        fw = jnp.exp(g + m_prev - m_new)
        lw = jnp.exp(m_loc - m_new)
        c_ref[h] = fw * c_prev + lw * c_loc
        n_ref[h] = fw * n_prev + lw * n_loc
        m_ref[h] = m_new


def mlstm_chunkwise(q, k, v, o, gates, b_gates, g_head, heads, dk, dv):
    bsz, s, _ = q.shape
    chunk = min(MLSTM_CHUNK, s)
    gates_r = jnp.swapaxes(gates, 1, 2)
    g2 = 2 * heads
    return pl.pallas_call(
        functools.partial(_mlstm_chunk_kernel, heads=heads, dk=dk, dv=dv),
        grid=(bsz, s // chunk),
        in_specs=[_row_spec(chunk, heads * dk), _row_spec(chunk, heads * dk),
                  _row_spec(chunk, heads * dv), _row_spec(chunk, heads * dv),
                  pl.BlockSpec((None, g2, chunk), lambda b, c: (b, 0, c)),
                  _row_spec(chunk, g2),
                  _const_spec((g2, 1)), _const_spec((1, g2)), _const_spec((heads, dv))],
        out_specs=[_row_spec(chunk, heads * dv),
                   pl.BlockSpec((None, heads, dk, dv), lambda b, c: (b, 0, 0, 0)),
                   pl.BlockSpec((None, heads, 1, dk), lambda b, c: (b, 0, 0, 0)),
                   pl.BlockSpec((None, heads, 1, 1), lambda b, c: (b, 0, 0, 0))],
        out_shape=[jax.ShapeDtypeStruct((bsz, s, heads * dv), BF16),
                   jax.ShapeDtypeStruct((bsz, heads, dk, dv), F32),
                   jax.ShapeDtypeStruct((bsz, heads, 1, dk), F32),
                   jax.ShapeDtypeStruct((bsz, heads, 1, 1), F32)],
        compiler_params=_params("parallel", "arbitrary"),
        name="mlstm_chunkwise",
    )(q, k, v, o, gates_r, gates, b_gates.reshape(g2, 1), b_gates.reshape(1, g2), g_head)


def _mlstm_step_kernel(q_ref, k_ref, v_ref, o_ref, g_ref, bg_ref, gh_ref, c_ref, n_ref, m_ref,
                       hh_ref, co_ref, no_ref, mo_ref, *, heads, dk, dv):
    nb = q_ref.shape[0]
    gates = g_ref[...] + bg_ref[...]
    li = gates[:, :heads]
    lf = _log_sigmoid(gates[:, heads:])
    m_st = m_ref[...]
    m_new = jnp.maximum(lf + m_st, li)
    fw_all = jnp.exp(lf + m_st - m_new)
    iw_all = jnp.exp(li - m_new)
    floor_all = jnp.exp(-m_new)
    mo_ref[...] = m_new
    eye = lax.broadcasted_iota(jnp.int32, (dk, dk), 0) == lax.broadcasted_iota(jnp.int32, (dk, dk), 1)

    def to_col(r):
        return jnp.sum(jnp.where(eye, r, 0.0), axis=1, keepdims=True)

    for b in range(nb):
        for h in range(heads):
            fw = fw_all[b:b + 1, h:h + 1]
            iw = iw_all[b:b + 1, h:h + 1]
            q_r = q_ref[b:b + 1, h * dk:(h + 1) * dk]
            k_r = k_ref[b:b + 1, h * dk:(h + 1) * dk]
            v_r = v_ref[b:b + 1, h * dv:(h + 1) * dv]
            c_new = fw * c_ref[b, h] + (iw * to_col(k_r)) * v_r
            n_new = fw * n_ref[b, h:h + 1, :] + iw * k_r
            co_ref[b, h] = c_new
            no_ref[b, h:h + 1, :] = n_new
            num = jnp.sum(to_col(q_r) * c_new, axis=0, keepdims=True)
            den = jnp.sum(q_r * n_new, axis=1, keepdims=True)
            hval = num / jnp.maximum(jnp.abs(den), floor_all[b:b + 1, h:h + 1])
            hn = _rms(hval, gh_ref[h:h + 1, :])
            gate = jax.nn.sigmoid(o_ref[b:b + 1, h * dv:(h + 1) * dv])
            hh_ref[b:b + 1, h * dv:(h + 1) * dv] = hn * gate


def mlstm_step(q, k, v, o, gates, b_gates, g_head, c_st, n_st, m_st, heads, dk, dv):
    bsz = q.shape[0]
    nb = min(STEP_BATCH, bsz)
    g2 = 2 * heads
    rows = lambda w: pl.BlockSpec((nb, w), lambda i: (i, 0))
    return pl.pallas_call(
        functools.partial(_mlstm_step_kernel, heads=heads, dk=dk, dv=dv),
        grid=(bsz // nb,),
        in_specs=[rows(heads * dk), rows(heads * dk), rows(heads * dv), rows(heads * dv), rows(g2),
                  _const_spec((1, g2)), _const_spec((heads, dv)),
                  pl.BlockSpec((nb, heads, dk, dv), lambda i: (i, 0, 0, 0)),
                  pl.BlockSpec((nb, heads, dk), lambda i: (i, 0, 0)),
                  rows(heads)],
        out_specs=[rows(heads * dv),
                   pl.BlockSpec((nb, heads, dk, dv), lambda i: (i, 0, 0, 0)),
                   pl.BlockSpec((nb, heads, dk), lambda i: (i, 0, 0)),
                   rows(heads)],
        out_shape=[jax.ShapeDtypeStruct((bsz, heads * dv), F32),
                   jax.ShapeDtypeStruct((bsz, heads, dk, dv), F32),
                   jax.ShapeDtypeStruct((bsz, heads, dk), F32),
                   jax.ShapeDtypeStruct((bsz, heads), F32)],
        compiler_params=_params("parallel"),
        name="mlstm_step",
    )(q, k, v, o, gates, b_gates.reshape(1, g2), g_head, c_st, n_st, m_st)


def _ffn_seq_kernel(x_ref, xh_ref, buf_ref, g_ref, sh_ref, sc_ref, gt_ref, wa_ref, wg_ref, wc_ref,
                    bc_ref, wd_ref, gf_ref, o_ref, alast_ref, ext_ref, acc_ref, *, fc, final_norm):
    i = pl.program_id(1)
    tm = x_ref.shape[0]
    ff = wa_ref.shape[1]
    x = x_ref[...]
    g, sh, sc = g_ref[...], sh_ref[...], sc_ref[...]
    hn = _norm_mod(x, g, sh, sc).astype(BF16)
    hh = _norm_mod(xh_ref[...], g, sh, sc).astype(BF16)
    first = i == 0
    last = i == pl.num_programs(1) - 1

    for c in range(ff // fc):
        cs = slice(c * fc, (c + 1) * fc)
        wa = wa_ref[:, cs]
        a = jnp.dot(hn, wa, preferred_element_type=F32)
        gt = jnp.dot(hn, wg_ref[:, cs], preferred_element_type=F32)
        a_halo = jnp.dot(hh, wa, preferred_element_type=F32)
        ext_ref[0:8, :] = jnp.where(first, buf_ref[:, cs], a_halo)
        ext_ref[8:8 + tm, :] = a
        conv = (bc_ref[:, cs] + ext_ref[6:6 + tm, :] * wc_ref[0:1, cs]
                + ext_ref[7:7 + tm, :] * wc_ref[1:2, cs] + a * wc_ref[2:3, cs])
        act = (conv * jax.nn.sigmoid(conv) * gt).astype(BF16)
        part = jnp.dot(act, wd_ref[cs, :], preferred_element_type=F32)
        if c == 0:
            acc_ref[...] = part
        else:
            acc_ref[...] += part

        @pl.when(last)
        def _():
            alast_ref[:, cs] = a[tm - 8:tm, :]

    y = x + gt_ref[...] * acc_ref[...]
    if final_norm:
        y = _rms(y, gf_ref[...])
    o_ref[...] = y


def conv_ffn_seq(x, buf8, g, sh, sc, gt, w_up_a, w_up_g, w_conv, b_conv, w_down, g_final, final_norm, name):
    bsz, s, d = x.shape
    ff = w_up_a.shape[1]
    tm = min(ROW_TILE, s)
    fc = min(FFN_CHUNK, ff)
    halo = lambda b, i: (b, jnp.maximum(i * (tm // 8) - 1, 0), 0)
    return pl.pallas_call(
        functools.partial(_ffn_seq_kernel, fc=fc, final_norm=final_norm),
        grid=(bsz, s // tm),
        in_specs=[_row_spec(tm, d), pl.BlockSpec((None, 8, d), halo),
                  pl.BlockSpec((None, 8, ff), lambda b, i: (b, 0, 0)),
                  _const_spec((1, d)), _mod_spec(False, tm, d), _mod_spec(False, tm, d), _mod_spec(False, tm, d),
                  _const_spec(w_up_a.shape), _const_spec(w_up_g.shape), _const_spec(w_conv.shape),
                  _const_spec(b_conv.shape), _const_spec(w_down.shape), _const_spec((1, d))],
        out_specs=[_row_spec(tm, d), pl.BlockSpec((None, 8, ff), lambda b, i: (b, 0, 0))],
        out_shape=[jax.ShapeDtypeStruct((bsz, s, d), F32), jax.ShapeDtypeStruct((bsz, 8, ff), F32)],
        scratch_shapes=[pltpu.VMEM((tm + 8, fc), F32), pltpu.VMEM((tm, d), F32)],
        compiler_params=_params("parallel", "arbitrary"),
        name=name,
    )(x, x, buf8, g, sh, sc, gt, w_up_a, w_up_g, w_conv, b_conv, w_down, g_final)


def _ffn_tok_kernel(x_ref, b0_ref, b1_ref, g_ref, sh_ref, sc_ref, gt_ref, wa_ref, wg_ref, wc_ref,
                    bc_ref, wd_ref, gf_ref, o_ref, a_ref, acc_ref, *, final_norm):
    c = pl.program_id(0)
    x = x_ref[...]
    hn = _norm_mod(x, g_ref[...], sh_ref[...], sc_ref[...]).astype(BF16)
    a = jnp.dot(hn, wa_ref[...], preferred_element_type=F32)
    gt = jnp.dot(hn, wg_ref[...], preferred_element_type=F32)
    a_ref[...] = a
    conv = bc_ref[...] + b0_ref[...] * wc_ref[0:1, :] + b1_ref[...] * wc_ref[1:2, :] + a * wc_ref[2:3, :]
    act = (conv * jax.nn.sigmoid(conv) * gt).astype(BF16)
    part = jnp.dot(act, wd_ref[...], preferred_element_type=F32)

    @pl.when(c == 0)
    def _():
        acc_ref[...] = part

    @pl.when(c > 0)
    def _():
        acc_ref[...] += part

    y = x + gt_ref[...] * acc_ref[...]
    if final_norm:
        y = _rms(y, gf_ref[...])
    o_ref[...] = y


def conv_ffn_tok(x, buf0, buf1, g, sh, sc, gt, w_up_a, w_up_g, w_conv, b_conv, w_down, g_final, final_norm, name):
    bsz, d = x.shape
    ff = w_up_a.shape[1]
    fc = min(FFN_CHUNK, ff)
    full = lambda w: pl.BlockSpec((bsz, w), lambda c: (0, 0))
    cols = lambda r: pl.BlockSpec((r, fc), lambda c: (0, c))
    return pl.pallas_call(
        functools.partial(_ffn_tok_kernel, final_norm=final_norm),
        grid=(ff // fc,),
        in_specs=[full(d), cols(bsz), cols(bsz), _const_spec((1, d)), full(d), full(d), full(d),
                  cols(d), cols(d), cols(w_conv.shape[0]), cols(1),
                  pl.BlockSpec((fc, d), lambda c: (c, 0)), _const_spec((1, d))],
        out_specs=[full(d), cols(bsz)],
        out_shape=[jax.ShapeDtypeStruct((bsz, d), F32), jax.ShapeDtypeStruct((bsz, ff), F32)],
        scratch_shapes=[pltpu.VMEM((bsz, d), F32)],
        compiler_params=_params("arbitrary"),
        name=name,
    )(x, buf0, buf1, g, sh, sc, gt, w_up_a, w_up_g, w_conv, b_conv, w_down, g_final)


def _latent_kernel(x_ref, g_ref, sh_ref, sc_ref, w_ref, gkv_ref, rc_ref, ra_ref, rb_ref, *refs,
                   kv_lora, rope, with_kv):
    if with_kv:
        wuk_ref, wuv_ref, ckv_ref, kpe_ref, kcat_ref, v_ref = refs
    else:
        ckv_ref, kpe_ref = refs
    hn = _norm_mod(x_ref[...], g_ref[...], sh_ref[...], sc_ref[...]).astype(BF16)
    lat = jnp.dot(hn, w_ref[...], preferred_element_type=F32)
    ckv = _rms(lat[:, :kv_lora], gkv_ref[...])
    ckv_ref[...] = ckv
    kpe = _rope3(lat[:, kv_lora:kv_lora + LANES], rc_ref[...], ra_ref[...], rb_ref[...])
    kpe_ref[...] = kpe[:, :rope]
    if with_kv:
        cb = ckv.astype(BF16)
        kn = jnp.dot(cb, wuk_ref[...], preferred_element_type=F32)
        kpe_hi = pltpu.roll(kpe, 64, axis=1)
        for h in range(kn.shape[1] // LANES):
            hs = slice(h * LANES, (h + 1) * LANES)
            kcat_ref[:, hs] = (kn[:, hs] + kpe_hi).astype(BF16)
        v_ref[...] = jnp.dot(cb, wuv_ref[...], preferred_element_type=F32).astype(BF16)


def shared_latent(x, g, sh, sc, w_dkv_p, g_kv, tabs, kv_lora, rope, w_uk_r=None, w_uv_r=None):
    gq, r, d = x.shape
    tm = min(ROW_TILE, r)
    per_row = sh.shape[1] != 1
    with_kv = w_uk_r is not None
    tab_spec = pl.BlockSpec((tm, LANES), lambda g_, i: (i, 0))
    in_specs = [_row_spec(tm, d), _const_spec((1, d)), _mod_spec(per_row, tm, d), _mod_spec(per_row, tm, d),
                _const_spec(w_dkv_p.shape), _const_spec((1, kv_lora)), tab_spec, tab_spec, tab_spec]
    out_specs = [_row_spec(tm, kv_lora), _row_spec(tm, rope)]
    out_shape = [jax.ShapeDtypeStruct((gq, r, kv_lora), F32), jax.ShapeDtypeStruct((gq, r, rope), F32)]
    args = [x, g, sh, sc, w_dkv_p, g_kv, *tabs]
    if with_kv:
        in_specs += [_const_spec(w_uk_r.shape), _const_spec(w_uv_r.shape)]
        out_specs += [_row_spec(tm, w_uk_r.shape[1]), _row_spec(tm, w_uv_r.shape[1])]
        out_shape += [jax.ShapeDtypeStruct((gq, r, w_uk_r.shape[1]), BF16),
                      jax.ShapeDtypeStruct((gq, r, w_uv_r.shape[1]), BF16)]
        args += [w_uk_r, w_uv_r]
    return pl.pallas_call(
        functools.partial(_latent_kernel, kv_lora=kv_lora, rope=rope, with_kv=with_kv),
        grid=(gq, r // tm),
        in_specs=in_specs, out_specs=out_specs, out_shape=out_shape,
        compiler_params=_params("parallel", "parallel"),
        name="shared_latent_kv" if with_kv else "shared_latent",
    )(*args)


def _query_kernel(x_ref, g_ref, sh_ref, sc_ref, wdq_ref, gq_ref, wuq_ref, rc_ref, ra_ref, rb_ref, q_ref):
    hn = _norm_mod(x_ref[...], g_ref[...], sh_ref[...], sc_ref[...]).astype(BF16)
    qd = jnp.dot(hn, wdq_ref[...], preferred_element_type=F32)
    qn = _rms(qd, gq_ref[...]).astype(BF16)
    qf = jnp.dot(qn, wuq_ref[...], preferred_element_type=F32)
    rc, ra, rb = rc_ref[...], ra_ref[...], rb_ref[...]
    for h in range(qf.shape[1] // LANES):
        hs = slice(h * LANES, (h + 1) * LANES)
        q_ref[:, hs] = _rope3(qf[:, hs], rc, ra, rb).astype(q_ref.dtype)


def mla_queries(x, g, sh, sc, w_dq, g_q, w_uq_r, tabs):
    gq, r, d = x.shape
    tm = min(ROW_TILE, r)
    per_row = sh.shape[1] != 1
    tab_spec = pl.BlockSpec((tm, LANES), lambda g_, i: (i, 0))
    return pl.pallas_call(
        _query_kernel,
        grid=(gq, r // tm),
        in_specs=[_row_spec(tm, d), _const_spec((1, d)), _mod_spec(per_row, tm, d), _mod_spec(per_row, tm, d),
                  _const_spec(w_dq.shape), _const_spec(g_q.shape), _const_spec(w_uq_r.shape),
                  tab_spec, tab_spec, tab_spec],
        out_specs=_row_spec(tm, w_uq_r.shape[1]),
        out_shape=jax.ShapeDtypeStruct((gq, r, w_uq_r.shape[1]), BF16),
        compiler_params=_params("parallel", "parallel"),
        name="mla_queries",
    )(x, g, sh, sc, w_dq, g_q, w_uq_r, *tabs)


def _attn_kernel(q_ref, k_ref, v_ref, o_ref, *, v_head):
    qi = pl.program_id(2)
    tq = q_ref.shape[0]
    qs = (q_ref[:, :LANES], q_ref[:, LANES:])

    def tile(kt, carry, masked):
        start = pl.multiple_of(kt * tq, tq)
        ks = k_ref[pl.ds(start, tq), :]
        vs = v_ref[pl.ds(start, tq), :]
        out = []
        for h in range(2):
            m, l, acc = carry[h]
            s = lax.dot_general(qs[h], ks[:, h * LANES:(h + 1) * LANES], NT_DIMS, preferred_element_type=F32)
            if masked:
                row = lax.broadcasted_iota(jnp.int32, s.shape, 0)
                col = lax.broadcasted_iota(jnp.int32, s.shape, 1)
                s = jnp.where(col <= row, s, -jnp.inf)
            m_new = jnp.maximum(m, jnp.max(s, axis=1, keepdims=True))
            alpha = jnp.exp(m - m_new)
            p = jnp.exp(s - m_new)
            l = alpha * l + jnp.sum(p, axis=1, keepdims=True)
            acc = alpha * acc + jnp.dot(p.astype(BF16), vs, preferred_element_type=F32)
            out.append((m_new, l, acc))
        return tuple(out)

    init = tuple((jnp.full((tq, 1), -jnp.inf, F32), jnp.zeros((tq, 1), F32), jnp.zeros((tq, LANES), F32))
                 for _ in range(2))
    carry = lax.fori_loop(0, qi, lambda kt, c: tile(kt, c, False), init)
    (_, l0, a0), (_, l1, a1) = tile(qi, carry, True)
    lane = lax.broadcasted_iota(jnp.int32, a0.shape, 1)
    o_ref[...] = jnp.where(lane < v_head, a0 / l0, a1 / l1).astype(o_ref.dtype)


def prompt_attention(q, kcat, v, v_head):
    bsz, s, hw = q.shape
    pairs = hw // (2 * LANES)
    tq = min(ATTN_TILE, s)
    return pl.pallas_call(
        functools.partial(_attn_kernel, v_head=v_head),
        grid=(bsz, pairs, s // tq),
        in_specs=[pl.BlockSpec((None, tq, 2 * LANES), lambda b, j, i: (b, i, j)),
                  pl.BlockSpec((None, s, 2 * LANES), lambda b, j, i: (b, 0, j)),
                  pl.BlockSpec((None, s, LANES), lambda b, j, i: (b, 0, j))],
        out_specs=pl.BlockSpec((None, tq, LANES), lambda b, j, i: (b, i, j)),
        out_shape=jax.ShapeDtypeStruct((bsz, s, pairs * LANES), BF16),
        compiler_params=_params("parallel", "parallel", "arbitrary"),
        name="prompt_attention",
    )(q, kcat, v)


def _head_proj_kernel(a_ref, w_ref, o_ref):
    o_ref[...] = jnp.dot(a_ref[...].astype(BF16), w_ref[...], preferred_element_type=F32).astype(o_ref.dtype)


def head_proj_lanes(a, w, out_dtype, name):
    bsz = a.shape[0]
    nh, kk, n = w.shape
    return pl.pallas_call(
        _head_proj_kernel,
        grid=(nh,),
        in_specs=[pl.BlockSpec((bsz, kk), lambda h: (0, h)), pl.BlockSpec((None, kk, n), lambda h: (h, 0, 0))],
        out_specs=pl.BlockSpec((None, bsz, n), lambda h: (h, 0, 0)),
        out_shape=jax.ShapeDtypeStruct((nh, bsz, n), out_dtype),
        compiler_params=_params("parallel"),
        name=name,
    )(a, w)


def _paged_attn_kernel(pt_ref, ql_ref, qp_ref, cn_ref, kn_ref, *refs, pages):
    del pt_ref
    ckv_refs = refs[:pages]
    kpe_refs = refs[pages:2 * pages]
    o_ref, m_ref, l_ref, acc_ref = refs[2 * pages:]
    j = pl.program_id(1)
    ql = ql_ref[...]
    qp = qp_ref[...]

    @pl.when(j == 0)
    def _():
        cn = cn_ref[...].astype(BF16).astype(F32)
        kn = kn_ref[...].astype(BF16).astype(F32)
        s_new = (jnp.sum(ql.astype(F32) * cn, axis=1, keepdims=True)
                 + jnp.sum(qp.astype(F32) * kn, axis=1, keepdims=True))
        m_ref[...] = s_new
        l_ref[...] = jnp.ones_like(l_ref)
        acc_ref[...] = jnp.broadcast_to(cn, acc_ref.shape)

    cks = [r[...].astype(BF16) for r in ckv_refs]
    s = jnp.concatenate(
        [lax.dot_general(ql, ck, NT_DIMS, preferred_element_type=F32)
         + lax.dot_general(qp, kr[...].astype(BF16), NT_DIMS, preferred_element_type=F32)
         for ck, kr in zip(cks, kpe_refs)], axis=1)
    m = m_ref[...]
    m_new = jnp.maximum(m, jnp.max(s, axis=1, keepdims=True))
    alpha = jnp.exp(m - m_new)
    p = jnp.exp(s - m_new)
    l_ref[...] = alpha * l_ref[...] + jnp.sum(p, axis=1, keepdims=True)
    pb = p.astype(BF16)
    ps = ckv_refs[0].shape[0]
    pv = jnp.dot(pb[:, :ps], cks[0], preferred_element_type=F32)
    for i in range(1, pages):
        pv += jnp.dot(pb[:, i * ps:(i + 1) * ps], cks[i], preferred_element_type=F32)
    acc_ref[...] = alpha * acc_ref[...] + pv
    m_ref[...] = m_new

    @pl.when(j == pl.num_programs(1) - 1)
    def _():
        o_ref[...] = acc_ref[...] / l_ref[...]


def paged_attention(q_lat, q_pe, ckv_new, kpe_new, cache_ckv, cache_kpe, page_table):
    bsz, nh, c = q_lat.shape
    r = q_pe.shape[-1]
    n_pages = page_table.shape[1]
    ps = cache_ckv.shape[1]
    pages = min(PAGES_PER_STEP, n_pages)
    per_b = lambda w: pl.BlockSpec((None, nh, w), lambda b, j, pt: (b, 0, 0))
    new_b = lambda w: pl.BlockSpec((None, 1, w), lambda b, j, pt: (b, 0, 0))

    def page_spec(width, i):
        return pl.BlockSpec((None, ps, width), lambda b, j, pt: (pt[b * n_pages + j * pages + i], 0, 0))

    in_specs = [per_b(c), per_b(r), new_b(c), new_b(r)]
    in_specs += [page_spec(c, i) for i in range(pages)]
    in_specs += [page_spec(r, i) for i in range(pages)]
    return pl.pallas_call(
        functools.partial(_paged_attn_kernel, pages=pages),
        grid_spec=pltpu.PrefetchScalarGridSpec(
            num_scalar_prefetch=1,
            grid=(bsz, n_pages // pages),
            in_specs=in_specs,
            out_specs=pl.BlockSpec((None, nh, c), lambda b, j, pt: (b, 0, 0)),
            scratch_shapes=[pltpu.VMEM((nh, 1), F32), pltpu.VMEM((nh, 1), F32), pltpu.VMEM((nh, c), F32)]),
        out_shape=jax.ShapeDtypeStruct((bsz, nh, c), F32),
        compiler_params=_params("parallel", "arbitrary"),
        name="paged_attention",
    )(page_table.reshape(-1), q_lat, q_pe, ckv_new, kpe_new,
      *([cache_ckv] * pages), *([cache_kpe] * pages))


def _rope_tables(pos, rope, lo, scale, passthrough):
    half = rope // 2
    freq = ROPE_THETA ** (-jnp.arange(half, dtype=F32) / half)
    ang = pos.astype(F32)[:, None] * freq[None, :]
    cos, sin = jnp.cos(ang), jnp.sin(ang)
    n = pos.shape[0]
    zeros = lambda w: jnp.zeros((n, w), F32)
    tail = LANES - lo - rope
    c = jnp.concatenate([jnp.full((n, lo), passthrough, F32), cos, cos, zeros(tail)], axis=1)
    sa = jnp.concatenate([zeros(lo), -sin, zeros(half + tail)], axis=1)
    sb = jnp.concatenate([zeros(lo + half), sin, zeros(tail)], axis=1)
    return c * scale, sa * scale, sb * scale


def _prep_weights(w_up, w_down, w_m_in, w_m_out, w_dkv, w_uk, w_uv, w_dq, w_uq, w_o, dims):
    heads, dk, dv = dims["m_heads"], dims["m_dk"], dims["m_dv"]
    nh, nope, rope, kv_lora = dims["mla_heads"], dims["qk_nope"], dims["qk_rope"], dims["kv_lora"]
    ff = w_down.shape[1]
    qd, vd = heads * dk, heads * dv
    d = w_up.shape[1]
    pw = {}
    pw["w_up_a"] = w_up[:, :, :ff].astype(BF16)
    pw["w_up_g"] = w_up[:, :, ff:].astype(BF16)
    pw["w_down"] = w_down.astype(BF16)
    pw["w_m_q"] = w_m_in[:, :, :qd].astype(BF16)
    pw["w_m_k"] = w_m_in[:, :, qd:2 * qd].astype(BF16)
    pw["w_m_v"] = w_m_in[:, :, 2 * qd:2 * qd + vd].astype(BF16)
    pw["w_m_o"] = w_m_in[:, :, 2 * qd + vd:2 * qd + 2 * vd].astype(BF16)
    gates = w_m_in[:, :, 2 * qd + 2 * vd:]
    pw["w_m_g"] = jnp.pad(gates, ((0, 0), (0, 0), (0, LANES - gates.shape[-1]))).astype(BF16)
    pw["w_m_out"] = w_m_out.astype(BF16)
    pw["w_dkv"] = jnp.pad(w_dkv, ((0, 0), (0, LANES - rope))).astype(BF16)
    pw["w_uk_r"] = jnp.pad(w_uk, ((0, 0), (0, 0), (0, LANES - nope))).reshape(kv_lora, nh * LANES).astype(BF16)
    pw["w_uv_r"] = w_uv.reshape(kv_lora, -1).astype(BF16)
    nb = w_uq.shape[0]
    wq = w_uq.reshape(nb, w_uq.shape[1], nh, nope + rope)
    pw["w_uq_r"] = jnp.pad(wq, ((0, 0), (0, 0), (0, 0), (0, LANES - nope - rope))).reshape(
        nb, w_uq.shape[1], nh * LANES).astype(BF16)
    pw["w_dq"] = w_dq.astype(BF16)
    pw["w_o"] = w_o.astype(BF16)
    wukt = jnp.transpose(w_uk, (1, 2, 0))
    pw["w_uk_t"] = jnp.pad(wukt, ((0, 0), (0, LANES - nope), (0, 0))).astype(BF16)
    pw["w_uv_t"] = jnp.transpose(w_uv, (1, 0, 2)).astype(BF16)
    return pw


def _trunk(x, mods, mods_kv, pos, conv_bufs, m_states, kv_past, pw, small, dims):
    heads, dk, dv = dims["m_heads"], dims["m_dk"], dims["m_dv"]
    nh, nope, rope, kv_lora, v_head = (dims["mla_heads"], dims["qk_nope"], dims["qk_rope"],
                                       dims["kv_lora"], dims["v_head"])
    depth, n_a = dims["depth"], dims["n_a"]
    is_prompt = kv_past is None
    gq, r, d = x.shape
    att_scale = (nope + rope) ** -0.5
    new_c, new_n, new_m, new_conv = [], [], [], []
    ckv = kpe = kcat = vv = None
    y = None
    for layer in range(depth):
        sh1, sc1, gt1, sh2, sc2, gt2 = mods[layer]
        g1 = small["g_norm1"][layer][None, :]
        g2 = small["g_norm2"][layer][None, :]
        if layer == n_a:
            sh_kv, sc_kv = mods_kv
            tabs = _rope_tables(pos, rope, 0, 1.0, 0.0)
            if is_prompt:
                ckv, kpe, kcat, vv = shared_latent(x, small["g_kv_in"][None, :], sh_kv, sc_kv, pw["w_dkv"],
                                                   small["g_kv"][None, :], tabs, kv_lora, rope,
                                                   pw["w_uk_r"], pw["w_uv_r"])
            else:
                ckv, kpe = shared_latent(x, small["g_kv_in"][None, :], sh_kv, sc_kv, pw["w_dkv"],
                                         small["g_kv"][None, :], tabs, kv_lora, rope)
        if layer < n_a:
            ws = [pw["w_m_q"][layer], pw["w_m_k"][layer], pw["w_m_v"][layer], pw["w_m_o"][layer], pw["w_m_g"][layer]]
            if is_prompt:
                q, k, v, o, gates = norm_mod_proj(x, g1, sh1, sc1, ws, [BF16, F32, BF16, F32, F32],
                                                  [dk ** -0.5, 1.0, 1.0, 1.0, 1.0], "mlstm_in_proj")
                hh, c_st, n_st, m_st = mlstm_chunkwise(q, k, v, o, gates[..., :2 * heads],
                                                       small["b_m_gates"][layer], small["g_m_head"][layer],
                                                       heads, dk, dv)
                n_st = n_st.reshape(gq, heads, dk)
                m_st = m_st.reshape(gq, heads)
            else:
                q, k, v, o, gates = norm_mod_proj(x, g1, sh1, sc1, ws, [F32] * 5,
                                                  [dk ** -0.5, 1.0, 1.0, 1.0, 1.0], "mlstm_in_proj_tok")
                hh, c_st, n_st, m_st = mlstm_step(q[0], k[0], v[0], o[0], gates[0, :, :2 * heads],
                                                  small["b_m_gates"][layer], small["g_m_head"][layer],
                                                  m_states[0][layer], m_states[1][layer], m_states[2][layer],
                                                  heads, dk, dv)
                hh = hh[None]
            new_c.append(c_st)
            new_n.append(n_st)
            new_m.append(m_st)
            x = resid_proj(hh, pw["w_m_out"][layer], x, gt1, "mlstm_out_proj")
        else:
            j = layer - n_a
            qtabs = _rope_tables(pos, rope, nope, att_scale, 1.0)
            qh = mla_queries(x, g1, sh1, sc1, pw["w_dq"][j], small["g_q"][j][None, :], pw["w_uq_r"][j], qtabs)
            if is_prompt:
                att = prompt_attention(qh, kcat, vv, v_head)
            else:
                bsz = r
                q2 = qh[0]
                q_lat = head_proj_lanes(q2, pw["w_uk_t"], BF16, "absorb_q")
                q_lat = jnp.swapaxes(q_lat, 0, 1)
                q_pe = q2.reshape(bsz, nh, LANES)[:, :, nope:nope + rope]
                o_lat = paged_attention(q_lat, q_pe, ckv[0][:, None, :], kpe[0][:, None, :],
                                        kv_past[0], kv_past[1], kv_past[2])
                o_lat = o_lat.reshape(bsz, nh * kv_lora)
                att = head_proj_lanes(o_lat, pw["w_uv_t"], F32, "unabsorb_o")
                att = jnp.swapaxes(att, 0, 1).reshape(1, bsz, nh * v_head)
            x = resid_proj(att, pw["w_o"][j], x, gt1, "mla_out_proj")
        final = layer == depth - 1
        gf = small["g_final"][None, :]
        if is_prompt:
            x, a_last = conv_ffn_seq(x, conv_bufs[layer], g2, sh2, sc2, gt2, pw["w_up_a"][layer], pw["w_up_g"][layer],
                                     small["w_conv"][layer], small["b_conv"][layer][None, :], pw["w_down"][layer],
                                     gf, final, "conv_ffn_seq")
            new_conv.append(a_last[:, 6:8, :])
        else:
            buf = conv_bufs[layer]
            x2, a_new = conv_ffn_tok(x[0], buf[:, 0, :], buf[:, 1, :], g2, sh2[0], sc2[0], gt2[0],
                                     pw["w_up_a"][layer], pw["w_up_g"][layer], small["w_conv"][layer],
                                     small["b_conv"][layer][None, :], pw["w_down"][layer], gf, final, "conv_ffn_tok")
            x = x2[None]
            new_conv.append(jnp.stack([buf[:, 1, :], a_new], axis=1))
    return x, jnp.stack(new_c), jnp.stack(new_n), jnp.stack(new_m), jnp.stack(new_conv), ckv, kpe


def kernel(x_prompt, x_sample, state_mlstm_C, state_mlstm_n, state_mlstm_m, state_conv, cache_ckv, cache_kpe,
           page_table, c_prompt, c_sample, g_norm1, g_norm2, w_ada, b_ada, w_up, w_conv, b_conv, w_down,
           w_m_in, b_m_gates, g_m_head, w_m_out, g_kv_in, w_ada_kv, b_ada_kv, w_dkv, g_kv, w_uk, w_uv,
           w_dq, g_q, w_uq, w_o, g_final):
    bp, s, d = x_prompt.shape
    bs, t, _ = x_sample.shape
    assert t == 1, "the sample path handles one new token per sequence"
    depth = w_ada.shape[0]
    n_a = w_m_in.shape[0]
    heads, dv = g_m_head.shape[1], g_m_head.shape[2]
    dk = state_mlstm_C.shape[3]
    kv_lora, nh, nope = w_uk.shape
    v_head = w_uv.shape[2]
    rope = w_dkv.shape[1] - kv_lora
    ff = w_down.shape[1]
    dims = dict(m_heads=heads, m_dk=dk, m_dv=dv, mla_heads=nh, qk_nope=nope, qk_rope=rope, kv_lora=kv_lora,
                v_head=v_head, depth=depth, n_a=n_a)
    past_len = page_table.shape[1] * cache_ckv.shape[1]

    pw = _prep_weights(w_up, w_down, w_m_in, w_m_out, w_dkv, w_uk, w_uv, w_dq, w_uq, w_o, dims)
    small = dict(g_norm1=g_norm1, g_norm2=g_norm2, w_conv=w_conv, b_conv=b_conv, b_m_gates=b_m_gates,
                 g_m_head=g_m_head, g_kv_in=g_kv_in, g_kv=g_kv, g_q=g_q, g_final=g_final)

    c_all = jnp.concatenate([c_prompt, c_sample], axis=0)
    mod = ada_mod(c_all, w_ada, b_ada[:, None, :])
    mod_kv = ada_mod(c_all, w_ada_kv[None], b_ada_kv[None, None, :])[0]

    def split(m, n, lo, hi, per_row):
        parts = jnp.split(m[lo:hi], n, axis=-1)
        return [p[None] if per_row else p[:, None, :] for p in parts]

    mods_p = [split(mod[l], 6, 0, bp, False) for l in range(depth)]
    mods_s = [split(mod[l], 6, bp, bp + bs, True) for l in range(depth)]
    kv_p = split(mod_kv, 2, 0, bp, False)
    kv_s = split(mod_kv, 2, bp, bp + bs, True)

    conv0 = [jnp.zeros((bp, 8, ff), F32)] * depth
    y_p, c_p, n_p, m_p, conv_p, ckv_p, kpe_p = _trunk(
        x_prompt, mods_p, kv_p, jnp.arange(s), conv0, None, None, pw, small, dims)

    pos_s = jnp.full((bs,), past_len, jnp.int32)
    y_s, c_s, n_s, m_s, conv_s, ckv_s, kpe_s = _trunk(
        x_sample.reshape(1, bs, d), mods_s, kv_s, pos_s, state_conv,
        (state_mlstm_C, state_mlstm_n, state_mlstm_m), (cache_ckv, cache_kpe, page_table), pw, small, dims)

    return (y_p, y_s.reshape(bs, 1, d), c_p, n_p, m_p, conv_p, ckv_p, kpe_p,
            c_s, n_s, m_s, conv_s, ckv_s.reshape(bs, 1, kv_lora), kpe_s.reshape(bs, 1, rope))
```

```python
import functools
import math

import jax
import jax.numpy as jnp
from jax import lax
from jax.experimental import pallas as pl
from jax.experimental.pallas import tpu as pltpu

F32 = jnp.float32
BF16 = jnp.bfloat16

NORM_EPS = 1e-6
ROPE_THETA = 10000.0
LANES = 128
VMEM_LIMIT = 56 * 1024 * 1024

ROW_TILE = 512
MLSTM_CHUNK = 256
ATTN_TILE = 1024
ATTN_BLOCK = 512
FFN_CHUNK = 256
PAGED_CHUNK = 512
LOG2_E = math.log2(math.e)
STEP_BATCH = 8

NT_DIMS = (((1,), (1,)), ((), ()))
TN_DIMS = (((0,), (0,)), ((), ()))


def _params(*sem):
    return pltpu.CompilerParams(dimension_semantics=sem, vmem_limit_bytes=VMEM_LIMIT)


def _const_spec(shape):
    nd = len(shape)
    return pl.BlockSpec(shape, lambda *_: (0,) * nd, pipeline_mode=pl.Buffered(1))


def _rms(x, g):
    return x * lax.rsqrt(jnp.mean(x * x, axis=-1, keepdims=True) + NORM_EPS) * g


def _norm_mod(x, g, sh, sc):
    return _rms(x, g) * (1.0 + sc) + sh


def _log_sigmoid(x):
    return jnp.minimum(x, 0.0) - jnp.log(1.0 + jnp.exp(-jnp.abs(x)))


def _rope3(t, c, sa, sb):
    return t * c + pltpu.roll(t, LANES - 16, axis=1) * sa + pltpu.roll(t, 16, axis=1) * sb


def _row_spec(tm, width):
    return pl.BlockSpec((None, tm, width), lambda g, i: (g, i, 0))


def _mod_spec(per_row, tm, width):
    if per_row:
        return pl.BlockSpec((None, tm, width), lambda g, i: (g, i, 0))
    return pl.BlockSpec((None, 1, width), lambda g, i: (g, 0, 0))


def _ada_kernel(c_ref, w_ref, b_ref, o_ref):
    c = c_ref[...]
    a = (c * jax.nn.sigmoid(c)).astype(BF16)
    o_ref[...] = jnp.dot(a, w_ref[...].astype(BF16), preferred_element_type=F32) + b_ref[...]


def ada_mod(c, w, b, tn=1024):
    m, d = c.shape
    nl, _, n = w.shape
    tn = min(tn, n)
    return pl.pallas_call(
        _ada_kernel,
        grid=(nl, n // tn),
        in_specs=[pl.BlockSpec((m, d), lambda l, j: (0, 0)),
                  pl.BlockSpec((None, d, tn), lambda l, j: (l, 0, j)),
                  pl.BlockSpec((None, 1, tn), lambda l, j: (l, 0, j))],
        out_specs=pl.BlockSpec((None, m, tn), lambda l, j: (l, 0, j)),
        out_shape=jax.ShapeDtypeStruct((nl, m, n), F32),
        compiler_params=_params("parallel", "parallel"),
        name="ada_mod",
    )(c, w, b)


def _proj_kernel(x_ref, g_ref, sh_ref, sc_ref, *refs, scales):
    n = len(scales)
    hn = _norm_mod(x_ref[...], g_ref[...], sh_ref[...], sc_ref[...]).astype(BF16)
    for w_ref, o_ref, s in zip(refs[:n], refs[n:], scales):
        acc = jnp.dot(hn, w_ref[...], preferred_element_type=F32)
        if s != 1.0:
            acc = acc * s
        o_ref[...] = acc.astype(o_ref.dtype)


def norm_mod_proj(x, g, sh, sc, ws, out_dtypes, scales, name):
    gq, r, d = x.shape
    tm = min(ROW_TILE, r)
    per_row = sh.shape[1] != 1
    in_specs = [_row_spec(tm, d), _const_spec((1, d)),
                _mod_spec(per_row, tm, d), _mod_spec(per_row, tm, d)]
    in_specs += [_const_spec(w.shape) for w in ws]
    return pl.pallas_call(
        functools.partial(_proj_kernel, scales=tuple(scales)),
        grid=(gq, r // tm),
        in_specs=in_specs,
        out_specs=[_row_spec(tm, w.shape[1]) for w in ws],
        out_shape=[jax.ShapeDtypeStruct((gq, r, w.shape[1]), dt) for w, dt in zip(ws, out_dtypes)],
        compiler_params=_params("parallel", "parallel"),
        name=name,
    )(x, g, sh, sc, *ws)


def _resid_kernel(a_ref, w_ref, x_ref, gt_ref, o_ref):
    mix = jnp.dot(a_ref[...].astype(BF16), w_ref[...], preferred_element_type=F32)
    o_ref[...] = x_ref[...] + gt_ref[...] * mix


def resid_proj(a, w, x, gt, name):
    gq, r, d = x.shape
    k = a.shape[-1]
    tm = min(ROW_TILE, r)
    per_row = gt.shape[1] != 1
    return pl.pallas_call(
        _resid_kernel,
        grid=(gq, r // tm),
        in_specs=[_row_spec(tm, k), _const_spec(w.shape), _row_spec(tm, d), _mod_spec(per_row, tm, d)],
        out_specs=_row_spec(tm, d),
        out_shape=jax.ShapeDtypeStruct((gq, r, d), F32),
        compiler_params=_params("parallel", "parallel"),
        name=name,
    )(a, w, x, gt)


def _mlstm_chunk_kernel(q_ref, k_ref, v_ref, o_ref, gr_ref, gc_ref, bgc_ref, bgr_ref, gh_ref,
                        hh_ref, c_ref, n_ref, m_ref, *, heads, dk, dv):
    ci = pl.program_id(1)
    chunk = q_ref.shape[0]

    @pl.when(ci == 0)
    def _():
        c_ref[...] = jnp.zeros_like(c_ref)
        n_ref[...] = jnp.zeros_like(n_ref)
        m_ref[...] = jnp.zeros_like(m_ref)

    gates_r = gr_ref[...] + bgc_ref[...]
    gates_c = gc_ref[...] + bgr_ref[...]
    row = lax.broadcasted_iota(jnp.int32, (chunk, chunk), 0)
    col = lax.broadcasted_iota(jnp.int32, (chunk, chunk), 1)
    causal = col <= row

    s_raw, q_c = [], []
    for h in range(heads):
        q = q_ref[:, h * dk:(h + 1) * dk]
        s_raw.append(lax.dot_general(q, k_ref[:, h * dk:(h + 1) * dk].astype(BF16), NT_DIMS,
                                     preferred_element_type=F32))
        q_c.append(jnp.dot(q, c_ref[h].astype(BF16), preferred_element_type=F32))

    for h in range(heads):
        q = q_ref[:, h * dk:(h + 1) * dk]
        k = k_ref[:, h * dk:(h + 1) * dk]
        v = v_ref[:, h * dv:(h + 1) * dv]
        li_r = gates_r[h:h + 1, :]
        lf_r = _log_sigmoid(gates_r[heads + h:heads + h + 1, :])
        li_c = gates_c[:, h:h + 1]
        lf_c = _log_sigmoid(gates_c[:, heads + h:heads + h + 1])

        b_c = jnp.sum(jnp.where(causal, lf_r, 0.0), axis=1, keepdims=True)
        b_r = jnp.sum(jnp.where(row <= col, lf_c, 0.0), axis=0, keepdims=True)
        g = jnp.sum(lf_r, axis=1, keepdims=True)

        m_prev = m_ref[h]
        c_prev = c_ref[h]
        n_prev = n_ref[h]

        a_c = g - b_c + li_c
        m_loc = jnp.max(a_c, axis=0, keepdims=True)
        kw = k * jnp.exp(a_c - m_loc)
        c_loc = lax.dot_general(kw.astype(BF16), v, TN_DIMS, preferred_element_type=F32)
        n_loc = jnp.sum(kw, axis=0, keepdims=True)

        dmat = jnp.where(causal, b_c - b_r + li_r, -jnp.inf)
        w0 = b_c + m_prev
        m_s = jnp.maximum(w0, jnp.max(dmat, axis=1, keepdims=True))
        w_inter = jnp.exp(w0 - m_s)
        s = s_raw[h] * jnp.exp(dmat - m_s)
        num = w_inter * q_c[h] + jnp.dot(s.astype(BF16), v, preferred_element_type=F32)
        den = (w_inter * jnp.sum(q.astype(F32) * n_prev, axis=1, keepdims=True)
               + jnp.sum(s, axis=1, keepdims=True))
        hval = num / jnp.maximum(jnp.abs(den), jnp.exp(-m_s))
        hn = _rms(hval, gh_ref[h:h + 1, :])
        gate = jax.nn.sigmoid(o_ref[:, h * dv:(h + 1) * dv])
        hh_ref[:, h * dv:(h + 1) * dv] = (hn * gate).astype(hh_ref.dtype)

        m_new = jnp.maximum(g + m_prev, m_loc)
        fw = jnp.exp(g + m_prev - m_new)
        lw = jnp.exp(m_loc - m_new)
        c_ref[h] = fw * c_prev + lw * c_loc
        n_ref[h] = fw * n_prev + lw * n_loc
        m_ref[h] = m_new


def mlstm_chunkwise(q, k, v, o, gates, b_gates, g_head, heads, dk, dv):
    bsz, s, _ = q.shape
    chunk = min(MLSTM_CHUNK, s)
    gates_r = jnp.swapaxes(gates, 1, 2)
    g2 = 2 * heads
    return pl.pallas_call(
        functools.partial(_mlstm_chunk_kernel, heads=heads, dk=dk, dv=dv),
        grid=(bsz, s // chunk),
        in_specs=[_row_spec(chunk, heads * dk), _row_spec(chunk, heads * dk),
                  _row_spec(chunk, heads * dv), _row_spec(chunk, heads * dv),
                  pl.BlockSpec((None, g2, chunk), lambda b, c: (b, 0, c)),
                  _row_spec(chunk, g2),
                  _const_spec((g2, 1)), _const_spec((1, g2)), _const_spec((heads, dv))],
        out_specs=[_row_spec(chunk, heads * dv),
                   pl.BlockSpec((None, heads, dk, dv), lambda b, c: (b, 0, 0, 0)),
                   pl.BlockSpec((None, heads, 1, dk), lambda b, c: (b, 0, 0, 0)),
                   pl.BlockSpec((None, heads, 1, 1), lambda b, c: (b, 0, 0, 0))],
        out_shape=[jax.ShapeDtypeStruct((bsz, s, heads * dv), BF16),
                   jax.ShapeDtypeStruct((bsz, heads, dk, dv), F32),
                   jax.ShapeDtypeStruct((bsz, heads, 1, dk), F32),
                   jax.ShapeDtypeStruct((bsz, heads, 1, 1), F32)],
        compiler_params=_params("parallel", "arbitrary"),
        name="mlstm_chunkwise",
    )(q, k, v, o, gates_r, gates, b_gates.reshape(g2, 1), b_gates.reshape(1, g2), g_head)


def _mlstm_step_kernel(q_ref, k_ref, v_ref, o_ref, g_ref, bg_ref, gh_ref, c_ref, n_ref, m_ref,
                       hh_ref, co_ref, no_ref, mo_ref, *, heads, dk, dv):
    nb = q_ref.shape[0]
    gates = g_ref[...] + bg_ref[...]
    li = gates[:, :heads]
    lf = _log_sigmoid(gates[:, heads:])
    m_st = m_ref[...]
    m_new = jnp.maximum(lf + m_st, li)
    fw_all = jnp.exp(lf + m_st - m_new)
    iw_all = jnp.exp(li - m_new)
    floor_all = jnp.exp(-m_new)
    mo_ref[...] = m_new
    eye = lax.broadcasted_iota(jnp.int32, (dk, dk), 0) == lax.broadcasted_iota(jnp.int32, (dk, dk), 1)

    def to_col(r):
        return jnp.sum(jnp.where(eye, r, 0.0), axis=1, keepdims=True)

    hr = range(heads)
    for b in range(nb):
        q_r = [q_ref[b:b + 1, h * dk:(h + 1) * dk] for h in hr]
        k_r = [k_ref[b:b + 1, h * dk:(h + 1) * dk] for h in hr]
        k_c = [iw_all[b:b + 1, h:h + 1] * to_col(k_r[h]) for h in hr]
        c_new = [fw_all[b:b + 1, h:h + 1] * c_ref[b, h] + k_c[h] * v_ref[b:b + 1, h * dv:(h + 1) * dv] for h in hr]
        num = [jnp.dot(q_r[h].astype(BF16), c_new[h].astype(BF16), preferred_element_type=F32) for h in hr]
        n_new = [fw_all[b:b + 1, h:h + 1] * n_ref[b, h:h + 1, :] + iw_all[b:b + 1, h:h + 1] * k_r[h] for h in hr]
        den = [jnp.sum(q_r[h] * n_new[h], axis=1, keepdims=True) for h in hr]
        hval = [num[h] / jnp.maximum(jnp.abs(den[h]), floor_all[b:b + 1, h:h + 1]) for h in hr]
        hn = [_rms(hval[h], gh_ref[h:h + 1, :]) for h in hr]
        for h in hr:
            co_ref[b, h] = c_new[h]
            no_ref[b, h:h + 1, :] = n_new[h]
            hh_ref[b:b + 1, h * dv:(h + 1) * dv] = hn[h] * jax.nn.sigmoid(o_ref[b:b + 1, h * dv:(h + 1) * dv])


def mlstm_step(q, k, v, o, gates, b_gates, g_head, c_st, n_st, m_st, heads, dk, dv):
    bsz = q.shape[0]
    nb = min(STEP_BATCH, bsz)
    g2 = 2 * heads
    rows = lambda w: pl.BlockSpec((nb, w), lambda i: (i, 0))
    return pl.pallas_call(
        functools.partial(_mlstm_step_kernel, heads=heads, dk=dk, dv=dv),
        grid=(bsz // nb,),
        in_specs=[rows(heads * dk), rows(heads * dk), rows(heads * dv), rows(heads * dv), rows(g2),
                  _const_spec((1, g2)), _const_spec((heads, dv)),
                  pl.BlockSpec((nb, heads, dk, dv), lambda i: (i, 0, 0, 0)),
                  pl.BlockSpec((nb, heads, dk), lambda i: (i, 0, 0)),
                  rows(heads)],
        out_specs=[rows(heads * dv),
                   pl.BlockSpec((nb, heads, dk, dv), lambda i: (i, 0, 0, 0)),
                   pl.BlockSpec((nb, heads, dk), lambda i: (i, 0, 0)),
                   rows(heads)],
        out_shape=[jax.ShapeDtypeStruct((bsz, heads * dv), F32),
                   jax.ShapeDtypeStruct((bsz, heads, dk, dv), F32),
                   jax.ShapeDtypeStruct((bsz, heads, dk), F32),
                   jax.ShapeDtypeStruct((bsz, heads), F32)],
        compiler_params=_params("parallel"),
        name="mlstm_step",
    )(q, k, v, o, gates, b_gates.reshape(1, g2), g_head, c_st, n_st, m_st)


def _ffn_seq_kernel(x_ref, xh_ref, buf_ref, g_ref, sh_ref, sc_ref, gt_ref, wa_ref, wg_ref, wc_ref,
                    bc_ref, wd_ref, gf_ref, o_ref, alast_ref, *, fc, final_norm):
    i = pl.program_id(1)
    tm = x_ref.shape[0]
    ff = wa_ref.shape[1]
    x = x_ref[...]
    g, sh, sc = g_ref[...], sh_ref[...], sc_ref[...]
    hn = _norm_mod(x, g, sh, sc).astype(BF16)
    hh = _norm_mod(xh_ref[...], g, sh, sc).astype(BF16)
    first = i == 0

    def up(c):
        cs = slice(c * fc, (c + 1) * fc)
        wa = wa_ref[:, cs]
        return (jnp.dot(hn, wa, preferred_element_type=F32),
                jnp.dot(hn, wg_ref[:, cs], preferred_element_type=F32),
                jnp.dot(hh, wa, preferred_element_type=F32))

    n_chunks = ff // fc
    acc = None
    nxt = up(0)
    for c in range(n_chunks):
        cs = slice(c * fc, (c + 1) * fc)
        a, gt, a_halo = nxt
        if c + 1 < n_chunks:
            nxt = up(c + 1)
        ext = jnp.concatenate([jnp.where(first, buf_ref[:, cs], a_halo), a], axis=0)
        conv = (bc_ref[:, cs] + ext[6:6 + tm, :] * wc_ref[0:1, cs]
                + ext[7:7 + tm, :] * wc_ref[1:2, cs] + a * wc_ref[2:3, cs])
        act = (conv * jax.nn.sigmoid(conv) * gt).astype(BF16)
        part = jnp.dot(act, wd_ref[cs, :], preferred_element_type=F32)
        acc = part if acc is None else acc + part
        alast_ref[:, cs] = a[tm - 8:tm, :]

    y = x + gt_ref[...] * acc
    if final_norm:
        y = _rms(y, gf_ref[...])
    o_ref[...] = y


def conv_ffn_seq(x, buf8, g, sh, sc, gt, w_up_a, w_up_g, w_conv, b_conv, w_down, g_final, final_norm, name):
    bsz, s, d = x.shape
    ff = w_up_a.shape[1]
    tm = min(ROW_TILE, s)
    fc = min(FFN_CHUNK, ff)
    halo = lambda b, i: (b, jnp.maximum(i * (tm // 8) - 1, 0), 0)
    return pl.pallas_call(
        functools.partial(_ffn_seq_kernel, fc=fc, final_norm=final_norm),
        grid=(bsz, s // tm),
        in_specs=[_row_spec(tm, d), pl.BlockSpec((None, 8, d), halo),
                  pl.BlockSpec((None, 8, ff), lambda b, i: (b, 0, 0)),
                  _const_spec((1, d)), _mod_spec(False, tm, d), _mod_spec(False, tm, d), _mod_spec(False, tm, d),
                  _const_spec(w_up_a.shape), _const_spec(w_up_g.shape), _const_spec(w_conv.shape),
                  _const_spec(b_conv.shape), _const_spec(w_down.shape), _const_spec((1, d))],
        out_specs=[_row_spec(tm, d), pl.BlockSpec((None, 8, ff), lambda b, i: (b, 0, 0))],
        out_shape=[jax.ShapeDtypeStruct((bsz, s, d), F32), jax.ShapeDtypeStruct((bsz, 8, ff), F32)],
        compiler_params=_params("parallel", "arbitrary"),
        name=name,
    )(x, x, buf8, g, sh, sc, gt, w_up_a, w_up_g, w_conv, b_conv, w_down, g_final)


def _ffn_tok_kernel(x_ref, b0_ref, b1_ref, g_ref, sh_ref, sc_ref, gt_ref, wa_ref, wg_ref, wc_ref,
                    bc_ref, wd_ref, gf_ref, o_ref, a_ref, acc_ref, *, final_norm):
    c = pl.program_id(0)
    x = x_ref[...]
    hn = _norm_mod(x, g_ref[...], sh_ref[...], sc_ref[...]).astype(BF16)
    a = jnp.dot(hn, wa_ref[...], preferred_element_type=F32)
    gt = jnp.dot(hn, wg_ref[...], preferred_element_type=F32)
    a_ref[...] = a
    conv = bc_ref[...] + b0_ref[...] * wc_ref[0:1, :] + b1_ref[...] * wc_ref[1:2, :] + a * wc_ref[2:3, :]
    act = (conv * jax.nn.sigmoid(conv) * gt).astype(BF16)
    part = jnp.dot(act, wd_ref[...], preferred_element_type=F32)

    @pl.when(c == 0)
    def _():
        acc_ref[...] = part

    @pl.when(c > 0)
    def _():
        acc_ref[...] += part

    y = x + gt_ref[...] * acc_ref[...]
    if final_norm:
        y = _rms(y, gf_ref[...])
    o_ref[...] = y


def conv_ffn_tok(x, buf0, buf1, g, sh, sc, gt, w_up_a, w_up_g, w_conv, b_conv, w_down, g_final, final_norm, name):
    bsz, d = x.shape
    ff = w_up_a.shape[1]
    fc = min(FFN_CHUNK, ff)
    full = lambda w: pl.BlockSpec((bsz, w), lambda c: (0, 0))
    cols = lambda r: pl.BlockSpec((r, fc), lambda c: (0, c))
    return pl.pallas_call(
        functools.partial(_ffn_tok_kernel, final_norm=final_norm),
        grid=(ff // fc,),
        in_specs=[full(d), cols(bsz), cols(bsz), _const_spec((1, d)), full(d), full(d), full(d),
                  cols(d), cols(d), cols(w_conv.shape[0]), cols(1),
                  pl.BlockSpec((fc, d), lambda c: (c, 0)), _const_spec((1, d))],
        out_specs=[full(d), cols(bsz)],
        out_shape=[jax.ShapeDtypeStruct((bsz, d), F32), jax.ShapeDtypeStruct((bsz, ff), F32)],
        scratch_shapes=[pltpu.VMEM((bsz, d), F32)],
        compiler_params=_params("arbitrary"),
        name=name,
    )(x, buf0, buf1, g, sh, sc, gt, w_up_a, w_up_g, w_conv, b_conv, w_down, g_final)


def _latent_kernel(x_ref, g_ref, sh_ref, sc_ref, w_ref, gkv_ref, rc_ref, ra_ref, rb_ref, *refs,
                   kv_lora, rope, with_kv):
    if with_kv:
        wuk_ref, wuv_ref, one_ref, ckv_ref, kpe_ref, kcat_ref, v_ref = refs
    else:
        ckv_ref, kpe_ref = refs
    hn = _norm_mod(x_ref[...], g_ref[...], sh_ref[...], sc_ref[...]).astype(BF16)
    lat = jnp.dot(hn, w_ref[...], preferred_element_type=F32)
    ckv = _rms(lat[:, :kv_lora], gkv_ref[...])
    ckv_ref[...] = ckv
    kpe = _rope3(lat[:, kv_lora:kv_lora + LANES], rc_ref[...], ra_ref[...], rb_ref[...])
    kpe_ref[...] = kpe[:, :rope]
    if with_kv:
        cb = ckv.astype(BF16)
        kn = jnp.dot(cb, wuk_ref[...], preferred_element_type=F32)
        kpe_hi = pltpu.roll(kpe, 64, axis=1)
        for h in range(kn.shape[1] // LANES):
            hs = slice(h * LANES, (h + 1) * LANES)
            kcat_ref[:, hs] = (kn[:, hs] + kpe_hi).astype(BF16)
        v_ref[...] = (jnp.dot(cb, wuv_ref[...], preferred_element_type=F32) + one_ref[...]).astype(BF16)


def shared_latent(x, g, sh, sc, w_dkv_p, g_kv, tabs, kv_lora, rope, w_uk_r=None, w_uv_r=None, v_head=None):
    gq, r, d = x.shape
    tm = min(ROW_TILE, r)
    per_row = sh.shape[1] != 1
    with_kv = w_uk_r is not None
    tab_spec = pl.BlockSpec((tm, LANES), lambda g_, i: (i, 0))
    in_specs = [_row_spec(tm, d), _const_spec((1, d)), _mod_spec(per_row, tm, d), _mod_spec(per_row, tm, d),
                _const_spec(w_dkv_p.shape), _const_spec((1, kv_lora)), tab_spec, tab_spec, tab_spec]
    out_specs = [_row_spec(tm, kv_lora), _row_spec(tm, rope)]
    out_shape = [jax.ShapeDtypeStruct((gq, r, kv_lora), F32), jax.ShapeDtypeStruct((gq, r, rope), F32)]
    args = [x, g, sh, sc, w_dkv_p, g_kv, *tabs]
    if with_kv:
        ones_lane = (jnp.arange(w_uv_r.shape[1]) % LANES == v_head).astype(F32)[None, :]
        in_specs += [_const_spec(w_uk_r.shape), _const_spec(w_uv_r.shape), _const_spec(ones_lane.shape)]
        out_specs += [_row_spec(tm, w_uk_r.shape[1]), _row_spec(tm, w_uv_r.shape[1])]
        out_shape += [jax.ShapeDtypeStruct((gq, r, w_uk_r.shape[1]), BF16),
                      jax.ShapeDtypeStruct((gq, r, w_uv_r.shape[1]), BF16)]
        args += [w_uk_r, w_uv_r, ones_lane]
    return pl.pallas_call(
        functools.partial(_latent_kernel, kv_lora=kv_lora, rope=rope, with_kv=with_kv),
        grid=(gq, r // tm),
        in_specs=in_specs, out_specs=out_specs, out_shape=out_shape,
        compiler_params=_params("parallel", "parallel"),
        name="shared_latent_kv" if with_kv else "shared_latent",
    )(*args)


def _query_kernel(x_ref, g_ref, sh_ref, sc_ref, wdq_ref, gq_ref, wuq_ref, rc_ref, ra_ref, rb_ref, q_ref):
    hn = _norm_mod(x_ref[...], g_ref[...], sh_ref[...], sc_ref[...]).astype(BF16)
    qd = jnp.dot(hn, wdq_ref[...], preferred_element_type=F32)
    qn = _rms(qd, gq_ref[...]).astype(BF16)
    qf = jnp.dot(qn, wuq_ref[...], preferred_element_type=F32)
    rc, ra, rb = rc_ref[...], ra_ref[...], rb_ref[...]
    for h in range(qf.shape[1] // LANES):
        hs = slice(h * LANES, (h + 1) * LANES)
        q_ref[:, hs] = _rope3(qf[:, hs], rc, ra, rb).astype(q_ref.dtype)


def mla_queries(x, g, sh, sc, w_dq, g_q, w_uq_r, tabs):
    gq, r, d = x.shape
    tm = min(ROW_TILE, r)
    per_row = sh.shape[1] != 1
    tab_spec = pl.BlockSpec((tm, LANES), lambda g_, i: (i, 0))
    return pl.pallas_call(
        _query_kernel,
        grid=(gq, r // tm),
        in_specs=[_row_spec(tm, d), _const_spec((1, d)), _mod_spec(per_row, tm, d), _mod_spec(per_row, tm, d),
                  _const_spec(w_dq.shape), _const_spec(g_q.shape), _const_spec(w_uq_r.shape),
                  tab_spec, tab_spec, tab_spec],
        out_specs=_row_spec(tm, w_uq_r.shape[1]),
        out_shape=jax.ShapeDtypeStruct((gq, r, w_uq_r.shape[1]), BF16),
        compiler_params=_params("parallel", "parallel"),
        name="mla_queries",
    )(x, g, sh, sc, w_dq, g_q, w_uq_r, *tabs)


def _attn_kernel(q_ref, k_ref, v_ref, o_ref, *, v_head, blk):
    qi = pl.program_id(2)
    tq = q_ref.shape[0]
    nsub = tq // blk
    row = lax.broadcasted_iota(jnp.int32, (blk, blk), 0)
    col = lax.broadcasted_iota(jnp.int32, (blk, blk), 1)
    chains = [(h, r) for h in range(2) for r in range(nsub)]

    def scores(h, r, kstart):
        hs = slice(h * LANES, (h + 1) * LANES)
        return lax.dot_general(q_ref[r * blk:(r + 1) * blk, hs], k_ref[pl.ds(kstart, blk), hs], NT_DIMS,
                               preferred_element_type=F32)

    def update(h, s, kstart, masked, state):
        m, acc = state
        if masked:
            s = jnp.where(col <= row, s, -jnp.inf)
        m_new = jnp.maximum(m, jnp.max(s, axis=1, keepdims=True))
        p = jnp.exp2(s - m_new).astype(BF16)
        pv = jnp.dot(p, v_ref[pl.ds(kstart, blk), h * LANES:(h + 1) * LANES], preferred_element_type=F32)
        return m_new, jnp.exp2(m - m_new) * acc + pv

    def run(tasks, carry):
        carry = list(carry)
        nxt = scores(*chains[tasks[0][0]], tasks[0][1])
        for t, (i, kstart, masked) in enumerate(tasks):
            s = nxt
            if t + 1 < len(tasks):
                nxt = scores(*chains[tasks[t + 1][0]], tasks[t + 1][1])
            carry[i] = update(chains[i][0], s, kstart, masked, carry[i])
        return tuple(carry)

    def full_chunk(kt, carry):
        kstart = pl.multiple_of(kt * blk, blk)
        return run([(i, kstart, False) for i in range(len(chains))], carry)

    init = tuple((jnp.full((blk, 1), -jnp.inf, F32), jnp.zeros((blk, LANES), F32)) for _ in chains)
    carry = lax.fori_loop(0, qi * nsub, full_chunk, init)
    base = pl.multiple_of(qi * tq, tq)
    diag = [(i, base + c * blk, c == r) for c in range(nsub) for i, (_, r) in enumerate(chains) if c <= r]
    carry = run(diag, carry)

    outs = []
    for h in range(2):
        acc = jnp.concatenate([carry[h * nsub + r][1] for r in range(nsub)], axis=0)
        outs.append(acc / acc[:, v_head:v_head + 1])
    lane = lax.broadcasted_iota(jnp.int32, (tq, LANES), 1)
    o_ref[...] = jnp.where(lane < v_head, outs[0], pltpu.roll(outs[1], v_head, axis=1)).astype(o_ref.dtype)


def prompt_attention(q, kcat, v1, v_head):
    bsz, s, hw = q.shape
    pairs = hw // (2 * LANES)
    tq = min(ATTN_TILE, s)
    blk = min(ATTN_BLOCK, tq)
    pair_rows = lambda rows, idx: pl.BlockSpec((None, rows, 2 * LANES), idx)
    return pl.pallas_call(
        functools.partial(_attn_kernel, v_head=v_head, blk=blk),
        grid=(bsz, pairs, s // tq),
        in_specs=[pair_rows(tq, lambda b, j, i: (b, i, j)),
                  pair_rows(s, lambda b, j, i: (b, 0, j)),
                  pair_rows(s, lambda b, j, i: (b, 0, j))],
        out_specs=pl.BlockSpec((None, tq, 2 * v_head), lambda b, j, i: (b, i, j)),
        out_shape=jax.ShapeDtypeStruct((bsz, s, pairs * 2 * v_head), BF16),
        compiler_params=_params("parallel", "parallel", "arbitrary"),
        name="prompt_attention",
    )(q, kcat, v1)


def _head_proj_kernel(a_ref, w_ref, o_ref):
    o_ref[...] = jnp.dot(a_ref[...].astype(BF16), w_ref[...], preferred_element_type=F32).astype(o_ref.dtype)


def head_proj_lanes(a, w, out_dtype, name):
    bsz = a.shape[0]
    nh, kk, n = w.shape
    return pl.pallas_call(
        _head_proj_kernel,
        grid=(nh,),
        in_specs=[pl.BlockSpec((bsz, kk), lambda h: (0, h)), pl.BlockSpec((None, kk, n), lambda h: (h, 0, 0))],
        out_specs=pl.BlockSpec((None, bsz, n), lambda h: (h, 0, 0)),
        out_shape=jax.ShapeDtypeStruct((nh, bsz, n), out_dtype),
        compiler_params=_params("parallel"),
        name=name,
    )(a, w)


def _paged_attn_kernel(pt_ref, ql_ref, qp_ref, cn_ref, kn_ref, ckv_hbm, kpe_hbm, o_ref,
                       ckv_buf, kpe_buf, cb_ref, s_ref, p_ref, sem, *, n_pages, ps, chunk):
    b = pl.program_id(0)
    slot = b % 2

    def page_copies(bi, sl):
        out = []
        for pg in range(n_pages):
            page = pt_ref[bi * n_pages + pg]
            rows = pl.ds(pg * ps, ps)
            out.append(pltpu.make_async_copy(ckv_hbm.at[page], ckv_buf.at[sl, rows, :], sem.at[0, sl]))
            out.append(pltpu.make_async_copy(kpe_hbm.at[page], kpe_buf.at[sl, :, rows], sem.at[1, sl]))
        return out

    @pl.when(b == 0)
    def _():
        for cp in page_copies(0, 0):
            cp.start()

    @pl.when(b + 1 < pl.num_programs(0))
    def _():
        for cp in page_copies(b + 1, 1 - slot):
            cp.start()

    for cp in page_copies(b, slot):
        cp.wait()

    ql = ql_ref[...]
    qp = qp_ref[...]
    n_chunks = n_pages * ps // chunk
    for c in range(n_chunks):
        cs = slice(c * chunk, (c + 1) * chunk)
        ck = ckv_buf[slot, cs, :].astype(BF16)
        cb_ref[cs, :] = ck
        s_ref[:, cs] = (lax.dot_general(ql, ck, NT_DIMS, preferred_element_type=F32)
                        + jnp.dot(qp, kpe_buf[slot, :, cs].astype(BF16), preferred_element_type=F32))

    cn = cn_ref[...].astype(BF16).astype(F32)
    kn = kn_ref[...].astype(BF16).astype(F32)
    s_new = (jnp.sum(ql.astype(F32) * cn, axis=1, keepdims=True)
             + jnp.sum(qp.astype(F32) * kn, axis=1, keepdims=True))
    s = s_ref[...]
    m = jnp.maximum(jnp.max(s, axis=1, keepdims=True), s_new)
    p = jnp.exp2(s - m)
    p_new = jnp.exp2(s_new - m)
    l = jnp.sum(p, axis=1, keepdims=True) + p_new
    p_ref[...] = p.astype(BF16)
    acc = p_new.astype(BF16).astype(F32) * cn
    for c in range(n_chunks):
        cs = slice(c * chunk, (c + 1) * chunk)
        acc += jnp.dot(p_ref[:, cs], cb_ref[cs, :], preferred_element_type=F32)
    o_ref[...] = acc / l


def paged_attention(q_lat, q_pe, ckv_new, kpe_new, cache_ckv, cache_kpe_t, page_table):
    bsz, nh, c = q_lat.shape
    r = q_pe.shape[-1]
    n_pages = page_table.shape[1]
    ps = cache_ckv.shape[1]
    past = n_pages * ps
    chunk = min(PAGED_CHUNK, past)
    per_b = lambda rows, w: pl.BlockSpec((None, rows, w), lambda b, pt: (b, 0, 0))
    return pl.pallas_call(
        functools.partial(_paged_attn_kernel, n_pages=n_pages, ps=ps, chunk=chunk),
        grid_spec=pltpu.PrefetchScalarGridSpec(
            num_scalar_prefetch=1,
            grid=(bsz,),
            in_specs=[per_b(nh, c), per_b(nh, r), per_b(1, c), per_b(1, r),
                      pl.BlockSpec(memory_space=pl.ANY), pl.BlockSpec(memory_space=pl.ANY)],
            out_specs=per_b(nh, c),
            scratch_shapes=[pltpu.VMEM((2, past, c), F32), pltpu.VMEM((2, r, past), F32),
                            pltpu.VMEM((past, c), BF16), pltpu.VMEM((nh, past), F32),
                            pltpu.VMEM((nh, past), BF16), pltpu.SemaphoreType.DMA((2, 2))]),
        out_shape=jax.ShapeDtypeStruct((bsz, nh, c), F32),
        compiler_params=_params("arbitrary"),
        name="paged_attention",
    )(page_table.reshape(-1), q_lat, q_pe, ckv_new, kpe_new, cache_ckv, cache_kpe_t)


def _rope_tables(pos, rope, lo, scale, passthrough):
    half = rope // 2
    freq = ROPE_THETA ** (-jnp.arange(half, dtype=F32) / half)
    ang = pos.astype(F32)[:, None] * freq[None, :]
    cos, sin = jnp.cos(ang), jnp.sin(ang)
    n = pos.shape[0]
    zeros = lambda w: jnp.zeros((n, w), F32)
    tail = LANES - lo - rope
    c = jnp.concatenate([jnp.full((n, lo), passthrough, F32), cos, cos, zeros(tail)], axis=1)
    sa = jnp.concatenate([zeros(lo), -sin, zeros(half + tail)], axis=1)
    sb = jnp.concatenate([zeros(lo + half), sin, zeros(tail)], axis=1)
    return c * scale, sa * scale, sb * scale


def _prep_weights(w_up, w_down, w_m_in, w_m_out, w_dkv, w_uk, w_uv, w_dq, w_uq, w_o, dims):
    heads, dk, dv = dims["m_heads"], dims["m_dk"], dims["m_dv"]
    nh, nope, rope, kv_lora = dims["mla_heads"], dims["qk_nope"], dims["qk_rope"], dims["kv_lora"]
    ff = w_down.shape[1]
    qd, vd = heads * dk, heads * dv
    d = w_up.shape[1]
    pw = {}
    pw["w_up_a"] = w_up[:, :, :ff].astype(BF16)
    pw["w_up_g"] = w_up[:, :, ff:].astype(BF16)
    pw["w_down"] = w_down.astype(BF16)
    pw["w_m_q"] = w_m_in[:, :, :qd].astype(BF16)
    pw["w_m_k"] = w_m_in[:, :, qd:2 * qd].astype(BF16)
    pw["w_m_v"] = w_m_in[:, :, 2 * qd:2 * qd + vd].astype(BF16)
    pw["w_m_o"] = w_m_in[:, :, 2 * qd + vd:2 * qd + 2 * vd].astype(BF16)
    gates = w_m_in[:, :, 2 * qd + 2 * vd:]
    pw["w_m_g"] = jnp.pad(gates, ((0, 0), (0, 0), (0, LANES - gates.shape[-1]))).astype(BF16)
    pw["w_m_out"] = w_m_out.astype(BF16)
    pw["w_dkv"] = jnp.pad(w_dkv, ((0, 0), (0, LANES - rope))).astype(BF16)
    pw["w_uk_r"] = jnp.pad(w_uk, ((0, 0), (0, 0), (0, LANES - nope))).reshape(kv_lora, nh * LANES).astype(BF16)
    v_head = w_uv.shape[2]
    pw["w_uv_r"] = jnp.pad(w_uv, ((0, 0), (0, 0), (0, LANES - v_head))).reshape(kv_lora, nh * LANES).astype(BF16)
    nb = w_uq.shape[0]
    wq = w_uq.reshape(nb, w_uq.shape[1], nh, nope + rope)
    pw["w_uq_r"] = jnp.pad(wq, ((0, 0), (0, 0), (0, 0), (0, LANES - nope - rope))).reshape(
        nb, w_uq.shape[1], nh * LANES).astype(BF16)
    pw["w_dq"] = w_dq.astype(BF16)
    pw["w_o"] = w_o.astype(BF16)
    wukt = jnp.transpose(w_uk, (1, 2, 0))
    pw["w_uk_t"] = jnp.pad(wukt, ((0, 0), (0, LANES - nope), (0, 0))).astype(BF16)
    pw["w_uv_t"] = jnp.transpose(w_uv, (1, 0, 2)).astype(BF16)
    return pw


def _trunk(x, mods, mods_kv, pos, conv_bufs, m_states, kv_past, pw, small, dims):
    heads, dk, dv = dims["m_heads"], dims["m_dk"], dims["m_dv"]
    nh, nope, rope, kv_lora, v_head = (dims["mla_heads"], dims["qk_nope"], dims["qk_rope"],
                                       dims["kv_lora"], dims["v_head"])
    depth, n_a = dims["depth"], dims["n_a"]
    is_prompt = kv_past is None
    gq, r, d = x.shape
    att_scale = (nope + rope) ** -0.5 * LOG2_E
    new_c, new_n, new_m, new_conv = [], [], [], []
    ckv = kpe = kcat = vv = None
    y = None
    for layer in range(depth):
        sh1, sc1, gt1, sh2, sc2, gt2 = mods[layer]
        g1 = small["g_norm1"][layer][None, :]
        g2 = small["g_norm2"][layer][None, :]
        if layer == n_a:
            sh_kv, sc_kv = mods_kv
            tabs = _rope_tables(pos, rope, 0, 1.0, 0.0)
            if is_prompt:
                ckv, kpe, kcat, vv = shared_latent(x, small["g_kv_in"][None, :], sh_kv, sc_kv, pw["w_dkv"],
                                                   small["g_kv"][None, :], tabs, kv_lora, rope,
                                                   pw["w_uk_r"], pw["w_uv_r"], v_head)
            else:
                ckv, kpe = shared_latent(x, small["g_kv_in"][None, :], sh_kv, sc_kv, pw["w_dkv"],
                                         small["g_kv"][None, :], tabs, kv_lora, rope)
        if layer < n_a:
            ws = [pw["w_m_q"][layer], pw["w_m_k"][layer], pw["w_m_v"][layer], pw["w_m_o"][layer], pw["w_m_g"][layer]]
            if is_prompt:
                q, k, v, o, gates = norm_mod_proj(x, g1, sh1, sc1, ws, [BF16, F32, BF16, F32, F32],
                                                  [dk ** -0.5, 1.0, 1.0, 1.0, 1.0], "mlstm_in_proj")
                hh, c_st, n_st, m_st = mlstm_chunkwise(q, k, v, o, gates[..., :2 * heads],
                                                       small["b_m_gates"][layer], small["g_m_head"][layer],
                                                       heads, dk, dv)
                n_st = n_st.reshape(gq, heads, dk)
                m_st = m_st.reshape(gq, heads)
            else:
                q, k, v, o, gates = norm_mod_proj(x, g1, sh1, sc1, ws, [F32] * 5,
                                                  [dk ** -0.5, 1.0, 1.0, 1.0, 1.0], "mlstm_in_proj_tok")
                hh, c_st, n_st, m_st = mlstm_step(q[0], k[0], v[0], o[0], gates[0, :, :2 * heads],
                                                  small["b_m_gates"][layer], small["g_m_head"][layer],
                                                  m_states[0][layer], m_states[1][layer], m_states[2][layer],
                                                  heads, dk, dv)
                hh = hh[None]
            new_c.append(c_st)
            new_n.append(n_st)
            new_m.append(m_st)
            x = resid_proj(hh, pw["w_m_out"][layer], x, gt1, "mlstm_out_proj")
        else:
            j = layer - n_a
            qtabs = _rope_tables(pos, rope, nope, att_scale, 1.0)
            qh = mla_queries(x, g1, sh1, sc1, pw["w_dq"][j], small["g_q"][j][None, :], pw["w_uq_r"][j], qtabs)
            if is_prompt:
                att = prompt_attention(qh, kcat, vv, v_head)
            else:
                bsz = r
                q2 = qh[0]
                q_lat = head_proj_lanes(q2, pw["w_uk_t"], BF16, "absorb_q")
                q_lat = jnp.swapaxes(q_lat, 0, 1)
                q_pe = q2.reshape(bsz, nh, LANES)[:, :, nope:nope + rope]
                o_lat = paged_attention(q_lat, q_pe, ckv[0][:, None, :], kpe[0][:, None, :],
                                        kv_past[0], kv_past[1], kv_past[2])
                o_lat = o_lat.reshape(bsz, nh * kv_lora)
                att = head_proj_lanes(o_lat, pw["w_uv_t"], F32, "unabsorb_o")
                att = jnp.swapaxes(att, 0, 1).reshape(1, bsz, nh * v_head)
            x = resid_proj(att, pw["w_o"][j], x, gt1, "mla_out_proj")
        final = layer == depth - 1
        gf = small["g_final"][None, :]
        if is_prompt:
            x, a_last = conv_ffn_seq(x, conv_bufs[layer], g2, sh2, sc2, gt2, pw["w_up_a"][layer], pw["w_up_g"][layer],
                                     small["w_conv"][layer], small["b_conv"][layer][None, :], pw["w_down"][layer],
                                     gf, final, "conv_ffn_seq")
            new_conv.append(a_last[:, 6:8, :])
        else:
            buf = conv_bufs[layer]
            x2, a_new = conv_ffn_tok(x[0], buf[:, 0, :], buf[:, 1, :], g2, sh2[0], sc2[0], gt2[0],
                                     pw["w_up_a"][layer], pw["w_up_g"][layer], small["w_conv"][layer],
                                     small["b_conv"][layer][None, :], pw["w_down"][layer], gf, final, "conv_ffn_tok")
            x = x2[None]
            new_conv.append(jnp.stack([buf[:, 1, :], a_new], axis=1))
    return x, jnp.stack(new_c), jnp.stack(new_n), jnp.stack(new_m), jnp.stack(new_conv), ckv, kpe


def kernel(x_prompt, x_sample, state_mlstm_C, state_mlstm_n, state_mlstm_m, state_conv, cache_ckv, cache_kpe,
           page_table, c_prompt, c_sample, g_norm1, g_norm2, w_ada, b_ada, w_up, w_conv, b_conv, w_down,
           w_m_in, b_m_gates, g_m_head, w_m_out, g_kv_in, w_ada_kv, b_ada_kv, w_dkv, g_kv, w_uk, w_uv,
           w_dq, g_q, w_uq, w_o, g_final):
    bp, s, d = x_prompt.shape
    bs, t, _ = x_sample.shape
    assert t == 1, "the sample path handles one new token per sequence"
    depth = w_ada.shape[0]
    n_a = w_m_in.shape[0]
    heads, dv = g_m_head.shape[1], g_m_head.shape[2]
    dk = state_mlstm_C.shape[3]
    kv_lora, nh, nope = w_uk.shape
    v_head = w_uv.shape[2]
    rope = w_dkv.shape[1] - kv_lora
    ff = w_down.shape[1]
    dims = dict(m_heads=heads, m_dk=dk, m_dv=dv, mla_heads=nh, qk_nope=nope, qk_rope=rope, kv_lora=kv_lora,
                v_head=v_head, depth=depth, n_a=n_a)
    past_len = page_table.shape[1] * cache_ckv.shape[1]

    pw = _prep_weights(w_up, w_down, w_m_in, w_m_out, w_dkv, w_uk, w_uv, w_dq, w_uq, w_o, dims)
    small = dict(g_norm1=g_norm1, g_norm2=g_norm2, w_conv=w_conv, b_conv=b_conv, b_m_gates=b_m_gates,
                 g_m_head=g_m_head, g_kv_in=g_kv_in, g_kv=g_kv, g_q=g_q, g_final=g_final)

    c_all = jnp.concatenate([c_prompt, c_sample], axis=0)
    mod = ada_mod(c_all, w_ada, b_ada[:, None, :])
    mod_kv = ada_mod(c_all, w_ada_kv[None], b_ada_kv[None, None, :])[0]

    def split(m, n, lo, hi, per_row):
        parts = jnp.split(m[lo:hi], n, axis=-1)
        return [p[None] if per_row else p[:, None, :] for p in parts]

    mods_p = [split(mod[l], 6, 0, bp, False) for l in range(depth)]
    mods_s = [split(mod[l], 6, bp, bp + bs, True) for l in range(depth)]
    kv_p = split(mod_kv, 2, 0, bp, False)
    kv_s = split(mod_kv, 2, bp, bp + bs, True)

    conv0 = [jnp.zeros((bp, 8, ff), F32)] * depth
    y_p, c_p, n_p, m_p, conv_p, ckv_p, kpe_p = _trunk(
        x_prompt, mods_p, kv_p, jnp.arange(s), conv0, None, None, pw, small, dims)

    pos_s = jnp.full((bs,), past_len, jnp.int32)
    y_s, c_s, n_s, m_s, conv_s, ckv_s, kpe_s = _trunk(
        x_sample.reshape(1, bs, d), mods_s, kv_s, pos_s, state_conv,
        (state_mlstm_C, state_mlstm_n, state_mlstm_m), (cache_ckv, jnp.swapaxes(cache_kpe, 1, 2), page_table), pw, small, dims)

    return (y_p, y_s.reshape(bs, 1, d), c_p, n_p, m_p, conv_p, ckv_p, kpe_p,
            c_s, n_s, m_s, conv_s, ckv_s.reshape(bs, 1, kv_lora), kpe_s.reshape(bs, 1, rope))
```

```python
import functools
import math

import jax
import jax.numpy as jnp
from jax import lax
from jax.experimental import pallas as pl
from jax.experimental.pallas import tpu as pltpu

F32 = jnp.float32
BF16 = jnp.bfloat16

NORM_EPS = 1e-6
ROPE_THETA = 10000.0
LANES = 128
VMEM_LIMIT = 56 * 1024 * 1024

ROW_TILE = 512
FFN_ROW_TILE = 512
MLSTM_CHUNK = 256
ATTN_TILE = 1024
ATTN_BLOCK = 512
FFN_CHUNK = 256
PAGED_CHUNK = 512
LOG2_E = math.log2(math.e)
SOFTMAX_MIN_ROW_SUM = 2.0 ** -60
STEP_BATCH = 8

NT_DIMS = (((1,), (1,)), ((), ()))
TN_DIMS = (((0,), (0,)), ((), ()))


def _params(*sem):
    return pltpu.CompilerParams(dimension_semantics=sem, vmem_limit_bytes=VMEM_LIMIT)


def _const_spec(shape):
    nd = len(shape)
    return pl.BlockSpec(shape, lambda *_: (0,) * nd, pipeline_mode=pl.Buffered(1))


def _rms(x, g):
    return x * lax.rsqrt(jnp.mean(x * x, axis=-1, keepdims=True) + NORM_EPS) * g


def _norm_mod(x, g, sh, sc):
    return _rms(x, g) * (1.0 + sc) + sh


def _log_sigmoid(x):
    return jnp.minimum(x, 0.0) - jnp.log(1.0 + jnp.exp(-jnp.abs(x)))


def _rope3(t, c, sa, sb):
    return t * c + pltpu.roll(t, LANES - 16, axis=1) * sa + pltpu.roll(t, 16, axis=1) * sb


def _row_spec(tm, width):
    return pl.BlockSpec((None, tm, width), lambda g, i: (g, i, 0))


def _mod_spec(per_row, tm, width):
    if per_row:
        return pl.BlockSpec((None, tm, width), lambda g, i: (g, i, 0))
    return pl.BlockSpec((None, 1, width), lambda g, i: (g, 0, 0))


def _ada_kernel(c_ref, w_ref, b_ref, o_ref):
    c = c_ref[...]
    a = (c * jax.nn.sigmoid(c)).astype(BF16)
    o_ref[...] = jnp.dot(a, w_ref[...].astype(BF16), preferred_element_type=F32) + b_ref[...]


def ada_mod(c, w, b, tn=1024):
    m, d = c.shape
    nl, _, n = w.shape
    tn = min(tn, n)
    return pl.pallas_call(
        _ada_kernel,
        grid=(nl, n // tn),
        in_specs=[pl.BlockSpec((m, d), lambda l, j: (0, 0)),
                  pl.BlockSpec((None, d, tn), lambda l, j: (l, 0, j)),
                  pl.BlockSpec((None, 1, tn), lambda l, j: (l, 0, j))],
        out_specs=pl.BlockSpec((None, m, tn), lambda l, j: (l, 0, j)),
        out_shape=jax.ShapeDtypeStruct((nl, m, n), F32),
        compiler_params=_params("parallel", "parallel"),
        name="ada_mod",
    )(c, w, b)


def _proj_kernel(x_ref, g_ref, sh_ref, sc_ref, *refs, scales):
    n = len(scales)
    hn = _norm_mod(x_ref[...], g_ref[...], sh_ref[...], sc_ref[...]).astype(BF16)
    for w_ref, o_ref, s in zip(refs[:n], refs[n:], scales):
        acc = jnp.dot(hn, w_ref[...], preferred_element_type=F32)
        if s != 1.0:
            acc = acc * s
        o_ref[...] = acc.astype(o_ref.dtype)


def norm_mod_proj(x, g, sh, sc, ws, out_dtypes, scales, name):
    gq, r, d = x.shape
    tm = min(ROW_TILE, r)
    per_row = sh.shape[1] != 1
    in_specs = [_row_spec(tm, d), _const_spec((1, d)),
                _mod_spec(per_row, tm, d), _mod_spec(per_row, tm, d)]
    in_specs += [_const_spec(w.shape) for w in ws]
    return pl.pallas_call(
        functools.partial(_proj_kernel, scales=tuple(scales)),
        grid=(gq, r // tm),
        in_specs=in_specs,
        out_specs=[_row_spec(tm, w.shape[1]) for w in ws],
        out_shape=[jax.ShapeDtypeStruct((gq, r, w.shape[1]), dt) for w, dt in zip(ws, out_dtypes)],
        compiler_params=_params("parallel", "parallel"),
        name=name,
    )(x, g, sh, sc, *ws)


def _resid_kernel(a_ref, w_ref, x_ref, gt_ref, o_ref):
    mix = jnp.dot(a_ref[...].astype(BF16), w_ref[...], preferred_element_type=F32)
    o_ref[...] = x_ref[...] + gt_ref[...] * mix


def resid_proj(a, w, x, gt, name):
    gq, r, d = x.shape
    k = a.shape[-1]
    tm = min(ROW_TILE, r)
    per_row = gt.shape[1] != 1
    return pl.pallas_call(
        _resid_kernel,
        grid=(gq, r // tm),
        in_specs=[_row_spec(tm, k), _const_spec(w.shape), _row_spec(tm, d), _mod_spec(per_row, tm, d)],
        out_specs=_row_spec(tm, d),
        out_shape=jax.ShapeDtypeStruct((gq, r, d), F32),
        compiler_params=_params("parallel", "parallel"),
        name=name,
    )(a, w, x, gt)


def _mlstm_chunk_kernel(q_ref, k_ref, v_ref, o_ref, gr_ref, gc_ref, bgc_ref, bgr_ref, gh_ref,
                        hh_ref, c_ref, n_ref, m_ref, *, heads, dk, dv):
    ci = pl.program_id(1)
    chunk = q_ref.shape[0]

    @pl.when(ci == 0)
    def _():
        c_ref[...] = jnp.zeros_like(c_ref)
        n_ref[...] = jnp.zeros_like(n_ref)
        m_ref[...] = jnp.zeros_like(m_ref)

    gates_r = gr_ref[...] + bgc_ref[...]
    gates_c = gc_ref[...] + bgr_ref[...]
    row = lax.broadcasted_iota(jnp.int32, (chunk, chunk), 0)
    col = lax.broadcasted_iota(jnp.int32, (chunk, chunk), 1)
    causal = col <= row

    s_raw, q_c = [], []
    for h in range(heads):
        q = q_ref[:, h * dk:(h + 1) * dk]
        s_raw.append(lax.dot_general(q, k_ref[:, h * dk:(h + 1) * dk].astype(BF16), NT_DIMS,
                                     preferred_element_type=F32))
        q_c.append(jnp.dot(q, c_ref[h].astype(BF16), preferred_element_type=F32))

    for h in range(heads):
        q = q_ref[:, h * dk:(h + 1) * dk]
        k = k_ref[:, h * dk:(h + 1) * dk]
        v = v_ref[:, h * dv:(h + 1) * dv]
        li_r = gates_r[h:h + 1, :]
        lf_r = _log_sigmoid(gates_r[heads + h:heads + h + 1, :])
        li_c = gates_c[:, h:h + 1]
        lf_c = _log_sigmoid(gates_c[:, heads + h:heads + h + 1])

        lf_low = jnp.where(causal, lf_r, 0.0)
        half = chunk // 2
        b_c = jnp.sum(lf_low[:, :half] + lf_low[:, half:], axis=1, keepdims=True)
        b_r = jnp.sum(jnp.where(row <= col, lf_c, 0.0), axis=0, keepdims=True)
        g = jnp.sum(lf_r, axis=1, keepdims=True)

        m_prev = m_ref[h]
        c_prev = c_ref[h]
        n_prev = n_ref[h]

        a_c = g - b_c + li_c
        m_loc = jnp.max(a_c, axis=0, keepdims=True)
        kw = k * jnp.exp(a_c - m_loc)
        c_loc = lax.dot_general(kw.astype(BF16), v, TN_DIMS, preferred_element_type=F32)
        n_loc = jnp.sum(kw, axis=0, keepdims=True)

        dmat = jnp.where(causal, b_c - b_r + li_r, -jnp.inf)
        w0 = b_c + m_prev
        m_s = jnp.maximum(w0, jnp.max(dmat, axis=1, keepdims=True))
        w_inter = jnp.exp(w0 - m_s)
        s = s_raw[h] * jnp.exp(dmat - m_s)
        num = w_inter * q_c[h] + jnp.dot(s.astype(BF16), v, preferred_element_type=F32)
        den = (w_inter * jnp.sum(q.astype(F32) * n_prev, axis=1, keepdims=True)
               + jnp.sum(s, axis=1, keepdims=True))
        hval = num / jnp.maximum(jnp.abs(den), jnp.exp(-m_s))
        hn = _rms(hval, gh_ref[h:h + 1, :])
        gate = jax.nn.sigmoid(o_ref[:, h * dv:(h + 1) * dv])
        hh_ref[:, h * dv:(h + 1) * dv] = (hn * gate).astype(hh_ref.dtype)

        m_new = jnp.maximum(g + m_prev, m_loc)
        fw = jnp.exp(g + m_prev - m_new)
        lw = jnp.exp(m_loc - m_new)
        c_ref[h] = fw * c_prev + lw * c_loc
        n_ref[h] = fw * n_prev + lw * n_loc
        m_ref[h] = m_new


def mlstm_chunkwise(q, k, v, o, gates, b_gates, g_head, heads, dk, dv):
    bsz, s, _ = q.shape
    chunk = min(MLSTM_CHUNK, s)
    gates_r = jnp.swapaxes(gates, 1, 2)
    g2 = 2 * heads
    return pl.pallas_call(
        functools.partial(_mlstm_chunk_kernel, heads=heads, dk=dk, dv=dv),
        grid=(bsz, s // chunk),
        in_specs=[_row_spec(chunk, heads * dk), _row_spec(chunk, heads * dk),
                  _row_spec(chunk, heads * dv), _row_spec(chunk, heads * dv),
                  pl.BlockSpec((None, g2, chunk), lambda b, c: (b, 0, c)),
                  _row_spec(chunk, g2),
                  _const_spec((g2, 1)), _const_spec((1, g2)), _const_spec((heads, dv))],
        out_specs=[_row_spec(chunk, heads * dv),
                   pl.BlockSpec((None, heads, dk, dv), lambda b, c: (b, 0, 0, 0)),
                   pl.BlockSpec((None, heads, 1, dk), lambda b, c: (b, 0, 0, 0)),
                   pl.BlockSpec((None, heads, 1, 1), lambda b, c: (b, 0, 0, 0))],
        out_shape=[jax.ShapeDtypeStruct((bsz, s, heads * dv), BF16),
                   jax.ShapeDtypeStruct((bsz, heads, dk, dv), F32),
                   jax.ShapeDtypeStruct((bsz, heads, 1, dk), F32),
                   jax.ShapeDtypeStruct((bsz, heads, 1, 1), F32)],
        compiler_params=_params("parallel", "arbitrary"),
        name="mlstm_chunkwise",
    )(q, k, v, o, gates_r, gates, b_gates.reshape(g2, 1), b_gates.reshape(1, g2), g_head)


def _mlstm_step_kernel(q_ref, k_ref, v_ref, o_ref, g_ref, bg_ref, gh_ref, c_ref, n_ref, m_ref,
                       hh_ref, co_ref, no_ref, mo_ref, *, heads, dk, dv):
    nb = q_ref.shape[0]
    gates = g_ref[...] + bg_ref[...]
    li = gates[:, :heads]
    lf = _log_sigmoid(gates[:, heads:])
    m_st = m_ref[...]
    m_new = jnp.maximum(lf + m_st, li)
    fw_all = jnp.exp(lf + m_st - m_new)
    iw_all = jnp.exp(li - m_new)
    floor_all = jnp.exp(-m_new)
    mo_ref[...] = m_new
    eye = lax.broadcasted_iota(jnp.int32, (dk, dk), 0) == lax.broadcasted_iota(jnp.int32, (dk, dk), 1)

    def to_col(r):
        return jnp.sum(jnp.where(eye, r, 0.0), axis=1, keepdims=True)

    hr = range(heads)
    for b in range(nb):
        q_r = [q_ref[b:b + 1, h * dk:(h + 1) * dk] for h in hr]
        k_r = [k_ref[b:b + 1, h * dk:(h + 1) * dk] for h in hr]
        k_c = [iw_all[b:b + 1, h:h + 1] * to_col(k_r[h]) for h in hr]
        c_new = [fw_all[b:b + 1, h:h + 1] * c_ref[b, h] + k_c[h] * v_ref[b:b + 1, h * dv:(h + 1) * dv] for h in hr]
        num = [jnp.dot(q_r[h].astype(BF16), c_new[h].astype(BF16), preferred_element_type=F32) for h in hr]
        n_new = [fw_all[b:b + 1, h:h + 1] * n_ref[b, h:h + 1, :] + iw_all[b:b + 1, h:h + 1] * k_r[h] for h in hr]
        den = [jnp.sum(q_r[h] * n_new[h], axis=1, keepdims=True) for h in hr]
        hval = [num[h] / jnp.maximum(jnp.abs(den[h]), floor_all[b:b + 1, h:h + 1]) for h in hr]
        hn = [_rms(hval[h], gh_ref[h:h + 1, :]) for h in hr]
        for h in hr:
            co_ref[b, h] = c_new[h]
            no_ref[b, h:h + 1, :] = n_new[h]
            hh_ref[b:b + 1, h * dv:(h + 1) * dv] = hn[h] * jax.nn.sigmoid(o_ref[b:b + 1, h * dv:(h + 1) * dv])


def mlstm_step(q, k, v, o, gates, b_gates, g_head, c_st, n_st, m_st, heads, dk, dv):
    bsz = q.shape[0]
    nb = min(STEP_BATCH, bsz)
    g2 = 2 * heads
    rows = lambda w: pl.BlockSpec((nb, w), lambda i: (i, 0))
    return pl.pallas_call(
        functools.partial(_mlstm_step_kernel, heads=heads, dk=dk, dv=dv),
        grid=(bsz // nb,),
        in_specs=[rows(heads * dk), rows(heads * dk), rows(heads * dv), rows(heads * dv), rows(g2),
                  _const_spec((1, g2)), _const_spec((heads, dv)),
                  pl.BlockSpec((nb, heads, dk, dv), lambda i: (i, 0, 0, 0)),
                  pl.BlockSpec((nb, heads, dk), lambda i: (i, 0, 0)),
                  rows(heads)],
        out_specs=[rows(heads * dv),
                   pl.BlockSpec((nb, heads, dk, dv), lambda i: (i, 0, 0, 0)),
                   pl.BlockSpec((nb, heads, dk), lambda i: (i, 0, 0)),
                   rows(heads)],
        out_shape=[jax.ShapeDtypeStruct((bsz, heads * dv), F32),
                   jax.ShapeDtypeStruct((bsz, heads, dk, dv), F32),
                   jax.ShapeDtypeStruct((bsz, heads, dk), F32),
                   jax.ShapeDtypeStruct((bsz, heads), F32)],
        compiler_params=_params("parallel"),
        name="mlstm_step",
    )(q, k, v, o, gates, b_gates.reshape(1, g2), g_head, c_st, n_st, m_st)


def _ffn_seq_kernel(x_ref, xh_ref, buf_ref, g_ref, sh_ref, sc_ref, gt_ref, wa_ref, wg_ref, wc_ref,
                    bc_ref, wd_ref, gf_ref, o_ref, alast_ref, *, fc, final_norm):
    i = pl.program_id(1)
    tm = x_ref.shape[0]
    ff = wa_ref.shape[1]
    x = x_ref[...]
    g, sh, sc = g_ref[...], sh_ref[...], sc_ref[...]
    hn = _norm_mod(x, g, sh, sc).astype(BF16)
    hh = _norm_mod(xh_ref[...], g, sh, sc).astype(BF16)
    first = i == 0

    def up(c):
        cs = slice(c * fc, (c + 1) * fc)
        wa = wa_ref[:, cs]
        return (jnp.dot(hn, wa, preferred_element_type=F32),
                jnp.dot(hn, wg_ref[:, cs], preferred_element_type=F32),
                jnp.dot(hh, wa, preferred_element_type=F32))

    n_chunks = ff // fc
    acc = None
    nxt = up(0)
    for c in range(n_chunks):
        cs = slice(c * fc, (c + 1) * fc)
        a, gt, a_halo = nxt
        if c + 1 < n_chunks:
            nxt = up(c + 1)
        ext = jnp.concatenate([jnp.where(first, buf_ref[:, cs], a_halo), a], axis=0)
        conv = (bc_ref[:, cs] + ext[6:6 + tm, :] * wc_ref[0:1, cs]
                + ext[7:7 + tm, :] * wc_ref[1:2, cs] + a * wc_ref[2:3, cs])
        act = (conv * jax.nn.sigmoid(conv) * gt).astype(BF16)
        part = jnp.dot(act, wd_ref[cs, :], preferred_element_type=F32)
        acc = part if acc is None else acc + part
        alast_ref[:, cs] = a[tm - 8:tm, :]

    y = x + gt_ref[...] * acc
    if final_norm:
        y = _rms(y, gf_ref[...])
    o_ref[...] = y


def conv_ffn_seq(x, buf8, g, sh, sc, gt, w_up_a, w_up_g, w_conv, b_conv, w_down, g_final, final_norm, name):
    bsz, s, d = x.shape
    ff = w_up_a.shape[1]
    tm = min(FFN_ROW_TILE, s)
    fc = min(FFN_CHUNK, ff)
    halo =lambda b, i: (b, jnp.maximum(i * (tm // 8) - 1, 0), 0)
    return pl.pallas_call(
        functools.partial(_ffn_seq_kernel, fc=fc, final_norm=final_norm),
        grid=(bsz, s // tm),
        in_specs=[_row_spec(tm, d), pl.BlockSpec((None, 8, d), halo),
                  pl.BlockSpec((None, 8, ff), lambda b, i: (b, 0, 0)),
                  _const_spec((1, d)), _mod_spec(False, tm, d), _mod_spec(False, tm, d), _mod_spec(False, tm, d),
                  _const_spec(w_up_a.shape), _const_spec(w_up_g.shape), _const_spec(w_conv.shape),
                  _const_spec(b_conv.shape), _const_spec(w_down.shape), _const_spec((1, d))],
        out_specs=[_row_spec(tm, d), pl.BlockSpec((None, 8, ff), lambda b, i: (b, 0, 0))],
        out_shape=[jax.ShapeDtypeStruct((bsz, s, d), F32), jax.ShapeDtypeStruct((bsz, 8, ff), F32)],
        compiler_params=_params("parallel", "arbitrary"),
        name=name,
    )(x, x, buf8, g, sh, sc, gt, w_up_a, w_up_g, w_conv, b_conv, w_down, g_final)


def _ffn_tok_kernel(x_ref, b0_ref, b1_ref, g_ref, sh_ref, sc_ref, gt_ref, wa_ref, wg_ref, wc_ref,
                    bc_ref, wd_ref, gf_ref, o_ref, a_ref, acc_ref, *, final_norm):
    c = pl.program_id(0)
    x = x_ref[...]
    hn = _norm_mod(x, g_ref[...], sh_ref[...], sc_ref[...]).astype(BF16)
    a = jnp.dot(hn, wa_ref[...], preferred_element_type=F32)
    gt = jnp.dot(hn, wg_ref[...], preferred_element_type=F32)
    a_ref[...] = a
    conv = bc_ref[...] + b0_ref[...] * wc_ref[0:1, :] + b1_ref[...] * wc_ref[1:2, :] + a * wc_ref[2:3, :]
    act = (conv * jax.nn.sigmoid(conv) * gt).astype(BF16)
    part = jnp.dot(act, wd_ref[...], preferred_element_type=F32)

    @pl.when(c == 0)
    def _():
        acc_ref[...] = part

    @pl.when(c > 0)
    def _():
        acc_ref[...] += part

    y = x + gt_ref[...] * acc_ref[...]
    if final_norm:
        y = _rms(y, gf_ref[...])
    o_ref[...] = y


def conv_ffn_tok(x, buf0, buf1, g, sh, sc, gt, w_up_a, w_up_g, w_conv, b_conv, w_down, g_final, final_norm, name):
    bsz, d = x.shape
    ff = w_up_a.shape[1]
    fc = min(FFN_CHUNK, ff)
    full = lambda w: pl.BlockSpec((bsz, w), lambda c: (0, 0))
    cols = lambda r: pl.BlockSpec((r, fc), lambda c: (0, c))
    return pl.pallas_call(
        functools.partial(_ffn_tok_kernel, final_norm=final_norm),
        grid=(ff // fc,),
        in_specs=[full(d), cols(bsz), cols(bsz), _const_spec((1, d)), full(d), full(d), full(d),
                  cols(d), cols(d), cols(w_conv.shape[0]), cols(1),
                  pl.BlockSpec((fc, d), lambda c: (c, 0)), _const_spec((1, d))],
        out_specs=[full(d), cols(bsz)],
        out_shape=[jax.ShapeDtypeStruct((bsz, d), F32), jax.ShapeDtypeStruct((bsz, ff), F32)],
        scratch_shapes=[pltpu.VMEM((bsz, d), F32)],
        compiler_params=_params("arbitrary"),
        name=name,
    )(x, buf0, buf1, g, sh, sc, gt, w_up_a, w_up_g, w_conv, b_conv, w_down, g_final)


def _latent_kernel(x_ref, g_ref, sh_ref, sc_ref, w_ref, gkv_ref, rc_ref, ra_ref, rb_ref, *refs,
                   kv_lora, rope, with_kv):
    if with_kv:
        wuk_ref, wuv_ref, one_ref, ckv_ref, kpe_ref, kcat_ref, v_ref = refs
    else:
        ckv_ref, kpe_ref = refs
    hn = _norm_mod(x_ref[...], g_ref[...], sh_ref[...], sc_ref[...]).astype(BF16)
    lat = jnp.dot(hn, w_ref[...], preferred_element_type=F32)
    ckv = _rms(lat[:, :kv_lora], gkv_ref[...])
    ckv_ref[...] = ckv
    kpe = _rope3(lat[:, kv_lora:kv_lora + LANES], rc_ref[...], ra_ref[...], rb_ref[...])
    kpe_ref[...] = kpe[:, :rope]
    if with_kv:
        cb = ckv.astype(BF16)
        kn = jnp.dot(cb, wuk_ref[...], preferred_element_type=F32)
        kpe_hi = pltpu.roll(kpe, 64, axis=1)
        for h in range(kn.shape[1] // LANES):
            hs = slice(h * LANES, (h + 1) * LANES)
            kcat_ref[:, hs] = (kn[:, hs] + kpe_hi).astype(BF16)
        v_ref[...] = (jnp.dot(cb, wuv_ref[...], preferred_element_type=F32) + one_ref[...]).astype(BF16)


def shared_latent(x, g, sh, sc, w_dkv_p, g_kv, tabs, kv_lora, rope, w_uk_r=None, w_uv_r=None, v_head=None):
    gq, r, d = x.shape
    tm = min(ROW_TILE, r)
    per_row = sh.shape[1] != 1
    with_kv = w_uk_r is not None
    tab_spec = pl.BlockSpec((tm, LANES), lambda g_, i: (i, 0))
    in_specs = [_row_spec(tm, d), _const_spec((1, d)), _mod_spec(per_row, tm, d), _mod_spec(per_row, tm, d),
                _const_spec(w_dkv_p.shape), _const_spec((1, kv_lora)), tab_spec, tab_spec, tab_spec]
    out_specs = [_row_spec(tm, kv_lora), _row_spec(tm, rope)]
    out_shape = [jax.ShapeDtypeStruct((gq, r, kv_lora), F32), jax.ShapeDtypeStruct((gq, r, rope), F32)]
    args = [x, g, sh, sc, w_dkv_p, g_kv, *tabs]
    if with_kv:
        ones_lane = (jnp.arange(w_uv_r.shape[1]) % LANES == v_head).astype(F32)[None, :]
        in_specs += [_const_spec(w_uk_r.shape), _const_spec(w_uv_r.shape), _const_spec(ones_lane.shape)]
        out_specs += [_row_spec(tm, w_uk_r.shape[1]), _row_spec(tm, w_uv_r.shape[1])]
        out_shape += [jax.ShapeDtypeStruct((gq, r, w_uk_r.shape[1]), BF16),
                      jax.ShapeDtypeStruct((gq, r, w_uv_r.shape[1]), BF16)]
        args += [w_uk_r, w_uv_r, ones_lane]
    return pl.pallas_call(
        functools.partial(_latent_kernel, kv_lora=kv_lora, rope=rope, with_kv=with_kv),
        grid=(gq, r // tm),
        in_specs=in_specs, out_specs=out_specs, out_shape=out_shape,
        compiler_params=_params("parallel", "parallel"),
        name="shared_latent_kv" if with_kv else "shared_latent",
    )(*args)


def _query_kernel(x_ref, g_ref, sh_ref, sc_ref, wdq_ref, gq_ref, wuq_ref, rc_ref, ra_ref, rb_ref, q_ref):
    hn = _norm_mod(x_ref[...], g_ref[...], sh_ref[...], sc_ref[...]).astype(BF16)
    qd = jnp.dot(hn, wdq_ref[...], preferred_element_type=F32)
    qn = _rms(qd, gq_ref[...]).astype(BF16)
    qf = jnp.dot(qn, wuq_ref[...], preferred_element_type=F32)
    rc, ra, rb = rc_ref[...], ra_ref[...], rb_ref[...]
    for h in range(qf.shape[1] // LANES):
        hs = slice(h * LANES, (h + 1) * LANES)
        q_ref[:, hs] = _rope3(qf[:, hs], rc, ra, rb).astype(q_ref.dtype)


def mla_queries(x, g, sh, sc, w_dq, g_q, w_uq_r, tabs):
    gq, r, d = x.shape
    tm = min(ROW_TILE, r)
    per_row = sh.shape[1] != 1
    tab_spec = pl.BlockSpec((tm, LANES), lambda g_, i: (i, 0))
    return pl.pallas_call(
        _query_kernel,
        grid=(gq, r // tm),
        in_specs=[_row_spec(tm, d), _const_spec((1, d)), _mod_spec(per_row, tm, d), _mod_spec(per_row, tm, d),
                  _const_spec(w_dq.shape), _const_spec(g_q.shape), _const_spec(w_uq_r.shape),
                  tab_spec, tab_spec, tab_spec],
        out_specs=_row_spec(tm, w_uq_r.shape[1]),
        out_shape=jax.ShapeDtypeStruct((gq, r, w_uq_r.shape[1]), BF16),
        compiler_params=_params("parallel", "parallel"),
        name="mla_queries",
    )(x, g, sh, sc, w_dq, g_q, w_uq_r, *tabs)


def _attn_kernel(q_ref, k_ref, v_ref, o_ref, kn_ref, *, v_head, blk):
    qi = pl.program_id(2)
    tq = q_ref.shape[0]
    nsub = tq // blk
    n_full = qi * nsub
    base = pl.multiple_of(qi * tq, tq)
    row = lax.broadcasted_iota(jnp.int32, (blk, blk), 0)
    col = lax.broadcasted_iota(jnp.int32, (blk, blk), 1)
    chains = [(h, r) for h in range(2) for r in range(nsub)]
    diag = [(i, base + c * blk, c == r) for c in range(nsub) for i, (_, r) in enumerate(chains) if c <= r]

    def scores(h, r, kstart):
        hs = slice(h * LANES, (h + 1) * LANES)
        return lax.dot_general(q_ref[r * blk:(r + 1) * blk, hs], k_ref[pl.ds(kstart, blk), hs], NT_DIMS,
                               preferred_element_type=F32)

    def values(h, kstart):
        return v_ref[pl.ds(kstart, blk), h * LANES:(h + 1) * LANES]

    def run(tasks, carry, update):
        carry = list(carry)
        nxt = scores(*chains[tasks[0][0]], tasks[0][1])
        for t, (i, kstart, masked) in enumerate(tasks):
            s = nxt
            if t + 1 < len(tasks):
                nxt = scores(*chains[tasks[t + 1][0]], tasks[t + 1][1])
            if masked:
                s = jnp.where(col <= row, s, -jnp.inf)
            carry[i] = update(i, s, kstart, carry[i])
        return tuple(carry)

    def sweep(update, init):
        def full_chunk(kt, carry):
            kstart = pl.multiple_of(kt * blk, blk)
            return run([(i, kstart, False) for i in range(len(chains))], carry, update)
        return run(diag, lax.fori_loop(0, n_full, full_chunk, init), update)

    def store(accs):
        outs = []
        for h in range(2):
            acc = jnp.concatenate(accs[h * nsub:(h + 1) * nsub], axis=0)
            outs.append(acc / acc[:, v_head:v_head + 1])
        lane = lax.broadcasted_iota(jnp.int32, (tq, LANES), 1)
        o_ref[...] = jnp.where(lane < v_head, outs[0], pltpu.roll(outs[1], v_head, axis=1)).astype(o_ref.dtype)

    @pl.when(qi == 0)
    def _():
        kn_ref[...] = jnp.zeros_like(kn_ref)

    for h in range(2):
        kt = k_ref[pl.ds(base, tq), h * LANES:(h + 1) * LANES].astype(F32)
        ksq = jnp.max(jnp.sum(kt * kt, axis=1, keepdims=True), axis=0, keepdims=True)
        kn_ref[h] = jnp.maximum(kn_ref[h], ksq)

    bounds = []
    for h, r in chains:
        qf = q_ref[r * blk:(r + 1) * blk, h * LANES:(h + 1) * LANES].astype(F32)
        bounds.append(jnp.sqrt(jnp.sum(qf * qf, axis=1, keepdims=True) * kn_ref[h]))

    def update_bounded(i, s, kstart, acc):
        p = jnp.exp2(s - bounds[i]).astype(BF16)
        return acc + jnp.dot(p, values(chains[i][0], kstart), preferred_element_type=F32)

    accs = sweep(update_bounded, tuple(jnp.zeros((blk, LANES), F32) for _ in chains))
    store(accs)
    smallest = accs[0][:, v_head:v_head + 1]
    for acc in accs[1:]:
        smallest = jnp.minimum(smallest, acc[:, v_head:v_head + 1])
    row_sums_ok = jnp.min(smallest) >= SOFTMAX_MIN_ROW_SUM

    @pl.when(jnp.logical_not(row_sums_ok))
    def _():
        def update_online(i, s, kstart, state):
            m, acc = state
            m_new = jnp.maximum(m, jnp.max(s, axis=1, keepdims=True))
            p = jnp.exp2(s - m_new).astype(BF16)
            pv = jnp.dot(p, values(chains[i][0], kstart), preferred_element_type=F32)
            return m_new, jnp.exp2(m - m_new) * acc + pv

        init = tuple((jnp.full((blk, 1), -jnp.inf, F32), jnp.zeros((blk, LANES), F32)) for _ in chains)
        store([acc for _, acc in sweep(update_online, init)])


def prompt_attention(q, kcat, v1, v_head):
    bsz, s, hw = q.shape
    pairs = hw // (2 * LANES)
    tq = min(ATTN_TILE, s)
    blk = min(ATTN_BLOCK, tq)
    pair_rows = lambda rows, idx: pl.BlockSpec((None, rows, 2 * LANES), idx)
    return pl.pallas_call(
        functools.partial(_attn_kernel, v_head=v_head, blk=blk),
        grid=(bsz, pairs, s // tq),
        in_specs=[pair_rows(tq, lambda b, j, i: (b, i, j)),
                  pair_rows(s, lambda b, j, i: (b, 0, j)),
                  pair_rows(s, lambda b, j, i: (b, 0, j))],
        out_specs=pl.BlockSpec((None, tq, 2 * v_head), lambda b, j, i: (b, i, j)),
        out_shape=jax.ShapeDtypeStruct((bsz, s, pairs * 2 * v_head), BF16),
        scratch_shapes=[pltpu.VMEM((2, 1, 1), F32)],
        compiler_params=_params("parallel", "parallel", "arbitrary"),
        name="prompt_attention",
    )(q, kcat, v1)


def _head_proj_kernel(a_ref, w_ref, o_ref):
    o_ref[...] = jnp.dot(a_ref[...].astype(BF16), w_ref[...], preferred_element_type=F32).astype(o_ref.dtype)


def head_proj_lanes(a, w, out_dtype, name):
    bsz = a.shape[0]
    nh, kk, n = w.shape
    return pl.pallas_call(
        _head_proj_kernel,
        grid=(nh,),
        in_specs=[pl.BlockSpec((bsz, kk), lambda h: (0, h)), pl.BlockSpec((None, kk, n), lambda h: (h, 0, 0))],
        out_specs=pl.BlockSpec((None, bsz, n), lambda h: (h, 0, 0)),
        out_shape=jax.ShapeDtypeStruct((nh, bsz, n), out_dtype),
        compiler_params=_params("parallel"),
        name=name,
    )(a, w)


def _paged_attn_kernel(pt_ref, ql_ref, qp_ref, cn_ref, kn_ref, ckv_hbm, kpe_hbm, o_ref,
                       ckv_buf, kpe_buf, cb_ref, s_ref, p_ref, sem, *, n_pages, ps, chunk):
    b = pl.program_id(0)
    slot = b % 2

    def page_copies(bi, sl):
        out = []
        for pg in range(n_pages):
            page = pt_ref[bi * n_pages + pg]
            rows = pl.ds(pg * ps, ps)
            out.append(pltpu.make_async_copy(ckv_hbm.at[page], ckv_buf.at[sl, rows, :], sem.at[0, sl]))
            out.append(pltpu.make_async_copy(kpe_hbm.at[page], kpe_buf.at[sl, :, rows], sem.at[1, sl]))
        return out

    @pl.when(b == 0)
    def _():
        for cp in page_copies(0, 0):
            cp.start()

    @pl.when(b + 1 < pl.num_programs(0))
    def _():
        for cp in page_copies(b + 1, 1 - slot):
            cp.start()

    for cp in page_copies(b, slot):
        cp.wait()

    ql = ql_ref[...]
    qp = qp_ref[...]
    n_chunks = n_pages * ps // chunk
    chunks = [slice(c * chunk, (c + 1) * chunk) for c in range(n_chunks)]
    s_lat = []
    for cs in chunks:
        ck = ckv_buf[slot, cs, :].astype(BF16)
        cb_ref[cs, :] = ck
        s_lat.append(lax.dot_general(ql, ck, NT_DIMS, preferred_element_type=F32))
    for cs, sl in zip(chunks, s_lat):
        s_ref[:, cs] = sl + jnp.dot(qp, kpe_buf[slot, :, cs].astype(BF16), preferred_element_type=F32)

    cn = cn_ref[...].astype(BF16).astype(F32)
    kn = kn_ref[...].astype(BF16).astype(F32)
    s_new = (jnp.sum(ql.astype(F32) * cn, axis=1, keepdims=True)
             + jnp.sum(qp.astype(F32) * kn, axis=1, keepdims=True))
    s = s_ref[...]
    m = jnp.maximum(jnp.max(s, axis=1, keepdims=True), s_new)
    p = jnp.exp2(s - m)
    p_new = jnp.exp2(s_new - m)
    l = jnp.sum(p, axis=1, keepdims=True) + p_new
    p_ref[...] = p.astype(BF16)
    accs = [p_new.astype(BF16).astype(F32) * cn, jnp.zeros((ql.shape[0], cn.shape[1]), F32)]
    for c, cs in enumerate(chunks):
        accs[c % 2] += jnp.dot(p_ref[:, cs], cb_ref[cs, :], preferred_element_type=F32)
    o_ref[...] = (accs[0] + accs[1]) / l


def paged_attention(q_lat, q_pe, ckv_new, kpe_new, cache_ckv, cache_kpe_t, page_table):
    bsz, nh, c = q_lat.shape
    r = q_pe.shape[-1]
    n_pages = page_table.shape[1]
    ps = cache_ckv.shape[1]
    past = n_pages * ps
    chunk = min(PAGED_CHUNK, past)
    per_b = lambda rows, w: pl.BlockSpec((None, rows, w), lambda b, pt: (b, 0, 0))
    return pl.pallas_call(
        functools.partial(_paged_attn_kernel, n_pages=n_pages, ps=ps, chunk=chunk),
        grid_spec=pltpu.PrefetchScalarGridSpec(
            num_scalar_prefetch=1,
            grid=(bsz,),
            in_specs=[per_b(nh, c), per_b(nh, r), per_b(1, c), per_b(1, r),
                      pl.BlockSpec(memory_space=pl.ANY), pl.BlockSpec(memory_space=pl.ANY)],
            out_specs=per_b(nh, c),
            scratch_shapes=[pltpu.VMEM((2, past, c), F32), pltpu.VMEM((2, r, past), F32),
                            pltpu.VMEM((past, c), BF16), pltpu.VMEM((nh, past), F32),
                            pltpu.VMEM((nh, past), BF16), pltpu.SemaphoreType.DMA((2, 2))]),
        out_shape=jax.ShapeDtypeStruct((bsz, nh, c), F32),
        compiler_params=_params("arbitrary"),
        name="paged_attention",
    )(page_table.reshape(-1), q_lat, q_pe, ckv_new, kpe_new, cache_ckv, cache_kpe_t)


def _rope_tables(pos, rope, lo, scale, passthrough):
    half = rope // 2
    freq = ROPE_THETA ** (-jnp.arange(half, dtype=F32) / half)
    ang = pos.astype(F32)[:, None] * freq[None, :]
    cos, sin = jnp.cos(ang), jnp.sin(ang)
    n = pos.shape[0]
    zeros = lambda w: jnp.zeros((n, w), F32)
    tail = LANES - lo - rope
    c = jnp.concatenate([jnp.full((n, lo), passthrough, F32), cos, cos, zeros(tail)], axis=1)
    sa = jnp.concatenate([zeros(lo), -sin, zeros(half + tail)], axis=1)
    sb = jnp.concatenate([zeros(lo + half), sin, zeros(tail)], axis=1)
    return c * scale, sa * scale, sb * scale


def _prep_weights(w_up, w_down, w_m_in, w_m_out, w_dkv, w_uk, w_uv, w_dq, w_uq, w_o, dims):
    heads, dk, dv = dims["m_heads"], dims["m_dk"], dims["m_dv"]
    nh, nope, rope, kv_lora = dims["mla_heads"], dims["qk_nope"], dims["qk_rope"], dims["kv_lora"]
    ff = w_down.shape[1]
    qd, vd = heads * dk, heads * dv
    d = w_up.shape[1]
    pw = {}
    pw["w_up_a"] = w_up[:, :, :ff].astype(BF16)
    pw["w_up_g"] = w_up[:, :, ff:].astype(BF16)
    pw["w_down"] = w_down.astype(BF16)
    pw["w_m_q"] = w_m_in[:, :, :qd].astype(BF16)
    pw["w_m_k"] = w_m_in[:, :, qd:2 * qd].astype(BF16)
    pw["w_m_v"] = w_m_in[:, :, 2 * qd:2 * qd + vd].astype(BF16)
    pw["w_m_o"] = w_m_in[:, :, 2 * qd + vd:2 * qd + 2 * vd].astype(BF16)
    gates = w_m_in[:, :, 2 * qd + 2 * vd:]
    pw["w_m_g"] = jnp.pad(gates, ((0, 0), (0, 0), (0, LANES - gates.shape[-1]))).astype(BF16)
    pw["w_m_out"] = w_m_out.astype(BF16)
    pw["w_dkv"] = jnp.pad(w_dkv, ((0, 0), (0, LANES - rope))).astype(BF16)
    pw["w_uk_r"] = jnp.pad(w_uk, ((0, 0), (0, 0), (0, LANES - nope))).reshape(kv_lora, nh * LANES).astype(BF16)
    v_head = w_uv.shape[2]
    pw["w_uv_r"] = jnp.pad(w_uv, ((0, 0), (0, 0), (0, LANES - v_head))).reshape(kv_lora, nh * LANES).astype(BF16)
    nb = w_uq.shape[0]
    wq = w_uq.reshape(nb, w_uq.shape[1], nh, nope + rope)
    pw["w_uq_r"] = jnp.pad(wq, ((0, 0), (0, 0), (0, 0), (0, LANES - nope - rope))).reshape(
        nb, w_uq.shape[1], nh * LANES).astype(BF16)
    pw["w_dq"] = w_dq.astype(BF16)
    pw["w_o"] = w_o.astype(BF16)
    wukt = jnp.transpose(w_uk, (1, 2, 0))
    pw["w_uk_t"] = jnp.pad(wukt, ((0, 0), (0, LANES - nope), (0, 0))).astype(BF16)
    pw["w_uv_t"] = jnp.transpose(w_uv, (1, 0, 2)).astype(BF16)
    return pw


def _trunk(x, mods, mods_kv, pos, conv_bufs, m_states, kv_past, pw, small, dims):
    heads, dk, dv = dims["m_heads"], dims["m_dk"], dims["m_dv"]
    nh, nope, rope, kv_lora, v_head = (dims["mla_heads"], dims["qk_nope"], dims["qk_rope"],
                                       dims["kv_lora"], dims["v_head"])
    depth, n_a = dims["depth"], dims["n_a"]
    is_prompt = kv_past is None
    gq, r, d = x.shape
    att_scale = (nope + rope) ** -0.5 * LOG2_E
    new_c, new_n, new_m, new_conv = [], [], [], []
    ckv = kpe = kcat = vv = None
    y = None
    for layer in range(depth):
        sh1, sc1, gt1, sh2, sc2, gt2 = mods[layer]
        g1 = small["g_norm1"][layer][None, :]
        g2 = small["g_norm2"][layer][None, :]
        if layer == n_a:
            sh_kv, sc_kv = mods_kv
            tabs = _rope_tables(pos, rope, 0, 1.0, 0.0)
            if is_prompt:
                ckv, kpe, kcat, vv = shared_latent(x, small["g_kv_in"][None, :], sh_kv, sc_kv, pw["w_dkv"],
                                                   small["g_kv"][None, :], tabs, kv_lora, rope,
                                                   pw["w_uk_r"], pw["w_uv_r"], v_head)
            else:
                ckv, kpe = shared_latent(x, small["g_kv_in"][None, :], sh_kv, sc_kv, pw["w_dkv"],
                                         small["g_kv"][None, :], tabs, kv_lora, rope)
        if layer < n_a:
            ws = [pw["w_m_q"][layer], pw["w_m_k"][layer], pw["w_m_v"][layer], pw["w_m_o"][layer], pw["w_m_g"][layer]]
            if is_prompt:
                q, k, v, o, gates = norm_mod_proj(x, g1, sh1, sc1, ws, [BF16, F32, BF16, F32, F32],
                                                  [dk ** -0.5, 1.0, 1.0, 1.0, 1.0], "mlstm_in_proj")
                hh, c_st, n_st, m_st = mlstm_chunkwise(q, k, v, o, gates[..., :2 * heads],
                                                       small["b_m_gates"][layer], small["g_m_head"][layer],
                                                       heads, dk, dv)
                n_st = n_st.reshape(gq, heads, dk)
                m_st = m_st.reshape(gq, heads)
            else:
                q, k, v, o, gates = norm_mod_proj(x, g1, sh1, sc1, ws, [F32] * 5,
                                                  [dk ** -0.5, 1.0, 1.0, 1.0, 1.0], "mlstm_in_proj_tok")
                hh, c_st, n_st, m_st = mlstm_step(q[0], k[0], v[0], o[0], gates[0, :, :2 * heads],
                                                  small["b_m_gates"][layer], small["g_m_head"][layer],
                                                  m_states[0][layer], m_states[1][layer], m_states[2][layer],
                                                  heads, dk, dv)
                hh = hh[None]
            new_c.append(c_st)
            new_n.append(n_st)
            new_m.append(m_st)
            x = resid_proj(hh, pw["w_m_out"][layer], x, gt1, "mlstm_out_proj")
        else:
            j = layer - n_a
            qtabs = _rope_tables(pos, rope, nope, att_scale, 1.0)
            qh = mla_queries(x, g1, sh1, sc1, pw["w_dq"][j], small["g_q"][j][None, :], pw["w_uq_r"][j], qtabs)
            if is_prompt:
                att = prompt_attention(qh, kcat, vv, v_head)
            else:
                bsz = r
                q2 = qh[0]
                q_lat = head_proj_lanes(q2, pw["w_uk_t"], BF16, "absorb_q")
                q_lat = jnp.swapaxes(q_lat, 0, 1)
                q_pe = q2.reshape(bsz, nh, LANES)[:, :, nope:nope + rope]
                o_lat = paged_attention(q_lat, q_pe, ckv[0][:, None, :], kpe[0][:, None, :],
                                        kv_past[0], kv_past[1], kv_past[2])
                o_lat = o_lat.reshape(bsz, nh * kv_lora)
                att = head_proj_lanes(o_lat, pw["w_uv_t"], F32, "unabsorb_o")
                att = jnp.swapaxes(att, 0, 1).reshape(1, bsz, nh * v_head)
            x = resid_proj(att, pw["w_o"][j], x, gt1, "mla_out_proj")
        final = layer == depth - 1
        gf = small["g_final"][None, :]
        if is_prompt:
            x, a_last = conv_ffn_seq(x, conv_bufs[layer], g2, sh2, sc2, gt2, pw["w_up_a"][layer], pw["w_up_g"][layer],
                                     small["w_conv"][layer], small["b_conv"][layer][None, :], pw["w_down"][layer],
                                     gf, final, "conv_ffn_seq")
            new_conv.append(a_last[:, 6:8, :])
        else:
            buf = conv_bufs[layer]
            x2, a_new = conv_ffn_tok(x[0], buf[:, 0, :], buf[:, 1, :], g2, sh2[0], sc2[0], gt2[0],
                                     pw["w_up_a"][layer], pw["w_up_g"][layer], small["w_conv"][layer],
                                     small["b_conv"][layer][None, :], pw["w_down"][layer], gf, final, "conv_ffn_tok")
            x = x2[None]
            new_conv.append(jnp.stack([buf[:, 1, :], a_new], axis=1))
    return x, jnp.stack(new_c), jnp.stack(new_n), jnp.stack(new_m), jnp.stack(new_conv), ckv, kpe


def kernel(x_prompt, x_sample, state_mlstm_C, state_mlstm_n, state_mlstm_m, state_conv, cache_ckv, cache_kpe,
           page_table, c_prompt, c_sample, g_norm1, g_norm2, w_ada, b_ada, w_up, w_conv, b_conv, w_down,
           w_m_in, b_m_gates, g_m_head, w_m_out, g_kv_in, w_ada_kv, b_ada_kv, w_dkv, g_kv, w_uk, w_uv,
           w_dq, g_q, w_uq, w_o, g_final):
    bp, s, d = x_prompt.shape
    bs, t, _ = x_sample.shape
    assert t == 1, "the sample path handles one new token per sequence"
    depth = w_ada.shape[0]
    n_a = w_m_in.shape[0]
    heads, dv = g_m_head.shape[1], g_m_head.shape[2]
    dk = state_mlstm_C.shape[3]
    kv_lora, nh, nope = w_uk.shape
    v_head = w_uv.shape[2]
    rope = w_dkv.shape[1] - kv_lora
    ff = w_down.shape[1]
    dims = dict(m_heads=heads, m_dk=dk, m_dv=dv, mla_heads=nh, qk_nope=nope, qk_rope=rope, kv_lora=kv_lora,
                v_head=v_head, depth=depth, n_a=n_a)
    past_len = page_table.shape[1] * cache_ckv.shape[1]

    pw = _prep_weights(w_up, w_down, w_m_in, w_m_out, w_dkv, w_uk, w_uv, w_dq, w_uq, w_o, dims)
    small = dict(g_norm1=g_norm1, g_norm2=g_norm2, w_conv=w_conv, b_conv=b_conv, b_m_gates=b_m_gates,
                 g_m_head=g_m_head, g_kv_in=g_kv_in, g_kv=g_kv, g_q=g_q, g_final=g_final)

    c_all = jnp.concatenate([c_prompt, c_sample], axis=0)
    mod = ada_mod(c_all, w_ada, b_ada[:, None, :])
    mod_kv = ada_mod(c_all, w_ada_kv[None], b_ada_kv[None, None, :])[0]

    def split(m, n, lo, hi, per_row):
        parts = jnp.split(m[lo:hi], n, axis=-1)
        return [p[None] if per_row else p[:, None, :] for p in parts]

    mods_p = [split(mod[l], 6, 0, bp, False) for l in range(depth)]
    mods_s = [split(mod[l], 6, bp, bp + bs, True) for l in range(depth)]
    kv_p = split(mod_kv, 2, 0, bp, False)
    kv_s = split(mod_kv, 2, bp, bp + bs, True)

    conv0 = [jnp.zeros((bp, 8, ff), F32)] * depth
    y_p, c_p, n_p, m_p, conv_p, ckv_p, kpe_p = _trunk(
        x_prompt, mods_p, kv_p, jnp.arange(s), conv0, None, None, pw, small, dims)

    pos_s = jnp.full((bs,), past_len, jnp.int32)
    y_s, c_s, n_s, m_s, conv_s, ckv_s, kpe_s = _trunk(
        x_sample.reshape(1, bs, d), mods_s, kv_s, pos_s, state_conv,
        (state_mlstm_C, state_mlstm_n, state_mlstm_m), (cache_ckv, jnp.swapaxes(cache_kpe, 1, 2), page_table), pw, small, dims)

    return (y_p, y_s.reshape(bs, 1, d), c_p, n_p, m_p, conv_p, ckv_p, kpe_p,
            c_s, n_s, m_s, conv_s, ckv_s.reshape(bs, 1, kv_lora), kpe_s.reshape(bs, 1, rope))
```

```python
import functools
import math

import jax
import jax.numpy as jnp
from jax import lax
from jax.experimental import pallas as pl
from jax.experimental.pallas import tpu as pltpu

F32 = jnp.float32
BF16 = jnp.bfloat16

NORM_EPS = 1e-6
ROPE_THETA = 10000.0
LANES = 128
BF16_ROWS = 16
VMEM_LIMIT = 56 * 1024 * 1024

ROW_TILE = 512
FFN_ROW_TILE = 512
MLSTM_CHUNK = 256
ATTN_TILE = 1024
ATTN_BLOCK = 512
FFN_CHUNK = 256
PAGED_CHUNK = 512
LOG2_E = math.log2(math.e)
SOFTMAX_MIN_ROW_SUM = 2.0 ** -60
STEP_BATCH = 8

NT_DIMS = (((1,), (1,)), ((), ()))
TN_DIMS = (((0,), (0,)), ((), ()))


def _params(*sem):
    return pltpu.CompilerParams(dimension_semantics=sem, vmem_limit_bytes=VMEM_LIMIT)


def _const_spec(shape):
    nd = len(shape)
    return pl.BlockSpec(shape, lambda *_: (0,) * nd, pipeline_mode=pl.Buffered(1))


def _rms(x, g):
    return x * lax.rsqrt(jnp.mean(x * x, axis=-1, keepdims=True) + NORM_EPS) * g


def _norm_mod(x, g, sh, sc):
    return _rms(x, g) * (1.0 + sc) + sh


def _log_sigmoid(x):
    return jnp.minimum(x, 0.0) - jnp.log(1.0 + jnp.exp(-jnp.abs(x)))


def _rope3(t, c, sa, sb):
    return t * c + pltpu.roll(t, LANES - 16, axis=1) * sa + pltpu.roll(t, 16, axis=1) * sb


def _row_spec(tm, width):
    return pl.BlockSpec((None, tm, width), lambda g, i: (g, i, 0))


def _mod_spec(per_row, tm, width):
    if per_row:
        return pl.BlockSpec((None, tm, width), lambda g, i: (g, i, 0))
    return pl.BlockSpec((None, 1, width), lambda g, i: (g, 0, 0))


def _ada_kernel(c_ref, w_ref, b_ref, o_ref):
    c = c_ref[...]
    a = (c * jax.nn.sigmoid(c)).astype(BF16)
    o_ref[...] = jnp.dot(a, w_ref[...].astype(BF16), preferred_element_type=F32) + b_ref[...]


def ada_mod(c, w, b, tn=1024):
    m, d = c.shape
    nl, _, n = w.shape
    tn = min(tn, n)
    return pl.pallas_call(
        _ada_kernel,
        grid=(nl, n // tn),
        in_specs=[pl.BlockSpec((m, d), lambda l, j: (0, 0)),
                  pl.BlockSpec((None, d, tn), lambda l, j: (l, 0, j)),
                  pl.BlockSpec((None, 1, tn), lambda l, j: (l, 0, j))],
        out_specs=pl.BlockSpec((None, m, tn), lambda l, j: (l, 0, j)),
        out_shape=jax.ShapeDtypeStruct((nl, m, n), F32),
        compiler_params=_params("parallel", "parallel"),
        name="ada_mod",
    )(c, w, b)


def _proj_kernel(x_ref, g_ref, sh_ref, sc_ref, *refs, scales):
    n = len(scales)
    hn = _norm_mod(x_ref[...], g_ref[...], sh_ref[...], sc_ref[...]).astype(BF16)
    for w_ref, o_ref, s in zip(refs[:n], refs[n:], scales):
        acc = jnp.dot(hn, w_ref[...], preferred_element_type=F32)
        if s != 1.0:
            acc = acc * s
        o_ref[...] = acc.astype(o_ref.dtype)


def norm_mod_proj(x, g, sh, sc, ws, out_dtypes, scales, name):
    gq, r, d = x.shape
    tm = min(ROW_TILE, r)
    per_row = sh.shape[1] != 1
    in_specs = [_row_spec(tm, d), _const_spec((1, d)),
                _mod_spec(per_row, tm, d), _mod_spec(per_row, tm, d)]
    in_specs += [_const_spec(w.shape) for w in ws]
    return pl.pallas_call(
        functools.partial(_proj_kernel, scales=tuple(scales)),
        grid=(gq, r // tm),
        in_specs=in_specs,
        out_specs=[_row_spec(tm, w.shape[1]) for w in ws],
        out_shape=[jax.ShapeDtypeStruct((gq, r, w.shape[1]), dt) for w, dt in zip(ws, out_dtypes)],
        compiler_params=_params("parallel", "parallel"),
        name=name,
    )(x, g, sh, sc, *ws)


def _resid_kernel(a_ref, w_ref, x_ref, gt_ref, o_ref):
    mix = jnp.dot(a_ref[...].astype(BF16), w_ref[...], preferred_element_type=F32)
    o_ref[...] = x_ref[...] + gt_ref[...] * mix


def resid_proj(a, w, x, gt, name):
    gq, r, d = x.shape
    k = a.shape[-1]
    tm = min(ROW_TILE, r)
    per_row = gt.shape[1] != 1
    return pl.pallas_call(
        _resid_kernel,
        grid=(gq, r // tm),
        in_specs=[_row_spec(tm, k), _const_spec(w.shape), _row_spec(tm, d), _mod_spec(per_row, tm, d)],
        out_specs=_row_spec(tm, d),
        out_shape=jax.ShapeDtypeStruct((gq, r, d), F32),
        compiler_params=_params("parallel", "parallel"),
        name=name,
    )(a, w, x, gt)


def _mlstm_chunk_kernel(q_ref, k_ref, v_ref, o_ref, gr_ref, gc_ref, bgc_ref, bgr_ref, gh_ref,
                        hh_ref, c_ref, n_ref, m_ref, *, heads, dk, dv):
    ci = pl.program_id(1)
    chunk = q_ref.shape[0]

    @pl.when(ci == 0)
    def _():
        c_ref[...] = jnp.zeros_like(c_ref)
        n_ref[...] = jnp.zeros_like(n_ref)
        m_ref[...] = jnp.zeros_like(m_ref)

    gates_r = gr_ref[...] + bgc_ref[...]
    gates_c = gc_ref[...] + bgr_ref[...]
    row = lax.broadcasted_iota(jnp.int32, (chunk, chunk), 0)
    col = lax.broadcasted_iota(jnp.int32, (chunk, chunk), 1)
    causal = col <= row

    s_raw, q_c = [], []
    for h in range(heads):
        q = q_ref[:, h * dk:(h + 1) * dk]
        s_raw.append(lax.dot_general(q, k_ref[:, h * dk:(h + 1) * dk].astype(BF16), NT_DIMS,
                                     preferred_element_type=F32))
        q_c.append(jnp.dot(q, c_ref[h].astype(BF16), preferred_element_type=F32))

    for h in range(heads):
        q = q_ref[:, h * dk:(h + 1) * dk]
        k = k_ref[:, h * dk:(h + 1) * dk]
        v = v_ref[:, h * dv:(h + 1) * dv]
        li_r = gates_r[h:h + 1, :]
        lf_r = _log_sigmoid(gates_r[heads + h:heads + h + 1, :])
        li_c = gates_c[:, h:h + 1]
        lf_c = _log_sigmoid(gates_c[:, heads + h:heads + h + 1])

        lf_low = jnp.where(causal, lf_r, 0.0)
        half = chunk // 2
        b_c = jnp.sum(lf_low[:, :half] + lf_low[:, half:], axis=1, keepdims=True)
        b_r = jnp.sum(jnp.where(row <= col, lf_c, 0.0), axis=0, keepdims=True)
        g = jnp.sum(lf_r, axis=1, keepdims=True)

        m_prev = m_ref[h]
        c_prev = c_ref[h]
        n_prev = n_ref[h]

        a_c = g - b_c + li_c
        m_loc = jnp.max(a_c, axis=0, keepdims=True)
        kw = k * jnp.exp(a_c - m_loc)
        c_loc = lax.dot_general(kw.astype(BF16), v, TN_DIMS, preferred_element_type=F32)
        n_loc = jnp.sum(kw, axis=0, keepdims=True)

        dmat = jnp.where(causal, b_c - b_r + li_r, -jnp.inf)
        w0 = b_c + m_prev
        m_s = jnp.maximum(w0, jnp.max(dmat, axis=1, keepdims=True))
        w_inter = jnp.exp(w0 - m_s)
        s = s_raw[h] * jnp.exp(dmat - m_s)
        num = w_inter * q_c[h] + jnp.dot(s.astype(BF16), v, preferred_element_type=F32)
        den = (w_inter * jnp.sum(q.astype(F32) * n_prev, axis=1, keepdims=True)
               + jnp.sum(s, axis=1, keepdims=True))
        hval = num / jnp.maximum(jnp.abs(den), jnp.exp(-m_s))
        hn = _rms(hval, gh_ref[h:h + 1, :])
        gate = jax.nn.sigmoid(o_ref[:, h * dv:(h + 1) * dv])
        hh_ref[:, h * dv:(h + 1) * dv] = (hn * gate).astype(hh_ref.dtype)

        m_new = jnp.maximum(g + m_prev, m_loc)
        fw = jnp.exp(g + m_prev - m_new)
        lw = jnp.exp(m_loc - m_new)
        c_ref[h] = fw * c_prev + lw * c_loc
        n_ref[h] = fw * n_prev + lw * n_loc
        m_ref[h] = m_new


def mlstm_chunkwise(q, k, v, o, gates, b_gates, g_head, heads, dk, dv):
    bsz, s, _ = q.shape
    chunk = min(MLSTM_CHUNK, s)
    gates_r = jnp.swapaxes(gates, 1, 2)
    g2 = 2 * heads
    return pl.pallas_call(
        functools.partial(_mlstm_chunk_kernel, heads=heads, dk=dk, dv=dv),
        grid=(bsz, s // chunk),
        in_specs=[_row_spec(chunk, heads * dk), _row_spec(chunk, heads * dk),
                  _row_spec(chunk, heads * dv), _row_spec(chunk, heads * dv),
                  pl.BlockSpec((None, g2, chunk), lambda b, c: (b, 0, c)),
                  _row_spec(chunk, g2),
                  _const_spec((g2, 1)), _const_spec((1, g2)), _const_spec((heads, dv))],
        out_specs=[_row_spec(chunk, heads * dv),
                   pl.BlockSpec((None, heads, dk, dv), lambda b, c: (b, 0, 0, 0)),
                   pl.BlockSpec((None, heads, 1, dk), lambda b, c: (b, 0, 0, 0)),
                   pl.BlockSpec((None, heads, 1, 1), lambda b, c: (b, 0, 0, 0))],
        out_shape=[jax.ShapeDtypeStruct((bsz, s, heads * dv), BF16),
                   jax.ShapeDtypeStruct((bsz, heads, dk, dv), F32),
                   jax.ShapeDtypeStruct((bsz, heads, 1, dk), F32),
                   jax.ShapeDtypeStruct((bsz, heads, 1, 1), F32)],
        compiler_params=_params("parallel", "arbitrary"),
        name="mlstm_chunkwise",
    )(q, k, v, o, gates_r, gates, b_gates.reshape(g2, 1), b_gates.reshape(1, g2), g_head)


def _mlstm_step_kernel(q_ref, k_ref, v_ref, o_ref, g_ref, bg_ref, gh_ref, c_ref, n_ref, m_ref,
                       hh_ref, co_ref, no_ref, mo_ref, *, heads, dk, dv):
    nb = q_ref.shape[0]
    gates = g_ref[...] + bg_ref[...]
    li = gates[:, :heads]
    lf = _log_sigmoid(gates[:, heads:])
    m_st = m_ref[...]
    m_new = jnp.maximum(lf + m_st, li)
    fw_all = jnp.exp(lf + m_st - m_new)
    iw_all = jnp.exp(li - m_new)
    floor_all = jnp.exp(-m_new)
    mo_ref[...] = m_new
    eye = lax.broadcasted_iota(jnp.int32, (dk, dk), 0) == lax.broadcasted_iota(jnp.int32, (dk, dk), 1)

    def to_col(r):
        return jnp.sum(jnp.where(eye, r, 0.0), axis=1, keepdims=True)

    hr = range(heads)
    for b in range(nb):
        q_r = [q_ref[b:b + 1, h * dk:(h + 1) * dk] for h in hr]
        k_r = [k_ref[b:b + 1, h * dk:(h + 1) * dk] for h in hr]
        k_c = [iw_all[b:b + 1, h:h + 1] * to_col(k_r[h]) for h in hr]
        c_new = [fw_all[b:b + 1, h:h + 1] * c_ref[b, h] + k_c[h] * v_ref[b:b + 1, h * dv:(h + 1) * dv] for h in hr]
        num = [jnp.dot(q_r[h].astype(BF16), c_new[h].astype(BF16), preferred_element_type=F32) for h in hr]
        n_new = [fw_all[b:b + 1, h:h + 1] * n_ref[b, h:h + 1, :] + iw_all[b:b + 1, h:h + 1] * k_r[h] for h in hr]
        den = [jnp.sum(q_r[h] * n_new[h], axis=1, keepdims=True) for h in hr]
        hval = [num[h] / jnp.maximum(jnp.abs(den[h]), floor_all[b:b + 1, h:h + 1]) for h in hr]
        hn = [_rms(hval[h], gh_ref[h:h + 1, :]) for h in hr]
        for h in hr:
            co_ref[b, h] = c_new[h]
            no_ref[b, h:h + 1, :] = n_new[h]
            hh_ref[b:b + 1, h * dv:(h + 1) * dv] = hn[h] * jax.nn.sigmoid(o_ref[b:b + 1, h * dv:(h + 1) * dv])


def mlstm_step(q, k, v, o, gates, b_gates, g_head, c_st, n_st, m_st, heads, dk, dv):
    bsz = q.shape[0]
    nb = min(STEP_BATCH, bsz)
    g2 = 2 * heads
    rows = lambda w: pl.BlockSpec((nb, w), lambda i: (i, 0))
    return pl.pallas_call(
        functools.partial(_mlstm_step_kernel, heads=heads, dk=dk, dv=dv),
        grid=(bsz // nb,),
        in_specs=[rows(heads * dk), rows(heads * dk), rows(heads * dv), rows(heads * dv), rows(g2),
                  _const_spec((1, g2)), _const_spec((heads, dv)),
                  pl.BlockSpec((nb, heads, dk, dv), lambda i: (i, 0, 0, 0)),
                  pl.BlockSpec((nb, heads, dk), lambda i: (i, 0, 0)),
                  rows(heads)],
        out_specs=[rows(heads * dv),
                   pl.BlockSpec((nb, heads, dk, dv), lambda i: (i, 0, 0, 0)),
                   pl.BlockSpec((nb, heads, dk), lambda i: (i, 0, 0)),
                   rows(heads)],
        out_shape=[jax.ShapeDtypeStruct((bsz, heads * dv), F32),
                   jax.ShapeDtypeStruct((bsz, heads, dk, dv), F32),
                   jax.ShapeDtypeStruct((bsz, heads, dk), F32),
                   jax.ShapeDtypeStruct((bsz, heads), F32)],
        compiler_params=_params("parallel"),
        name="mlstm_step",
    )(q, k, v, o, gates, b_gates.reshape(1, g2), g_head, c_st, n_st, m_st)


def _ffn_seq_kernel(x_ref, xh_ref, buf_ref, g_ref, sh_ref, sc_ref, gt_ref, wa_ref, wg_ref, wc_ref,
                    bc_ref, wd_ref, gf_ref, o_ref, alast_ref, *, fc, final_norm):
    i = pl.program_id(1)
    tm = x_ref.shape[0]
    ff = wa_ref.shape[1]
    x = x_ref[...]
    g, sh, sc = g_ref[...], sh_ref[...], sc_ref[...]
    hn = _norm_mod(x, g, sh, sc).astype(BF16)
    hh = _norm_mod(xh_ref[...], g, sh, sc).astype(BF16)
    first = i == 0

    def up(c):
        cs = slice(c * fc, (c + 1) * fc)
        wa = wa_ref[:, cs]
        return (jnp.dot(hn, wa, preferred_element_type=F32),
                jnp.dot(hn, wg_ref[:, cs], preferred_element_type=F32),
                jnp.dot(hh, wa, preferred_element_type=F32))

    n_chunks = ff // fc
    acc = None
    nxt = up(0)
    for c in range(n_chunks):
        cs = slice(c * fc, (c + 1) * fc)
        a, gt, a_halo = nxt
        if c + 1 < n_chunks:
            nxt = up(c + 1)
        ext = jnp.concatenate([jnp.where(first, buf_ref[:, cs], a_halo), a], axis=0)
        conv = (bc_ref[:, cs] + ext[6:6 + tm, :] * wc_ref[0:1, cs]
                + ext[7:7 + tm, :] * wc_ref[1:2, cs] + a * wc_ref[2:3, cs])
        act = (conv * jax.nn.sigmoid(conv) * gt).astype(BF16)
        part = jnp.dot(act, wd_ref[cs, :], preferred_element_type=F32)
        acc = part if acc is None else acc + part
        alast_ref[:, cs] = a[tm - 8:tm, :]

    y = x + gt_ref[...] * acc
    if final_norm:
        y = _rms(y, gf_ref[...])
    o_ref[...] = y


def conv_ffn_seq(x, buf8, g, sh, sc, gt, w_up_a, w_up_g, w_conv, b_conv, w_down, g_final, final_norm, name):
    bsz, s, d = x.shape
    ff = w_up_a.shape[1]
    tm = min(FFN_ROW_TILE, s)
    fc = min(FFN_CHUNK, ff)
    halo =lambda b, i: (b, jnp.maximum(i * (tm // 8) - 1, 0), 0)
    return pl.pallas_call(
        functools.partial(_ffn_seq_kernel, fc=fc, final_norm=final_norm),
        grid=(bsz, s // tm),
        in_specs=[_row_spec(tm, d), pl.BlockSpec((None, 8, d), halo),
                  pl.BlockSpec((None, 8, ff), lambda b, i: (b, 0, 0)),
                  _const_spec((1, d)), _mod_spec(False, tm, d), _mod_spec(False, tm, d), _mod_spec(False, tm, d),
                  _const_spec(w_up_a.shape), _const_spec(w_up_g.shape), _const_spec(w_conv.shape),
                  _const_spec(b_conv.shape), _const_spec(w_down.shape), _const_spec((1, d))],
        out_specs=[_row_spec(tm, d), pl.BlockSpec((None, 8, ff), lambda b, i: (b, 0, 0))],
        out_shape=[jax.ShapeDtypeStruct((bsz, s, d), F32), jax.ShapeDtypeStruct((bsz, 8, ff), F32)],
        compiler_params=_params("parallel", "arbitrary"),
        name=name,
    )(x, x, buf8, g, sh, sc, gt, w_up_a, w_up_g, w_conv, b_conv, w_down, g_final)


def _ffn_tok_kernel(x_ref, b0_ref, b1_ref, g_ref, sh_ref, sc_ref, gt_ref, wa_ref, wg_ref, wc_ref,
                    bc_ref, wd_ref, gf_ref, o_ref, a_ref, acc_ref, *, final_norm):
    c = pl.program_id(0)
    x = x_ref[...]
    hn = _norm_mod(x, g_ref[...], sh_ref[...], sc_ref[...]).astype(BF16)
    a = jnp.dot(hn, wa_ref[...], preferred_element_type=F32)
    gt = jnp.dot(hn, wg_ref[...], preferred_element_type=F32)
    a_ref[...] = a
    conv = bc_ref[...] + b0_ref[...] * wc_ref[0:1, :] + b1_ref[...] * wc_ref[1:2, :] + a * wc_ref[2:3, :]
    act = (conv * jax.nn.sigmoid(conv) * gt).astype(BF16)
    part = jnp.dot(act, wd_ref[...], preferred_element_type=F32)

    @pl.when(c == 0)
    def _():
        acc_ref[...] = part

    @pl.when(c > 0)
    def _():
        acc_ref[...] += part

    y = x + gt_ref[...] * acc_ref[...]
    if final_norm:
        y = _rms(y, gf_ref[...])
    o_ref[...] = y


def conv_ffn_tok(x, buf0, buf1, g, sh, sc, gt, w_up_a, w_up_g, w_conv, b_conv, w_down, g_final, final_norm, name):
    bsz, d = x.shape
    ff = w_up_a.shape[1]
    fc = min(FFN_CHUNK, ff)
    full = lambda w: pl.BlockSpec((bsz, w), lambda c: (0, 0))
    cols = lambda r: pl.BlockSpec((r, fc), lambda c: (0, c))
    return pl.pallas_call(
        functools.partial(_ffn_tok_kernel, final_norm=final_norm),
        grid=(ff // fc,),
        in_specs=[full(d), cols(bsz), cols(bsz), _const_spec((1, d)), full(d), full(d), full(d),
                  cols(d), cols(d), cols(w_conv.shape[0]), cols(1),
                  pl.BlockSpec((fc, d), lambda c: (c, 0)), _const_spec((1, d))],
        out_specs=[full(d), cols(bsz)],
        out_shape=[jax.ShapeDtypeStruct((bsz, d), F32), jax.ShapeDtypeStruct((bsz, ff), F32)],
        scratch_shapes=[pltpu.VMEM((bsz, d), F32)],
        compiler_params=_params("arbitrary"),
        name=name,
    )(x, buf0, buf1, g, sh, sc, gt, w_up_a, w_up_g, w_conv, b_conv, w_down, g_final)


def _latent_kernel(x_ref, g_ref, sh_ref, sc_ref, w_ref, gkv_ref, rc_ref, ra_ref, rb_ref, *refs,
                   kv_lora, rope, with_kv):
    if with_kv:
        wuk_ref, wuv_ref, one_ref, ckv_ref, kpe_ref, kcat_ref, v_ref = refs
    else:
        ckv_ref, kpe_ref = refs
    hn = _norm_mod(x_ref[...], g_ref[...], sh_ref[...], sc_ref[...]).astype(BF16)
    lat = jnp.dot(hn, w_ref[...], preferred_element_type=F32)
    ckv = _rms(lat[:, :kv_lora], gkv_ref[...])
    ckv_ref[...] = ckv
    kpe = _rope3(lat[:, kv_lora:kv_lora + LANES], rc_ref[...], ra_ref[...], rb_ref[...])
    kpe_ref[...] = kpe[:, :rope]
    if with_kv:
        cb = ckv.astype(BF16)
        kn = jnp.dot(cb, wuk_ref[...], preferred_element_type=F32)
        kpe_hi = pltpu.roll(kpe, 64, axis=1)
        for h in range(kn.shape[1] // LANES):
            hs = slice(h * LANES, (h + 1) * LANES)
            kcat_ref[:, hs] = (kn[:, hs] + kpe_hi).astype(BF16)
        v_t = lax.dot_general(wuv_ref[...], cb, NT_DIMS, preferred_element_type=F32)
        v_ref[...] = (v_t + one_ref[...]).astype(BF16)


def shared_latent(x, g, sh, sc, w_dkv_p, g_kv, tabs, kv_lora, rope, w_uk_r=None, w_uv_r=None, v_head=None):
    gq, r, d = x.shape
    tm = min(ROW_TILE, r)
    per_row = sh.shape[1] != 1
    with_kv = w_uk_r is not None
    tab_spec = pl.BlockSpec((tm, LANES), lambda g_, i: (i, 0))
    in_specs = [_row_spec(tm, d), _const_spec((1, d)), _mod_spec(per_row, tm, d), _mod_spec(per_row, tm, d),
                _const_spec(w_dkv_p.shape), _const_spec((1, kv_lora)), tab_spec, tab_spec, tab_spec]
    out_specs = [_row_spec(tm, kv_lora), _row_spec(tm, rope)]
    out_shape = [jax.ShapeDtypeStruct((gq, r, kv_lora), F32), jax.ShapeDtypeStruct((gq, r, rope), F32)]
    args = [x, g, sh, sc, w_dkv_p, g_kv, *tabs]
    if with_kv:
        vt_rows = w_uv_r.shape[0]
        ones_row = (jnp.arange(vt_rows) % LANES == v_head).astype(F32)[:, None]
        in_specs += [_const_spec(w_uk_r.shape), _const_spec(w_uv_r.shape), _const_spec(ones_row.shape)]
        out_specs += [_row_spec(tm, w_uk_r.shape[1]), pl.BlockSpec((None, vt_rows, tm), lambda g_, i: (g_, 0, i))]
        out_shape += [jax.ShapeDtypeStruct((gq, r, w_uk_r.shape[1]), BF16),
                      jax.ShapeDtypeStruct((gq, vt_rows, r), BF16)]
        args += [w_uk_r, w_uv_r, ones_row]
    return pl.pallas_call(
        functools.partial(_latent_kernel, kv_lora=kv_lora, rope=rope, with_kv=with_kv),
        grid=(gq, r // tm),
        in_specs=in_specs, out_specs=out_specs, out_shape=out_shape,
        compiler_params=_params("parallel", "parallel"),
        name="shared_latent_kv" if with_kv else "shared_latent",
    )(*args)


def _query_kernel(x_ref, g_ref, sh_ref, sc_ref, wdq_ref, gq_ref, wuq_ref, wrot_ref, rc_ref, rs_ref, q_ref):
    hn = _norm_mod(x_ref[...], g_ref[...], sh_ref[...], sc_ref[...]).astype(BF16)
    qd = jnp.dot(hn, wdq_ref[...], preferred_element_type=F32)
    qn = _rms(qd, gq_ref[...]).astype(BF16)
    qf = jnp.dot(qn, wuq_ref[...], preferred_element_type=F32)
    qr = jnp.dot(qn, wrot_ref[...], preferred_element_type=F32)
    rc, rs = rc_ref[...], rs_ref[...]
    for h in range(qf.shape[1] // LANES):
        hs = slice(h * LANES, (h + 1) * LANES)
        q_ref[:, hs] = (qf[:, hs] * rc + qr[:, hs] * rs).astype(q_ref.dtype)


def mla_queries(x, g, sh, sc, w_dq, g_q, w_uq_r, w_uq_rot, tab_cos, tab_sin):
    gq, r, d = x.shape
    tm = min(ROW_TILE, r)
    per_row = sh.shape[1] != 1
    tab_spec = pl.BlockSpec((tm, LANES), lambda g_, i: (i, 0))
    return pl.pallas_call(
        _query_kernel,
        grid=(gq, r // tm),
        in_specs=[_row_spec(tm, d), _const_spec((1, d)), _mod_spec(per_row, tm, d), _mod_spec(per_row, tm, d),
                  _const_spec(w_dq.shape), _const_spec(g_q.shape), _const_spec(w_uq_r.shape),
                  _const_spec(w_uq_rot.shape), tab_spec, tab_spec],
        out_specs=_row_spec(tm, w_uq_r.shape[1]),
        out_shape=jax.ShapeDtypeStruct((gq, r, w_uq_r.shape[1]), BF16),
        compiler_params=_params("parallel", "parallel"),
        name="mla_queries",
    )(x, g, sh, sc, w_dq, g_q, w_uq_r, w_uq_rot, tab_cos, tab_sin)


def _attn_kernel(q_ref, k_ref, vt_ref, o_ref, kn_ref, *, v_head, v_rows, blk):
    qi = pl.program_id(2)
    tq = q_ref.shape[0]
    n_full = qi * (tq // blk)
    base = pl.multiple_of(qi * tq, tq)
    heads = range(2)
    diag = [(h, base + c * blk, c * blk, True) for c in range(tq // blk) for h in heads]

    def scores(h, kstart, q_lo):
        hs = slice(h * LANES, (h + 1) * LANES)
        return lax.dot_general(k_ref[pl.ds(kstart, blk), hs], q_ref[q_lo:tq, hs], NT_DIMS,
                               preferred_element_type=F32)

    def values(h, kstart):
        return vt_ref[h * LANES:h * LANES + v_rows, pl.ds(kstart, blk)]

    def cols_from(x, q_lo, new_cols):
        return new_cols if q_lo == 0 else jnp.concatenate([x[:, :q_lo], new_cols], axis=1)

    def run(tasks, carry, update):
        carry = list(carry)
        nxt = scores(*tasks[0][:3])
        for t, (h, kstart, q_lo, masked) in enumerate(tasks):
            s = nxt
            if t + 1 < len(tasks):
                nxt = scores(*tasks[t + 1][:3])
            if masked:
                key = lax.broadcasted_iota(jnp.int32, s.shape, 0)
                qry = lax.broadcasted_iota(jnp.int32, s.shape, 1)
                s = jnp.where(key <= qry, s, -jnp.inf)
            carry[h] = update(h, s, kstart, q_lo, carry[h])
        return tuple(carry)

    def sweep(update, init):
        def full_chunk(kt, carry):
            kstart = pl.multiple_of(kt * blk, blk)
            return run([(h, kstart, 0, False) for h in heads], carry, update)
        return run(diag, lax.fori_loop(0, n_full, full_chunk, init), update)

    def store(accs):
        outs = []
        for acc in accs:
            out_t = acc[:v_head] / acc[v_head:v_head + 1]
            outs.append(jnp.transpose(out_t))
        o_ref[...] = jnp.concatenate(outs, axis=1).astype(o_ref.dtype)

    @pl.when(qi == 0)
    def _():
        kn_ref[...] = jnp.zeros_like(kn_ref)

    bounds = []
    for h in heads:
        hs = slice(h * LANES, (h + 1) * LANES)
        kt = k_ref[pl.ds(base, tq), hs].astype(F32)
        ksq = jnp.max(jnp.sum(kt * kt, axis=1, keepdims=True), axis=0, keepdims=True)
        kn_ref[h] = jnp.maximum(kn_ref[h], ksq)
        qf = q_ref[:, hs].astype(F32)
        qsq = jnp.transpose(jnp.broadcast_to(jnp.sum(qf * qf, axis=1, keepdims=True), (tq, LANES)))[0:1, :]
        bounds.append(jnp.sqrt(qsq * kn_ref[h]))

    def update_bounded(h, s, kstart, q_lo, acc):
        p = jnp.exp2(s - bounds[h][:, q_lo:]).astype(BF16)
        pv = jnp.dot(values(h, kstart), p, preferred_element_type=F32)
        return cols_from(acc, q_lo, acc[:, q_lo:] + pv)

    accs = sweep(update_bounded, tuple(jnp.zeros((v_rows, tq), F32) for _ in heads))
    store(accs)
    smallest = jnp.minimum(accs[0][v_head:v_head + 1], accs[1][v_head:v_head + 1])
    row_sums_ok = jnp.min(smallest) >= SOFTMAX_MIN_ROW_SUM

    @pl.when(jnp.logical_not(row_sums_ok))
    def _():
        def update_online(h, s, kstart, q_lo, state):
            m, acc = state
            m_old = m[:, q_lo:]
            m_new = jnp.maximum(m_old, jnp.max(s, axis=0, keepdims=True))
            p = jnp.exp2(s - m_new).astype(BF16)
            pv = jnp.dot(values(h, kstart), p, preferred_element_type=F32)
            return (cols_from(m, q_lo, m_new),
                    cols_from(acc, q_lo, jnp.exp2(m_old - m_new) * acc[:, q_lo:] + pv))

        init = tuple((jnp.full((1, tq), -jnp.inf, F32), jnp.zeros((v_rows, tq), F32)) for _ in heads)
        store([acc for _, acc in sweep(update_online, init)])


def prompt_attention(q, kcat, vt, v_head):
    bsz, s, hw = q.shape
    pairs = hw // (2 * LANES)
    assert 2 * v_head == LANES, "two heads fill one 128-lane output block"
    tq = min(ATTN_TILE, s)
    blk = min(ATTN_BLOCK, tq)
    v_rows = -(-(v_head + 1) // BF16_ROWS) * BF16_ROWS
    return pl.pallas_call(
        functools.partial(_attn_kernel, v_head=v_head, v_rows=v_rows, blk=blk),
        grid=(bsz, pairs, s // tq),
        in_specs=[pl.BlockSpec((None, tq, 2 * LANES), lambda b, j, i: (b, i, j)),
                  pl.BlockSpec((None, s, 2 * LANES), lambda b, j, i: (b, 0, j)),
                  pl.BlockSpec((None, 2 * LANES, s), lambda b, j, i: (b, j, 0))],
        out_specs=pl.BlockSpec((None, tq, 2 * v_head), lambda b, j, i: (b, i, j)),
        out_shape=jax.ShapeDtypeStruct((bsz, s, pairs * 2 * v_head), BF16),
        scratch_shapes=[pltpu.VMEM((2, 1, 1), F32)],
        compiler_params=_params("parallel", "parallel", "arbitrary"),
        name="prompt_attention",
    )(q, kcat, vt)


def _head_proj_kernel(a_ref, w_ref, o_ref):
    o_ref[...] = jnp.dot(a_ref[...].astype(BF16), w_ref[...], preferred_element_type=F32).astype(o_ref.dtype)


def head_proj_lanes(a, w, out_dtype, name):
    bsz = a.shape[0]
    nh, kk, n = w.shape
    return pl.pallas_call(
        _head_proj_kernel,
        grid=(nh,),
        in_specs=[pl.BlockSpec((bsz, kk), lambda h: (0, h)), pl.BlockSpec((None, kk, n), lambda h: (h, 0, 0))],
        out_specs=pl.BlockSpec((None, bsz, n), lambda h: (h, 0, 0)),
        out_shape=jax.ShapeDtypeStruct((nh, bsz, n), out_dtype),
        compiler_params=_params("parallel"),
        name=name,
    )(a, w)


def _paged_attn_kernel(pt_ref, ql_ref, qp_ref, cn_ref, kn_ref, ckv_hbm, kpe_hbm, o_ref,
                       ckv_buf, kpe_buf, cb_ref, s_ref, p_ref, sem, *, n_pages, ps, chunk):
    b = pl.program_id(0)
    slot = b % 2

    def page_copies(bi, sl):
        out = []
        for pg in range(n_pages):
            page = pt_ref[bi * n_pages + pg]
            rows = pl.ds(pg * ps, ps)
            out.append(pltpu.make_async_copy(ckv_hbm.at[page], ckv_buf.at[sl, rows, :], sem.at[0, sl]))
            out.append(pltpu.make_async_copy(kpe_hbm.at[page], kpe_buf.at[sl, :, rows], sem.at[1, sl]))
        return out

    @pl.when(b == 0)
    def _():
        for cp in page_copies(0, 0):
            cp.start()

    @pl.when(b + 1 < pl.num_programs(0))
    def _():
        for cp in page_copies(b + 1, 1 - slot):
            cp.start()

    for cp in page_copies(b, slot):
        cp.wait()

    ql = ql_ref[...]
    qp = qp_ref[...]
    n_chunks = n_pages * ps // chunk
    chunks = [slice(c * chunk, (c + 1) * chunk) for c in range(n_chunks)]
    s_lat = []
    for cs in chunks:
        ck = ckv_buf[slot, cs, :].astype(BF16)
        cb_ref[cs, :] = ck
        s_lat.append(lax.dot_general(ql, ck, NT_DIMS, preferred_element_type=F32))
    for cs, sl in zip(chunks, s_lat):
        s_ref[:, cs] = sl + jnp.dot(qp, kpe_buf[slot, :, cs].astype(BF16), preferred_element_type=F32)

    cn = cn_ref[...].astype(BF16).astype(F32)
    kn = kn_ref[...].astype(BF16).astype(F32)
    s_new = (jnp.sum(ql.astype(F32) * cn, axis=1, keepdims=True)
             + jnp.sum(qp.astype(F32) * kn, axis=1, keepdims=True))
    s = s_ref[...]
    m = jnp.maximum(jnp.max(s, axis=1, keepdims=True), s_new)
    p = jnp.exp2(s - m)
    p_new = jnp.exp2(s_new - m)
    l = jnp.sum(p, axis=1, keepdims=True) + p_new
    p_ref[...] = p.astype(BF16)
    accs = [p_new.astype(BF16).astype(F32) * cn, jnp.zeros((ql.shape[0], cn.shape[1]), F32)]
    for c, cs in enumerate(chunks):
        accs[c % 2] += jnp.dot(p_ref[:, cs], cb_ref[cs, :], preferred_element_type=F32)
    o_ref[...] = (accs[0] + accs[1]) / l


def paged_attention(q_lat, q_pe, ckv_new, kpe_new, cache_ckv, cache_kpe_t, page_table):
    bsz, nh, c = q_lat.shape
    r = q_pe.shape[-1]
    n_pages = page_table.shape[1]
    ps = cache_ckv.shape[1]
    past = n_pages * ps
    chunk = min(PAGED_CHUNK, past)
    per_b = lambda rows, w: pl.BlockSpec((None, rows, w), lambda b, pt: (b, 0, 0))
    return pl.pallas_call(
        functools.partial(_paged_attn_kernel, n_pages=n_pages, ps=ps, chunk=chunk),
        grid_spec=pltpu.PrefetchScalarGridSpec(
            num_scalar_prefetch=1,
            grid=(bsz,),
            in_specs=[per_b(nh, c), per_b(nh, r), per_b(1, c), per_b(1, r),
                      pl.BlockSpec(memory_space=pl.ANY), pl.BlockSpec(memory_space=pl.ANY)],
            out_specs=per_b(nh, c),
            scratch_shapes=[pltpu.VMEM((2, past, c), F32), pltpu.VMEM((2, r, past), F32),
                            pltpu.VMEM((past, c), BF16), pltpu.VMEM((nh, past), F32),
                            pltpu.VMEM((nh, past), BF16), pltpu.SemaphoreType.DMA((2, 2))]),
        out_shape=jax.ShapeDtypeStruct((bsz, nh, c), F32),
        compiler_params=_params("arbitrary"),
        name="paged_attention",
    )(page_table.reshape(-1), q_lat, q_pe, ckv_new, kpe_new, cache_ckv, cache_kpe_t)


def _rope_tables(pos, rope, lo, scale, passthrough):
    half = rope // 2
    freq = ROPE_THETA ** (-jnp.arange(half, dtype=F32) / half)
    ang = pos.astype(F32)[:, None] * freq[None, :]
    cos, sin = jnp.cos(ang), jnp.sin(ang)
    n = pos.shape[0]
    zeros = lambda w: jnp.zeros((n, w), F32)
    tail = LANES - lo - rope
    c = jnp.concatenate([jnp.full((n, lo), passthrough, F32), cos, cos, zeros(tail)], axis=1)
    sa = jnp.concatenate([zeros(lo), -sin, zeros(half + tail)], axis=1)
    sb = jnp.concatenate([zeros(lo + half), sin, zeros(tail)], axis=1)
    return c * scale, sa * scale, sb * scale


def _prep_weights(w_up, w_down, w_m_in, w_m_out, w_dkv, w_uk, w_uv, w_dq, w_uq, w_o, dims):
    heads, dk, dv = dims["m_heads"], dims["m_dk"], dims["m_dv"]
    nh, nope, rope, kv_lora = dims["mla_heads"], dims["qk_nope"], dims["qk_rope"], dims["kv_lora"]
    ff = w_down.shape[1]
    qd, vd = heads * dk, heads * dv
    d = w_up.shape[1]
    pw = {}
    pw["w_up_a"] = w_up[:, :, :ff].astype(BF16)
    pw["w_up_g"] = w_up[:, :, ff:].astype(BF16)
    pw["w_down"] = w_down.astype(BF16)
    pw["w_m_q"] = w_m_in[:, :, :qd].astype(BF16)
    pw["w_m_k"] = w_m_in[:, :, qd:2 * qd].astype(BF16)
    pw["w_m_v"] = w_m_in[:, :, 2 * qd:2 * qd + vd].astype(BF16)
    pw["w_m_o"] = w_m_in[:, :, 2 * qd + vd:2 * qd + 2 * vd].astype(BF16)
    gates = w_m_in[:, :, 2 * qd + 2 * vd:]
    pw["w_m_g"] = jnp.pad(gates, ((0, 0), (0, 0), (0, LANES - gates.shape[-1]))).astype(BF16)
    pw["w_m_out"] = w_m_out.astype(BF16)
    pw["w_dkv"] = jnp.pad(w_dkv, ((0, 0), (0, LANES - rope))).astype(BF16)
    pw["w_uk_r"] = jnp.pad(w_uk, ((0, 0), (0, 0), (0, LANES - nope))).reshape(kv_lora, nh * LANES).astype(BF16)
    v_head = w_uv.shape[2]
    pw["w_uv_r"] = jnp.pad(w_uv, ((0, 0), (0, 0), (0, LANES - v_head))).reshape(kv_lora, nh * LANES).T.astype(BF16)
    nb = w_uq.shape[0]
    wq = w_uq.reshape(nb, w_uq.shape[1], nh, nope + rope)
    pw["w_uq_r"] = jnp.pad(wq, ((0, 0), (0, 0), (0, 0), (0, LANES - nope - rope))).reshape(
        nb, w_uq.shape[1], nh * LANES).astype(BF16)
    half = rope // 2
    partner = jnp.concatenate([-wq[..., nope + half:], wq[..., nope:nope + half]], axis=-1)
    pw["w_uq_rot"] = jnp.pad(partner, ((0, 0), (0, 0), (0, 0), (nope, LANES - nope - rope))).reshape(
        nb, w_uq.shape[1], nh * LANES).astype(BF16)
    pw["w_dq"] = w_dq.astype(BF16)
    pw["w_o"] = w_o.astype(BF16)
    wukt = jnp.transpose(w_uk, (1, 2, 0))
    pw["w_uk_t"] = jnp.pad(wukt, ((0, 0), (0, LANES - nope), (0, 0))).astype(BF16)
    pw["w_uv_t"] = jnp.transpose(w_uv, (1, 0, 2)).astype(BF16)
    return pw


def _trunk(x, mods, mods_kv, pos, conv_bufs, m_states, kv_past, pw, small, dims):
    heads, dk, dv = dims["m_heads"], dims["m_dk"], dims["m_dv"]
    nh, nope, rope, kv_lora, v_head = (dims["mla_heads"], dims["qk_nope"], dims["qk_rope"],
                                       dims["kv_lora"], dims["v_head"])
    depth, n_a = dims["depth"], dims["n_a"]
    is_prompt = kv_past is None
    gq, r, d = x.shape
    att_scale = (nope + rope) ** -0.5 * LOG2_E
    new_c, new_n, new_m, new_conv = [], [], [], []
    ckv = kpe = kcat = vv = None
    y = None
    for layer in range(depth):
        sh1, sc1, gt1, sh2, sc2, gt2 = mods[layer]
        g1 = small["g_norm1"][layer][None, :]
        g2 = small["g_norm2"][layer][None, :]
        if layer == n_a:
            sh_kv, sc_kv = mods_kv
            tabs = _rope_tables(pos, rope, 0, 1.0, 0.0)
            if is_prompt:
                ckv, kpe, kcat, vv = shared_latent(x, small["g_kv_in"][None, :], sh_kv, sc_kv, pw["w_dkv"],
                                                   small["g_kv"][None, :], tabs, kv_lora, rope,
                                                   pw["w_uk_r"], pw["w_uv_r"], v_head)
            else:
                ckv, kpe = shared_latent(x, small["g_kv_in"][None, :], sh_kv, sc_kv, pw["w_dkv"],
                                         small["g_kv"][None, :], tabs, kv_lora, rope)
        if layer < n_a:
            ws = [pw["w_m_q"][layer], pw["w_m_k"][layer], pw["w_m_v"][layer], pw["w_m_o"][layer], pw["w_m_g"][layer]]
            if is_prompt:
                q, k, v, o, gates = norm_mod_proj(x, g1, sh1, sc1, ws, [BF16, F32, BF16, F32, F32],
                                                  [dk ** -0.5, 1.0, 1.0, 1.0, 1.0], "mlstm_in_proj")
                hh, c_st, n_st, m_st = mlstm_chunkwise(q, k, v, o, gates[..., :2 * heads],
                                                       small["b_m_gates"][layer], small["g_m_head"][layer],
                                                       heads, dk, dv)
                n_st = n_st.reshape(gq, heads, dk)
                m_st = m_st.reshape(gq, heads)
            else:
                q, k, v, o, gates = norm_mod_proj(x, g1, sh1, sc1, ws, [F32] * 5,
                                                  [dk ** -0.5, 1.0, 1.0, 1.0, 1.0], "mlstm_in_proj_tok")
                hh, c_st, n_st, m_st = mlstm_step(q[0], k[0], v[0], o[0], gates[0, :, :2 * heads],
                                                  small["b_m_gates"][layer], small["g_m_head"][layer],
                                                  m_states[0][layer], m_states[1][layer], m_states[2][layer],
                                                  heads, dk, dv)
                hh = hh[None]
            new_c.append(c_st)
            new_n.append(n_st)
            new_m.append(m_st)
            x = resid_proj(hh, pw["w_m_out"][layer], x, gt1, "mlstm_out_proj")
        else:
            j = layer - n_a
            q_cos, q_msin, q_sin = _rope_tables(pos, rope, nope, att_scale, 1.0)
            qh = mla_queries(x, g1, sh1, sc1, pw["w_dq"][j], small["g_q"][j][None, :], pw["w_uq_r"][j],
                             pw["w_uq_rot"][j], q_cos, q_sin - q_msin)
            if is_prompt:
                att = prompt_attention(qh, kcat, vv, v_head)
            else:
                bsz = r
                q2 = qh[0]
                q_lat = head_proj_lanes(q2, pw["w_uk_t"], BF16, "absorb_q")
                q_lat = jnp.swapaxes(q_lat, 0, 1)
                q_pe = q2.reshape(bsz, nh, LANES)[:, :, nope:nope + rope]
                o_lat = paged_attention(q_lat, q_pe, ckv[0][:, None, :], kpe[0][:, None, :],
                                        kv_past[0], kv_past[1], kv_past[2])
                o_lat = o_lat.reshape(bsz, nh * kv_lora)
                att = head_proj_lanes(o_lat, pw["w_uv_t"], F32, "unabsorb_o")
                att = jnp.swapaxes(att, 0, 1).reshape(1, bsz, nh * v_head)
            x = resid_proj(att, pw["w_o"][j], x, gt1, "mla_out_proj")
        final = layer == depth - 1
        gf = small["g_final"][None, :]
        if is_prompt:
            x, a_last = conv_ffn_seq(x, conv_bufs[layer], g2, sh2, sc2, gt2, pw["w_up_a"][layer], pw["w_up_g"][layer],
                                     small["w_conv"][layer], small["b_conv"][layer][None, :], pw["w_down"][layer],
                                     gf, final, "conv_ffn_seq")
            new_conv.append(a_last[:, 6:8, :])
        else:
            buf = conv_bufs[layer]
            x2, a_new = conv_ffn_tok(x[0], buf[:, 0, :], buf[:, 1, :], g2, sh2[0], sc2[0], gt2[0],
                                     pw["w_up_a"][layer], pw["w_up_g"][layer], small["w_conv"][layer],
                                     small["b_conv"][layer][None, :], pw["w_down"][layer], gf, final, "conv_ffn_tok")
            x = x2[None]
            new_conv.append(jnp.stack([buf[:, 1, :], a_new], axis=1))
    return x, jnp.stack(new_c), jnp.stack(new_n), jnp.stack(new_m), jnp.stack(new_conv), ckv, kpe


def kernel(x_prompt, x_sample, state_mlstm_C, state_mlstm_n, state_mlstm_m, state_conv, cache_ckv, cache_kpe,
           page_table, c_prompt, c_sample, g_norm1, g_norm2, w_ada, b_ada, w_up, w_conv, b_conv, w_down,
           w_m_in, b_m_gates, g_m_head, w_m_out, g_kv_in, w_ada_kv, b_ada_kv, w_dkv, g_kv, w_uk, w_uv,
           w_dq, g_q, w_uq, w_o, g_final):
    bp, s, d = x_prompt.shape
    bs, t, _ = x_sample.shape
    assert t == 1, "the sample path handles one new token per sequence"
    depth = w_ada.shape[0]
    n_a = w_m_in.shape[0]
    heads, dv = g_m_head.shape[1], g_m_head.shape[2]
    dk = state_mlstm_C.shape[3]
    kv_lora, nh, nope = w_uk.shape
    v_head = w_uv.shape[2]
    rope = w_dkv.shape[1] - kv_lora
    ff = w_down.shape[1]
    dims = dict(m_heads=heads, m_dk=dk, m_dv=dv, mla_heads=nh, qk_nope=nope, qk_rope=rope, kv_lora=kv_lora,
                v_head=v_head, depth=depth, n_a=n_a)
    past_len = page_table.shape[1] * cache_ckv.shape[1]

    pw = _prep_weights(w_up, w_down, w_m_in, w_m_out, w_dkv, w_uk, w_uv, w_dq, w_uq, w_o, dims)
    small = dict(g_norm1=g_norm1, g_norm2=g_norm2, w_conv=w_conv, b_conv=b_conv, b_m_gates=b_m_gates,
                 g_m_head=g_m_head, g_kv_in=g_kv_in, g_kv=g_kv, g_q=g_q, g_final=g_final)

    c_all = jnp.concatenate([c_prompt, c_sample], axis=0)
    mod = ada_mod(c_all, w_ada, b_ada[:, None, :])
    mod_kv = ada_mod(c_all, w_ada_kv[None], b_ada_kv[None, None, :])[0]

    def split(m, n, lo, hi, per_row):
        parts = jnp.split(m[lo:hi], n, axis=-1)
        return [p[None] if per_row else p[:, None, :] for p in parts]

    mods_p = [split(mod[l], 6, 0, bp, False) for l in range(depth)]
    mods_s = [split(mod[l], 6, bp, bp + bs, True) for l in range(depth)]
    kv_p = split(mod_kv, 2, 0, bp, False)
    kv_s = split(mod_kv, 2, bp, bp + bs, True)

    conv0 = [jnp.zeros((bp, 8, ff), F32)] * depth
    y_p, c_p, n_p, m_p, conv_p, ckv_p, kpe_p = _trunk(
        x_prompt, mods_p, kv_p, jnp.arange(s), conv0, None, None, pw, small, dims)

    pos_s = jnp.full((bs,), past_len, jnp.int32)
    y_s, c_s, n_s, m_s, conv_s, ckv_s, kpe_s = _trunk(
        x_sample.reshape(1, bs, d), mods_s, kv_s, pos_s, state_conv,
        (state_mlstm_C, state_mlstm_n, state_mlstm_m), (cache_ckv, jnp.swapaxes(cache_kpe, 1, 2), page_table), pw, small, dims)

    return (y_p, y_s.reshape(bs, 1, d), c_p, n_p, m_p, conv_p, ckv_p, kpe_p,
            c_s, n_s, m_s, conv_s, ckv_s.reshape(bs, 1, kv_lora), kpe_s.reshape(bs, 1, rope))
```

```python
import functools
import math

import jax
import jax.numpy as jnp
from jax import lax
from jax.experimental import pallas as pl
from jax.experimental.pallas import tpu as pltpu

F32 = jnp.float32
BF16 = jnp.bfloat16

NORM_EPS = 1e-6
ROPE_THETA = 10000.0
LANES = 128
BF16_ROWS = 16
VMEM_LIMIT = 56 * 1024 * 1024

ROW_TILE = 512
FFN_ROW_TILE = 512
MLSTM_CHUNK = 256
ATTN_TILE = 2048
ATTN_BLOCK = 512
FFN_CHUNK = 256
PAGED_CHUNK = 512
LOG2_E = math.log2(math.e)
SOFTMAX_MIN_ROW_SUM = 2.0 ** -60
STEP_BATCH = 8

NT_DIMS = (((1,), (1,)), ((), ()))
TN_DIMS = (((0,), (0,)), ((), ()))


def _params(*sem):
    return pltpu.CompilerParams(dimension_semantics=sem, vmem_limit_bytes=VMEM_LIMIT)


def _const_spec(shape):
    nd = len(shape)
    return pl.BlockSpec(shape, lambda *_: (0,) * nd, pipeline_mode=pl.Buffered(1))


def _rms(x, g):
    return x * lax.rsqrt(jnp.mean(x * x, axis=-1, keepdims=True) + NORM_EPS) * g


def _norm_mod(x, g, sh, sc):
    return _rms(x, g) * (1.0 + sc) + sh


def _log_sigmoid(x):
    return jnp.minimum(x, 0.0) - jnp.log(1.0 + jnp.exp(-jnp.abs(x)))


def _rope3(t, c, sa, sb):
    return t * c + pltpu.roll(t, LANES - 16, axis=1) * sa + pltpu.roll(t, 16, axis=1) * sb


def _row_spec(tm, width):
    return pl.BlockSpec((None, tm, width), lambda g, i: (g, i, 0))


def _mod_spec(per_row, tm, width):
    if per_row:
        return pl.BlockSpec((None, tm, width), lambda g, i: (g, i, 0))
    return pl.BlockSpec((None, 1, width), lambda g, i: (g, 0, 0))


def _ada_kernel(c_ref, w_ref, b_ref, o_ref):
    c = c_ref[...]
    a = (c * jax.nn.sigmoid(c)).astype(BF16)
    o_ref[...] = jnp.dot(a, w_ref[...].astype(BF16), preferred_element_type=F32) + b_ref[...]


def ada_mod(c, w, b, tn=1024):
    m, d = c.shape
    nl, _, n = w.shape
    tn = min(tn, n)
    return pl.pallas_call(
        _ada_kernel,
        grid=(nl, n // tn),
        in_specs=[pl.BlockSpec((m, d), lambda l, j: (0, 0)),
                  pl.BlockSpec((None, d, tn), lambda l, j: (l, 0, j)),
                  pl.BlockSpec((None, 1, tn), lambda l, j: (l, 0, j))],
        out_specs=pl.BlockSpec((None, m, tn), lambda l, j: (l, 0, j)),
        out_shape=jax.ShapeDtypeStruct((nl, m, n), F32),
        compiler_params=_params("parallel", "parallel"),
        name="ada_mod",
    )(c, w, b)


def _proj_kernel(x_ref, g_ref, sh_ref, sc_ref, *refs, scales):
    n = len(scales)
    hn = _norm_mod(x_ref[...], g_ref[...], sh_ref[...], sc_ref[...]).astype(BF16)
    for w_ref, o_ref, s in zip(refs[:n], refs[n:], scales):
        acc = jnp.dot(hn, w_ref[...], preferred_element_type=F32)
        if s != 1.0:
            acc = acc * s
        o_ref[...] = acc.astype(o_ref.dtype)


def norm_mod_proj(x, g, sh, sc, ws, out_dtypes, scales, name):
    gq, r, d = x.shape
    tm = min(ROW_TILE, r)
    per_row = sh.shape[1] != 1
    in_specs = [_row_spec(tm, d), _const_spec((1, d)),
                _mod_spec(per_row, tm, d), _mod_spec(per_row, tm, d)]
    in_specs += [_const_spec(w.shape) for w in ws]
    return pl.pallas_call(
        functools.partial(_proj_kernel, scales=tuple(scales)),
        grid=(gq, r // tm),
        in_specs=in_specs,
        out_specs=[_row_spec(tm, w.shape[1]) for w in ws],
        out_shape=[jax.ShapeDtypeStruct((gq, r, w.shape[1]), dt) for w, dt in zip(ws, out_dtypes)],
        compiler_params=_params("parallel", "parallel"),
        name=name,
    )(x, g, sh, sc, *ws)


def _resid_kernel(a_ref, w_ref, x_ref, gt_ref, o_ref):
    mix = jnp.dot(a_ref[...].astype(BF16), w_ref[...], preferred_element_type=F32)
    o_ref[...] = x_ref[...] + gt_ref[...] * mix


def resid_proj(a, w, x, gt, name):
    gq, r, d = x.shape
    k = a.shape[-1]
    tm = min(ROW_TILE, r)
    per_row = gt.shape[1] != 1
    return pl.pallas_call(
        _resid_kernel,
        grid=(gq, r // tm),
        in_specs=[_row_spec(tm, k), _const_spec(w.shape), _row_spec(tm, d), _mod_spec(per_row, tm, d)],
        out_specs=_row_spec(tm, d),
        out_shape=jax.ShapeDtypeStruct((gq, r, d), F32),
        compiler_params=_params("parallel", "parallel"),
        name=name,
    )(a, w, x, gt)


def _mlstm_chunk_kernel(q_ref, k_ref, v_ref, o_ref, gr_ref, gc_ref, bgc_ref, bgr_ref, gh_ref,
                        hh_ref, c_ref, n_ref, m_ref, *, heads, dk, dv):
    ci = pl.program_id(1)
    chunk = q_ref.shape[0]

    @pl.when(ci == 0)
    def _():
        c_ref[...] = jnp.zeros_like(c_ref)
        n_ref[...] = jnp.zeros_like(n_ref)
        m_ref[...] = jnp.zeros_like(m_ref)

    gates_r = gr_ref[...] + bgc_ref[...]
    gates_c = gc_ref[...] + bgr_ref[...]
    row = lax.broadcasted_iota(jnp.int32, (chunk, chunk), 0)
    col = lax.broadcasted_iota(jnp.int32, (chunk, chunk), 1)
    causal = col <= row

    s_raw, q_c = [], []
    for h in range(heads):
        q = q_ref[:, h * dk:(h + 1) * dk]
        s_raw.append(lax.dot_general(q, k_ref[:, h * dk:(h + 1) * dk].astype(BF16), NT_DIMS,
                                     preferred_element_type=F32))
        q_c.append(jnp.dot(q, c_ref[h].astype(BF16), preferred_element_type=F32))

    for h in range(heads):
        q = q_ref[:, h * dk:(h + 1) * dk]
        k = k_ref[:, h * dk:(h + 1) * dk]
        v = v_ref[:, h * dv:(h + 1) * dv]
        li_r = gates_r[h:h + 1, :]
        lf_r = _log_sigmoid(gates_r[heads + h:heads + h + 1, :])
        li_c = gates_c[:, h:h + 1]
        lf_c = _log_sigmoid(gates_c[:, heads + h:heads + h + 1])

        lf_low = jnp.where(causal, lf_r, 0.0)
        half = chunk // 2
        b_c = jnp.sum(lf_low[:, :half] + lf_low[:, half:], axis=1, keepdims=True)
        b_r = jnp.sum(jnp.where(row <= col, lf_c, 0.0), axis=0, keepdims=True)
        g = jnp.sum(lf_r, axis=1, keepdims=True)

        m_prev = m_ref[h]
        c_prev = c_ref[h]
        n_prev = n_ref[h]

        a_c = g - b_c + li_c
        m_loc = jnp.max(a_c, axis=0, keepdims=True)
        kw = k * jnp.exp(a_c - m_loc)
        c_loc = lax.dot_general(kw.astype(BF16), v, TN_DIMS, preferred_element_type=F32)
        n_loc = jnp.sum(kw, axis=0, keepdims=True)

        dmat = jnp.where(causal, b_c - b_r + li_r, -jnp.inf)
        w0 = b_c + m_prev
        m_s = jnp.maximum(w0, jnp.max(dmat, axis=1, keepdims=True))
        w_inter = jnp.exp(w0 - m_s)
        s = s_raw[h] * jnp.exp(dmat - m_s)
        num = w_inter * q_c[h] + jnp.dot(s.astype(BF16), v, preferred_element_type=F32)
        den = (w_inter * jnp.sum(q.astype(F32) * n_prev, axis=1, keepdims=True)
               + jnp.sum(s, axis=1, keepdims=True))
        hval = num / jnp.maximum(jnp.abs(den), jnp.exp(-m_s))
        hn = _rms(hval, gh_ref[h:h + 1, :])
        gate = jax.nn.sigmoid(o_ref[:, h * dv:(h + 1) * dv])
        hh_ref[:, h * dv:(h + 1) * dv] = (hn * gate).astype(hh_ref.dtype)

        m_new = jnp.maximum(g + m_prev, m_loc)
        fw = jnp.exp(g + m_prev - m_new)
        lw = jnp.exp(m_loc - m_new)
        c_ref[h] = fw * c_prev + lw * c_loc
        n_ref[h] = fw * n_prev + lw * n_loc
        m_ref[h] = m_new


def mlstm_chunkwise(q, k, v, o, gates, b_gates, g_head, heads, dk, dv):
    bsz, s, _ = q.shape
    chunk = min(MLSTM_CHUNK, s)
    gates_r = jnp.swapaxes(gates, 1, 2)
    g2 = 2 * heads
    return pl.pallas_call(
        functools.partial(_mlstm_chunk_kernel, heads=heads, dk=dk, dv=dv),
        grid=(bsz, s // chunk),
        in_specs=[_row_spec(chunk, heads * dk), _row_spec(chunk, heads * dk),
                  _row_spec(chunk, heads * dv), _row_spec(chunk, heads * dv),
                  pl.BlockSpec((None, g2, chunk), lambda b, c: (b, 0, c)),
                  _row_spec(chunk, g2),
                  _const_spec((g2, 1)), _const_spec((1, g2)), _const_spec((heads, dv))],
        out_specs=[_row_spec(chunk, heads * dv),
                   pl.BlockSpec((None, heads, dk, dv), lambda b, c: (b, 0, 0, 0)),
                   pl.BlockSpec((None, heads, 1, dk), lambda b, c: (b, 0, 0, 0)),
                   pl.BlockSpec((None, heads, 1, 1), lambda b, c: (b, 0, 0, 0))],
        out_shape=[jax.ShapeDtypeStruct((bsz, s, heads * dv), BF16),
                   jax.ShapeDtypeStruct((bsz, heads, dk, dv), F32),
                   jax.ShapeDtypeStruct((bsz, heads, 1, dk), F32),
                   jax.ShapeDtypeStruct((bsz, heads, 1, 1), F32)],
        compiler_params=_params("parallel", "arbitrary"),
        name="mlstm_chunkwise",
    )(q, k, v, o, gates_r, gates, b_gates.reshape(g2, 1), b_gates.reshape(1, g2), g_head)


def _mlstm_step_kernel(q_ref, k_ref, v_ref, o_ref, g_ref, bg_ref, gh_ref, c_ref, n_ref, m_ref,
                       hh_ref, co_ref, no_ref, mo_ref, *, heads, dk, dv):
    nb = q_ref.shape[0]
    gates = g_ref[...] + bg_ref[...]
    li = gates[:, :heads]
    lf = _log_sigmoid(gates[:, heads:])
    m_st = m_ref[...]
    m_new = jnp.maximum(lf + m_st, li)
    fw_all = jnp.exp(lf + m_st - m_new)
    iw_all = jnp.exp(li - m_new)
    floor_all = jnp.exp(-m_new)
    mo_ref[...] = m_new
    eye = lax.broadcasted_iota(jnp.int32, (dk, dk), 0) == lax.broadcasted_iota(jnp.int32, (dk, dk), 1)

    def to_col(r):
        return jnp.sum(jnp.where(eye, r, 0.0), axis=1, keepdims=True)

    hr = range(heads)
    for b in range(nb):
        q_r = [q_ref[b:b + 1, h * dk:(h + 1) * dk] for h in hr]
        k_r = [k_ref[b:b + 1, h * dk:(h + 1) * dk] for h in hr]
        k_c = [iw_all[b:b + 1, h:h + 1] * to_col(k_r[h]) for h in hr]
        c_new = [fw_all[b:b + 1, h:h + 1] * c_ref[b, h] + k_c[h] * v_ref[b:b + 1, h * dv:(h + 1) * dv] for h in hr]
        num = [jnp.dot(q_r[h].astype(BF16), c_new[h].astype(BF16), preferred_element_type=F32) for h in hr]
        n_new = [fw_all[b:b + 1, h:h + 1] * n_ref[b, h:h + 1, :] + iw_all[b:b + 1, h:h + 1] * k_r[h] for h in hr]
        den = [jnp.sum(q_r[h] * n_new[h], axis=1, keepdims=True) for h in hr]
        hval = [num[h] / jnp.maximum(jnp.abs(den[h]), floor_all[b:b + 1, h:h + 1]) for h in hr]
        hn = [_rms(hval[h], gh_ref[h:h + 1, :]) for h in hr]
        for h in hr:
            co_ref[b, h] = c_new[h]
            no_ref[b, h:h + 1, :] = n_new[h]
            hh_ref[b:b + 1, h * dv:(h + 1) * dv] = hn[h] * jax.nn.sigmoid(o_ref[b:b + 1, h * dv:(h + 1) * dv])


def mlstm_step(q, k, v, o, gates, b_gates, g_head, c_st, n_st, m_st, heads, dk, dv):
    bsz = q.shape[0]
    nb = min(STEP_BATCH, bsz)
    g2 = 2 * heads
    rows = lambda w: pl.BlockSpec((nb, w), lambda i: (i, 0))
    return pl.pallas_call(
        functools.partial(_mlstm_step_kernel, heads=heads, dk=dk, dv=dv),
        grid=(bsz // nb,),
        in_specs=[rows(heads * dk), rows(heads * dk), rows(heads * dv), rows(heads * dv), rows(g2),
                  _const_spec((1, g2)), _const_spec((heads, dv)),
                  pl.BlockSpec((nb, heads, dk, dv), lambda i: (i, 0, 0, 0)),
                  pl.BlockSpec((nb, heads, dk), lambda i: (i, 0, 0)),
                  rows(heads)],
        out_specs=[rows(heads * dv),
                   pl.BlockSpec((nb, heads, dk, dv), lambda i: (i, 0, 0, 0)),
                   pl.BlockSpec((nb, heads, dk), lambda i: (i, 0, 0)),
                   rows(heads)],
        out_shape=[jax.ShapeDtypeStruct((bsz, heads * dv), F32),
                   jax.ShapeDtypeStruct((bsz, heads, dk, dv), F32),
                   jax.ShapeDtypeStruct((bsz, heads, dk), F32),
                   jax.ShapeDtypeStruct((bsz, heads), F32)],
        compiler_params=_params("parallel"),
        name="mlstm_step",
    )(q, k, v, o, gates, b_gates.reshape(1, g2), g_head, c_st, n_st, m_st)


def _ffn_seq_kernel(x_ref, xh_ref, buf_ref, g_ref, sh_ref, sc_ref, gt_ref, wa_ref, wg_ref, wc_ref,
                    bc_ref, wd_ref, gf_ref, o_ref, alast_ref, *, fc, final_norm):
    i = pl.program_id(1)
    tm = x_ref.shape[0]
    ff = wa_ref.shape[1]
    x = x_ref[...]
    g, sh, sc = g_ref[...], sh_ref[...], sc_ref[...]
    hn = _norm_mod(x, g, sh, sc).astype(BF16)
    hh = _norm_mod(xh_ref[...], g, sh, sc).astype(BF16)
    first = i == 0

    def up(c):
        cs = slice(c * fc, (c + 1) * fc)
        wa = wa_ref[:, cs]
        return (jnp.dot(hn, wa, preferred_element_type=F32),
                jnp.dot(hn, wg_ref[:, cs], preferred_element_type=F32),
                jnp.dot(hh, wa, preferred_element_type=F32))

    n_chunks = ff // fc
    acc = None
    nxt = up(0)
    for c in range(n_chunks):
        cs = slice(c * fc, (c + 1) * fc)
        a, gt, a_halo = nxt
        if c + 1 < n_chunks:
            nxt = up(c + 1)
        ext = jnp.concatenate([jnp.where(first, buf_ref[:, cs], a_halo), a], axis=0)
        conv = (bc_ref[:, cs] + ext[6:6 + tm, :] * wc_ref[0:1, cs]
                + ext[7:7 + tm, :] * wc_ref[1:2, cs] + a * wc_ref[2:3, cs])
        act = (conv * jax.nn.sigmoid(conv) * gt).astype(BF16)
        part = jnp.dot(act, wd_ref[cs, :], preferred_element_type=F32)
        acc = part if acc is None else acc + part
        alast_ref[:, cs] = a[tm - 8:tm, :]

    y = x + gt_ref[...] * acc
    if final_norm:
        y = _rms(y, gf_ref[...])
    o_ref[...] = y


def conv_ffn_seq(x, buf8, g, sh, sc, gt, w_up_a, w_up_g, w_conv, b_conv, w_down, g_final, final_norm, name):
    bsz, s, d = x.shape
    ff = w_up_a.shape[1]
    tm = min(FFN_ROW_TILE, s)
    fc = min(FFN_CHUNK, ff)
    halo =lambda b, i: (b, jnp.maximum(i * (tm // 8) - 1, 0), 0)
    return pl.pallas_call(
        functools.partial(_ffn_seq_kernel, fc=fc, final_norm=final_norm),
        grid=(bsz, s // tm),
        in_specs=[_row_spec(tm, d), pl.BlockSpec((None, 8, d), halo),
                  pl.BlockSpec((None, 8, ff), lambda b, i: (b, 0, 0)),
                  _const_spec((1, d)), _mod_spec(False, tm, d), _mod_spec(False, tm, d), _mod_spec(False, tm, d),
                  _const_spec(w_up_a.shape), _const_spec(w_up_g.shape), _const_spec(w_conv.shape),
                  _const_spec(b_conv.shape), _const_spec(w_down.shape), _const_spec((1, d))],
        out_specs=[_row_spec(tm, d), pl.BlockSpec((None, 8, ff), lambda b, i: (b, 0, 0))],
        out_shape=[jax.ShapeDtypeStruct((bsz, s, d), F32), jax.ShapeDtypeStruct((bsz, 8, ff), F32)],
        compiler_params=_params("parallel", "arbitrary"),
        name=name,
    )(x, x, buf8, g, sh, sc, gt, w_up_a, w_up_g, w_conv, b_conv, w_down, g_final)


def _ffn_tok_kernel(x_ref, b0_ref, b1_ref, g_ref, sh_ref, sc_ref, gt_ref, wa_ref, wg_ref, wc_ref,
                    bc_ref, wd_ref, gf_ref, o_ref, a_ref, acc_ref, *, final_norm):
    c = pl.program_id(0)
    x = x_ref[...]
    hn = _norm_mod(x, g_ref[...], sh_ref[...], sc_ref[...]).astype(BF16)
    a = jnp.dot(hn, wa_ref[...], preferred_element_type=F32)
    gt = jnp.dot(hn, wg_ref[...], preferred_element_type=F32)
    a_ref[...] = a
    conv = bc_ref[...] + b0_ref[...] * wc_ref[0:1, :] + b1_ref[...] * wc_ref[1:2, :] + a * wc_ref[2:3, :]
    act = (conv * jax.nn.sigmoid(conv) * gt).astype(BF16)
    part = jnp.dot(act, wd_ref[...], preferred_element_type=F32)

    @pl.when(c == 0)
    def _():
        acc_ref[...] = part

    @pl.when(c > 0)
    def _():
        acc_ref[...] += part

    y = x + gt_ref[...] * acc_ref[...]
    if final_norm:
        y = _rms(y, gf_ref[...])
    o_ref[...] = y


def conv_ffn_tok(x, buf0, buf1, g, sh, sc, gt, w_up_a, w_up_g, w_conv, b_conv, w_down, g_final, final_norm, name):
    bsz, d = x.shape
    ff = w_up_a.shape[1]
    fc = min(FFN_CHUNK, ff)
    full = lambda w: pl.BlockSpec((bsz, w), lambda c: (0, 0))
    cols = lambda r: pl.BlockSpec((r, fc), lambda c: (0, c))
    return pl.pallas_call(
        functools.partial(_ffn_tok_kernel, final_norm=final_norm),
        grid=(ff // fc,),
        in_specs=[full(d), cols(bsz), cols(bsz), _const_spec((1, d)), full(d), full(d), full(d),
                  cols(d), cols(d), cols(w_conv.shape[0]), cols(1),
                  pl.BlockSpec((fc, d), lambda c: (c, 0)), _const_spec((1, d))],
        out_specs=[full(d), cols(bsz)],
        out_shape=[jax.ShapeDtypeStruct((bsz, d), F32), jax.ShapeDtypeStruct((bsz, ff), F32)],
        scratch_shapes=[pltpu.VMEM((bsz, d), F32)],
        compiler_params=_params("arbitrary"),
        name=name,
    )(x, buf0, buf1, g, sh, sc, gt, w_up_a, w_up_g, w_conv, b_conv, w_down, g_final)


def _latent_kernel(x_ref, g_ref, sh_ref, sc_ref, w_ref, gkv_ref, rc_ref, ra_ref, rb_ref, *refs,
                   kv_lora, rope, with_kv):
    if with_kv:
        wuk_ref, wuv_ref, one_ref, ckv_ref, kpe_ref, kcat_ref, v_ref = refs
    else:
        ckv_ref, kpe_ref = refs
    hn = _norm_mod(x_ref[...], g_ref[...], sh_ref[...], sc_ref[...]).astype(BF16)
    lat = jnp.dot(hn, w_ref[...], preferred_element_type=F32)
    ckv = _rms(lat[:, :kv_lora], gkv_ref[...])
    ckv_ref[...] = ckv
    kpe = _rope3(lat[:, kv_lora:kv_lora + LANES], rc_ref[...], ra_ref[...], rb_ref[...])
    kpe_ref[...] = kpe[:, :rope]
    if with_kv:
        cb = ckv.astype(BF16)
        kn = jnp.dot(cb, wuk_ref[...], preferred_element_type=F32)
        kpe_hi = pltpu.roll(kpe, 64, axis=1)
        for h in range(kn.shape[1] // LANES):
            hs = slice(h * LANES, (h + 1) * LANES)
            kcat_ref[:, hs] = (kn[:, hs] + kpe_hi).astype(BF16)
        v_t = lax.dot_general(wuv_ref[...], cb, NT_DIMS, preferred_element_type=F32)
        v_ref[...] = (v_t + one_ref[...]).astype(BF16)


def shared_latent(x, g, sh, sc, w_dkv_p, g_kv, tabs, kv_lora, rope, w_uk_r=None, w_uv_r=None, v_head=None):
    gq, r, d = x.shape
    tm = min(ROW_TILE, r)
    per_row = sh.shape[1] != 1
    with_kv = w_uk_r is not None
    tab_spec = pl.BlockSpec((tm, LANES), lambda g_, i: (i, 0))
    in_specs = [_row_spec(tm, d), _const_spec((1, d)), _mod_spec(per_row, tm, d), _mod_spec(per_row, tm, d),
                _const_spec(w_dkv_p.shape), _const_spec((1, kv_lora)), tab_spec, tab_spec, tab_spec]
    out_specs = [_row_spec(tm, kv_lora), _row_spec(tm, rope)]
    out_shape = [jax.ShapeDtypeStruct((gq, r, kv_lora), F32), jax.ShapeDtypeStruct((gq, r, rope), F32)]
    args = [x, g, sh, sc, w_dkv_p, g_kv, *tabs]
    if with_kv:
        vt_rows = w_uv_r.shape[0]
        ones_row = (jnp.arange(vt_rows) % LANES == v_head).astype(F32)[:, None]
        in_specs += [_const_spec(w_uk_r.shape), _const_spec(w_uv_r.shape), _const_spec(ones_row.shape)]
        out_specs += [_row_spec(tm, w_uk_r.shape[1]), pl.BlockSpec((None, vt_rows, tm), lambda g_, i: (g_, 0, i))]
        out_shape += [jax.ShapeDtypeStruct((gq, r, w_uk_r.shape[1]), BF16),
                      jax.ShapeDtypeStruct((gq, vt_rows, r), BF16)]
        args += [w_uk_r, w_uv_r, ones_row]
    return pl.pallas_call(
        functools.partial(_latent_kernel, kv_lora=kv_lora, rope=rope, with_kv=with_kv),
        grid=(gq, r // tm),
        in_specs=in_specs, out_specs=out_specs, out_shape=out_shape,
        compiler_params=_params("parallel", "parallel"),
        name="shared_latent_kv" if with_kv else "shared_latent",
    )(*args)


def _query_kernel(x_ref, g_ref, sh_ref, sc_ref, wdq_ref, gq_ref, wuq_ref, wrot_ref, rc_ref, rs_ref, q_ref):
    hn = _norm_mod(x_ref[...], g_ref[...], sh_ref[...], sc_ref[...]).astype(BF16)
    qd = jnp.dot(hn, wdq_ref[...], preferred_element_type=F32)
    qn = _rms(qd, gq_ref[...]).astype(BF16)
    qf = jnp.dot(qn, wuq_ref[...], preferred_element_type=F32)
    qr = jnp.dot(qn, wrot_ref[...], preferred_element_type=F32)
    rc, rs = rc_ref[...], rs_ref[...]
    for h in range(qf.shape[1] // LANES):
        hs = slice(h * LANES, (h + 1) * LANES)
        q_ref[:, hs] = (qf[:, hs] * rc + qr[:, hs] * rs).astype(q_ref.dtype)


def mla_queries(x, g, sh, sc, w_dq, g_q, w_uq_r, w_uq_rot, tab_cos, tab_sin):
    gq, r, d = x.shape
    tm = min(ROW_TILE, r)
    per_row = sh.shape[1] != 1
    tab_spec = pl.BlockSpec((tm, LANES), lambda g_, i: (i, 0))
    return pl.pallas_call(
        _query_kernel,
        grid=(gq, r // tm),
        in_specs=[_row_spec(tm, d), _const_spec((1, d)), _mod_spec(per_row, tm, d), _mod_spec(per_row, tm, d),
                  _const_spec(w_dq.shape), _const_spec(g_q.shape), _const_spec(w_uq_r.shape),
                  _const_spec(w_uq_rot.shape), tab_spec, tab_spec],
        out_specs=_row_spec(tm, w_uq_r.shape[1]),
        out_shape=jax.ShapeDtypeStruct((gq, r, w_uq_r.shape[1]), BF16),
        compiler_params=_params("parallel", "parallel"),
        name="mla_queries",
    )(x, g, sh, sc, w_dq, g_q, w_uq_r, w_uq_rot, tab_cos, tab_sin)


def _attn_kernel(q_ref, k_ref, vt_ref, o_ref, kn_ref, *, v_head, v_rows, blk):
    qi = pl.program_id(2)
    tq = q_ref.shape[0]
    n_full = qi * (tq // blk)
    base = pl.multiple_of(qi * tq, tq)
    heads = range(2)
    diag = [(h, base + c * blk, c * blk, True) for c in range(tq // blk) for h in heads]

    def scores(h, kstart, q_lo):
        hs = slice(h * LANES, (h + 1) * LANES)
        return lax.dot_general(k_ref[pl.ds(kstart, blk), hs], q_ref[q_lo:tq, hs], NT_DIMS,
                               preferred_element_type=F32)

    def values(h, kstart):
        return vt_ref[h * LANES:h * LANES + v_rows, pl.ds(kstart, blk)]

    def cols_from(x, q_lo, new_cols):
        return new_cols if q_lo == 0 else jnp.concatenate([x[:, :q_lo], new_cols], axis=1)

    def run(tasks, carry, update):
        carry = list(carry)
        nxt = scores(*tasks[0][:3])
        for t, (h, kstart, q_lo, masked) in enumerate(tasks):
            s = nxt
            if t + 1 < len(tasks):
                nxt = scores(*tasks[t + 1][:3])
            if masked:
                key = lax.broadcasted_iota(jnp.int32, s.shape, 0)
                qry = lax.broadcasted_iota(jnp.int32, s.shape, 1)
                s = jnp.where(key <= qry, s, -jnp.inf)
            carry[h] = update(h, s, kstart, q_lo, carry[h])
        return tuple(carry)

    def sweep(update, init):
        def full_chunk(kt, carry):
            kstart = pl.multiple_of(kt * blk, blk)
            return run([(h, kstart, 0, False) for h in heads], carry, update)
        return run(diag, lax.fori_loop(0, n_full, full_chunk, init), update)

    def store(accs):
        outs = []
        for acc in accs:
            out_t = acc[:v_head] / acc[v_head:v_head + 1]
            outs.append(jnp.transpose(out_t))
        o_ref[...] = jnp.concatenate(outs, axis=1).astype(o_ref.dtype)

    @pl.when(qi == 0)
    def _():
        kn_ref[...] = jnp.zeros_like(kn_ref)

    bounds = []
    for h in heads:
        hs = slice(h * LANES, (h + 1) * LANES)
        kt = k_ref[pl.ds(base, tq), hs].astype(F32)
        ksq = jnp.max(jnp.sum(kt * kt, axis=1, keepdims=True), axis=0, keepdims=True)
        kn_ref[h] = jnp.maximum(kn_ref[h], ksq)
        qf = q_ref[:, hs].astype(F32)
        qsq = jnp.transpose(jnp.broadcast_to(jnp.sum(qf * qf, axis=1, keepdims=True), (tq, LANES)))[0:1, :]
        bounds.append(jnp.sqrt(qsq * kn_ref[h]))

    def update_bounded(h, s, kstart, q_lo, acc):
        p = jnp.exp2(s - bounds[h][:, q_lo:]).astype(BF16)
        pv = jnp.dot(values(h, kstart), p, preferred_element_type=F32)
        return cols_from(acc, q_lo, acc[:, q_lo:] + pv)

    accs = sweep(update_bounded, tuple(jnp.zeros((v_rows, tq), F32) for _ in heads))
    store(accs)
    smallest = jnp.minimum(accs[0][v_head:v_head + 1], accs[1][v_head:v_head + 1])
    row_sums_ok = jnp.min(smallest) >= SOFTMAX_MIN_ROW_SUM

    @pl.when(jnp.logical_not(row_sums_ok))
    def _():
        def update_online(h, s, kstart, q_lo, state):
            m, acc = state
            m_old = m[:, q_lo:]
            m_new = jnp.maximum(m_old, jnp.max(s, axis=0, keepdims=True))
            p = jnp.exp2(s - m_new).astype(BF16)
            pv = jnp.dot(values(h, kstart), p, preferred_element_type=F32)
            return (cols_from(m, q_lo, m_new),
                    cols_from(acc, q_lo, jnp.exp2(m_old - m_new) * acc[:, q_lo:] + pv))

        init = tuple((jnp.full((1, tq), -jnp.inf, F32), jnp.zeros((v_rows, tq), F32)) for _ in heads)
        store([acc for _, acc in sweep(update_online, init)])


def prompt_attention(q, kcat, vt, v_head):
    bsz, s, hw = q.shape
    pairs = hw // (2 * LANES)
    assert 2 * v_head == LANES, "two heads fill one 128-lane output block"
    tq = min(ATTN_TILE, s)
    blk = min(ATTN_BLOCK, tq)
    v_rows = -(-(v_head + 1) // BF16_ROWS) * BF16_ROWS
    return pl.pallas_call(
        functools.partial(_attn_kernel, v_head=v_head, v_rows=v_rows, blk=blk),
        grid=(bsz, pairs, s // tq),
        in_specs=[pl.BlockSpec((None, tq, 2 * LANES), lambda b, j, i: (b, i, j)),
                  pl.BlockSpec((None, s, 2 * LANES), lambda b, j, i: (b, 0, j)),
                  pl.BlockSpec((None, 2 * LANES, s), lambda b, j, i: (b, j, 0))],
        out_specs=pl.BlockSpec((None, tq, 2 * v_head), lambda b, j, i: (b, i, j)),
        out_shape=jax.ShapeDtypeStruct((bsz, s, pairs * 2 * v_head), BF16),
        scratch_shapes=[pltpu.VMEM((2, 1, 1), F32)],
        compiler_params=_params("parallel", "parallel", "arbitrary"),
        name="prompt_attention",
    )(q, kcat, vt)


def _head_proj_kernel(a_ref, w_ref, o_ref):
    o_ref[...] = jnp.dot(a_ref[...].astype(BF16), w_ref[...], preferred_element_type=F32).astype(o_ref.dtype)


def head_proj_lanes(a, w, out_dtype, name):
    bsz = a.shape[0]
    nh, kk, n = w.shape
    return pl.pallas_call(
        _head_proj_kernel,
        grid=(nh,),
        in_specs=[pl.BlockSpec((bsz, kk), lambda h: (0, h)), pl.BlockSpec((None, kk, n), lambda h: (h, 0, 0))],
        out_specs=pl.BlockSpec((None, bsz, n), lambda h: (h, 0, 0)),
        out_shape=jax.ShapeDtypeStruct((nh, bsz, n), out_dtype),
        compiler_params=_params("parallel"),
        name=name,
    )(a, w)


def _paged_attn_kernel(pt_ref, ql_ref, qp_ref, cn_ref, kn_ref, ckv_hbm, kpe_hbm, o_ref,
                       ckv_buf, kpe_buf, s_ref, cb0, p0, w0, cb1, p1, w1, sem, *, n_seq, n_pages, ps, chunk):
    step = pl.program_id(0)
    slot = step % 2
    nh, c_lat = ql_ref.shape
    sets = ((cb0, p0, w0), (cb1, p1, w1))

    def page_copies(bi, sl):
        out = []
        for pg in range(n_pages):
            page = pt_ref[bi * n_pages + pg]
            rows = pl.ds(pg * ps, ps)
            out.append(pltpu.make_async_copy(ckv_hbm.at[page], ckv_buf.at[sl, rows, :], sem.at[0, sl]))
            out.append(pltpu.make_async_copy(kpe_hbm.at[page], kpe_buf.at[sl, :, rows], sem.at[1, sl]))
        return out

    @pl.when(step == 0)
    def _():
        for cp in page_copies(0, 0):
            cp.start()
        cb1[...] = jnp.zeros_like(cb1)
        p1[...] = jnp.zeros_like(p1)
        w1[...] = jnp.concatenate([jnp.zeros((nh, c_lat), F32), jnp.ones((nh, LANES), F32)], axis=1)

    @pl.when(step + 1 < n_seq)
    def _():
        for cp in page_copies(step + 1, 1 - slot):
            cp.start()

    @pl.when(step < n_seq)
    def _():
        for cp in page_copies(step, slot):
            cp.wait()

    n_chunks = n_pages * ps // chunk
    chunks = [slice(c * chunk, (c + 1) * chunk) for c in range(n_chunks)]

    def main(par):
        (cb_w, p_w, w_w), (cb_r, p_r, w_r) = sets[par], sets[1 - par]
        ql = ql_ref[...]
        qp = qp_ref[...]
        w_prev = w_r[...]
        accs = [w_prev[:, :c_lat], jnp.zeros((nh, c_lat), F32)]
        s_lat = []
        for c, cs in enumerate(chunks):
            ck = ckv_buf[par, cs, :].astype(BF16)
            cb_w[cs, :] = ck
            s_lat.append(lax.dot_general(ql, ck, NT_DIMS, preferred_element_type=F32))
            accs[c % 2] += jnp.dot(p_r[:, cs], cb_r[cs, :], preferred_element_type=F32)
        o_ref[...] = (accs[0] + accs[1]) / w_prev[:, c_lat:c_lat + 1]

        for cs, sl in zip(chunks, s_lat):
            s_ref[:, cs] = sl + jnp.dot(qp, kpe_buf[par, :, cs].astype(BF16), preferred_element_type=F32)

        cn = cn_ref[...].astype(BF16).astype(F32)
        kn = kn_ref[...].astype(BF16).astype(F32)
        s_new = (jnp.sum(ql.astype(F32) * cn, axis=1, keepdims=True)
                 + jnp.sum(qp.astype(F32) * kn, axis=1, keepdims=True))
        s = s_ref[...]
        m = jnp.maximum(jnp.max(s, axis=1, keepdims=True), s_new)
        p = jnp.exp2(s - m)
        p_new = jnp.exp2(s_new - m)
        l = jnp.sum(p, axis=1, keepdims=True) + p_new
        p_w[...] = p.astype(BF16)
        w_w[...] = jnp.concatenate([p_new.astype(BF16).astype(F32) * cn, jnp.broadcast_to(l, (nh, LANES))], axis=1)

    for par in range(2):
        pl.when(step % 2 == par)(functools.partial(main, par))


def paged_attention(q_lat, q_pe, ckv_new, kpe_new, cache_ckv, cache_kpe_t, page_table):
    bsz, nh, c = q_lat.shape
    r = q_pe.shape[-1]
    n_pages = page_table.shape[1]
    ps = cache_ckv.shape[1]
    past = n_pages * ps
    chunk = min(PAGED_CHUNK, past)
    per_b = lambda rows, w: pl.BlockSpec((None, rows, w), lambda s_, pt: (jnp.minimum(s_, bsz - 1), 0, 0))
    return pl.pallas_call(
        functools.partial(_paged_attn_kernel, n_seq=bsz, n_pages=n_pages, ps=ps, chunk=chunk),
        grid_spec=pltpu.PrefetchScalarGridSpec(
            num_scalar_prefetch=1,
            grid=(bsz + 1,),
            in_specs=[per_b(nh, c), per_b(nh, r), per_b(1, c), per_b(1, r),
                      pl.BlockSpec(memory_space=pl.ANY), pl.BlockSpec(memory_space=pl.ANY)],
            out_specs=pl.BlockSpec((None, nh, c), lambda s_, pt: (jnp.maximum(s_ - 1, 0), 0, 0)),
            scratch_shapes=[pltpu.VMEM((2, past, c), F32), pltpu.VMEM((2, r, past), F32),
                            pltpu.VMEM((nh, past), F32)]
            + 2 * [pltpu.VMEM((past, c), BF16), pltpu.VMEM((nh, past), BF16), pltpu.VMEM((nh, c + LANES), F32)]
            + [pltpu.SemaphoreType.DMA((2, 2))]),
        out_shape=jax.ShapeDtypeStruct((bsz, nh, c), F32),
        compiler_params=_params("arbitrary"),
        name="paged_attention",
    )(page_table.reshape(-1), q_lat, q_pe, ckv_new, kpe_new, cache_ckv, cache_kpe_t)


def _rope_tables(pos, rope, lo, scale, passthrough):
    half = rope // 2
    freq = ROPE_THETA ** (-jnp.arange(half, dtype=F32) / half)
    ang = pos.astype(F32)[:, None] * freq[None, :]
    cos, sin = jnp.cos(ang), jnp.sin(ang)
    n = pos.shape[0]
    zeros = lambda w: jnp.zeros((n, w), F32)
    tail = LANES - lo - rope
    c = jnp.concatenate([jnp.full((n, lo), passthrough, F32), cos, cos, zeros(tail)], axis=1)
    sa = jnp.concatenate([zeros(lo), -sin, zeros(half + tail)], axis=1)
    sb = jnp.concatenate([zeros(lo + half), sin, zeros(tail)], axis=1)
    return c * scale, sa * scale, sb * scale


def _prep_weights(w_up, w_down, w_m_in, w_m_out, w_dkv, w_uk, w_uv, w_dq, w_uq, w_o, dims):
    heads, dk, dv = dims["m_heads"], dims["m_dk"], dims["m_dv"]
    nh, nope, rope, kv_lora = dims["mla_heads"], dims["qk_nope"], dims["qk_rope"], dims["kv_lora"]
    ff = w_down.shape[1]
    qd, vd = heads * dk, heads * dv
    d = w_up.shape[1]
    pw = {}
    pw["w_up_a"] = w_up[:, :, :ff].astype(BF16)
    pw["w_up_g"] = w_up[:, :, ff:].astype(BF16)
    pw["w_down"] = w_down.astype(BF16)
    pw["w_m_q"] = w_m_in[:, :, :qd].astype(BF16)
    pw["w_m_k"] = w_m_in[:, :, qd:2 * qd].astype(BF16)
    pw["w_m_v"] = w_m_in[:, :, 2 * qd:2 * qd + vd].astype(BF16)
    pw["w_m_o"] = w_m_in[:, :, 2 * qd + vd:2 * qd + 2 * vd].astype(BF16)
    gates = w_m_in[:, :, 2 * qd + 2 * vd:]
    pw["w_m_g"] = jnp.pad(gates, ((0, 0), (0, 0), (0, LANES - gates.shape[-1]))).astype(BF16)
    pw["w_m_out"] = w_m_out.astype(BF16)
    pw["w_dkv"] = jnp.pad(w_dkv, ((0, 0), (0, LANES - rope))).astype(BF16)
    pw["w_uk_r"] = jnp.pad(w_uk, ((0, 0), (0, 0), (0, LANES - nope))).reshape(kv_lora, nh * LANES).astype(BF16)
    v_head = w_uv.shape[2]
    pw["w_uv_r"] = jnp.pad(w_uv, ((0, 0), (0, 0), (0, LANES - v_head))).reshape(kv_lora, nh * LANES).T.astype(BF16)
    nb = w_uq.shape[0]
    wq = w_uq.reshape(nb, w_uq.shape[1], nh, nope + rope)
    pw["w_uq_r"] = jnp.pad(wq, ((0, 0), (0, 0), (0, 0), (0, LANES - nope - rope))).reshape(
        nb, w_uq.shape[1], nh * LANES).astype(BF16)
    half = rope // 2
    partner = jnp.concatenate([-wq[..., nope + half:], wq[..., nope:nope + half]], axis=-1)
    pw["w_uq_rot"] = jnp.pad(partner, ((0, 0), (0, 0), (0, 0), (nope, LANES - nope - rope))).reshape(
        nb, w_uq.shape[1], nh * LANES).astype(BF16)
    pw["w_dq"] = w_dq.astype(BF16)
    pw["w_o"] = w_o.astype(BF16)
    wukt = jnp.transpose(w_uk, (1, 2, 0))
    pw["w_uk_t"] = jnp.pad(wukt, ((0, 0), (0, LANES - nope), (0, 0))).astype(BF16)
    pw["w_uv_t"] = jnp.transpose(w_uv, (1, 0, 2)).astype(BF16)
    return pw


def _trunk(x, mods, mods_kv, pos, conv_bufs, m_states, kv_past, pw, small, dims):
    heads, dk, dv = dims["m_heads"], dims["m_dk"], dims["m_dv"]
    nh, nope, rope, kv_lora, v_head = (dims["mla_heads"], dims["qk_nope"], dims["qk_rope"],
                                       dims["kv_lora"], dims["v_head"])
    depth, n_a = dims["depth"], dims["n_a"]
    is_prompt = kv_past is None
    gq, r, d = x.shape
    att_scale = (nope + rope) ** -0.5 * LOG2_E
    new_c, new_n, new_m, new_conv = [], [], [], []
    ckv = kpe = kcat = vv = None
    y = None
    for layer in range(depth):
        sh1, sc1, gt1, sh2, sc2, gt2 = mods[layer]
        g1 = small["g_norm1"][layer][None, :]
        g2 = small["g_norm2"][layer][None, :]
        if layer == n_a:
            sh_kv, sc_kv = mods_kv
            tabs = _rope_tables(pos, rope, 0, 1.0, 0.0)
            if is_prompt:
                ckv, kpe, kcat, vv = shared_latent(x, small["g_kv_in"][None, :], sh_kv, sc_kv, pw["w_dkv"],
                                                   small["g_kv"][None, :], tabs, kv_lora, rope,
                                                   pw["w_uk_r"], pw["w_uv_r"], v_head)
            else:
                ckv, kpe = shared_latent(x, small["g_kv_in"][None, :], sh_kv, sc_kv, pw["w_dkv"],
                                         small["g_kv"][None, :], tabs, kv_lora, rope)
        if layer < n_a:
            ws = [pw["w_m_q"][layer], pw["w_m_k"][layer], pw["w_m_v"][layer], pw["w_m_o"][layer], pw["w_m_g"][layer]]
            if is_prompt:
                q, k, v, o, gates = norm_mod_proj(x, g1, sh1, sc1, ws, [BF16, F32, BF16, F32, F32],
                                                  [dk ** -0.5, 1.0, 1.0, 1.0, 1.0], "mlstm_in_proj")
                hh, c_st, n_st, m_st = mlstm_chunkwise(q, k, v, o, gates[..., :2 * heads],
                                                       small["b_m_gates"][layer], small["g_m_head"][layer],
                                                       heads, dk, dv)
                n_st = n_st.reshape(gq, heads, dk)
                m_st = m_st.reshape(gq, heads)
            else:
                q, k, v, o, gates = norm_mod_proj(x, g1, sh1, sc1, ws, [F32] * 5,
                                                  [dk ** -0.5, 1.0, 1.0, 1.0, 1.0], "mlstm_in_proj_tok")
                hh, c_st, n_st, m_st = mlstm_step(q[0], k[0], v[0], o[0], gates[0, :, :2 * heads],
                                                  small["b_m_gates"][layer], small["g_m_head"][layer],
                                                  m_states[0][layer], m_states[1][layer], m_states[2][layer],
                                                  heads, dk, dv)
                hh = hh[None]
            new_c.append(c_st)
            new_n.append(n_st)
            new_m.append(m_st)
            x = resid_proj(hh, pw["w_m_out"][layer], x, gt1, "mlstm_out_proj")
        else:
            j = layer - n_a
            q_cos, q_msin, q_sin = _rope_tables(pos, rope, nope, att_scale, 1.0)
            qh = mla_queries(x, g1, sh1, sc1, pw["w_dq"][j], small["g_q"][j][None, :], pw["w_uq_r"][j],
                             pw["w_uq_rot"][j], q_cos, q_sin - q_msin)
            if is_prompt:
                att = prompt_attention(qh, kcat, vv, v_head)
            else:
                bsz = r
                q2 = qh[0]
                q_lat = head_proj_lanes(q2, pw["w_uk_t"], BF16, "absorb_q")
                q_lat = jnp.swapaxes(q_lat, 0, 1)
                q_pe = q2.reshape(bsz, nh, LANES)[:, :, nope:nope + rope]
                o_lat = paged_attention(q_lat, q_pe, ckv[0][:, None, :], kpe[0][:, None, :],
                                        kv_past[0], kv_past[1], kv_past[2])
                o_lat = o_lat.reshape(bsz, nh * kv_lora)
                att = head_proj_lanes(o_lat, pw["w_uv_t"], F32, "unabsorb_o")
                att = jnp.swapaxes(att, 0, 1).reshape(1, bsz, nh * v_head)
            x = resid_proj(att, pw["w_o"][j], x, gt1, "mla_out_proj")
        final = layer == depth - 1
        gf = small["g_final"][None, :]
        if is_prompt:
            x, a_last = conv_ffn_seq(x, conv_bufs[layer], g2, sh2, sc2, gt2, pw["w_up_a"][layer], pw["w_up_g"][layer],
                                     small["w_conv"][layer], small["b_conv"][layer][None, :], pw["w_down"][layer],
                                     gf, final, "conv_ffn_seq")
            new_conv.append(a_last[:, 6:8, :])
        else:
            buf = conv_bufs[layer]
            x2, a_new = conv_ffn_tok(x[0], buf[:, 0, :], buf[:, 1, :], g2, sh2[0], sc2[0], gt2[0],
                                     pw["w_up_a"][layer], pw["w_up_g"][layer], small["w_conv"][layer],
                                     small["b_conv"][layer][None, :], pw["w_down"][layer], gf, final, "conv_ffn_tok")
            x = x2[None]
            new_conv.append(jnp.stack([buf[:, 1, :], a_new], axis=1))
    return x, jnp.stack(new_c), jnp.stack(new_n), jnp.stack(new_m), jnp.stack(new_conv), ckv, kpe


def kernel(x_prompt, x_sample, state_mlstm_C, state_mlstm_n, state_mlstm_m, state_conv, cache_ckv, cache_kpe,
           page_table, c_prompt, c_sample, g_norm1, g_norm2, w_ada, b_ada, w_up, w_conv, b_conv, w_down,
           w_m_in, b_m_gates, g_m_head, w_m_out, g_kv_in, w_ada_kv, b_ada_kv, w_dkv, g_kv, w_uk, w_uv,
           w_dq, g_q, w_uq, w_o, g_final):
    bp, s, d = x_prompt.shape
    bs, t, _ = x_sample.shape
    assert t == 1, "the sample path handles one new token per sequence"
    depth = w_ada.shape[0]
    n_a = w_m_in.shape[0]
    heads, dv = g_m_head.shape[1], g_m_head.shape[2]
    dk = state_mlstm_C.shape[3]
    kv_lora, nh, nope = w_uk.shape
    v_head = w_uv.shape[2]
    rope = w_dkv.shape[1] - kv_lora
    ff = w_down.shape[1]
    dims = dict(m_heads=heads, m_dk=dk, m_dv=dv, mla_heads=nh, qk_nope=nope, qk_rope=rope, kv_lora=kv_lora,
                v_head=v_head, depth=depth, n_a=n_a)
    past_len = page_table.shape[1] * cache_ckv.shape[1]

    pw = _prep_weights(w_up, w_down, w_m_in, w_m_out, w_dkv, w_uk, w_uv, w_dq, w_uq, w_o, dims)
    small = dict(g_norm1=g_norm1, g_norm2=g_norm2, w_conv=w_conv, b_conv=b_conv, b_m_gates=b_m_gates,
                 g_m_head=g_m_head, g_kv_in=g_kv_in, g_kv=g_kv, g_q=g_q, g_final=g_final)

    c_all = jnp.concatenate([c_prompt, c_sample], axis=0)
    mod = ada_mod(c_all, w_ada, b_ada[:, None, :])
    mod_kv = ada_mod(c_all, w_ada_kv[None], b_ada_kv[None, None, :])[0]

    def split(m, n, lo, hi, per_row):
        parts = jnp.split(m[lo:hi], n, axis=-1)
        return [p[None] if per_row else p[:, None, :] for p in parts]

    mods_p = [split(mod[l], 6, 0, bp, False) for l in range(depth)]
    mods_s = [split(mod[l], 6, bp, bp + bs, True) for l in range(depth)]
    kv_p = split(mod_kv, 2, 0, bp, False)
    kv_s = split(mod_kv, 2, bp, bp + bs, True)

    conv0 = [jnp.zeros((bp, 8, ff), F32)] * depth
    y_p, c_p, n_p, m_p, conv_p, ckv_p, kpe_p = _trunk(
        x_prompt, mods_p, kv_p, jnp.arange(s), conv0, None, None, pw, small, dims)

    pos_s = jnp.full((bs,), past_len, jnp.int32)
    y_s, c_s, n_s, m_s, conv_s, ckv_s, kpe_s = _trunk(
        x_sample.reshape(1, bs, d), mods_s, kv_s, pos_s, state_conv,
        (state_mlstm_C, state_mlstm_n, state_mlstm_m), (cache_ckv, jnp.swapaxes(cache_kpe, 1, 2), page_table), pw, small, dims)

    return (y_p, y_s.reshape(bs, 1, d), c_p, n_p, m_p, conv_p, ckv_p, kpe_p,
            c_s, n_s, m_s, conv_s, ckv_s.reshape(bs, 1, kv_lora), kpe_s.reshape(bs, 1, rope))
```

```python
import functools
import math
from typing import NamedTuple

import jax
import jax.numpy as jnp
from jax import lax
from jax.experimental import pallas as pl
from jax.experimental.pallas import tpu as pltpu

F32 = jnp.float32
BF16 = jnp.bfloat16

NORM_EPS = 1e-6
ROPE_THETA = 10000.0
LANES = 128
BF16_ROWS = 16
VMEM_LIMIT = 56 * 1024 * 1024

ROW_TILE = 512
FFN_ROW_TILE = 512
MLSTM_CHUNK = 256
ATTN_TILE = 2048
ATTN_BLOCK = 512
FFN_CHUNK = 256
PAGED_CHUNK = 512
LOG2_E = math.log2(math.e)
SOFTMAX_MIN_ROW_SUM = 2.0 ** -60
STEP_BATCH = 8

NT_DIMS = (((1,), (1,)), ((), ()))
TN_DIMS = (((0,), (0,)), ((), ()))


def _params(*sem):
    return pltpu.CompilerParams(dimension_semantics=sem, vmem_limit_bytes=VMEM_LIMIT)


def _const_spec(shape):
    nd = len(shape)
    return pl.BlockSpec(shape, lambda *_: (0,) * nd, pipeline_mode=pl.Buffered(1))


class WeightView(NamedTuple):
    array: jax.Array
    block: tuple
    index: tuple

    @property
    def shape(self):
        return tuple(d for d in self.block if d is not None)

    def spec(self):
        return pl.BlockSpec(self.block, lambda *_: self.index, pipeline_mode=pl.Buffered(1))

    def chunk_spec(self, axis, size):
        n = self.block[axis] // size
        block = tuple(size if a == axis else d for a, d in enumerate(self.block))
        index = lambda c: tuple(i * n + c if a == axis else i for a, i in enumerate(self.index))
        return pl.BlockSpec(block, index)


def _rms(x, g):
    return x * lax.rsqrt(jnp.mean(x * x, axis=-1, keepdims=True) + NORM_EPS) * g


def _norm_mod(x, g, sh, sc):
    return _rms(x, g) * (1.0 + sc) + sh


def _log_sigmoid(x):
    return jnp.minimum(x, 0.0) - jnp.log(1.0 + jnp.exp(-jnp.abs(x)))


def _rope3(t, c, sa, sb):
    return t * c + pltpu.roll(t, LANES - 16, axis=1) * sa + pltpu.roll(t, 16, axis=1) * sb


def _row_spec(tm, width):
    return pl.BlockSpec((None, tm, width), lambda g, i: (g, i, 0))


def _mod_spec(per_row, tm, width):
    if per_row:
        return pl.BlockSpec((None, tm, width), lambda g, i: (g, i, 0))
    return pl.BlockSpec((None, 1, width), lambda g, i: (g, 0, 0))


def _ada_kernel(c_ref, w_ref, b_ref, o_ref):
    c = c_ref[...]
    a = (c * jax.nn.sigmoid(c)).astype(BF16)
    o_ref[...] = jnp.dot(a, w_ref[...].astype(BF16), preferred_element_type=F32) + b_ref[...]


def ada_mod(c, w, b, tn=1024):
    m, d = c.shape
    nl, _, n = w.shape
    tn = min(tn, n)
    return pl.pallas_call(
        _ada_kernel,
        grid=(nl, n // tn),
        in_specs=[pl.BlockSpec((m, d), lambda l, j: (0, 0)),
                  pl.BlockSpec((None, d, tn), lambda l, j: (l, 0, j)),
                  pl.BlockSpec((None, 1, tn), lambda l, j: (l, 0, j))],
        out_specs=pl.BlockSpec((None, m, tn), lambda l, j: (l, 0, j)),
        out_shape=jax.ShapeDtypeStruct((nl, m, n), F32),
        compiler_params=_params("parallel", "parallel"),
        name="ada_mod",
    )(c, w, b)


def _proj_kernel(x_ref, g_ref, sh_ref, sc_ref, *refs, scales):
    n = len(scales)
    hn = _norm_mod(x_ref[...], g_ref[...], sh_ref[...], sc_ref[...]).astype(BF16)
    for w_ref, o_ref, s in zip(refs[:n], refs[n:], scales):
        acc = jnp.dot(hn, w_ref[...], preferred_element_type=F32)
        if s != 1.0:
            acc = acc * s
        o_ref[...] = acc.astype(o_ref.dtype)


def norm_mod_proj(x, g, sh, sc, ws, out_dtypes, scales, name):
    gq, r, d = x.shape
    tm = min(ROW_TILE, r)
    per_row = sh.shape[1] != 1
    in_specs = [_row_spec(tm, d), _const_spec((1, d)),
                _mod_spec(per_row, tm, d), _mod_spec(per_row, tm, d)]
    in_specs += [w.spec() for w in ws]
    return pl.pallas_call(
        functools.partial(_proj_kernel, scales=tuple(scales)),
        grid=(gq, r // tm),
        in_specs=in_specs,
        out_specs=[_row_spec(tm, w.shape[1]) for w in ws],
        out_shape=[jax.ShapeDtypeStruct((gq, r, w.shape[1]), dt) for w, dt in zip(ws, out_dtypes)],
        compiler_params=_params("parallel", "parallel"),
        name=name,
    )(x, g, sh, sc, *[w.array for w in ws])


def _resid_kernel(a_ref, w_ref, x_ref, gt_ref, o_ref):
    mix = jnp.dot(a_ref[...].astype(BF16), w_ref[...], preferred_element_type=F32)
    o_ref[...] = x_ref[...] + gt_ref[...] * mix


def resid_proj(a, w, x, gt, name):
    gq, r, d = x.shape
    k = a.shape[-1]
    tm = min(ROW_TILE, r)
    per_row = gt.shape[1] != 1
    return pl.pallas_call(
        _resid_kernel,
        grid=(gq, r // tm),
        in_specs=[_row_spec(tm, k), _const_spec(w.shape), _row_spec(tm, d), _mod_spec(per_row, tm, d)],
        out_specs=_row_spec(tm, d),
        out_shape=jax.ShapeDtypeStruct((gq, r, d), F32),
        compiler_params=_params("parallel", "parallel"),
        name=name,
    )(a, w, x, gt)


def _mlstm_chunk_kernel(q_ref, k_ref, v_ref, o_ref, gr_ref, gc_ref, bgc_ref, bgr_ref, gh_ref,
                        hh_ref, c_ref, n_ref, m_ref, *, heads, dk, dv):
    ci = pl.program_id(1)
    chunk = q_ref.shape[0]

    @pl.when(ci == 0)
    def _():
        c_ref[...] = jnp.zeros_like(c_ref)
        n_ref[...] = jnp.zeros_like(n_ref)
        m_ref[...] = jnp.zeros_like(m_ref)

    gates_r = gr_ref[...] + bgc_ref[...]
    gates_c = gc_ref[...] + bgr_ref[...]
    row = lax.broadcasted_iota(jnp.int32, (chunk, chunk), 0)
    col = lax.broadcasted_iota(jnp.int32, (chunk, chunk), 1)
    causal = col <= row

    s_raw, q_c = [], []
    for h in range(heads):
        q = q_ref[:, h * dk:(h + 1) * dk]
        s_raw.append(lax.dot_general(q, k_ref[:, h * dk:(h + 1) * dk].astype(BF16), NT_DIMS,
                                     preferred_element_type=F32))
        q_c.append(jnp.dot(q, c_ref[h].astype(BF16), preferred_element_type=F32))

    for h in range(heads):
        q = q_ref[:, h * dk:(h + 1) * dk]
        k = k_ref[:, h * dk:(h + 1) * dk]
        v = v_ref[:, h * dv:(h + 1) * dv]
        li_r = gates_r[h:h + 1, :]
        lf_r = _log_sigmoid(gates_r[heads + h:heads + h + 1, :])
        li_c = gates_c[:, h:h + 1]
        lf_c = _log_sigmoid(gates_c[:, heads + h:heads + h + 1])

        lf_low = jnp.where(causal, lf_r, 0.0)
        half = chunk // 2
        b_c = jnp.sum(lf_low[:, :half] + lf_low[:, half:], axis=1, keepdims=True)
        b_r = jnp.sum(jnp.where(row <= col, lf_c, 0.0), axis=0, keepdims=True)
        g = jnp.sum(lf_r, axis=1, keepdims=True)

        m_prev = m_ref[h]
        c_prev = c_ref[h]
        n_prev = n_ref[h]

        a_c = g - b_c + li_c
        m_loc = jnp.max(a_c, axis=0, keepdims=True)
        kw = k * jnp.exp(a_c - m_loc)
        c_loc = lax.dot_general(kw.astype(BF16), v, TN_DIMS, preferred_element_type=F32)
        n_loc = jnp.sum(kw, axis=0, keepdims=True)

        dmat = jnp.where(causal, b_c - b_r + li_r, -jnp.inf)
        w0 = b_c + m_prev
        m_s = jnp.maximum(w0, jnp.max(dmat, axis=1, keepdims=True))
        w_inter = jnp.exp(w0 - m_s)
        s = s_raw[h] * jnp.exp(dmat - m_s)
        num = w_inter * q_c[h] + jnp.dot(s.astype(BF16), v, preferred_element_type=F32)
        den = (w_inter * jnp.sum(q.astype(F32) * n_prev, axis=1, keepdims=True)
               + jnp.sum(s, axis=1, keepdims=True))
        hval = num / jnp.maximum(jnp.abs(den), jnp.exp(-m_s))
        hn = _rms(hval, gh_ref[h:h + 1, :])
        gate = jax.nn.sigmoid(o_ref[:, h * dv:(h + 1) * dv])
        hh_ref[:, h * dv:(h + 1) * dv] = (hn * gate).astype(hh_ref.dtype)

        m_new = jnp.maximum(g + m_prev, m_loc)
        fw = jnp.exp(g + m_prev - m_new)
        lw = jnp.exp(m_loc - m_new)
        c_ref[h] = fw * c_prev + lw * c_loc
        n_ref[h] = fw * n_prev + lw * n_loc
        m_ref[h] = m_new


def mlstm_chunkwise(q, k, v, o, gates, b_gates, g_head, heads, dk, dv):
    bsz, s, _ = q.shape
    chunk = min(MLSTM_CHUNK, s)
    gates_r = jnp.swapaxes(gates, 1, 2)
    g2 = 2 * heads
    return pl.pallas_call(
        functools.partial(_mlstm_chunk_kernel, heads=heads, dk=dk, dv=dv),
        grid=(bsz, s // chunk),
        in_specs=[_row_spec(chunk, heads * dk), _row_spec(chunk, heads * dk),
                  _row_spec(chunk, heads * dv), _row_spec(chunk, heads * dv),
                  pl.BlockSpec((None, g2, chunk), lambda b, c: (b, 0, c)),
                  _row_spec(chunk, g2),
                  _const_spec((g2, 1)), _const_spec((1, g2)), _const_spec((heads, dv))],
        out_specs=[_row_spec(chunk, heads * dv),
                   pl.BlockSpec((None, heads, dk, dv), lambda b, c: (b, 0, 0, 0)),
                   pl.BlockSpec((None, heads, 1, dk), lambda b, c: (b, 0, 0, 0)),
                   pl.BlockSpec((None, heads, 1, 1), lambda b, c: (b, 0, 0, 0))],
        out_shape=[jax.ShapeDtypeStruct((bsz, s, heads * dv), BF16),
                   jax.ShapeDtypeStruct((bsz, heads, dk, dv), F32),
                   jax.ShapeDtypeStruct((bsz, heads, 1, dk), F32),
                   jax.ShapeDtypeStruct((bsz, heads, 1, 1), F32)],
        compiler_params=_params("parallel", "arbitrary"),
        name="mlstm_chunkwise",
    )(q, k, v, o, gates_r, gates, b_gates.reshape(g2, 1), b_gates.reshape(1, g2), g_head)


def _mlstm_step_kernel(q_ref, k_ref, v_ref, o_ref, g_ref, bg_ref, gh_ref, c_ref, n_ref, m_ref,
                       hh_ref, co_ref, no_ref, mo_ref, *, heads, dk, dv):
    nb = q_ref.shape[0]
    gates = g_ref[...] + bg_ref[...]
    li = gates[:, :heads]
    lf = _log_sigmoid(gates[:, heads:])
    m_st = m_ref[...]
    m_new = jnp.maximum(lf + m_st, li)
    fw_all = jnp.exp(lf + m_st - m_new)
    iw_all = jnp.exp(li - m_new)
    floor_all = jnp.exp(-m_new)
    mo_ref[...] = m_new
    eye = lax.broadcasted_iota(jnp.int32, (dk, dk), 0) == lax.broadcasted_iota(jnp.int32, (dk, dk), 1)

    def to_col(r):
        return jnp.sum(jnp.where(eye, r, 0.0), axis=1, keepdims=True)

    hr = range(heads)
    for b in range(nb):
        q_r = [q_ref[b:b + 1, h * dk:(h + 1) * dk] for h in hr]
        k_r = [k_ref[b:b + 1, h * dk:(h + 1) * dk] for h in hr]
        k_c = [iw_all[b:b + 1, h:h + 1] * to_col(k_r[h]) for h in hr]
        c_new = [fw_all[b:b + 1, h:h + 1] * c_ref[b, h] + k_c[h] * v_ref[b:b + 1, h * dv:(h + 1) * dv] for h in hr]
        num = [jnp.dot(q_r[h].astype(BF16), c_new[h].astype(BF16), preferred_element_type=F32) for h in hr]
        n_new = [fw_all[b:b + 1, h:h + 1] * n_ref[b, h:h + 1, :] + iw_all[b:b + 1, h:h + 1] * k_r[h] for h in hr]
        den = [jnp.sum(q_r[h] * n_new[h], axis=1, keepdims=True) for h in hr]
        hval = [num[h] / jnp.maximum(jnp.abs(den[h]), floor_all[b:b + 1, h:h + 1]) for h in hr]
        hn = [_rms(hval[h], gh_ref[h:h + 1, :]) for h in hr]
        for h in hr:
            co_ref[b, h] = c_new[h]
            no_ref[b, h:h + 1, :] = n_new[h]
            hh_ref[b:b + 1, h * dv:(h + 1) * dv] = hn[h] * jax.nn.sigmoid(o_ref[b:b + 1, h * dv:(h + 1) * dv])


def mlstm_step(q, k, v, o, gates, b_gates, g_head, c_st, n_st, m_st, heads, dk, dv):
    bsz = q.shape[0]
    nb = min(STEP_BATCH, bsz)
    g2 = 2 * heads
    rows = lambda w: pl.BlockSpec((nb, w), lambda i: (i, 0))
    return pl.pallas_call(
        functools.partial(_mlstm_step_kernel, heads=heads, dk=dk, dv=dv),
        grid=(bsz // nb,),
        in_specs=[rows(heads * dk), rows(heads * dk), rows(heads * dv), rows(heads * dv), rows(g2),
                  _const_spec((1, g2)), _const_spec((heads, dv)),
                  pl.BlockSpec((nb, heads, dk, dv), lambda i: (i, 0, 0, 0)),
                  pl.BlockSpec((nb, heads, dk), lambda i: (i, 0, 0)),
                  rows(heads)],
        out_specs=[rows(heads * dv),
                   pl.BlockSpec((nb, heads, dk, dv), lambda i: (i, 0, 0, 0)),
                   pl.BlockSpec((nb, heads, dk), lambda i: (i, 0, 0)),
                   rows(heads)],
        out_shape=[jax.ShapeDtypeStruct((bsz, heads * dv), F32),
                   jax.ShapeDtypeStruct((bsz, heads, dk, dv), F32),
                   jax.ShapeDtypeStruct((bsz, heads, dk), F32),
                   jax.ShapeDtypeStruct((bsz, heads), F32)],
        compiler_params=_params("parallel"),
        name="mlstm_step",
    )(q, k, v, o, gates, b_gates.reshape(1, g2), g_head, c_st, n_st, m_st)


def _ffn_seq_kernel(x_ref, xh_ref, mix_ref, mixh_ref, wm_ref, gtm_ref, buf_ref, g_ref, sh_ref, sc_ref, gt_ref,
                    wa_ref, wg_ref, wc_ref, bc_ref, wd_ref, gf_ref, o_ref, alast_ref, *, fc, final_norm):
    i = pl.program_id(1)
    tm = x_ref.shape[0]
    ff = wa_ref.shape[1]
    gtm, wm = gtm_ref[...], wm_ref[...]
    x = x_ref[...] + gtm * jnp.dot(mix_ref[...], wm, preferred_element_type=F32)
    halo_rows = mixh_ref.shape[0]
    xh = xh_ref[...] + gtm * jnp.dot(mixh_ref[...], wm, preferred_element_type=F32)[halo_rows - 8:]
    g, sh, sc = g_ref[...], sh_ref[...], sc_ref[...]
    hn = _norm_mod(x, g, sh, sc).astype(BF16)
    hh = _norm_mod(xh, g, sh, sc).astype(BF16)
    first = i == 0

    def up(c):
        cs = slice(c * fc, (c + 1) * fc)
        wa = wa_ref[:, cs]
        return (jnp.dot(hn, wa, preferred_element_type=F32),
                jnp.dot(hn, wg_ref[:, cs], preferred_element_type=F32),
                jnp.dot(hh, wa, preferred_element_type=F32))

    n_chunks = ff // fc
    acc = None
    nxt = up(0)
    for c in range(n_chunks):
        cs = slice(c * fc, (c + 1) * fc)
        a, gt, a_halo = nxt
        if c + 1 < n_chunks:
            nxt = up(c + 1)
        ext = jnp.concatenate([jnp.where(first, buf_ref[:, cs], a_halo), a], axis=0)
        conv = (bc_ref[:, cs] + ext[6:6 + tm, :] * wc_ref[0:1, cs]
                + ext[7:7 + tm, :] * wc_ref[1:2, cs] + a * wc_ref[2:3, cs])
        act = (conv * jax.nn.sigmoid(conv) * gt).astype(BF16)
        part = jnp.dot(act, wd_ref[cs, :], preferred_element_type=F32)
        acc = part if acc is None else acc + part
        alast_ref[:, cs] = a[tm - 8:tm, :]

    y = x + gt_ref[...] * acc
    if final_norm:
        y = _rms(y, gf_ref[...])
    o_ref[...] = y


def conv_ffn_seq(x, mix, w_mix, gt_mix, buf8, g, sh, sc, gt, w_up_a, w_up_g, w_conv, b_conv, w_down, g_final,
                 final_norm, name):
    bsz, s, d = x.shape
    ff = w_up_a.shape[1]
    kk = mix.shape[-1]
    tm = min(FFN_ROW_TILE, s)
    fc = min(FFN_CHUNK, ff)
    halo = lambda b, i: (b, jnp.maximum(i * (tm // 8) - 1, 0), 0)
    halo_bf16 = lambda b, i: (b, jnp.maximum(i * (tm // BF16_ROWS) - 1, 0), 0)
    return pl.pallas_call(
        functools.partial(_ffn_seq_kernel, fc=fc, final_norm=final_norm),
        grid=(bsz, s // tm),
        in_specs=[_row_spec(tm, d), pl.BlockSpec((None, 8, d), halo),
                  _row_spec(tm, kk), pl.BlockSpec((None, BF16_ROWS, kk), halo_bf16),
                  _const_spec(w_mix.shape), _mod_spec(False, tm, d),
                  pl.BlockSpec((None, 8, ff), lambda b, i: (b, 0, 0)),
                  _const_spec((1, d)), _mod_spec(False, tm, d), _mod_spec(False, tm, d), _mod_spec(False, tm, d),
                  w_up_a.spec(), w_up_g.spec(), _const_spec(w_conv.shape),
                  _const_spec(b_conv.shape), w_down.spec(), _const_spec((1, d))],
        out_specs=[_row_spec(tm, d), pl.BlockSpec((None, 8, ff), lambda b, i: (b, 0, 0))],
        out_shape=[jax.ShapeDtypeStruct((bsz, s, d), F32), jax.ShapeDtypeStruct((bsz, 8, ff), F32)],
        compiler_params=_params("parallel", "arbitrary"),
        name=name,
    )(x, x, mix, mix, w_mix, gt_mix, buf8, g, sh, sc, gt, w_up_a.array, w_up_g.array, w_conv, b_conv,
      w_down.array, g_final)


def _ffn_tok_kernel(x_ref, b0_ref, b1_ref, g_ref, sh_ref, sc_ref, gt_ref, wa_ref, wg_ref, wc_ref,
                    bc_ref, wd_ref, gf_ref, o_ref, a_ref, acc_ref, *, final_norm):
    c = pl.program_id(0)
    x = x_ref[...]
    hn = _norm_mod(x, g_ref[...], sh_ref[...], sc_ref[...]).astype(BF16)
    a = jnp.dot(hn, wa_ref[...], preferred_element_type=F32)
    gt = jnp.dot(hn, wg_ref[...], preferred_element_type=F32)
    a_ref[...] = a
    conv = bc_ref[...] + b0_ref[...] * wc_ref[0:1, :] + b1_ref[...] * wc_ref[1:2, :] + a * wc_ref[2:3, :]
    act = (conv * jax.nn.sigmoid(conv) * gt).astype(BF16)
    part = jnp.dot(act, wd_ref[...], preferred_element_type=F32)

    @pl.when(c == 0)
    def _():
        acc_ref[...] = part

    @pl.when(c > 0)
    def _():
        acc_ref[...] += part

    y = x + gt_ref[...] * acc_ref[...]
    if final_norm:
        y = _rms(y, gf_ref[...])
    o_ref[...] = y


def conv_ffn_tok(x, buf0, buf1, g, sh, sc, gt, w_up_a, w_up_g, w_conv, b_conv, w_down, g_final, final_norm, name):
    bsz, d = x.shape
    ff = w_up_a.shape[1]
    fc = min(FFN_CHUNK, ff)
    full = lambda w: pl.BlockSpec((bsz, w), lambda c: (0, 0))
    cols = lambda r: pl.BlockSpec((r, fc), lambda c: (0, c))
    up_axis, down_axis = len(w_up_a.block) - 1, len(w_down.block) - 2
    return pl.pallas_call(
        functools.partial(_ffn_tok_kernel, final_norm=final_norm),
        grid=(ff // fc,),
        in_specs=[full(d), cols(bsz), cols(bsz), _const_spec((1, d)), full(d), full(d), full(d),
                  w_up_a.chunk_spec(up_axis, fc), w_up_g.chunk_spec(up_axis, fc), cols(w_conv.shape[0]), cols(1),
                  w_down.chunk_spec(down_axis, fc), _const_spec((1, d))],
        out_specs=[full(d), cols(bsz)],
        out_shape=[jax.ShapeDtypeStruct((bsz, d), F32), jax.ShapeDtypeStruct((bsz, ff), F32)],
        scratch_shapes=[pltpu.VMEM((bsz, d), F32)],
        compiler_params=_params("arbitrary"),
        name=name,
    )(x, buf0, buf1, g, sh, sc, gt, w_up_a.array, w_up_g.array, w_conv, b_conv, w_down.array, g_final)


def _latent_kernel(x_ref, g_ref, sh_ref, sc_ref, w_ref, gkv_ref, rc_ref, ra_ref, rb_ref, *refs,
                   kv_lora, rope, with_kv):
    if with_kv:
        wuk_ref, wuv_ref, one_ref, ckv_ref, kpe_ref, kcat_ref, v_ref = refs
    else:
        ckv_ref, kpe_ref = refs
    hn = _norm_mod(x_ref[...], g_ref[...], sh_ref[...], sc_ref[...]).astype(BF16)
    lat = jnp.dot(hn, w_ref[...], preferred_element_type=F32)
    ckv = _rms(lat[:, :kv_lora], gkv_ref[...])
    ckv_ref[...] = ckv
    kpe = _rope3(lat[:, kv_lora:kv_lora + LANES], rc_ref[...], ra_ref[...], rb_ref[...])
    kpe_ref[...] = kpe[:, :rope]
    if with_kv:
        cb = ckv.astype(BF16)
        kn = jnp.dot(cb, wuk_ref[...], preferred_element_type=F32)
        kpe_hi = pltpu.roll(kpe, 64, axis=1)
        for h in range(kn.shape[1] // LANES):
            hs = slice(h * LANES, (h + 1) * LANES)
            kcat_ref[:, hs] = (kn[:, hs] + kpe_hi).astype(BF16)
        v_t = lax.dot_general(wuv_ref[...], cb, NT_DIMS, preferred_element_type=F32)
        v_ref[...] = (v_t + one_ref[...]).astype(BF16)


def shared_latent(x, g, sh, sc, w_dkv_p, g_kv, tabs, kv_lora, rope, w_uk_r=None, w_uv_r=None, v_head=None):
    gq, r, d = x.shape
    tm = min(ROW_TILE, r)
    per_row = sh.shape[1] != 1
    with_kv = w_uk_r is not None
    tab_spec = pl.BlockSpec((tm, LANES), lambda g_, i: (i, 0))
    in_specs = [_row_spec(tm, d), _const_spec((1, d)), _mod_spec(per_row, tm, d), _mod_spec(per_row, tm, d),
                _const_spec(w_dkv_p.shape), _const_spec((1, kv_lora)), tab_spec, tab_spec, tab_spec]
    out_specs = [_row_spec(tm, kv_lora), _row_spec(tm, rope)]
    out_shape = [jax.ShapeDtypeStruct((gq, r, kv_lora), F32), jax.ShapeDtypeStruct((gq, r, rope), F32)]
    args = [x, g, sh, sc, w_dkv_p, g_kv, *tabs]
    if with_kv:
        vt_rows = w_uv_r.shape[0]
        ones_row = (jnp.arange(vt_rows) % LANES == v_head).astype(F32)[:, None]
        in_specs += [_const_spec(w_uk_r.shape), _const_spec(w_uv_r.shape), _const_spec(ones_row.shape)]
        out_specs += [_row_spec(tm, w_uk_r.shape[1]), pl.BlockSpec((None, vt_rows, tm), lambda g_, i: (g_, 0, i))]
        out_shape += [jax.ShapeDtypeStruct((gq, r, w_uk_r.shape[1]), BF16),
                      jax.ShapeDtypeStruct((gq, vt_rows, r), BF16)]
        args += [w_uk_r, w_uv_r, ones_row]
    return pl.pallas_call(
        functools.partial(_latent_kernel, kv_lora=kv_lora, rope=rope, with_kv=with_kv),
        grid=(gq, r // tm),
        in_specs=in_specs, out_specs=out_specs, out_shape=out_shape,
        compiler_params=_params("parallel", "parallel"),
        name="shared_latent_kv" if with_kv else "shared_latent",
    )(*args)


def _query_kernel(x_ref, g_ref, sh_ref, sc_ref, wdq_ref, gq_ref, wuq_ref, wrot_ref, rc_ref, rs_ref, q_ref):
    hn = _norm_mod(x_ref[...], g_ref[...], sh_ref[...], sc_ref[...]).astype(BF16)
    qd = jnp.dot(hn, wdq_ref[...], preferred_element_type=F32)
    qn = _rms(qd, gq_ref[...]).astype(BF16)
    qf = jnp.dot(qn, wuq_ref[...], preferred_element_type=F32)
    qr = jnp.dot(qn, wrot_ref[...], preferred_element_type=F32)
    rc, rs = rc_ref[...], rs_ref[...]
    for h in range(qf.shape[1] // LANES):
        hs = slice(h * LANES, (h + 1) * LANES)
        q_ref[:, hs] = (qf[:, hs] * rc + qr[:, hs] * rs).astype(q_ref.dtype)


def mla_queries(x, g, sh, sc, w_dq, g_q, w_uq_r, w_uq_rot, tab_cos, tab_sin):
    gq, r, d = x.shape
    tm = min(ROW_TILE, r)
    per_row = sh.shape[1] != 1
    tab_spec = pl.BlockSpec((tm, LANES), lambda g_, i: (i, 0))
    return pl.pallas_call(
        _query_kernel,
        grid=(gq, r // tm),
        in_specs=[_row_spec(tm, d), _const_spec((1, d)), _mod_spec(per_row, tm, d), _mod_spec(per_row, tm, d),
                  _const_spec(w_dq.shape), _const_spec(g_q.shape), _const_spec(w_uq_r.shape),
                  _const_spec(w_uq_rot.shape), tab_spec, tab_spec],
        out_specs=_row_spec(tm, w_uq_r.shape[1]),
        out_shape=jax.ShapeDtypeStruct((gq, r, w_uq_r.shape[1]), BF16),
        compiler_params=_params("parallel", "parallel"),
        name="mla_queries",
    )(x, g, sh, sc, w_dq, g_q, w_uq_r, w_uq_rot, tab_cos, tab_sin)


def _attn_kernel(q_ref, k_ref, vt_ref, o_ref, kn_ref, *, v_head, v_rows, blk):
    qi = pl.program_id(2)
    tq = q_ref.shape[0]
    n_full = qi * (tq // blk)
    base = pl.multiple_of(qi * tq, tq)
    heads = range(2)
    diag = [(h, base + c * blk, c * blk, True) for c in range(tq // blk) for h in heads]

    def scores(h, kstart, q_lo):
        hs = slice(h * LANES, (h + 1) * LANES)
        return lax.dot_general(k_ref[pl.ds(kstart, blk), hs], q_ref[q_lo:tq, hs], NT_DIMS,
                               preferred_element_type=F32)

    def values(h, kstart):
        return vt_ref[h * LANES:h * LANES + v_rows, pl.ds(kstart, blk)]

    def cols_from(x, q_lo, new_cols):
        return new_cols if q_lo == 0 else jnp.concatenate([x[:, :q_lo], new_cols], axis=1)

    def run(tasks, carry, update):
        carry = list(carry)
        nxt = scores(*tasks[0][:3])
        for t, (h, kstart, q_lo, masked) in enumerate(tasks):
            s = nxt
            if t + 1 < len(tasks):
                nxt = scores(*tasks[t + 1][:3])
            if masked:
                key = lax.broadcasted_iota(jnp.int32, s.shape, 0)
                qry = lax.broadcasted_iota(jnp.int32, s.shape, 1)
                s = jnp.where(key <= qry, s, -jnp.inf)
            carry[h] = update(h, s, kstart, q_lo, carry[h])
        return tuple(carry)

    def sweep(update, init):
        def full_chunk(kt, carry):
            kstart = pl.multiple_of(kt * blk, blk)
            return run([(h, kstart, 0, False) for h in heads], carry, update)
        return run(diag, lax.fori_loop(0, n_full, full_chunk, init), update)

    def store(accs):
        outs = []
        for acc in accs:
            out_t = acc[:v_head] / acc[v_head:v_head + 1]
            outs.append(jnp.transpose(out_t))
        o_ref[...] = jnp.concatenate(outs, axis=1).astype(o_ref.dtype)

    @pl.when(qi == 0)
    def _():
        kn_ref[...] = jnp.zeros_like(kn_ref)

    bounds = []
    for h in heads:
        hs = slice(h * LANES, (h + 1) * LANES)
        kt = k_ref[pl.ds(base, tq), hs].astype(F32)
        ksq = jnp.max(jnp.sum(kt * kt, axis=1, keepdims=True), axis=0, keepdims=True)
        kn_ref[h] = jnp.maximum(kn_ref[h], ksq)
        qf = q_ref[:, hs].astype(F32)
        qsq = jnp.transpose(jnp.broadcast_to(jnp.sum(qf * qf, axis=1, keepdims=True), (tq, LANES)))[0:1, :]
        bounds.append(jnp.sqrt(qsq * kn_ref[h]))

    def update_bounded(h, s, kstart, q_lo, acc):
        p = jnp.exp2(s - bounds[h][:, q_lo:]).astype(BF16)
        pv = jnp.dot(values(h, kstart), p, preferred_element_type=F32)
        return cols_from(acc, q_lo, acc[:, q_lo:] + pv)

    accs = sweep(update_bounded, tuple(jnp.zeros((v_rows, tq), F32) for _ in heads))
    store(accs)
    smallest = jnp.minimum(accs[0][v_head:v_head + 1], accs[1][v_head:v_head + 1])
    row_sums_ok = jnp.min(smallest) >= SOFTMAX_MIN_ROW_SUM

    @pl.when(jnp.logical_not(row_sums_ok))
    def _():
        def update_online(h, s, kstart, q_lo, state):
            m, acc = state
            m_old = m[:, q_lo:]
            m_new = jnp.maximum(m_old, jnp.max(s, axis=0, keepdims=True))
            p = jnp.exp2(s - m_new).astype(BF16)
            pv = jnp.dot(values(h, kstart), p, preferred_element_type=F32)
            return (cols_from(m, q_lo, m_new),
                    cols_from(acc, q_lo, jnp.exp2(m_old - m_new) * acc[:, q_lo:] + pv))

        init = tuple((jnp.full((1, tq), -jnp.inf, F32), jnp.zeros((v_rows, tq), F32)) for _ in heads)
        store([acc for _, acc in sweep(update_online, init)])


def prompt_attention(q, kcat, vt, v_head):
    bsz, s, hw = q.shape
    pairs = hw // (2 * LANES)
    assert 2 * v_head == LANES, "two heads fill one 128-lane output block"
    tq = min(ATTN_TILE, s)
    blk = min(ATTN_BLOCK, tq)
    v_rows = -(-(v_head + 1) // BF16_ROWS) * BF16_ROWS
    return pl.pallas_call(
        functools.partial(_attn_kernel, v_head=v_head, v_rows=v_rows, blk=blk),
        grid=(bsz, pairs, s // tq),
        in_specs=[pl.BlockSpec((None, tq, 2 * LANES), lambda b, j, i: (b, i, j)),
                  pl.BlockSpec((None, s, 2 * LANES), lambda b, j, i: (b, 0, j)),
                  pl.BlockSpec((None, 2 * LANES, s), lambda b, j, i: (b, j, 0))],
        out_specs=pl.BlockSpec((None, tq, 2 * v_head), lambda b, j, i: (b, i, j)),
        out_shape=jax.ShapeDtypeStruct((bsz, s, pairs * 2 * v_head), BF16),
        scratch_shapes=[pltpu.VMEM((2, 1, 1), F32)],
        compiler_params=_params("parallel", "parallel", "arbitrary"),
        name="prompt_attention",
    )(q, kcat, vt)


def _head_proj_kernel(a_ref, w_ref, o_ref):
    o_ref[...] = jnp.dot(a_ref[...].astype(BF16), w_ref[...], preferred_element_type=F32).astype(o_ref.dtype)


def head_proj_lanes(a, w, out_dtype, name):
    bsz = a.shape[0]
    nh, kk, n = w.shape
    return pl.pallas_call(
        _head_proj_kernel,
        grid=(nh,),
        in_specs=[pl.BlockSpec((bsz, kk), lambda h: (0, h)), pl.BlockSpec((None, kk, n), lambda h: (h, 0, 0))],
        out_specs=pl.BlockSpec((None, bsz, n), lambda h: (h, 0, 0)),
        out_shape=jax.ShapeDtypeStruct((nh, bsz, n), out_dtype),
        compiler_params=_params("parallel"),
        name=name,
    )(a, w)


def _paged_attn_kernel(pt_ref, ql_ref, qp_ref, cn_ref, kn_ref, ckv_hbm, kpe_hbm, o_ref,
                       ckv_buf, kpe_buf, s_ref, cb0, p0, w0, cb1, p1, w1, sem, *, n_seq, n_pages, ps, chunk):
    step = pl.program_id(0)
    slot = step % 2
    nh, c_lat = ql_ref.shape
    sets = ((cb0, p0, w0), (cb1, p1, w1))

    def page_copies(bi, sl):
        out = []
        for pg in range(n_pages):
            page = pt_ref[bi * n_pages + pg]
            rows = pl.ds(pg * ps, ps)
            out.append(pltpu.make_async_copy(ckv_hbm.at[page], ckv_buf.at[sl, rows, :], sem.at[0, sl]))
            out.append(pltpu.make_async_copy(kpe_hbm.at[page], kpe_buf.at[sl, :, rows], sem.at[1, sl]))
        return out

    @pl.when(step == 0)
    def _():
        for cp in page_copies(0, 0):
            cp.start()
        cb1[...] = jnp.zeros_like(cb1)
        p1[...] = jnp.zeros_like(p1)
        w1[...] = jnp.concatenate([jnp.zeros((nh, c_lat), F32), jnp.ones((nh, LANES), F32)], axis=1)

    @pl.when(step + 1 < n_seq)
    def _():
        for cp in page_copies(step + 1, 1 - slot):
            cp.start()

    @pl.when(step < n_seq)
    def _():
        for cp in page_copies(step, slot):
            cp.wait()

    n_chunks = n_pages * ps // chunk
    chunks = [slice(c * chunk, (c + 1) * chunk) for c in range(n_chunks)]

    def main(par):
        (cb_w, p_w, w_w), (cb_r, p_r, w_r) = sets[par], sets[1 - par]
        ql = ql_ref[...]
        qp = qp_ref[...]
        w_prev = w_r[...]
        accs = [w_prev[:, :c_lat], jnp.zeros((nh, c_lat), F32)]
        s_lat = []
        for c, cs in enumerate(chunks):
            ck = ckv_buf[par, cs, :].astype(BF16)
            cb_w[cs, :] = ck
            s_lat.append(lax.dot_general(ql, ck, NT_DIMS, preferred_element_type=F32))
            accs[c % 2] += jnp.dot(p_r[:, cs], cb_r[cs, :], preferred_element_type=F32)
        o_ref[...] = (accs[0] + accs[1]) / w_prev[:, c_lat:c_lat + 1]

        for cs, sl in zip(chunks, s_lat):
            s_ref[:, cs] = sl + jnp.dot(qp, kpe_buf[par, :, cs].astype(BF16), preferred_element_type=F32)

        cn = cn_ref[...].astype(BF16).astype(F32)
        kn = kn_ref[...].astype(BF16).astype(F32)
        s_new = (jnp.sum(ql.astype(F32) * cn, axis=1, keepdims=True)
                 + jnp.sum(qp.astype(F32) * kn, axis=1, keepdims=True))
        s = s_ref[...]
        m = jnp.maximum(jnp.max(s, axis=1, keepdims=True), s_new)
        p = jnp.exp2(s - m)
        p_new = jnp.exp2(s_new - m)
        l = jnp.sum(p, axis=1, keepdims=True) + p_new
        p_w[...] = p.astype(BF16)
        w_w[...] = jnp.concatenate([p_new.astype(BF16).astype(F32) * cn, jnp.broadcast_to(l, (nh, LANES))], axis=1)

    for par in range(2):
        pl.when(step % 2 == par)(functools.partial(main, par))


def paged_attention(q_lat, q_pe, ckv_new, kpe_new, cache_ckv, cache_kpe_t, page_table):
    bsz, nh, c = q_lat.shape
    r = q_pe.shape[-1]
    n_pages = page_table.shape[1]
    ps = cache_ckv.shape[1]
    past = n_pages * ps
    chunk = min(PAGED_CHUNK, past)
    per_b = lambda rows, w: pl.BlockSpec((None, rows, w), lambda s_, pt: (jnp.minimum(s_, bsz - 1), 0, 0))
    return pl.pallas_call(
        functools.partial(_paged_attn_kernel, n_seq=bsz, n_pages=n_pages, ps=ps, chunk=chunk),
        grid_spec=pltpu.PrefetchScalarGridSpec(
            num_scalar_prefetch=1,
            grid=(bsz + 1,),
            in_specs=[per_b(nh, c), per_b(nh, r), per_b(1, c), per_b(1, r),
                      pl.BlockSpec(memory_space=pl.ANY), pl.BlockSpec(memory_space=pl.ANY)],
            out_specs=pl.BlockSpec((None, nh, c), lambda s_, pt: (jnp.maximum(s_ - 1, 0), 0, 0)),
            scratch_shapes=[pltpu.VMEM((2, past, c), F32), pltpu.VMEM((2, r, past), F32),
                            pltpu.VMEM((nh, past), F32)]
            + 2 * [pltpu.VMEM((past, c), BF16), pltpu.VMEM((nh, past), BF16), pltpu.VMEM((nh, c + LANES), F32)]
            + [pltpu.SemaphoreType.DMA((2, 2))]),
        out_shape=jax.ShapeDtypeStruct((bsz, nh, c), F32),
        compiler_params=_params("arbitrary"),
        name="paged_attention",
    )(page_table.reshape(-1), q_lat, q_pe, ckv_new, kpe_new, cache_ckv, cache_kpe_t)


def _rope_tables(pos, rope, lo, scale, passthrough):
    half = rope // 2
    freq = ROPE_THETA ** (-jnp.arange(half, dtype=F32) / half)
    ang = pos.astype(F32)[:, None] * freq[None, :]
    cos, sin = jnp.cos(ang), jnp.sin(ang)
    n = pos.shape[0]
    zeros = lambda w: jnp.zeros((n, w), F32)
    tail = LANES - lo - rope
    c = jnp.concatenate([jnp.full((n, lo), passthrough, F32), cos, cos, zeros(tail)], axis=1)
    sa = jnp.concatenate([zeros(lo), -sin, zeros(half + tail)], axis=1)
    sb = jnp.concatenate([zeros(lo + half), sin, zeros(tail)], axis=1)
    return c * scale, sa * scale, sb * scale


def _prep_weights(w_up, w_down, w_m_in, w_m_out, w_dkv, w_uk, w_uv, w_dq, w_uq, w_o, dims):
    heads, dk, dv = dims["m_heads"], dims["m_dk"], dims["m_dv"]
    nh, nope, rope, kv_lora = dims["mla_heads"], dims["qk_nope"], dims["qk_rope"], dims["kv_lora"]
    ff = w_down.shape[1]
    qd, vd = heads * dk, heads * dv
    d = w_up.shape[1]
    pw = {}
    w_up_b, w_down_b, w_m_in_b = w_up.astype(BF16), w_down.astype(BF16), w_m_in.astype(BF16)
    depth, n_a = w_up.shape[0], w_m_in.shape[0]
    pw["w_up_a"] = [WeightView(w_up_b, (None, d, ff), (l, 0, 0)) for l in range(depth)]
    pw["w_up_g"] = [WeightView(w_up_b, (None, d, ff), (l, 0, 1)) for l in range(depth)]
    pw["w_down"] = [WeightView(w_down_b, (None, ff, d), (l, 0, 0)) for l in range(depth)]
    assert (2 * qd) % vd == 0, "value / output-gate columns must start on a multiple of their width"
    pw["w_m_q"] = [WeightView(w_m_in_b, (None, d, qd), (l, 0, 0)) for l in range(n_a)]
    pw["w_m_k"] = [WeightView(w_m_in_b, (None, d, qd), (l, 0, 1)) for l in range(n_a)]
    pw["w_m_v"] = [WeightView(w_m_in_b, (None, d, vd), (l, 0, 2 * qd // vd)) for l in range(n_a)]
    pw["w_m_o"] = [WeightView(w_m_in_b, (None, d, vd), (l, 0, 2 * qd // vd + 1)) for l in range(n_a)]
    gates = jnp.pad(w_m_in[:, :, 2 * qd + 2 * vd:], ((0, 0), (0, 0), (0, LANES - 2 * heads))).astype(BF16)
    pw["w_m_g"] = [WeightView(gates, (None, d, LANES), (l, 0, 0)) for l in range(n_a)]
    pw["w_m_out"] = w_m_out.astype(BF16)
    pw["w_dkv"] = jnp.pad(w_dkv, ((0, 0), (0, LANES - rope))).astype(BF16)
    pw["w_uk_r"] = jnp.pad(w_uk, ((0, 0), (0, 0), (0, LANES - nope))).reshape(kv_lora, nh * LANES).astype(BF16)
    v_head = w_uv.shape[2]
    pw["w_uv_r"] = jnp.pad(w_uv, ((0, 0), (0, 0), (0, LANES - v_head))).reshape(kv_lora, nh * LANES).T.astype(BF16)
    nb = w_uq.shape[0]
    wq = w_uq.reshape(nb, w_uq.shape[1], nh, nope + rope)
    pw["w_uq_r"] = jnp.pad(wq, ((0, 0), (0, 0), (0, 0), (0, LANES - nope - rope))).reshape(
        nb, w_uq.shape[1], nh * LANES).astype(BF16)
    half = rope // 2
    partner = jnp.concatenate([-wq[..., nope + half:], wq[..., nope:nope + half]], axis=-1)
    pw["w_uq_rot"] = jnp.pad(partner, ((0, 0), (0, 0), (0, 0), (nope, LANES - nope - rope))).reshape(
        nb, w_uq.shape[1], nh * LANES).astype(BF16)
    pw["w_dq"] = w_dq.astype(BF16)
    pw["w_o"] = w_o.astype(BF16)
    wukt = jnp.transpose(w_uk, (1, 2, 0))
    pw["w_uk_t"] = jnp.pad(wukt, ((0, 0), (0, LANES - nope), (0, 0))).astype(BF16)
    pw["w_uv_t"] = jnp.transpose(w_uv, (1, 0, 2)).astype(BF16)
    return pw


def _trunk(x, mods, mods_kv, pos, conv_bufs, m_states, kv_past, pw, small, dims):
    heads, dk, dv = dims["m_heads"], dims["m_dk"], dims["m_dv"]
    nh, nope, rope, kv_lora, v_head = (dims["mla_heads"], dims["qk_nope"], dims["qk_rope"],
                                       dims["kv_lora"], dims["v_head"])
    depth, n_a = dims["depth"], dims["n_a"]
    is_prompt = kv_past is None
    gq, r, d = x.shape
    att_scale = (nope + rope) ** -0.5 * LOG2_E
    new_c, new_n, new_m, new_conv = [], [], [], []
    ckv = kpe = kcat = vv = None
    y = None
    for layer in range(depth):
        sh1, sc1, gt1, sh2, sc2, gt2 = mods[layer]
        g1 = small["g_norm1"][layer][None, :]
        g2 = small["g_norm2"][layer][None, :]
        if layer == n_a:
            sh_kv, sc_kv = mods_kv
            tabs = _rope_tables(pos, rope, 0, 1.0, 0.0)
            if is_prompt:
                ckv, kpe, kcat, vv = shared_latent(x, small["g_kv_in"][None, :], sh_kv, sc_kv, pw["w_dkv"],
                                                   small["g_kv"][None, :], tabs, kv_lora, rope,
                                                   pw["w_uk_r"], pw["w_uv_r"], v_head)
            else:
                ckv, kpe = shared_latent(x, small["g_kv_in"][None, :], sh_kv, sc_kv, pw["w_dkv"],
                                         small["g_kv"][None, :], tabs, kv_lora, rope)
        if layer < n_a:
            ws = [pw["w_m_q"][layer], pw["w_m_k"][layer], pw["w_m_v"][layer], pw["w_m_o"][layer], pw["w_m_g"][layer]]
            if is_prompt:
                q, k, v, o, gates = norm_mod_proj(x, g1, sh1, sc1, ws, [BF16, F32, BF16, F32, F32],
                                                  [dk ** -0.5, 1.0, 1.0, 1.0, 1.0], "mlstm_in_proj")
                hh, c_st, n_st, m_st = mlstm_chunkwise(q, k, v, o, gates[..., :2 * heads],
                                                       small["b_m_gates"][layer], small["g_m_head"][layer],
                                                       heads, dk, dv)
                n_st = n_st.reshape(gq, heads, dk)
                m_st = m_st.reshape(gq, heads)
            else:
                q, k, v, o, gates = norm_mod_proj(x, g1, sh1, sc1, ws, [F32] * 5,
                                                  [dk ** -0.5, 1.0, 1.0, 1.0, 1.0], "mlstm_in_proj_tok")
                hh, c_st, n_st, m_st = mlstm_step(q[0], k[0], v[0], o[0], gates[0, :, :2 * heads],
                                                  small["b_m_gates"][layer], small["g_m_head"][layer],
                                                  m_states[0][layer], m_states[1][layer], m_states[2][layer],
                                                  heads, dk, dv)
                hh = hh[None]
            new_c.append(c_st)
            new_n.append(n_st)
            new_m.append(m_st)
            mix, w_mix = hh, pw["w_m_out"][layer]
        else:
            j = layer - n_a
            q_cos, q_msin, q_sin = _rope_tables(pos, rope, nope, att_scale, 1.0)
            qh = mla_queries(x, g1, sh1, sc1, pw["w_dq"][j], small["g_q"][j][None, :], pw["w_uq_r"][j],
                             pw["w_uq_rot"][j], q_cos, q_sin - q_msin)
            if is_prompt:
                att = prompt_attention(qh, kcat, vv, v_head)
            else:
                bsz = r
                q2 = qh[0]
                q_lat = head_proj_lanes(q2, pw["w_uk_t"], BF16, "absorb_q")
                q_lat = jnp.swapaxes(q_lat, 0, 1)
                q_pe = q2.reshape(bsz, nh, LANES)[:, :, nope:nope + rope]
                o_lat = paged_attention(q_lat, q_pe, ckv[0][:, None, :], kpe[0][:, None, :],
                                        kv_past[0], kv_past[1], kv_past[2])
                o_lat = o_lat.reshape(bsz, nh * kv_lora)
                att = head_proj_lanes(o_lat, pw["w_uv_t"], F32, "unabsorb_o")
                att = jnp.swapaxes(att, 0, 1).reshape(1, bsz, nh * v_head)
            mix, w_mix = att, pw["w_o"][j]
        final = layer == depth - 1
        gf = small["g_final"][None, :]
        if is_prompt:
            x, a_last = conv_ffn_seq(x, mix, w_mix, gt1, conv_bufs[layer], g2, sh2, sc2, gt2,
                                     pw["w_up_a"][layer], pw["w_up_g"][layer], small["w_conv"][layer],
                                     small["b_conv"][layer][None, :], pw["w_down"][layer], gf, final, "conv_ffn_seq")
            new_conv.append(a_last[:, 6:8, :])
        else:
            x = resid_proj(mix, w_mix, x, gt1, "mixer_out_proj")
            buf = conv_bufs[layer]
            x2, a_new = conv_ffn_tok(x[0], buf[:, 0, :], buf[:, 1, :], g2, sh2[0], sc2[0], gt2[0],
                                     pw["w_up_a"][layer], pw["w_up_g"][layer], small["w_conv"][layer],
                                     small["b_conv"][layer][None, :], pw["w_down"][layer], gf, final, "conv_ffn_tok")
            x = x2[None]
            new_conv.append(jnp.stack([buf[:, 1, :], a_new], axis=1))
    return x, jnp.stack(new_c), jnp.stack(new_n), jnp.stack(new_m), jnp.stack(new_conv), ckv, kpe


def kernel(x_prompt, x_sample, state_mlstm_C, state_mlstm_n, state_mlstm_m, state_conv, cache_ckv, cache_kpe,
           page_table, c_prompt, c_sample, g_norm1, g_norm2, w_ada, b_ada, w_up, w_conv, b_conv, w_down,
           w_m_in, b_m_gates, g_m_head, w_m_out, g_kv_in, w_ada_kv, b_ada_kv, w_dkv, g_kv, w_uk, w_uv,
           w_dq, g_q, w_uq, w_o, g_final):
    bp, s, d = x_prompt.shape
    bs, t, _ = x_sample.shape
    assert t == 1, "the sample path handles one new token per sequence"
    depth = w_ada.shape[0]
    n_a = w_m_in.shape[0]
    heads, dv = g_m_head.shape[1], g_m_head.shape[2]
    dk = state_mlstm_C.shape[3]
    kv_lora, nh, nope = w_uk.shape
    v_head = w_uv.shape[2]
    rope = w_dkv.shape[1] - kv_lora
    ff = w_down.shape[1]
    dims = dict(m_heads=heads, m_dk=dk, m_dv=dv, mla_heads=nh, qk_nope=nope, qk_rope=rope, kv_lora=kv_lora,
                v_head=v_head, depth=depth, n_a=n_a)
    past_len = page_table.shape[1] * cache_ckv.shape[1]

    pw = _prep_weights(w_up, w_down, w_m_in, w_m_out, w_dkv, w_uk, w_uv, w_dq, w_uq, w_o, dims)
    small = dict(g_norm1=g_norm1, g_norm2=g_norm2, w_conv=w_conv, b_conv=b_conv, b_m_gates=b_m_gates,
                 g_m_head=g_m_head, g_kv_in=g_kv_in, g_kv=g_kv, g_q=g_q, g_final=g_final)

    c_all = jnp.concatenate([c_prompt, c_sample], axis=0)
    mod = ada_mod(c_all, w_ada, b_ada[:, None, :])
    mod_kv = ada_mod(c_all, w_ada_kv[None], b_ada_kv[None, None, :])[0]

    def split(m, n, lo, hi, per_row):
        parts = jnp.split(m[lo:hi], n, axis=-1)
        return [p[None] if per_row else p[:, None, :] for p in parts]

    mods_p = [split(mod[l], 6, 0, bp, False) for l in range(depth)]
    mods_s = [split(mod[l], 6, bp, bp + bs, True) for l in range(depth)]
    kv_p = split(mod_kv, 2, 0, bp, False)
    kv_s = split(mod_kv, 2, bp, bp + bs, True)

    conv0 = [jnp.zeros((bp, 8, ff), F32)] * depth
    y_p, c_p, n_p, m_p, conv_p, ckv_p, kpe_p = _trunk(
        x_prompt, mods_p, kv_p, jnp.arange(s), conv0, None, None, pw, small, dims)

    pos_s = jnp.full((bs,), past_len, jnp.int32)
    y_s, c_s, n_s, m_s, conv_s, ckv_s, kpe_s = _trunk(
        x_sample.reshape(1, bs, d), mods_s, kv_s, pos_s, state_conv,
        (state_mlstm_C, state_mlstm_n, state_mlstm_m), (cache_ckv, jnp.swapaxes(cache_kpe, 1, 2), page_table), pw, small, dims)

    return (y_p, y_s.reshape(bs, 1, d), c_p, n_p, m_p, conv_p, ckv_p, kpe_p,
            c_s, n_s, m_s, conv_s, ckv_s.reshape(bs, 1, kv_lora), kpe_s.reshape(bs, 1, rope))
```

```python
import functools
import math
from typing import NamedTuple

import jax
import jax.numpy as jnp
from jax import lax
from jax.experimental import pallas as pl
from jax.experimental.pallas import tpu as pltpu

F32 = jnp.float32
BF16 = jnp.bfloat16

NORM_EPS = 1e-6
ROPE_THETA = 10000.0
LANES = 128
BF16_ROWS = 16
VMEM_LIMIT = 56 * 1024 * 1024

ROW_TILE = 512
FFN_ROW_TILE = 512
MLSTM_CHUNK = 256
ATTN_TILE = 2048
ATTN_BLOCK = 512
FFN_CHUNK = 256
PAGED_CHUNK = 512
LOG2_E = math.log2(math.e)
SOFTMAX_MIN_ROW_SUM = 2.0 ** -60
STEP_BATCH = 8

NT_DIMS = (((1,), (1,)), ((), ()))
TN_DIMS = (((0,), (0,)), ((), ()))


def _params(*sem):
    return pltpu.CompilerParams(dimension_semantics=sem, vmem_limit_bytes=VMEM_LIMIT)


def _const_spec(shape):
    nd = len(shape)
    return pl.BlockSpec(shape, lambda *_: (0,) * nd, pipeline_mode=pl.Buffered(1))


class WeightView(NamedTuple):
    array: jax.Array
    block: tuple
    index: tuple

    @property
    def shape(self):
        return tuple(d for d in self.block if d is not None)

    def spec(self):
        return pl.BlockSpec(self.block, lambda *_: self.index, pipeline_mode=pl.Buffered(1))

    def chunk_spec(self, axis, size):
        n = self.block[axis] // size
        block = tuple(size if a == axis else d for a, d in enumerate(self.block))
        index = lambda c: tuple(i * n + c if a == axis else i for a, i in enumerate(self.index))
        return pl.BlockSpec(block, index)


def _rms(x, g):
    return x * lax.rsqrt(jnp.mean(x * x, axis=-1, keepdims=True) + NORM_EPS) * g


def _norm_mod(x, g, sh, sc):
    return _rms(x, g) * (1.0 + sc) + sh


def _log_sigmoid(x):
    return jnp.minimum(x, 0.0) - jnp.log(1.0 + jnp.exp(-jnp.abs(x)))


def _rope3(t, c, sa, sb):
    return t * c + pltpu.roll(t, LANES - 16, axis=1) * sa + pltpu.roll(t, 16, axis=1) * sb


def _row_spec(tm, width):
    return pl.BlockSpec((None, tm, width), lambda g, i: (g, i, 0))


def _mod_spec(per_row, tm, width):
    if per_row:
        return pl.BlockSpec((None, tm, width), lambda g, i: (g, i, 0))
    return pl.BlockSpec((None, 1, width), lambda g, i: (g, 0, 0))


def _ada_kernel(c_ref, w_ref, b_ref, o_ref):
    c = c_ref[...]
    a = (c * jax.nn.sigmoid(c)).astype(BF16)
    o_ref[...] = jnp.dot(a, w_ref[...].astype(BF16), preferred_element_type=F32) + b_ref[...]


def ada_mod(c, w, b, tn=1024):
    m, d = c.shape
    nl, _, n = w.shape
    tn = min(tn, n)
    return pl.pallas_call(
        _ada_kernel,
        grid=(nl, n // tn),
        in_specs=[pl.BlockSpec((m, d), lambda l, j: (0, 0)),
                  pl.BlockSpec((None, d, tn), lambda l, j: (l, 0, j)),
                  pl.BlockSpec((None, 1, tn), lambda l, j: (l, 0, j))],
        out_specs=pl.BlockSpec((None, m, tn), lambda l, j: (l, 0, j)),
        out_shape=jax.ShapeDtypeStruct((nl, m, n), F32),
        compiler_params=_params("parallel", "parallel"),
        name="ada_mod",
    )(c, w, b)


def _proj_kernel(x_ref, g_ref, sh_ref, sc_ref, *refs, scales):
    n = len(scales)
    hn = _norm_mod(x_ref[...], g_ref[...], sh_ref[...], sc_ref[...]).astype(BF16)
    for w_ref, o_ref, s in zip(refs[:n], refs[n:], scales):
        acc = jnp.dot(hn, w_ref[...], preferred_element_type=F32)
        if s != 1.0:
            acc = acc * s
        o_ref[...] = acc.astype(o_ref.dtype)


def norm_mod_proj(x, g, sh, sc, ws, out_dtypes, scales, name):
    gq, r, d = x.shape
    tm = min(ROW_TILE, r)
    per_row = sh.shape[1] != 1
    in_specs = [_row_spec(tm, d), _const_spec((1, d)),
                _mod_spec(per_row, tm, d), _mod_spec(per_row, tm, d)]
    in_specs += [w.spec() for w in ws]
    return pl.pallas_call(
        functools.partial(_proj_kernel, scales=tuple(scales)),
        grid=(gq, r // tm),
        in_specs=in_specs,
        out_specs=[_row_spec(tm, w.shape[1]) for w in ws],
        out_shape=[jax.ShapeDtypeStruct((gq, r, w.shape[1]), dt) for w, dt in zip(ws, out_dtypes)],
        compiler_params=_params("parallel", "parallel"),
        name=name,
    )(x, g, sh, sc, *[w.array for w in ws])


def _resid_kernel(a_ref, w_ref, x_ref, gt_ref, o_ref):
    mix = jnp.dot(a_ref[...].astype(BF16), w_ref[...], preferred_element_type=F32)
    o_ref[...] = x_ref[...] + gt_ref[...] * mix


def resid_proj(a, w, x, gt, name):
    gq, r, d = x.shape
    k = a.shape[-1]
    tm = min(ROW_TILE, r)
    per_row = gt.shape[1] != 1
    return pl.pallas_call(
        _resid_kernel,
        grid=(gq, r // tm),
        in_specs=[_row_spec(tm, k), _const_spec(w.shape), _row_spec(tm, d), _mod_spec(per_row, tm, d)],
        out_specs=_row_spec(tm, d),
        out_shape=jax.ShapeDtypeStruct((gq, r, d), F32),
        compiler_params=_params("parallel", "parallel"),
        name=name,
    )(a, w, x, gt)


def _mlstm_chunk_kernel(q_ref, k_ref, v_ref, o_ref, gr_ref, gc_ref, bgc_ref, bgr_ref, gh_ref,
                        hh_ref, c_ref, n_ref, m_ref, *, heads, dk, dv):
    ci = pl.program_id(1)
    chunk = q_ref.shape[0]

    @pl.when(ci == 0)
    def _():
        c_ref[...] = jnp.zeros_like(c_ref)
        n_ref[...] = jnp.zeros_like(n_ref)
        m_ref[...] = jnp.zeros_like(m_ref)

    gates_r = gr_ref[...] + bgc_ref[...]
    gates_c = gc_ref[...] + bgr_ref[...]
    row = lax.broadcasted_iota(jnp.int32, (chunk, chunk), 0)
    col = lax.broadcasted_iota(jnp.int32, (chunk, chunk), 1)
    causal = col <= row

    s_raw, q_c = [], []
    for h in range(heads):
        q = q_ref[:, h * dk:(h + 1) * dk]
        s_raw.append(lax.dot_general(q, k_ref[:, h * dk:(h + 1) * dk].astype(BF16), NT_DIMS,
                                     preferred_element_type=F32))
        q_c.append(jnp.dot(q, c_ref[h].astype(BF16), preferred_element_type=F32))

    hr = range(heads)
    qs = [q_ref[:, h * dk:(h + 1) * dk] for h in hr]
    vs = [v_ref[:, h * dv:(h + 1) * dv] for h in hr]
    li_r = [gates_r[h:h + 1, :] for h in hr]
    lf_r = [_log_sigmoid(gates_r[heads + h:heads + h + 1, :]) for h in hr]
    li_c = [gates_c[:, h:h + 1] for h in hr]
    lf_c = [_log_sigmoid(gates_c[:, heads + h:heads + h + 1]) for h in hr]
    m_prev = [m_ref[h] for h in hr]
    n_prev = [n_ref[h] for h in hr]

    def lane_cumsum(x_r):
        low = jnp.where(causal, x_r, 0.0)
        if chunk % LANES == 0 and chunk > LANES:
            low = functools.reduce(jnp.add, [low[:, t:t + LANES] for t in range(0, chunk, LANES)])
        return jnp.sum(low, axis=1, keepdims=True)

    b_c = [lane_cumsum(lf_r[h]) for h in hr]
    b_r = [jnp.sum(jnp.where(row <= col, lf_c[h], 0.0), axis=0, keepdims=True) for h in hr]
    g = [jnp.sum(lf_r[h], axis=1, keepdims=True) for h in hr]

    a_c = [g[h] - b_c[h] + li_c[h] for h in hr]
    m_loc = [jnp.max(a_c[h], axis=0, keepdims=True) for h in hr]
    kw = [k_ref[:, h * dk:(h + 1) * dk] * jnp.exp(a_c[h] - m_loc[h]) for h in hr]
    c_loc = [lax.dot_general(kw[h].astype(BF16), vs[h], TN_DIMS, preferred_element_type=F32) for h in hr]
    n_loc = [jnp.sum(kw[h], axis=0, keepdims=True) for h in hr]

    dmat = [jnp.where(causal, b_c[h] - b_r[h] + li_r[h], -jnp.inf) for h in hr]
    w0 = [b_c[h] + m_prev[h] for h in hr]
    m_s = [jnp.maximum(w0[h], jnp.max(dmat[h], axis=1, keepdims=True)) for h in hr]
    w_inter = [jnp.exp(w0[h] - m_s[h]) for h in hr]
    s = [s_raw[h] * jnp.exp(dmat[h] - m_s[h]) for h in hr]
    num = [w_inter[h] * q_c[h] + jnp.dot(s[h].astype(BF16), vs[h], preferred_element_type=F32) for h in hr]
    def lane_fold(x):
        width = x.shape[1]
        if width % LANES or width == LANES:
            return x
        return functools.reduce(jnp.add, [x[:, t:t + LANES] for t in range(0, width, LANES)])

    def row_total(a, b):
        a, b = lane_fold(a), lane_fold(b)
        if a.shape == b.shape:
            return jnp.sum(a + b, axis=1, keepdims=True)
        return jnp.sum(a, axis=1, keepdims=True) + jnp.sum(b, axis=1, keepdims=True)

    den = [row_total(w_inter[h] * (qs[h].astype(F32) * n_prev[h]), s[h]) for h in hr]
    hval = [num[h] / jnp.maximum(jnp.abs(den[h]), jnp.exp(-m_s[h])) for h in hr]
    hn = [_rms(hval[h], gh_ref[h:h + 1, :]) for h in hr]
    for h in hr:
        gate = jax.nn.sigmoid(o_ref[:, h * dv:(h + 1) * dv])
        hh_ref[:, h * dv:(h + 1) * dv] = (hn[h] * gate).astype(hh_ref.dtype)

    for h in hr:
        m_new = jnp.maximum(g[h] + m_prev[h], m_loc[h])
        fw = jnp.exp(g[h] + m_prev[h] - m_new)
        lw = jnp.exp(m_loc[h] - m_new)
        c_ref[h] = fw * c_ref[h] + lw * c_loc[h]
        n_ref[h] = fw * n_prev[h] + lw * n_loc[h]
        m_ref[h] = m_new


def mlstm_chunkwise(q, k, v, o, gates, b_gates, g_head, heads, dk, dv):
    bsz, s, _ = q.shape
    chunk = min(MLSTM_CHUNK, s)
    gates_r = jnp.swapaxes(gates, 1, 2)
    g2 = 2 * heads
    return pl.pallas_call(
        functools.partial(_mlstm_chunk_kernel, heads=heads, dk=dk, dv=dv),
        grid=(bsz, s // chunk),
        in_specs=[_row_spec(chunk, heads * dk), _row_spec(chunk, heads * dk),
                  _row_spec(chunk, heads * dv), _row_spec(chunk, heads * dv),
                  pl.BlockSpec((None, g2, chunk), lambda b, c: (b, 0, c)),
                  _row_spec(chunk, g2),
                  _const_spec((g2, 1)), _const_spec((1, g2)), _const_spec((heads, dv))],
        out_specs=[_row_spec(chunk, heads * dv),
                   pl.BlockSpec((None, heads, dk, dv), lambda b, c: (b, 0, 0, 0)),
                   pl.BlockSpec((None, heads, 1, dk), lambda b, c: (b, 0, 0, 0)),
                   pl.BlockSpec((None, heads, 1, 1), lambda b, c: (b, 0, 0, 0))],
        out_shape=[jax.ShapeDtypeStruct((bsz, s, heads * dv), BF16),
                   jax.ShapeDtypeStruct((bsz, heads, dk, dv), F32),
                   jax.ShapeDtypeStruct((bsz, heads, 1, dk), F32),
                   jax.ShapeDtypeStruct((bsz, heads, 1, 1), F32)],
        compiler_params=_params("parallel", "arbitrary"),
        name="mlstm_chunkwise",
    )(q, k, v, o, gates_r, gates, b_gates.reshape(g2, 1), b_gates.reshape(1, g2), g_head)


def _mlstm_step_kernel(q_ref, k_ref, v_ref, o_ref, g_ref, bg_ref, gh_ref, c_ref, n_ref, m_ref,
                       hh_ref, co_ref, no_ref, mo_ref, *, heads, dk, dv):
    nb = q_ref.shape[0]
    gates = g_ref[...] + bg_ref[...]
    li = gates[:, :heads]
    lf = _log_sigmoid(gates[:, heads:])
    m_st = m_ref[...]
    m_new = jnp.maximum(lf + m_st, li)
    fw_all = jnp.exp(lf + m_st - m_new)
    iw_all = jnp.exp(li - m_new)
    floor_all = jnp.exp(-m_new)
    mo_ref[...] = m_new
    eye = lax.broadcasted_iota(jnp.int32, (dk, dk), 0) == lax.broadcasted_iota(jnp.int32, (dk, dk), 1)

    def to_col(r):
        return jnp.sum(jnp.where(eye, r, 0.0), axis=1, keepdims=True)

    hr = range(heads)
    for b in range(nb):
        q_r = [q_ref[b:b + 1, h * dk:(h + 1) * dk] for h in hr]
        k_r = [k_ref[b:b + 1, h * dk:(h + 1) * dk] for h in hr]
        k_c = [iw_all[b:b + 1, h:h + 1] * to_col(k_r[h]) for h in hr]
        c_new = [fw_all[b:b + 1, h:h + 1] * c_ref[b, h] + k_c[h] * v_ref[b:b + 1, h * dv:(h + 1) * dv] for h in hr]
        num = [jnp.dot(q_r[h].astype(BF16), c_new[h].astype(BF16), preferred_element_type=F32) for h in hr]
        n_new = [fw_all[b:b + 1, h:h + 1] * n_ref[b, h:h + 1, :] + iw_all[b:b + 1, h:h + 1] * k_r[h] for h in hr]
        den = [jnp.sum(q_r[h] * n_new[h], axis=1, keepdims=True) for h in hr]
        hval = [num[h] / jnp.maximum(jnp.abs(den[h]), floor_all[b:b + 1, h:h + 1]) for h in hr]
        hn = [_rms(hval[h], gh_ref[h:h + 1, :]) for h in hr]
        for h in hr:
            co_ref[b, h] = c_new[h]
            no_ref[b, h:h + 1, :] = n_new[h]
            hh_ref[b:b + 1, h * dv:(h + 1) * dv] = hn[h] * jax.nn.sigmoid(o_ref[b:b + 1, h * dv:(h + 1) * dv])


def mlstm_step(q, k, v, o, gates, b_gates, g_head, c_st, n_st, m_st, heads, dk, dv):
    bsz = q.shape[0]
    nb = min(STEP_BATCH, bsz)
    g2 = 2 * heads
    rows = lambda w: pl.BlockSpec((nb, w), lambda i: (i, 0))
    return pl.pallas_call(
        functools.partial(_mlstm_step_kernel, heads=heads, dk=dk, dv=dv),
        grid=(bsz // nb,),
        in_specs=[rows(heads * dk), rows(heads * dk), rows(heads * dv), rows(heads * dv), rows(g2),
                  _const_spec((1, g2)), _const_spec((heads, dv)),
                  pl.BlockSpec((nb, heads, dk, dv), lambda i: (i, 0, 0, 0)),
                  pl.BlockSpec((nb, heads, dk), lambda i: (i, 0, 0)),
                  rows(heads)],
        out_specs=[rows(heads * dv),
                   pl.BlockSpec((nb, heads, dk, dv), lambda i: (i, 0, 0, 0)),
                   pl.BlockSpec((nb, heads, dk), lambda i: (i, 0, 0)),
                   rows(heads)],
        out_shape=[jax.ShapeDtypeStruct((bsz, heads * dv), F32),
                   jax.ShapeDtypeStruct((bsz, heads, dk, dv), F32),
                   jax.ShapeDtypeStruct((bsz, heads, dk), F32),
                   jax.ShapeDtypeStruct((bsz, heads), F32)],
        compiler_params=_params("parallel"),
        name="mlstm_step",
    )(q, k, v, o, gates, b_gates.reshape(1, g2), g_head, c_st, n_st, m_st)


def _ffn_seq_kernel(x_ref, xh_ref, mix_ref, mixh_ref, wm_ref, gtm_ref, buf_ref, g_ref, sh_ref, sc_ref, gt_ref,
                    wa_ref, wg_ref, wc_ref, bc_ref, wd_ref, gf_ref, o_ref, alast_ref, *, fc, final_norm):
    i = pl.program_id(1)
    tm = x_ref.shape[0]
    ff = wa_ref.shape[1]
    gtm, wm = gtm_ref[...], wm_ref[...]
    x = x_ref[...] + gtm * jnp.dot(mix_ref[...], wm, preferred_element_type=F32)
    halo_rows = mixh_ref.shape[0]
    xh = xh_ref[...] + gtm * jnp.dot(mixh_ref[...], wm, preferred_element_type=F32)[halo_rows - 8:]
    g, sh, sc = g_ref[...], sh_ref[...], sc_ref[...]
    hn = _norm_mod(x, g, sh, sc).astype(BF16)
    hh = _norm_mod(xh, g, sh, sc).astype(BF16)
    first = i == 0

    def up(c):
        cs = slice(c * fc, (c + 1) * fc)
        wa = wa_ref[:, cs]
        return (jnp.dot(hn, wa, preferred_element_type=F32),
                jnp.dot(hn, wg_ref[:, cs], preferred_element_type=F32),
                jnp.dot(hh, wa, preferred_element_type=F32))

    n_chunks = ff // fc
    acc = None
    nxt = up(0)
    for c in range(n_chunks):
        cs = slice(c * fc, (c + 1) * fc)
        a, gt, a_halo = nxt
        if c + 1 < n_chunks:
            nxt = up(c + 1)
        ext = jnp.concatenate([jnp.where(first, buf_ref[:, cs], a_halo), a], axis=0)
        conv = (bc_ref[:, cs] + ext[6:6 + tm, :] * wc_ref[0:1, cs]
                + ext[7:7 + tm, :] * wc_ref[1:2, cs] + a * wc_ref[2:3, cs])
        act = (conv * jax.nn.sigmoid(conv) * gt).astype(BF16)
        part = jnp.dot(act, wd_ref[cs, :], preferred_element_type=F32)
        acc = part if acc is None else acc + part
        alast_ref[:, cs] = a[tm - 8:tm, :]

    y = x + gt_ref[...] * acc
    if final_norm:
        y = _rms(y, gf_ref[...])
    o_ref[...] = y


def conv_ffn_seq(x, mix, w_mix, gt_mix, buf8, g, sh, sc, gt, w_up_a, w_up_g, w_conv, b_conv, w_down, g_final,
                 final_norm, name):
    bsz, s, d = x.shape
    ff = w_up_a.shape[1]
    kk = mix.shape[-1]
    tm = min(FFN_ROW_TILE, s)
    fc = min(FFN_CHUNK, ff)
    halo = lambda b, i: (b, jnp.maximum(i * (tm // 8) - 1, 0), 0)
    halo_bf16 = lambda b, i: (b, jnp.maximum(i * (tm // BF16_ROWS) - 1, 0), 0)
    return pl.pallas_call(
        functools.partial(_ffn_seq_kernel, fc=fc, final_norm=final_norm),
        grid=(bsz, s // tm),
        in_specs=[_row_spec(tm, d), pl.BlockSpec((None, 8, d), halo),
                  _row_spec(tm, kk), pl.BlockSpec((None, BF16_ROWS, kk), halo_bf16),
                  _const_spec(w_mix.shape), _mod_spec(False, tm, d),
                  pl.BlockSpec((None, 8, ff), lambda b, i: (b, 0, 0)),
                  _const_spec((1, d)), _mod_spec(False, tm, d), _mod_spec(False, tm, d), _mod_spec(False, tm, d),
                  w_up_a.spec(), w_up_g.spec(), _const_spec(w_conv.shape),
                  _const_spec(b_conv.shape), w_down.spec(), _const_spec((1, d))],
        out_specs=[_row_spec(tm, d), pl.BlockSpec((None, 8, ff), lambda b, i: (b, 0, 0))],
        out_shape=[jax.ShapeDtypeStruct((bsz, s, d), F32), jax.ShapeDtypeStruct((bsz, 8, ff), F32)],
        compiler_params=_params("parallel", "arbitrary"),
        name=name,
    )(x, x, mix, mix, w_mix, gt_mix, buf8, g, sh, sc, gt, w_up_a.array, w_up_g.array, w_conv, b_conv,
      w_down.array, g_final)


def _ffn_tok_kernel(x_ref, b0_ref, b1_ref, g_ref, sh_ref, sc_ref, gt_ref, wa_ref, wg_ref, wc_ref,
                    bc_ref, wd_ref, gf_ref, o_ref, a_ref, acc_ref, *, final_norm):
    c = pl.program_id(0)
    x = x_ref[...]
    hn = _norm_mod(x, g_ref[...], sh_ref[...], sc_ref[...]).astype(BF16)
    a = jnp.dot(hn, wa_ref[...], preferred_element_type=F32)
    gt = jnp.dot(hn, wg_ref[...], preferred_element_type=F32)
    a_ref[...] = a
    conv = bc_ref[...] + b0_ref[...] * wc_ref[0:1, :] + b1_ref[...] * wc_ref[1:2, :] + a * wc_ref[2:3, :]
    act = (conv * jax.nn.sigmoid(conv) * gt).astype(BF16)
    part = jnp.dot(act, wd_ref[...], preferred_element_type=F32)

    @pl.when(c == 0)
    def _():
        acc_ref[...] = part

    @pl.when(c > 0)
    def _():
        acc_ref[...] += part

    y = x + gt_ref[...] * acc_ref[...]
    if final_norm:
        y = _rms(y, gf_ref[...])
    o_ref[...] = y


def conv_ffn_tok(x, buf0, buf1, g, sh, sc, gt, w_up_a, w_up_g, w_conv, b_conv, w_down, g_final, final_norm, name):
    bsz, d = x.shape
    ff = w_up_a.shape[1]
    fc = min(FFN_CHUNK, ff)
    full = lambda w: pl.BlockSpec((bsz, w), lambda c: (0, 0))
    cols = lambda r: pl.BlockSpec((r, fc), lambda c: (0, c))
    up_axis, down_axis = len(w_up_a.block) - 1, len(w_down.block) - 2
    return pl.pallas_call(
        functools.partial(_ffn_tok_kernel, final_norm=final_norm),
        grid=(ff // fc,),
        in_specs=[full(d), cols(bsz), cols(bsz), _const_spec((1, d)), full(d), full(d), full(d),
                  w_up_a.chunk_spec(up_axis, fc), w_up_g.chunk_spec(up_axis, fc), cols(w_conv.shape[0]), cols(1),
                  w_down.chunk_spec(down_axis, fc), _const_spec((1, d))],
        out_specs=[full(d), cols(bsz)],
        out_shape=[jax.ShapeDtypeStruct((bsz, d), F32), jax.ShapeDtypeStruct((bsz, ff), F32)],
        scratch_shapes=[pltpu.VMEM((bsz, d), F32)],
        compiler_params=_params("arbitrary"),
        name=name,
    )(x, buf0, buf1, g, sh, sc, gt, w_up_a.array, w_up_g.array, w_conv, b_conv, w_down.array, g_final)


def _latent_kernel(x_ref, g_ref, sh_ref, sc_ref, w_ref, gkv_ref, rc_ref, ra_ref, rb_ref, *refs,
                   kv_lora, rope, with_kv):
    if with_kv:
        wuk_ref, wuv_ref, one_ref, ckv_ref, kpe_ref, kcat_ref, v_ref, ksq_ref = refs
    else:
        ckv_ref, kpe_ref = refs
    hn = _norm_mod(x_ref[...], g_ref[...], sh_ref[...], sc_ref[...]).astype(BF16)
    lat = jnp.dot(hn, w_ref[...], preferred_element_type=F32)
    ckv = _rms(lat[:, :kv_lora], gkv_ref[...])
    ckv_ref[...] = ckv
    kpe = _rope3(lat[:, kv_lora:kv_lora + LANES], rc_ref[...], ra_ref[...], rb_ref[...])
    kpe_ref[...] = kpe[:, :rope]
    if with_kv:
        cb = ckv.astype(BF16)
        kn = jnp.dot(cb, wuk_ref[...], preferred_element_type=F32)
        kpe_hi = pltpu.roll(kpe, 64, axis=1)
        for h in range(kn.shape[1] // LANES):
            hs = slice(h * LANES, (h + 1) * LANES)
            kb = (kn[:, hs] + kpe_hi).astype(BF16)
            kcat_ref[:, hs] = kb
            kb = kb.astype(F32)
            ksq_ref[:, h:h + 1] = jnp.max(jnp.sum(kb * kb, axis=1, keepdims=True), axis=0, keepdims=True)
        v_t = lax.dot_general(wuv_ref[...], cb, NT_DIMS, preferred_element_type=F32)
        v_ref[...] = (v_t + one_ref[...]).astype(BF16)


def shared_latent(x, g, sh, sc, w_dkv_p, g_kv, tabs, kv_lora, rope, w_uk_r=None, w_uv_r=None, v_head=None):
    gq, r, d = x.shape
    tm = min(ROW_TILE, r)
    per_row = sh.shape[1] != 1
    with_kv = w_uk_r is not None
    tab_spec = pl.BlockSpec((tm, LANES), lambda g_, i: (i, 0))
    in_specs = [_row_spec(tm, d), _const_spec((1, d)), _mod_spec(per_row, tm, d), _mod_spec(per_row, tm, d),
                _const_spec(w_dkv_p.shape), _const_spec((1, kv_lora)), tab_spec, tab_spec, tab_spec]
    out_specs = [_row_spec(tm, kv_lora), _row_spec(tm, rope)]
    out_shape = [jax.ShapeDtypeStruct((gq, r, kv_lora), F32), jax.ShapeDtypeStruct((gq, r, rope), F32)]
    args = [x, g, sh, sc, w_dkv_p, g_kv, *tabs]
    if with_kv:
        vt_rows = w_uv_r.shape[0]
        ones_row = (jnp.arange(vt_rows) % LANES == v_head).astype(F32)[:, None]
        in_specs += [_const_spec(w_uk_r.shape), _const_spec(w_uv_r.shape), _const_spec(ones_row.shape)]
        n_heads = w_uk_r.shape[1] // LANES
        out_specs += [_row_spec(tm, w_uk_r.shape[1]), pl.BlockSpec((None, vt_rows, tm), lambda g_, i: (g_, 0, i)),
                      pl.BlockSpec((None, None, 1, n_heads), lambda g_, i: (g_, i, 0, 0))]
        out_shape += [jax.ShapeDtypeStruct((gq, r, w_uk_r.shape[1]), BF16),
                      jax.ShapeDtypeStruct((gq, vt_rows, r), BF16),
                      jax.ShapeDtypeStruct((gq, r // tm, 1, n_heads), F32)]
        args += [w_uk_r, w_uv_r, ones_row]
    return pl.pallas_call(
        functools.partial(_latent_kernel, kv_lora=kv_lora, rope=rope, with_kv=with_kv),
        grid=(gq, r // tm),
        in_specs=in_specs, out_specs=out_specs, out_shape=out_shape,
        compiler_params=_params("parallel", "parallel"),
        name="shared_latent_kv" if with_kv else "shared_latent",
    )(*args)


def _query_kernel(x_ref, g_ref, sh_ref, sc_ref, wdq_ref, gq_ref, wuq_ref, wrot_ref, rc_ref, rs_ref, q_ref, qsq_ref):
    hn = _norm_mod(x_ref[...], g_ref[...], sh_ref[...], sc_ref[...]).astype(BF16)
    qd = jnp.dot(hn, wdq_ref[...], preferred_element_type=F32)
    qn = _rms(qd, gq_ref[...]).astype(BF16)
    qf = jnp.dot(qn, wuq_ref[...], preferred_element_type=F32)
    qr = jnp.dot(qn, wrot_ref[...], preferred_element_type=F32)
    rc, rs = rc_ref[...], rs_ref[...]
    for h in range(qf.shape[1] // LANES):
        hs = slice(h * LANES, (h + 1) * LANES)
        qb = (qf[:, hs] * rc + qr[:, hs] * rs).astype(q_ref.dtype)
        q_ref[:, hs] = qb
        qb = qb.astype(F32)
        qsq_ref[:, h:h + 1] = jnp.sum(qb * qb, axis=1, keepdims=True)


def mla_queries(x, g, sh, sc, w_dq, g_q, w_uq_r, w_uq_rot, tab_cos, tab_sin):
    gq, r, d = x.shape
    tm = min(ROW_TILE, r)
    per_row = sh.shape[1] != 1
    n_heads = w_uq_r.shape[1] // LANES
    tab_spec = pl.BlockSpec((tm, LANES), lambda g_, i: (i, 0))
    return pl.pallas_call(
        _query_kernel,
        grid=(gq, r // tm),
        in_specs=[_row_spec(tm, d), _const_spec((1, d)), _mod_spec(per_row, tm, d), _mod_spec(per_row, tm, d),
                  _const_spec(w_dq.shape), _const_spec(g_q.shape), _const_spec(w_uq_r.shape),
                  _const_spec(w_uq_rot.shape), tab_spec, tab_spec],
        out_specs=[_row_spec(tm, w_uq_r.shape[1]), _row_spec(tm, n_heads)],
        out_shape=[jax.ShapeDtypeStruct((gq, r, w_uq_r.shape[1]), BF16),
                   jax.ShapeDtypeStruct((gq, r, n_heads), F32)],
        compiler_params=_params("parallel", "parallel"),
        name="mla_queries",
    )(x, g, sh, sc, w_dq, g_q, w_uq_r, w_uq_rot, tab_cos, tab_sin)


def _attn_kernel(q_ref, k_ref, vt_ref, qsq_ref, ksq_ref, o_ref, *, v_head, v_rows, blk, key_blocks_per_tile):
    qi = pl.program_id(2)
    tq = q_ref.shape[0]
    n_full = qi * (tq // blk)
    base = pl.multiple_of(qi * tq, tq)
    heads = range(2)
    diag = [(h, base + c * blk, c * blk, True) for c in range(tq // blk) for h in heads]

    def scores(h, kstart, q_lo):
        hs = slice(h * LANES, (h + 1) * LANES)
        return lax.dot_general(k_ref[pl.ds(kstart, blk), hs], q_ref[q_lo:tq, hs], NT_DIMS,
                               preferred_element_type=F32)

    def values(h, kstart):
        return vt_ref[h * LANES:h * LANES + v_rows, pl.ds(kstart, blk)]

    def cols_from(x, q_lo, new_cols):
        return new_cols if q_lo == 0 else jnp.concatenate([x[:, :q_lo], new_cols], axis=1)

    def run(tasks, carry, update):
        carry = list(carry)
        nxt = scores(*tasks[0][:3])
        for t, (h, kstart, q_lo, masked) in enumerate(tasks):
            s = nxt
            if t + 1 < len(tasks):
                nxt = scores(*tasks[t + 1][:3])
            if masked:
                key = lax.broadcasted_iota(jnp.int32, s.shape, 0)
                qry = lax.broadcasted_iota(jnp.int32, s.shape, 1)
                s = jnp.where(key <= qry, s, -jnp.inf)
            carry[h] = update(h, s, kstart, q_lo, carry[h])
        return tuple(carry)

    def sweep(update, init):
        def full_chunk(kt, carry):
            kstart = pl.multiple_of(kt * blk, blk)
            return run([(h, kstart, 0, False) for h in heads], carry, update)
        return run(diag, lax.fori_loop(0, n_full, full_chunk, init), update)

    def store(accs):
        outs = []
        for acc in accs:
            out_t = acc[:v_head] / acc[v_head:v_head + 1]
            outs.append(jnp.transpose(out_t))
        o_ref[...] = jnp.concatenate(outs, axis=1).astype(o_ref.dtype)

    ksq_blocks = ksq_ref[...]
    visible = lax.broadcasted_iota(jnp.int32, ksq_blocks.shape, 1) < (qi + 1) * key_blocks_per_tile
    ksq_max = jnp.max(jnp.where(visible, ksq_blocks, 0.0), axis=1, keepdims=True)
    bounds = [jnp.sqrt(qsq_ref[h:h + 1, :] * ksq_max[h:h + 1, :]) for h in heads]

    def update_bounded(h, s, kstart, q_lo, acc):
        p = jnp.exp2(s - bounds[h][:, q_lo:]).astype(BF16)
        pv = jnp.dot(values(h, kstart), p, preferred_element_type=F32)
        return cols_from(acc, q_lo, acc[:, q_lo:] + pv)

    accs = sweep(update_bounded, tuple(jnp.zeros((v_rows, tq), F32) for _ in heads))
    store(accs)
    smallest = jnp.minimum(accs[0][v_head:v_head + 1], accs[1][v_head:v_head + 1])
    row_sums_ok = jnp.min(smallest) >= SOFTMAX_MIN_ROW_SUM

    @pl.when(jnp.logical_not(row_sums_ok))
    def _():
        def update_online(h, s, kstart, q_lo, state):
            m, acc = state
            m_old = m[:, q_lo:]
            m_new = jnp.maximum(m_old, jnp.max(s, axis=0, keepdims=True))
            p = jnp.exp2(s - m_new).astype(BF16)
            pv = jnp.dot(values(h, kstart), p, preferred_element_type=F32)
            return (cols_from(m, q_lo, m_new),
                    cols_from(acc, q_lo, jnp.exp2(m_old - m_new) * acc[:, q_lo:] + pv))

        init = tuple((jnp.full((1, tq), -jnp.inf, F32), jnp.zeros((v_rows, tq), F32)) for _ in heads)
        store([acc for _, acc in sweep(update_online, init)])


def prompt_attention(q, kcat, vt, qsq, ksq_blocks, v_head):
    bsz, s, hw = q.shape
    pairs = hw // (2 * LANES)
    assert 2 * v_head == LANES, "two heads fill one 128-lane output block"
    tq = min(ATTN_TILE, s)
    blk = min(ATTN_BLOCK, tq)
    n_blocks = ksq_blocks.shape[1]
    assert n_blocks % (s // tq) == 0, "query tiles must cover whole key-norm blocks"
    v_rows = -(-(v_head + 1) // BF16_ROWS) * BF16_ROWS
    qsq_t = jnp.swapaxes(qsq, 1, 2).reshape(bsz, pairs, 2, s)
    ksq_t = jnp.swapaxes(ksq_blocks, 1, 2).reshape(bsz, pairs, 2, n_blocks)
    return pl.pallas_call(
        functools.partial(_attn_kernel, v_head=v_head, v_rows=v_rows, blk=blk,
                          key_blocks_per_tile=n_blocks // (s // tq)),
        grid=(bsz, pairs, s // tq),
        in_specs=[pl.BlockSpec((None, tq, 2 * LANES), lambda b, j, i: (b, i, j)),
                  pl.BlockSpec((None, s, 2 * LANES), lambda b, j, i: (b, 0, j)),
                  pl.BlockSpec((None, 2 * LANES, s), lambda b, j, i: (b, j, 0)),
                  pl.BlockSpec((None, None, 2, tq), lambda b, j, i: (b, j, 0, i)),
                  pl.BlockSpec((None, None, 2, n_blocks), lambda b, j, i: (b, j, 0, 0))],
        out_specs=pl.BlockSpec((None, tq, 2 * v_head), lambda b, j, i: (b, i, j)),
        out_shape=jax.ShapeDtypeStruct((bsz, s, pairs * 2 * v_head), BF16),
        compiler_params=_params("parallel", "parallel", "arbitrary"),
        name="prompt_attention",
    )(q, kcat, vt, qsq_t, ksq_t)


def _head_proj_kernel(a_ref, w_ref, o_ref):
    o_ref[...] = jnp.dot(a_ref[...].astype(BF16), w_ref[...], preferred_element_type=F32).astype(o_ref.dtype)


def head_proj_lanes(a, w, out_dtype, name):
    bsz = a.shape[0]
    nh, kk, n = w.shape
    return pl.pallas_call(
        _head_proj_kernel,
        grid=(nh,),
        in_specs=[pl.BlockSpec((bsz, kk), lambda h: (0, h)), pl.BlockSpec((None, kk, n), lambda h: (h, 0, 0))],
        out_specs=pl.BlockSpec((None, bsz, n), lambda h: (h, 0, 0)),
        out_shape=jax.ShapeDtypeStruct((nh, bsz, n), out_dtype),
        compiler_params=_params("parallel"),
        name=name,
    )(a, w)


def _paged_attn_kernel(pt_ref, ql_ref, qp_ref, cn_ref, kn_ref, ckv_hbm, kpe_hbm, o_ref,
                       ckv_buf, kpe_buf, s_ref, cb0, p0, w0, cb1, p1, w1, sem, *, n_seq, n_pages, ps, chunk):
    step = pl.program_id(0)
    slot = step % 2
    nh, c_lat = ql_ref.shape
    sets = ((cb0, p0, w0), (cb1, p1, w1))

    def page_copies(bi, sl):
        out = []
        for pg in range(n_pages):
            page = pt_ref[bi * n_pages + pg]
            rows = pl.ds(pg * ps, ps)
            out.append(pltpu.make_async_copy(ckv_hbm.at[page], ckv_buf.at[sl, rows, :], sem.at[0, sl]))
            out.append(pltpu.make_async_copy(kpe_hbm.at[page], kpe_buf.at[sl, :, rows], sem.at[1, sl]))
        return out

    @pl.when(step == 0)
    def _():
        for cp in page_copies(0, 0):
            cp.start()
        cb1[...] = jnp.zeros_like(cb1)
        p1[...] = jnp.zeros_like(p1)
        w1[...] = jnp.concatenate([jnp.zeros((nh, c_lat), F32), jnp.ones((nh, LANES), F32)], axis=1)

    @pl.when(step + 1 < n_seq)
    def _():
        for cp in page_copies(step + 1, 1 - slot):
            cp.start()

    @pl.when(step < n_seq)
    def _():
        for cp in page_copies(step, slot):
            cp.wait()

    n_chunks = n_pages * ps // chunk
    chunks = [slice(c * chunk, (c + 1) * chunk) for c in range(n_chunks)]

    def main(par):
        (cb_w, p_w, w_w), (cb_r, p_r, w_r) = sets[par], sets[1 - par]
        ql = ql_ref[...]
        qp = qp_ref[...]
        w_prev = w_r[...]
        accs = [w_prev[:, :c_lat], jnp.zeros((nh, c_lat), F32)]
        s_lat = []
        for c, cs in enumerate(chunks):
            ck = ckv_buf[par, cs, :].astype(BF16)
            cb_w[cs, :] = ck
            s_lat.append(lax.dot_general(ql, ck, NT_DIMS, preferred_element_type=F32))
            accs[c % 2] += jnp.dot(p_r[:, cs], cb_r[cs, :], preferred_element_type=F32)
        o_ref[...] = (accs[0] + accs[1]) / w_prev[:, c_lat:c_lat + 1]

        for cs, sl in zip(chunks, s_lat):
            s_ref[:, cs] = sl + jnp.dot(qp, kpe_buf[par, :, cs].astype(BF16), preferred_element_type=F32)

        cn = cn_ref[...].astype(BF16).astype(F32)
        kn = kn_ref[...].astype(BF16).astype(F32)
        s_new = (jnp.sum(ql.astype(F32) * cn, axis=1, keepdims=True)
                 + jnp.sum(qp.astype(F32) * kn, axis=1, keepdims=True))
        s = s_ref[...]
        m = jnp.maximum(jnp.max(s, axis=1, keepdims=True), s_new)
        p = jnp.exp2(s - m)
        p_new = jnp.exp2(s_new - m)
        l = jnp.sum(p, axis=1, keepdims=True) + p_new
        p_w[...] = p.astype(BF16)
        w_w[...] = jnp.concatenate([p_new.astype(BF16).astype(F32) * cn, jnp.broadcast_to(l, (nh, LANES))], axis=1)

    for par in range(2):
        pl.when(step % 2 == par)(functools.partial(main, par))


def paged_attention(q_lat, q_pe, ckv_new, kpe_new, cache_ckv, cache_kpe_t, page_table):
    bsz, nh, c = q_lat.shape
    r = q_pe.shape[-1]
    n_pages = page_table.shape[1]
    ps = cache_ckv.shape[1]
    past = n_pages * ps
    assert bsz >= 2, "the two-slot pipeline needs at least two sequences"
    chunk = min(PAGED_CHUNK, past)
    per_b = lambda rows, w: pl.BlockSpec((None, rows, w), lambda s_, pt: (jnp.minimum(s_, bsz - 1), 0, 0))
    return pl.pallas_call(
        functools.partial(_paged_attn_kernel, n_seq=bsz, n_pages=n_pages, ps=ps, chunk=chunk),
        grid_spec=pltpu.PrefetchScalarGridSpec(
            num_scalar_prefetch=1,
            grid=(bsz + 1,),
            in_specs=[per_b(nh, c), per_b(nh, r), per_b(1, c), per_b(1, r),
                      pl.BlockSpec(memory_space=pl.ANY), pl.BlockSpec(memory_space=pl.ANY)],
            out_specs=pl.BlockSpec((None, nh, c), lambda s_, pt: (jnp.maximum(s_ - 1, 0), 0, 0)),
            scratch_shapes=[pltpu.VMEM((2, past, c), F32), pltpu.VMEM((2, r, past), F32),
                            pltpu.VMEM((nh, past), F32)]
            + 2 * [pltpu.VMEM((past, c), BF16), pltpu.VMEM((nh, past), BF16), pltpu.VMEM((nh, c + LANES), F32)]
            + [pltpu.SemaphoreType.DMA((2, 2))]),
        out_shape=jax.ShapeDtypeStruct((bsz, nh, c), F32),
        compiler_params=_params("arbitrary"),
        name="paged_attention",
    )(page_table.reshape(-1), q_lat, q_pe, ckv_new, kpe_new, cache_ckv, cache_kpe_t)


def _rope_tables(pos, rope, lo, scale, passthrough):
    half = rope // 2
    freq = ROPE_THETA ** (-jnp.arange(half, dtype=F32) / half)
    ang = pos.astype(F32)[:, None] * freq[None, :]
    cos, sin = jnp.cos(ang), jnp.sin(ang)
    n = pos.shape[0]
    zeros = lambda w: jnp.zeros((n, w), F32)
    tail = LANES - lo - rope
    c = jnp.concatenate([jnp.full((n, lo), passthrough, F32), cos, cos, zeros(tail)], axis=1)
    sa = jnp.concatenate([zeros(lo), -sin, zeros(half + tail)], axis=1)
    sb = jnp.concatenate([zeros(lo + half), sin, zeros(tail)], axis=1)
    return c * scale, sa * scale, sb * scale


def _prep_weights(w_up, w_down, w_m_in, w_m_out, w_dkv, w_uk, w_uv, w_dq, w_uq, w_o, dims):
    heads, dk, dv = dims["m_heads"], dims["m_dk"], dims["m_dv"]
    nh, nope, rope, kv_lora = dims["mla_heads"], dims["qk_nope"], dims["qk_rope"], dims["kv_lora"]
    ff = w_down.shape[1]
    qd, vd = heads * dk, heads * dv
    d = w_up.shape[1]
    pw = {}
    w_up_b, w_down_b, w_m_in_b = w_up.astype(BF16), w_down.astype(BF16), w_m_in.astype(BF16)
    depth, n_a = w_up.shape[0], w_m_in.shape[0]
    pw["w_up_a"] = [WeightView(w_up_b, (None, d, ff), (l, 0, 0)) for l in range(depth)]
    pw["w_up_g"] = [WeightView(w_up_b, (None, d, ff), (l, 0, 1)) for l in range(depth)]
    pw["w_down"] = [WeightView(w_down_b, (None, ff, d), (l, 0, 0)) for l in range(depth)]
    assert (2 * qd) % vd == 0, "value / output-gate columns must start on a multiple of their width"
    pw["w_m_q"] = [WeightView(w_m_in_b, (None, d, qd), (l, 0, 0)) for l in range(n_a)]
    pw["w_m_k"] = [WeightView(w_m_in_b, (None, d, qd), (l, 0, 1)) for l in range(n_a)]
    pw["w_m_v"] = [WeightView(w_m_in_b, (None, d, vd), (l, 0, 2 * qd // vd)) for l in range(n_a)]
    pw["w_m_o"] = [WeightView(w_m_in_b, (None, d, vd), (l, 0, 2 * qd // vd + 1)) for l in range(n_a)]
    gates = jnp.pad(w_m_in[:, :, 2 * qd + 2 * vd:], ((0, 0), (0, 0), (0, LANES - 2 * heads))).astype(BF16)
    pw["w_m_g"] = [WeightView(gates, (None, d, LANES), (l, 0, 0)) for l in range(n_a)]
    pw["w_m_out"] = w_m_out.astype(BF16)
    pw["w_dkv"] = jnp.pad(w_dkv, ((0, 0), (0, LANES - rope))).astype(BF16)
    pw["w_uk_r"] = jnp.pad(w_uk, ((0, 0), (0, 0), (0, LANES - nope))).reshape(kv_lora, nh * LANES).astype(BF16)
    v_head = w_uv.shape[2]
    pw["w_uv_r"] = jnp.pad(w_uv, ((0, 0), (0, 0), (0, LANES - v_head))).reshape(kv_lora, nh * LANES).T.astype(BF16)
    nb = w_uq.shape[0]
    wq = w_uq.reshape(nb, w_uq.shape[1], nh, nope + rope)
    pw["w_uq_r"] = jnp.pad(wq, ((0, 0), (0, 0), (0, 0), (0, LANES - nope - rope))).reshape(
        nb, w_uq.shape[1], nh * LANES).astype(BF16)
    half = rope // 2
    partner = jnp.concatenate([-wq[..., nope + half:], wq[..., nope:nope + half]], axis=-1)
    pw["w_uq_rot"] = jnp.pad(partner, ((0, 0), (0, 0), (0, 0), (nope, LANES - nope - rope))).reshape(
        nb, w_uq.shape[1], nh * LANES).astype(BF16)
    pw["w_dq"] = w_dq.astype(BF16)
    pw["w_o"] = w_o.astype(BF16)
    wukt = jnp.transpose(w_uk, (1, 2, 0))
    pw["w_uk_t"] = jnp.pad(wukt, ((0, 0), (0, LANES - nope), (0, 0))).astype(BF16)
    pw["w_uv_t"] = jnp.transpose(w_uv, (1, 0, 2)).astype(BF16)
    return pw


def _trunk(x, mods, mods_kv, pos, conv_bufs, m_states, kv_past, pw, small, dims):
    heads, dk, dv = dims["m_heads"], dims["m_dk"], dims["m_dv"]
    nh, nope, rope, kv_lora, v_head = (dims["mla_heads"], dims["qk_nope"], dims["qk_rope"],
                                       dims["kv_lora"], dims["v_head"])
    depth, n_a = dims["depth"], dims["n_a"]
    is_prompt = kv_past is None
    gq, r, d = x.shape
    att_scale = (nope + rope) ** -0.5 * LOG2_E
    new_c, new_n, new_m, new_conv = [], [], [], []
    ckv = kpe = kcat = vv = ksq = None
    y = None
    for layer in range(depth):
        sh1, sc1, gt1, sh2, sc2, gt2 = mods[layer]
        g1 = small["g_norm1"][layer][None, :]
        g2 = small["g_norm2"][layer][None, :]
        if layer == n_a:
            sh_kv, sc_kv = mods_kv
            tabs = _rope_tables(pos, rope, 0, 1.0, 0.0)
            if is_prompt:
                ckv, kpe, kcat, vv, ksq = shared_latent(x, small["g_kv_in"][None, :], sh_kv, sc_kv, pw["w_dkv"],
                                                        small["g_kv"][None, :], tabs, kv_lora, rope,
                                                        pw["w_uk_r"], pw["w_uv_r"], v_head)
            else:
                ckv, kpe = shared_latent(x, small["g_kv_in"][None, :], sh_kv, sc_kv, pw["w_dkv"],
                                         small["g_kv"][None, :], tabs, kv_lora, rope)
        if layer < n_a:
            ws = [pw["w_m_q"][layer], pw["w_m_k"][layer], pw["w_m_v"][layer], pw["w_m_o"][layer], pw["w_m_g"][layer]]
            if is_prompt:
                q, k, v, o, gates = norm_mod_proj(x, g1, sh1, sc1, ws, [BF16, F32, BF16, F32, F32],
                                                  [dk ** -0.5, 1.0, 1.0, 1.0, 1.0], "mlstm_in_proj")
                hh, c_st, n_st, m_st = mlstm_chunkwise(q, k, v, o, gates[..., :2 * heads],
                                                       small["b_m_gates"][layer], small["g_m_head"][layer],
                                                       heads, dk, dv)
                n_st = n_st.reshape(gq, heads, dk)
                m_st = m_st.reshape(gq, heads)
            else:
                q, k, v, o, gates = norm_mod_proj(x, g1, sh1, sc1, ws, [F32] * 5,
                                                  [dk ** -0.5, 1.0, 1.0, 1.0, 1.0], "mlstm_in_proj_tok")
                hh, c_st, n_st, m_st = mlstm_step(q[0], k[0], v[0], o[0], gates[0, :, :2 * heads],
                                                  small["b_m_gates"][layer], small["g_m_head"][layer],
                                                  m_states[0][layer], m_states[1][layer], m_states[2][layer],
                                                  heads, dk, dv)
                hh = hh[None]
            new_c.append(c_st)
            new_n.append(n_st)
            new_m.append(m_st)
            mix, w_mix = hh, pw["w_m_out"][layer]
        else:
            j = layer - n_a
            q_cos, q_msin, q_sin = _rope_tables(pos, rope, nope, att_scale, 1.0)
            qh, qsq = mla_queries(x, g1, sh1, sc1, pw["w_dq"][j], small["g_q"][j][None, :], pw["w_uq_r"][j],
                                  pw["w_uq_rot"][j], q_cos, q_sin - q_msin)
            if is_prompt:
                att = prompt_attention(qh, kcat, vv, qsq, ksq[:, :, 0, :], v_head)
            else:
                bsz = r
                q2 = qh[0]
                q_lat = head_proj_lanes(q2, pw["w_uk_t"], BF16, "absorb_q")
                q_lat = jnp.swapaxes(q_lat, 0, 1)
                q_pe = q2.reshape(bsz, nh, LANES)[:, :, nope:nope + rope]
                o_lat = paged_attention(q_lat, q_pe, ckv[0][:, None, :], kpe[0][:, None, :],
                                        kv_past[0], kv_past[1], kv_past[2])
                o_lat = o_lat.reshape(bsz, nh * kv_lora)
                att = head_proj_lanes(o_lat, pw["w_uv_t"], F32, "unabsorb_o")
                att = jnp.swapaxes(att, 0, 1).reshape(1, bsz, nh * v_head)
            mix, w_mix = att, pw["w_o"][j]
        final = layer == depth - 1
        gf = small["g_final"][None, :]
        if is_prompt:
            x, a_last = conv_ffn_seq(x, mix, w_mix, gt1, conv_bufs[layer], g2, sh2, sc2, gt2,
                                     pw["w_up_a"][layer], pw["w_up_g"][layer], small["w_conv"][layer],
                                     small["b_conv"][layer][None, :], pw["w_down"][layer], gf, final, "conv_ffn_seq")
            new_conv.append(a_last[:, 6:8, :])
        else:
            x = resid_proj(mix, w_mix, x, gt1, "mixer_out_proj")
            buf = conv_bufs[layer]
            x2, a_new = conv_ffn_tok(x[0], buf[:, 0, :], buf[:, 1, :], g2, sh2[0], sc2[0], gt2[0],
                                     pw["w_up_a"][layer], pw["w_up_g"][layer], small["w_conv"][layer],
                                     small["b_conv"][layer][None, :], pw["w_down"][layer], gf, final, "conv_ffn_tok")
            x = x2[None]
            new_conv.append(jnp.stack([buf[:, 1, :], a_new], axis=1))
    return x, jnp.stack(new_c), jnp.stack(new_n), jnp.stack(new_m), jnp.stack(new_conv), ckv, kpe


def kernel(x_prompt, x_sample, state_mlstm_C, state_mlstm_n, state_mlstm_m, state_conv, cache_ckv, cache_kpe,
           page_table, c_prompt, c_sample, g_norm1, g_norm2, w_ada, b_ada, w_up, w_conv, b_conv, w_down,
           w_m_in, b_m_gates, g_m_head, w_m_out, g_kv_in, w_ada_kv, b_ada_kv, w_dkv, g_kv, w_uk, w_uv,
           w_dq, g_q, w_uq, w_o, g_final):
    bp, s, d = x_prompt.shape
    bs, t, _ = x_sample.shape
    assert t == 1, "the sample path handles one new token per sequence"
    depth = w_ada.shape[0]
    n_a = w_m_in.shape[0]
    heads, dv = g_m_head.shape[1], g_m_head.shape[2]
    dk = state_mlstm_C.shape[3]
    kv_lora, nh, nope = w_uk.shape
    v_head = w_uv.shape[2]
    rope = w_dkv.shape[1] - kv_lora
    ff = w_down.shape[1]
    dims = dict(m_heads=heads, m_dk=dk, m_dv=dv, mla_heads=nh, qk_nope=nope, qk_rope=rope, kv_lora=kv_lora,
                v_head=v_head, depth=depth, n_a=n_a)
    past_len = page_table.shape[1] * cache_ckv.shape[1]

    pw = _prep_weights(w_up, w_down, w_m_in, w_m_out, w_dkv, w_uk, w_uv, w_dq, w_uq, w_o, dims)
    small = dict(g_norm1=g_norm1, g_norm2=g_norm2, w_conv=w_conv, b_conv=b_conv, b_m_gates=b_m_gates,
                 g_m_head=g_m_head, g_kv_in=g_kv_in, g_kv=g_kv, g_q=g_q, g_final=g_final)

    c_all = jnp.concatenate([c_prompt, c_sample], axis=0)
    mod = ada_mod(c_all, w_ada, b_ada[:, None, :])
    mod_kv = ada_mod(c_all, w_ada_kv[None], b_ada_kv[None, None, :])[0]

    def split(m, n, lo, hi, per_row):
        parts = jnp.split(m[lo:hi], n, axis=-1)
        return [p[None] if per_row else p[:, None, :] for p in parts]

    mods_p = [split(mod[l], 6, 0, bp, False) for l in range(depth)]
    mods_s = [split(mod[l], 6, bp, bp + bs, True) for l in range(depth)]
    kv_p = split(mod_kv, 2, 0, bp, False)
    kv_s = split(mod_kv, 2, bp, bp + bs, True)

    conv0 = [jnp.zeros((bp, 8, ff), F32)] * depth
    y_p, c_p, n_p, m_p, conv_p, ckv_p, kpe_p = _trunk(
        x_prompt, mods_p, kv_p, jnp.arange(s), conv0, None, None, pw, small, dims)

    pos_s = jnp.full((bs,), past_len, jnp.int32)
    y_s, c_s, n_s, m_s, conv_s, ckv_s, kpe_s = _trunk(
        x_sample.reshape(1, bs, d), mods_s, kv_s, pos_s, state_conv,
        (state_mlstm_C, state_mlstm_n, state_mlstm_m), (cache_ckv, jnp.swapaxes(cache_kpe, 1, 2), page_table), pw, small, dims)

    return (y_p, y_s.reshape(bs, 1, d), c_p, n_p, m_p, conv_p, ckv_p, kpe_p,
            c_s, n_s, m_s, conv_s, ckv_s.reshape(bs, 1, kv_lora), kpe_s.reshape(bs, 1, rope))
```

```python
import functools
import math
from typing import NamedTuple

import jax
import jax.numpy as jnp
from jax import lax
from jax.experimental import pallas as pl
from jax.experimental.pallas import tpu as pltpu

F32 = jnp.float32
BF16 = jnp.bfloat16

NORM_EPS = 1e-6
ROPE_THETA = 10000.0
LANES = 128
BF16_ROWS = 16
VMEM_LIMIT = 56 * 1024 * 1024

ROW_TILE = 512
FFN_ROW_TILE = 512
MLSTM_CHUNK = 256
ATTN_TILE = 2048
ATTN_BLOCK = 512
FFN_CHUNK = 256
PAGED_CHUNK = 512
LOG2_E = math.log2(math.e)
SOFTMAX_MIN_ROW_SUM = 2.0 ** -60
STEP_BATCH = 8

NT_DIMS = (((1,), (1,)), ((), ()))
TN_DIMS = (((0,), (0,)), ((), ()))


def _params(*sem):
    return pltpu.CompilerParams(dimension_semantics=sem, vmem_limit_bytes=VMEM_LIMIT)


def _const_spec(shape):
    nd = len(shape)
    return pl.BlockSpec(shape, lambda *_: (0,) * nd, pipeline_mode=pl.Buffered(1))


class WeightView(NamedTuple):
    array: jax.Array
    block: tuple
    index: tuple

    @property
    def shape(self):
        return tuple(d for d in self.block if d is not None)

    def spec(self):
        return pl.BlockSpec(self.block, lambda *_: self.index, pipeline_mode=pl.Buffered(1))

    def chunk_spec(self, axis, size):
        n = self.block[axis] // size
        block = tuple(size if a == axis else d for a, d in enumerate(self.block))
        index = lambda c: tuple(i * n + c if a == axis else i for a, i in enumerate(self.index))
        return pl.BlockSpec(block, index)


def _rms(x, g):
    return x * lax.rsqrt(jnp.mean(x * x, axis=-1, keepdims=True) + NORM_EPS) * g


def _norm_mod(x, g, sh, sc):
    return _rms(x, g) * (1.0 + sc) + sh


def _log_sigmoid(x):
    return jnp.minimum(x, 0.0) - jnp.log(1.0 + jnp.exp(-jnp.abs(x)))


def _rope3(t, c, sa, sb):
    return t * c + pltpu.roll(t, LANES - 16, axis=1) * sa + pltpu.roll(t, 16, axis=1) * sb


def _row_spec(tm, width):
    return pl.BlockSpec((None, tm, width), lambda g, i: (g, i, 0))


def _mod_spec(per_row, tm, width):
    if per_row:
        return pl.BlockSpec((None, tm, width), lambda g, i: (g, i, 0))
    return pl.BlockSpec((None, 1, width), lambda g, i: (g, 0, 0))


def _ada_kernel(c_ref, w_ref, b_ref, o_ref):
    c = c_ref[...]
    a = (c * jax.nn.sigmoid(c)).astype(BF16)
    o_ref[...] = jnp.dot(a, w_ref[...].astype(BF16), preferred_element_type=F32) + b_ref[...]


def ada_mod(c, w, b, tn=1024):
    m, d = c.shape
    nl, _, n = w.shape
    tn = min(tn, n)
    return pl.pallas_call(
        _ada_kernel,
        grid=(nl, n // tn),
        in_specs=[pl.BlockSpec((m, d), lambda l, j: (0, 0)),
                  pl.BlockSpec((None, d, tn), lambda l, j: (l, 0, j)),
                  pl.BlockSpec((None, 1, tn), lambda l, j: (l, 0, j))],
        out_specs=pl.BlockSpec((None, m, tn), lambda l, j: (l, 0, j)),
        out_shape=jax.ShapeDtypeStruct((nl, m, n), F32),
        compiler_params=_params("parallel", "parallel"),
        name="ada_mod",
    )(c, w, b)


def _proj_kernel(x_ref, g_ref, sh_ref, sc_ref, *refs, scales):
    n = len(scales)
    hn = _norm_mod(x_ref[...], g_ref[...], sh_ref[...], sc_ref[...]).astype(BF16)
    for w_ref, o_ref, s in zip(refs[:n], refs[n:], scales):
        acc = jnp.dot(hn, w_ref[...], preferred_element_type=F32)
        if s != 1.0:
            acc = acc * s
        o_ref[...] = acc.astype(o_ref.dtype)


def norm_mod_proj(x, g, sh, sc, ws, out_dtypes, scales, name):
    gq, r, d = x.shape
    tm = min(ROW_TILE, r)
    per_row = sh.shape[1] != 1
    in_specs = [_row_spec(tm, d), _const_spec((1, d)),
                _mod_spec(per_row, tm, d), _mod_spec(per_row, tm, d)]
    in_specs += [w.spec() for w in ws]
    return pl.pallas_call(
        functools.partial(_proj_kernel, scales=tuple(scales)),
        grid=(gq, r // tm),
        in_specs=in_specs,
        out_specs=[_row_spec(tm, w.shape[1]) for w in ws],
        out_shape=[jax.ShapeDtypeStruct((gq, r, w.shape[1]), dt) for w, dt in zip(ws, out_dtypes)],
        compiler_params=_params("parallel", "parallel"),
        name=name,
    )(x, g, sh, sc, *[w.array for w in ws])


def _resid_kernel(a_ref, w_ref, x_ref, gt_ref, o_ref):
    mix = jnp.dot(a_ref[...].astype(BF16), w_ref[...], preferred_element_type=F32)
    o_ref[...] = x_ref[...] + gt_ref[...] * mix


def resid_proj(a, w, x, gt, name):
    gq, r, d = x.shape
    k = a.shape[-1]
    tm = min(ROW_TILE, r)
    per_row = gt.shape[1] != 1
    return pl.pallas_call(
        _resid_kernel,
        grid=(gq, r // tm),
        in_specs=[_row_spec(tm, k), _const_spec(w.shape), _row_spec(tm, d), _mod_spec(per_row, tm, d)],
        out_specs=_row_spec(tm, d),
        out_shape=jax.ShapeDtypeStruct((gq, r, d), F32),
        compiler_params=_params("parallel", "parallel"),
        name=name,
    )(a, w, x, gt)


def _mlstm_chunk_kernel(q_ref, k_ref, v_ref, o_ref, gr_ref, gc_ref, bgc_ref, bgr_ref, gh_ref,
                        hh_ref, c_ref, n_ref, m_ref, *, heads, dk, dv):
    ci = pl.program_id(1)
    chunk = q_ref.shape[0]

    @pl.when(ci == 0)
    def _():
        c_ref[...] = jnp.zeros_like(c_ref)
        n_ref[...] = jnp.zeros_like(n_ref)
        m_ref[...] = jnp.zeros_like(m_ref)

    gates_r = gr_ref[...] + bgc_ref[...]
    gates_c = gc_ref[...] + bgr_ref[...]
    row = lax.broadcasted_iota(jnp.int32, (chunk, chunk), 0)
    col = lax.broadcasted_iota(jnp.int32, (chunk, chunk), 1)
    causal = col <= row

    s_raw, q_c = [], []
    for h in range(heads):
        q = q_ref[:, h * dk:(h + 1) * dk]
        s_raw.append(lax.dot_general(q, k_ref[:, h * dk:(h + 1) * dk].astype(BF16), NT_DIMS,
                                     preferred_element_type=F32))
        q_c.append(jnp.dot(q, c_ref[h].astype(BF16), preferred_element_type=F32))

    hr = range(heads)
    qs = [q_ref[:, h * dk:(h + 1) * dk] for h in hr]
    vs = [v_ref[:, h * dv:(h + 1) * dv] for h in hr]
    li_r = [gates_r[h:h + 1, :] for h in hr]
    lf_r = [_log_sigmoid(gates_r[heads + h:heads + h + 1, :]) for h in hr]
    li_c = [gates_c[:, h:h + 1] for h in hr]
    lf_c = [_log_sigmoid(gates_c[:, heads + h:heads + h + 1]) for h in hr]
    m_prev = [m_ref[h] for h in hr]
    n_prev = [n_ref[h] for h in hr]

    def lane_cumsum(x_r):
        low = jnp.where(causal, x_r, 0.0)
        if chunk % LANES == 0 and chunk > LANES:
            low = functools.reduce(jnp.add, [low[:, t:t + LANES] for t in range(0, chunk, LANES)])
        return jnp.sum(low, axis=1, keepdims=True)

    b_c = [lane_cumsum(lf_r[h]) for h in hr]
    b_r = [jnp.sum(jnp.where(row <= col, lf_c[h], 0.0), axis=0, keepdims=True) for h in hr]
    g = [jnp.sum(lf_r[h], axis=1, keepdims=True) for h in hr]

    a_c = [g[h] - b_c[h] + li_c[h] for h in hr]
    m_loc = [jnp.max(a_c[h], axis=0, keepdims=True) for h in hr]
    kw = [k_ref[:, h * dk:(h + 1) * dk] * jnp.exp(a_c[h] - m_loc[h]) for h in hr]
    c_loc = [lax.dot_general(kw[h].astype(BF16), vs[h], TN_DIMS, preferred_element_type=F32) for h in hr]
    n_loc = [jnp.sum(kw[h], axis=0, keepdims=True) for h in hr]

    dmat = [jnp.where(causal, b_c[h] - b_r[h] + li_r[h], -jnp.inf) for h in hr]
    w0 = [b_c[h] + m_prev[h] for h in hr]
    m_s = [jnp.maximum(w0[h], jnp.max(dmat[h], axis=1, keepdims=True)) for h in hr]
    w_inter = [jnp.exp(w0[h] - m_s[h]) for h in hr]
    s = [s_raw[h] * jnp.exp(dmat[h] - m_s[h]) for h in hr]
    num = [w_inter[h] * q_c[h] + jnp.dot(s[h].astype(BF16), vs[h], preferred_element_type=F32) for h in hr]
    def lane_fold(x):
        width = x.shape[1]
        if width % LANES or width == LANES:
            return x
        return functools.reduce(jnp.add, [x[:, t:t + LANES] for t in range(0, width, LANES)])

    def row_total(a, b):
        a, b = lane_fold(a), lane_fold(b)
        if a.shape == b.shape:
            return jnp.sum(a + b, axis=1, keepdims=True)
        return jnp.sum(a, axis=1, keepdims=True) + jnp.sum(b, axis=1, keepdims=True)

    den = [row_total(w_inter[h] * (qs[h].astype(F32) * n_prev[h]), s[h]) for h in hr]
    hval = [num[h] / jnp.maximum(jnp.abs(den[h]), jnp.exp(-m_s[h])) for h in hr]
    hn = [_rms(hval[h], gh_ref[h:h + 1, :]) for h in hr]
    for h in hr:
        gate = jax.nn.sigmoid(o_ref[:, h * dv:(h + 1) * dv])
        hh_ref[:, h * dv:(h + 1) * dv] = (hn[h] * gate).astype(hh_ref.dtype)

    for h in hr:
        m_new = jnp.maximum(g[h] + m_prev[h], m_loc[h])
        fw = jnp.exp(g[h] + m_prev[h] - m_new)
        lw = jnp.exp(m_loc[h] - m_new)
        c_ref[h] = fw * c_ref[h] + lw * c_loc[h]
        n_ref[h] = fw * n_prev[h] + lw * n_loc[h]
        m_ref[h] = m_new


def mlstm_chunkwise(q, k, v, o, gates, b_gates, g_head, heads, dk, dv):
    bsz, s, _ = q.shape
    chunk = min(MLSTM_CHUNK, s)
    gates_r = jnp.swapaxes(gates, 1, 2)
    g2 = 2 * heads
    return pl.pallas_call(
        functools.partial(_mlstm_chunk_kernel, heads=heads, dk=dk, dv=dv),
        grid=(bsz, s // chunk),
        in_specs=[_row_spec(chunk, heads * dk), _row_spec(chunk, heads * dk),
                  _row_spec(chunk, heads * dv), _row_spec(chunk, heads * dv),
                  pl.BlockSpec((None, g2, chunk), lambda b, c: (b, 0, c)),
                  _row_spec(chunk, g2),
                  _const_spec((g2, 1)), _const_spec((1, g2)), _const_spec((heads, dv))],
        out_specs=[_row_spec(chunk, heads * dv),
                   pl.BlockSpec((None, heads, dk, dv), lambda b, c: (b, 0, 0, 0)),
                   pl.BlockSpec((None, heads, 1, dk), lambda b, c: (b, 0, 0, 0)),
                   pl.BlockSpec((None, heads, 1, 1), lambda b, c: (b, 0, 0, 0))],
        out_shape=[jax.ShapeDtypeStruct((bsz, s, heads * dv), BF16),
                   jax.ShapeDtypeStruct((bsz, heads, dk, dv), F32),
                   jax.ShapeDtypeStruct((bsz, heads, 1, dk), F32),
                   jax.ShapeDtypeStruct((bsz, heads, 1, 1), F32)],
        compiler_params=_params("parallel", "arbitrary"),
        name="mlstm_chunkwise",
    )(q, k, v, o, gates_r, gates, b_gates.reshape(g2, 1), b_gates.reshape(1, g2), g_head)


def _mlstm_step_kernel(q_ref, k_ref, v_ref, o_ref, g_ref, bg_ref, gh_ref, c_ref, n_ref, m_ref,
                       hh_ref, co_ref, no_ref, mo_ref, *, heads, dk, dv):
    nb = q_ref.shape[0]
    gates = g_ref[...] + bg_ref[...]
    li = gates[:, :heads]
    lf = _log_sigmoid(gates[:, heads:])
    m_st = m_ref[...]
    m_new = jnp.maximum(lf + m_st, li)
    fw_all = jnp.exp(lf + m_st - m_new)
    iw_all = jnp.exp(li - m_new)
    floor_all = jnp.exp(-m_new)
    mo_ref[...] = m_new
    eye = lax.broadcasted_iota(jnp.int32, (dk, dk), 0) == lax.broadcasted_iota(jnp.int32, (dk, dk), 1)

    def to_col(r):
        return jnp.sum(jnp.where(eye, r, 0.0), axis=1, keepdims=True)

    hr = range(heads)
    for b in range(nb):
        q_r = [q_ref[b:b + 1, h * dk:(h + 1) * dk] for h in hr]
        k_r = [k_ref[b:b + 1, h * dk:(h + 1) * dk] for h in hr]
        k_c = [iw_all[b:b + 1, h:h + 1] * to_col(k_r[h]) for h in hr]
        c_new = [fw_all[b:b + 1, h:h + 1] * c_ref[b, h] + k_c[h] * v_ref[b:b + 1, h * dv:(h + 1) * dv] for h in hr]
        num = [jnp.dot(q_r[h].astype(BF16), c_new[h].astype(BF16), preferred_element_type=F32) for h in hr]
        n_new = [fw_all[b:b + 1, h:h + 1] * n_ref[b, h:h + 1, :] + iw_all[b:b + 1, h:h + 1] * k_r[h] for h in hr]
        den = [jnp.sum(q_r[h] * n_new[h], axis=1, keepdims=True) for h in hr]
        hval = [num[h] / jnp.maximum(jnp.abs(den[h]), floor_all[b:b + 1, h:h + 1]) for h in hr]
        hn = [_rms(hval[h], gh_ref[h:h + 1, :]) for h in hr]
        for h in hr:
            co_ref[b, h] = c_new[h]
            no_ref[b, h:h + 1, :] = n_new[h]
            hh_ref[b:b + 1, h * dv:(h + 1) * dv] = hn[h] * jax.nn.sigmoid(o_ref[b:b + 1, h * dv:(h + 1) * dv])


def mlstm_step(q, k, v, o, gates, b_gates, g_head, c_st, n_st, m_st, heads, dk, dv):
    bsz = q.shape[0]
    nb = min(STEP_BATCH, bsz)
    g2 = 2 * heads
    rows = lambda w: pl.BlockSpec((nb, w), lambda i: (i, 0))
    return pl.pallas_call(
        functools.partial(_mlstm_step_kernel, heads=heads, dk=dk, dv=dv),
        grid=(bsz // nb,),
        in_specs=[rows(heads * dk), rows(heads * dk), rows(heads * dv), rows(heads * dv), rows(g2),
                  _const_spec((1, g2)), _const_spec((heads, dv)),
                  pl.BlockSpec((nb, heads, dk, dv), lambda i: (i, 0, 0, 0)),
                  pl.BlockSpec((nb, heads, dk), lambda i: (i, 0, 0)),
                  rows(heads)],
        out_specs=[rows(heads * dv),
                   pl.BlockSpec((nb, heads, dk, dv), lambda i: (i, 0, 0, 0)),
                   pl.BlockSpec((nb, heads, dk), lambda i: (i, 0, 0)),
                   rows(heads)],
        out_shape=[jax.ShapeDtypeStruct((bsz, heads * dv), F32),
                   jax.ShapeDtypeStruct((bsz, heads, dk, dv), F32),
                   jax.ShapeDtypeStruct((bsz, heads, dk), F32),
                   jax.ShapeDtypeStruct((bsz, heads), F32)],
        compiler_params=_params("parallel"),
        name="mlstm_step",
    )(q, k, v, o, gates, b_gates.reshape(1, g2), g_head, c_st, n_st, m_st)


def _ffn_seq_kernel(x_ref, xh_ref, mix_ref, mixh_ref, wm_ref, gtm_ref, buf_ref, g_ref, sh_ref, sc_ref, gt_ref,
                    wa_ref, wg_ref, wc_ref, bc_ref, wd_ref, gf_ref, o_ref, alast_ref, *, fc, final_norm):
    i = pl.program_id(1)
    tm = x_ref.shape[0]
    ff = wa_ref.shape[1]
    gtm, wm = gtm_ref[...], wm_ref[...]
    x = x_ref[...] + gtm * jnp.dot(mix_ref[...], wm, preferred_element_type=F32)
    halo_rows = mixh_ref.shape[0]
    xh = xh_ref[...] + gtm * jnp.dot(mixh_ref[...], wm, preferred_element_type=F32)[halo_rows - 8:]
    g, sh, sc = g_ref[...], sh_ref[...], sc_ref[...]
    hn = _norm_mod(x, g, sh, sc).astype(BF16)
    hh = _norm_mod(xh, g, sh, sc).astype(BF16)
    first = i == 0

    def up(c):
        cs = slice(c * fc, (c + 1) * fc)
        wa = wa_ref[:, cs]
        return (jnp.dot(hn, wa, preferred_element_type=F32),
                jnp.dot(hn, wg_ref[:, cs], preferred_element_type=F32),
                jnp.dot(hh, wa, preferred_element_type=F32))

    n_chunks = ff // fc
    acc = None
    nxt = up(0)
    for c in range(n_chunks):
        cs = slice(c * fc, (c + 1) * fc)
        a, gt, a_halo = nxt
        if c + 1 < n_chunks:
            nxt = up(c + 1)
        ext = jnp.concatenate([jnp.where(first, buf_ref[:, cs], a_halo), a], axis=0)
        conv = (bc_ref[:, cs] + ext[6:6 + tm, :] * wc_ref[0:1, cs]
                + ext[7:7 + tm, :] * wc_ref[1:2, cs] + a * wc_ref[2:3, cs])
        act = (conv * jax.nn.sigmoid(conv) * gt).astype(BF16)
        part = jnp.dot(act, wd_ref[cs, :], preferred_element_type=F32)
        acc = part if acc is None else acc + part
        alast_ref[:, cs] = a[tm - 8:tm, :]

    y = x + gt_ref[...] * acc
    if final_norm:
        y = _rms(y, gf_ref[...])
    o_ref[...] = y


def conv_ffn_seq(x, mix, w_mix, gt_mix, buf8, g, sh, sc, gt, w_up_a, w_up_g, w_conv, b_conv, w_down, g_final,
                 final_norm, name):
    bsz, s, d = x.shape
    ff = w_up_a.shape[1]
    kk = mix.shape[-1]
    tm = min(FFN_ROW_TILE, s)
    fc = min(FFN_CHUNK, ff)
    halo = lambda b, i: (b, jnp.maximum(i * (tm // 8) - 1, 0), 0)
    halo_bf16 = lambda b, i: (b, jnp.maximum(i * (tm // BF16_ROWS) - 1, 0), 0)
    return pl.pallas_call(
        functools.partial(_ffn_seq_kernel, fc=fc, final_norm=final_norm),
        grid=(bsz, s // tm),
        in_specs=[_row_spec(tm, d), pl.BlockSpec((None, 8, d), halo),
                  _row_spec(tm, kk), pl.BlockSpec((None, BF16_ROWS, kk), halo_bf16),
                  _const_spec(w_mix.shape), _mod_spec(False, tm, d),
                  pl.BlockSpec((None, 8, ff), lambda b, i: (b, 0, 0)),
                  _const_spec((1, d)), _mod_spec(False, tm, d), _mod_spec(False, tm, d), _mod_spec(False, tm, d),
                  w_up_a.spec(), w_up_g.spec(), _const_spec(w_conv.shape),
                  _const_spec(b_conv.shape), w_down.spec(), _const_spec((1, d))],
        out_specs=[_row_spec(tm, d), pl.BlockSpec((None, 8, ff), lambda b, i: (b, 0, 0))],
        out_shape=[jax.ShapeDtypeStruct((bsz, s, d), F32), jax.ShapeDtypeStruct((bsz, 8, ff), F32)],
        compiler_params=_params("parallel", "arbitrary"),
        name=name,
    )(x, x, mix, mix, w_mix, gt_mix, buf8, g, sh, sc, gt, w_up_a.array, w_up_g.array, w_conv, b_conv,
      w_down.array, g_final)


def _ffn_tok_kernel(x_ref, b0_ref, b1_ref, g_ref, sh_ref, sc_ref, gt_ref, wa_ref, wg_ref, wc_ref,
                    bc_ref, wd_ref, gf_ref, o_ref, a_ref, acc_ref, *, final_norm):
    c = pl.program_id(0)
    x = x_ref[...]
    hn = _norm_mod(x, g_ref[...], sh_ref[...], sc_ref[...]).astype(BF16)
    a = jnp.dot(hn, wa_ref[...], preferred_element_type=F32)
    gt = jnp.dot(hn, wg_ref[...], preferred_element_type=F32)
    a_ref[...] = a
    conv = bc_ref[...] + b0_ref[...] * wc_ref[0:1, :] + b1_ref[...] * wc_ref[1:2, :] + a * wc_ref[2:3, :]
    act = (conv * jax.nn.sigmoid(conv) * gt).astype(BF16)
    part = jnp.dot(act, wd_ref[...], preferred_element_type=F32)

    @pl.when(c == 0)
    def _():
        acc_ref[...] = part

    @pl.when(c > 0)
    def _():
        acc_ref[...] += part

    y = x + gt_ref[...] * acc_ref[...]
    if final_norm:
        y = _rms(y, gf_ref[...])
    o_ref[...] = y


def conv_ffn_tok(x, buf0, buf1, g, sh, sc, gt, w_up_a, w_up_g, w_conv, b_conv, w_down, g_final, final_norm, name):
    bsz, d = x.shape
    ff = w_up_a.shape[1]
    fc = min(FFN_CHUNK, ff)
    full = lambda w: pl.BlockSpec((bsz, w), lambda c: (0, 0))
    cols = lambda r: pl.BlockSpec((r, fc), lambda c: (0, c))
    up_axis, down_axis = len(w_up_a.block) - 1, len(w_down.block) - 2
    return pl.pallas_call(
        functools.partial(_ffn_tok_kernel, final_norm=final_norm),
        grid=(ff // fc,),
        in_specs=[full(d), cols(bsz), cols(bsz), _const_spec((1, d)), full(d), full(d), full(d),
                  w_up_a.chunk_spec(up_axis, fc), w_up_g.chunk_spec(up_axis, fc), cols(w_conv.shape[0]), cols(1),
                  w_down.chunk_spec(down_axis, fc), _const_spec((1, d))],
        out_specs=[full(d), cols(bsz)],
        out_shape=[jax.ShapeDtypeStruct((bsz, d), F32), jax.ShapeDtypeStruct((bsz, ff), F32)],
        scratch_shapes=[pltpu.VMEM((bsz, d), F32)],
        compiler_params=_params("arbitrary"),
        name=name,
    )(x, buf0, buf1, g, sh, sc, gt, w_up_a.array, w_up_g.array, w_conv, b_conv, w_down.array, g_final)


def _latent_kernel(x_ref, g_ref, sh_ref, sc_ref, w_ref, gkv_ref, rc_ref, ra_ref, rb_ref, *refs,
                   kv_lora, rope, with_kv):
    if with_kv:
        wuk_ref, wuv_ref, one_ref, ckv_ref, kpe_ref, kcat_ref, v_ref, ksq_ref = refs
    else:
        ckv_ref, kpe_ref = refs
    hn = _norm_mod(x_ref[...], g_ref[...], sh_ref[...], sc_ref[...]).astype(BF16)
    lat = jnp.dot(hn, w_ref[...], preferred_element_type=F32)
    ckv = _rms(lat[:, :kv_lora], gkv_ref[...])
    ckv_ref[...] = ckv
    kpe = _rope3(lat[:, kv_lora:kv_lora + LANES], rc_ref[...], ra_ref[...], rb_ref[...])
    kpe_ref[...] = kpe[:, :rope]
    if with_kv:
        cb = ckv.astype(BF16)
        kn = jnp.dot(cb, wuk_ref[...], preferred_element_type=F32)
        kpe_hi = pltpu.roll(kpe, 64, axis=1)
        for h in range(kn.shape[1] // LANES):
            hs = slice(h * LANES, (h + 1) * LANES)
            kb = (kn[:, hs] + kpe_hi).astype(BF16)
            kcat_ref[:, hs] = kb
            kb = kb.astype(F32)
            ksq_ref[:, h:h + 1] = jnp.max(jnp.sum(kb * kb, axis=1, keepdims=True), axis=0, keepdims=True)
        v_t = lax.dot_general(wuv_ref[...], cb, NT_DIMS, preferred_element_type=F32)
        v_ref[...] = (v_t + one_ref[...]).astype(BF16)


def shared_latent(x, g, sh, sc, w_dkv_p, g_kv, tabs, kv_lora, rope, w_uk_r=None, w_uv_r=None, v_head=None):
    gq, r, d = x.shape
    tm = min(ROW_TILE, r)
    per_row = sh.shape[1] != 1
    with_kv = w_uk_r is not None
    tab_spec = pl.BlockSpec((tm, LANES), lambda g_, i: (i, 0))
    in_specs = [_row_spec(tm, d), _const_spec((1, d)), _mod_spec(per_row, tm, d), _mod_spec(per_row, tm, d),
                _const_spec(w_dkv_p.shape), _const_spec((1, kv_lora)), tab_spec, tab_spec, tab_spec]
    out_specs = [_row_spec(tm, kv_lora), _row_spec(tm, rope)]
    out_shape = [jax.ShapeDtypeStruct((gq, r, kv_lora), F32), jax.ShapeDtypeStruct((gq, r, rope), F32)]
    args = [x, g, sh, sc, w_dkv_p, g_kv, *tabs]
    if with_kv:
        vt_rows = w_uv_r.shape[0]
        ones_row = (jnp.arange(vt_rows) % LANES == v_head).astype(F32)[:, None]
        in_specs += [_const_spec(w_uk_r.shape), _const_spec(w_uv_r.shape), _const_spec(ones_row.shape)]
        n_heads = w_uk_r.shape[1] // LANES
        out_specs += [_row_spec(tm, w_uk_r.shape[1]), pl.BlockSpec((None, vt_rows, tm), lambda g_, i: (g_, 0, i)),
                      pl.BlockSpec((None, None, 1, n_heads), lambda g_, i: (g_, i, 0, 0))]
        out_shape += [jax.ShapeDtypeStruct((gq, r, w_uk_r.shape[1]), BF16),
                      jax.ShapeDtypeStruct((gq, vt_rows, r), BF16),
                      jax.ShapeDtypeStruct((gq, r // tm, 1, n_heads), F32)]
        args += [w_uk_r, w_uv_r, ones_row]
    return pl.pallas_call(
        functools.partial(_latent_kernel, kv_lora=kv_lora, rope=rope, with_kv=with_kv),
        grid=(gq, r // tm),
        in_specs=in_specs, out_specs=out_specs, out_shape=out_shape,
        compiler_params=_params("parallel", "parallel"),
        name="shared_latent_kv" if with_kv else "shared_latent",
    )(*args)


def _query_kernel(x_ref, g_ref, sh_ref, sc_ref, wdq_ref, gq_ref, wuq_ref, wrot_ref, rc_ref, rs_ref, q_ref, qsq_ref):
    hn = _norm_mod(x_ref[...], g_ref[...], sh_ref[...], sc_ref[...]).astype(BF16)
    qd = jnp.dot(hn, wdq_ref[...], preferred_element_type=F32)
    qn = _rms(qd, gq_ref[...]).astype(BF16)
    qf = jnp.dot(qn, wuq_ref[...], preferred_element_type=F32)
    qr = jnp.dot(qn, wrot_ref[...], preferred_element_type=F32)
    rc, rs = rc_ref[...], rs_ref[...]
    for h in range(qf.shape[1] // LANES):
        hs = slice(h * LANES, (h + 1) * LANES)
        qb = (qf[:, hs] * rc + qr[:, hs] * rs).astype(q_ref.dtype)
        q_ref[:, hs] = qb
        qb = qb.astype(F32)
        qsq_ref[:, h:h + 1] = jnp.sum(qb * qb, axis=1, keepdims=True)


def mla_queries(x, g, sh, sc, w_dq, g_q, w_uq_r, w_uq_rot, tab_cos, tab_sin):
    gq, r, d = x.shape
    tm = min(ROW_TILE, r)
    per_row = sh.shape[1] != 1
    n_heads = w_uq_r.shape[1] // LANES
    tab_spec = pl.BlockSpec((tm, LANES), lambda g_, i: (i, 0))
    return pl.pallas_call(
        _query_kernel,
        grid=(gq, r // tm),
        in_specs=[_row_spec(tm, d), _const_spec((1, d)), _mod_spec(per_row, tm, d), _mod_spec(per_row, tm, d),
                  _const_spec(w_dq.shape), _const_spec(g_q.shape), _const_spec(w_uq_r.shape),
                  _const_spec(w_uq_rot.shape), tab_spec, tab_spec],
        out_specs=[_row_spec(tm, w_uq_r.shape[1]), _row_spec(tm, n_heads)],
        out_shape=[jax.ShapeDtypeStruct((gq, r, w_uq_r.shape[1]), BF16),
                   jax.ShapeDtypeStruct((gq, r, n_heads), F32)],
        compiler_params=_params("parallel", "parallel"),
        name="mla_queries",
    )(x, g, sh, sc, w_dq, g_q, w_uq_r, w_uq_rot, tab_cos, tab_sin)


def _attn_kernel(q_ref, k_ref, vt_ref, qsq_ref, ksq_ref, o_ref, *, v_head, v_rows, blk, key_blocks_per_tile):
    qi = pl.program_id(2)
    tq = q_ref.shape[0]
    n_full = qi * (tq // blk)
    base = pl.multiple_of(qi * tq, tq)
    heads = range(2)
    diag = [(h, base + c * blk, c * blk, True) for c in range(tq // blk) for h in heads]

    def scores(h, kstart, q_lo):
        hs = slice(h * LANES, (h + 1) * LANES)
        return lax.dot_general(k_ref[pl.ds(kstart, blk), hs], q_ref[q_lo:tq, hs], NT_DIMS,
                               preferred_element_type=F32)

    def values(h, kstart):
        return vt_ref[h * LANES:h * LANES + v_rows, pl.ds(kstart, blk)]

    def cols_from(x, q_lo, new_cols):
        return new_cols if q_lo == 0 else jnp.concatenate([x[:, :q_lo], new_cols], axis=1)

    def run(tasks, carry, update):
        carry = list(carry)
        nxt = scores(*tasks[0][:3])
        for t, (h, kstart, q_lo, masked) in enumerate(tasks):
            s = nxt
            if t + 1 < len(tasks):
                nxt = scores(*tasks[t + 1][:3])
            if masked:
                key = lax.broadcasted_iota(jnp.int32, s.shape, 0)
                qry = lax.broadcasted_iota(jnp.int32, s.shape, 1)
                s = jnp.where(key <= qry, s, -jnp.inf)
            carry[h] = update(h, s, kstart, q_lo, carry[h])
        return tuple(carry)

    def sweep(update, init):
        per_trip = 2 if (tq // blk) % 2 == 0 else 1

        def full_chunks(trip, carry):
            kstart = pl.multiple_of(trip * (per_trip * blk), per_trip * blk)
            return run([(h, kstart + c * blk, 0, False) for c in range(per_trip) for h in heads], carry, update)
        return run(diag, lax.fori_loop(0, n_full // per_trip, full_chunks, init), update)

    def store(accs):
        outs = []
        for acc in accs:
            out_t = acc[:v_head] / acc[v_head:v_head + 1]
            outs.append(jnp.transpose(out_t))
        o_ref[...] = jnp.concatenate(outs, axis=1).astype(o_ref.dtype)

    ksq_blocks = ksq_ref[...]
    visible = lax.broadcasted_iota(jnp.int32, ksq_blocks.shape, 1) < (qi + 1) * key_blocks_per_tile
    ksq_max = jnp.max(jnp.where(visible, ksq_blocks, 0.0), axis=1, keepdims=True)
    bounds = [jnp.sqrt(qsq_ref[h:h + 1, :] * ksq_max[h:h + 1, :]) for h in heads]

    def update_bounded(h, s, kstart, q_lo, acc):
        p = jnp.exp2(s - bounds[h][:, q_lo:]).astype(BF16)
        pv = jnp.dot(values(h, kstart), p, preferred_element_type=F32)
        return cols_from(acc, q_lo, acc[:, q_lo:] + pv)

    accs = sweep(update_bounded, tuple(jnp.zeros((v_rows, tq), F32) for _ in heads))
    store(accs)
    smallest = jnp.minimum(accs[0][v_head:v_head + 1], accs[1][v_head:v_head + 1])
    row_sums_ok = jnp.min(smallest) >= SOFTMAX_MIN_ROW_SUM

    @pl.when(jnp.logical_not(row_sums_ok))
    def _():
        def update_online(h, s, kstart, q_lo, state):
            m, acc = state
            m_old = m[:, q_lo:]
            m_new = jnp.maximum(m_old, jnp.max(s, axis=0, keepdims=True))
            p = jnp.exp2(s - m_new).astype(BF16)
            pv = jnp.dot(values(h, kstart), p, preferred_element_type=F32)
            return (cols_from(m, q_lo, m_new),
                    cols_from(acc, q_lo, jnp.exp2(m_old - m_new) * acc[:, q_lo:] + pv))

        init = tuple((jnp.full((1, tq), -jnp.inf, F32), jnp.zeros((v_rows, tq), F32)) for _ in heads)
        store([acc for _, acc in sweep(update_online, init)])


def prompt_attention(q, kcat, vt, qsq, ksq_blocks, v_head):
    bsz, s, hw = q.shape
    pairs = hw // (2 * LANES)
    assert 2 * v_head == LANES, "two heads fill one 128-lane output block"
    tq = min(ATTN_TILE, s)
    blk = min(ATTN_BLOCK, tq)
    n_blocks = ksq_blocks.shape[1]
    assert n_blocks % (s // tq) == 0, "query tiles must cover whole key-norm blocks"
    v_rows = -(-(v_head + 1) // BF16_ROWS) * BF16_ROWS
    qsq_t = jnp.swapaxes(qsq, 1, 2).reshape(bsz, pairs, 2, s)
    ksq_t = jnp.swapaxes(ksq_blocks, 1, 2).reshape(bsz, pairs, 2, n_blocks)
    return pl.pallas_call(
        functools.partial(_attn_kernel, v_head=v_head, v_rows=v_rows, blk=blk,
                          key_blocks_per_tile=n_blocks // (s // tq)),
        grid=(bsz, pairs, s // tq),
        in_specs=[pl.BlockSpec((None, tq, 2 * LANES), lambda b, j, i: (b, i, j)),
                  pl.BlockSpec((None, s, 2 * LANES), lambda b, j, i: (b, 0, j)),
                  pl.BlockSpec((None, 2 * LANES, s), lambda b, j, i: (b, j, 0)),
                  pl.BlockSpec((None, None, 2, tq), lambda b, j, i: (b, j, 0, i)),
                  pl.BlockSpec((None, None, 2, n_blocks), lambda b, j, i: (b, j, 0, 0))],
        out_specs=pl.BlockSpec((None, tq, 2 * v_head), lambda b, j, i: (b, i, j)),
        out_shape=jax.ShapeDtypeStruct((bsz, s, pairs * 2 * v_head), BF16),
        compiler_params=_params("parallel", "parallel", "arbitrary"),
        name="prompt_attention",
    )(q, kcat, vt, qsq_t, ksq_t)


def _head_proj_kernel(a_ref, w_ref, o_ref):
    kk = w_ref.shape[1]
    for h in range(w_ref.shape[0]):
        o_ref[h] = jnp.dot(a_ref[:, h * kk:(h + 1) * kk].astype(BF16), w_ref[h],
                           preferred_element_type=F32).astype(o_ref.dtype)


def head_proj_lanes(a, w, out_dtype, name):
    bsz = a.shape[0]
    nh, kk, n = w.shape
    return pl.pallas_call(
        _head_proj_kernel,
        grid=(1,),
        in_specs=[pl.BlockSpec(a.shape, lambda i: (0, 0)), pl.BlockSpec(w.shape, lambda i: (0, 0, 0))],
        out_specs=pl.BlockSpec((nh, bsz, n), lambda i: (0, 0, 0)),
        out_shape=jax.ShapeDtypeStruct((nh, bsz, n), out_dtype),
        compiler_params=_params("arbitrary"),
        name=name,
    )(a, w)


def _paged_attn_kernel(pt_ref, ql_ref, qp_ref, cn_ref, kn_ref, ckv_hbm, kpe_hbm, o_ref,
                       ckv_buf, kpe_buf, s_ref, cb0, p0, w0, cb1, p1, w1, sem, *, n_seq, n_pages, ps, chunk):
    step = pl.program_id(0)
    slot = step % 2
    nh, c_lat = ql_ref.shape
    sets = ((cb0, p0, w0), (cb1, p1, w1))

    def page_copies(bi, sl):
        out = []
        for pg in range(n_pages):
            page = pt_ref[bi * n_pages + pg]
            rows = pl.ds(pg * ps, ps)
            out.append(pltpu.make_async_copy(ckv_hbm.at[page], ckv_buf.at[sl, rows, :], sem.at[0, sl]))
            out.append(pltpu.make_async_copy(kpe_hbm.at[page], kpe_buf.at[sl, :, rows], sem.at[1, sl]))
        return out

    @pl.when(step == 0)
    def _():
        for cp in page_copies(0, 0):
            cp.start()
        cb1[...] = jnp.zeros_like(cb1)
        p1[...] = jnp.zeros_like(p1)
        w1[...] = jnp.concatenate([jnp.zeros((nh, c_lat), F32), jnp.ones((nh, LANES), F32)], axis=1)

    @pl.when(step + 1 < n_seq)
    def _():
        for cp in page_copies(step + 1, 1 - slot):
            cp.start()

    @pl.when(step < n_seq)
    def _():
        for cp in page_copies(step, slot):
            cp.wait()

    n_chunks = n_pages * ps // chunk
    chunks = [slice(c * chunk, (c + 1) * chunk) for c in range(n_chunks)]

    def main(par):
        (cb_w, p_w, w_w), (cb_r, p_r, w_r) = sets[par], sets[1 - par]
        ql = ql_ref[...]
        qp = qp_ref[...]
        w_prev = w_r[...]
        accs = [w_prev[:, :c_lat], jnp.zeros((nh, c_lat), F32)]
        s_lat = []
        for c, cs in enumerate(chunks):
            ck = ckv_buf[par, cs, :].astype(BF16)
            cb_w[cs, :] = ck
            s_lat.append(lax.dot_general(ql, ck, NT_DIMS, preferred_element_type=F32))
            accs[c % 2] += jnp.dot(p_r[:, cs], cb_r[cs, :], preferred_element_type=F32)
        o_ref[...] = (accs[0] + accs[1]) / w_prev[:, c_lat:c_lat + 1]

        for cs, sl in zip(chunks, s_lat):
            s_ref[:, cs] = sl + jnp.dot(qp, kpe_buf[par, :, cs].astype(BF16), preferred_element_type=F32)

        cn = cn_ref[...].astype(BF16).astype(F32)
        kn = kn_ref[...].astype(BF16).astype(F32)
        s_new = (jnp.sum(ql.astype(F32) * cn, axis=1, keepdims=True)
                 + jnp.sum(qp.astype(F32) * kn, axis=1, keepdims=True))
        s = s_ref[...]
        m = jnp.maximum(jnp.max(s, axis=1, keepdims=True), s_new)
        p = jnp.exp2(s - m)
        p_new = jnp.exp2(s_new - m)
        l = jnp.sum(p, axis=1, keepdims=True) + p_new
        p_w[...] = p.astype(BF16)
        w_w[...] = jnp.concatenate([p_new.astype(BF16).astype(F32) * cn, jnp.broadcast_to(l, (nh, LANES))], axis=1)

    for par in range(2):
        pl.when(step % 2 == par)(functools.partial(main, par))


def paged_attention(q_lat, q_pe, ckv_new, kpe_new, cache_ckv, cache_kpe_t, page_table):
    bsz, nh, c = q_lat.shape
    r = q_pe.shape[-1]
    n_pages = page_table.shape[1]
    ps = cache_ckv.shape[1]
    past = n_pages * ps
    assert bsz >= 2, "the two-slot pipeline needs at least two sequences"
    chunk = min(PAGED_CHUNK, past)
    per_b = lambda rows, w: pl.BlockSpec((None, rows, w), lambda s_, pt: (jnp.minimum(s_, bsz - 1), 0, 0))
    return pl.pallas_call(
        functools.partial(_paged_attn_kernel, n_seq=bsz, n_pages=n_pages, ps=ps, chunk=chunk),
        grid_spec=pltpu.PrefetchScalarGridSpec(
            num_scalar_prefetch=1,
            grid=(bsz + 1,),
            in_specs=[per_b(nh, c), per_b(nh, r), per_b(1, c), per_b(1, r),
                      pl.BlockSpec(memory_space=pl.ANY), pl.BlockSpec(memory_space=pl.ANY)],
            out_specs=pl.BlockSpec((None, nh, c), lambda s_, pt: (jnp.maximum(s_ - 1, 0), 0, 0)),
            scratch_shapes=[pltpu.VMEM((2, past, c), F32), pltpu.VMEM((2, r, past), F32),
                            pltpu.VMEM((nh, past), F32)]
            + 2 * [pltpu.VMEM((past, c), BF16), pltpu.VMEM((nh, past), BF16), pltpu.VMEM((nh, c + LANES), F32)]
            + [pltpu.SemaphoreType.DMA((2, 2))]),
        out_shape=jax.ShapeDtypeStruct((bsz, nh, c), F32),
        compiler_params=_params("arbitrary"),
        name="paged_attention",
    )(page_table.reshape(-1), q_lat, q_pe, ckv_new, kpe_new, cache_ckv, cache_kpe_t)


def _rope_tables(pos, rope, lo, scale, passthrough):
    half = rope // 2
    freq = ROPE_THETA ** (-jnp.arange(half, dtype=F32) / half)
    ang = pos.astype(F32)[:, None] * freq[None, :]
    cos, sin = jnp.cos(ang), jnp.sin(ang)
    n = pos.shape[0]
    zeros = lambda w: jnp.zeros((n, w), F32)
    tail = LANES - lo - rope
    c = jnp.concatenate([jnp.full((n, lo), passthrough, F32), cos, cos, zeros(tail)], axis=1)
    sa = jnp.concatenate([zeros(lo), -sin, zeros(half + tail)], axis=1)
    sb = jnp.concatenate([zeros(lo + half), sin, zeros(tail)], axis=1)
    return c * scale, sa * scale, sb * scale


def _prep_weights(w_up, w_down, w_m_in, w_m_out, w_dkv, w_uk, w_uv, w_dq, w_uq, w_o, dims):
    heads, dk, dv = dims["m_heads"], dims["m_dk"], dims["m_dv"]
    nh, nope, rope, kv_lora = dims["mla_heads"], dims["qk_nope"], dims["qk_rope"], dims["kv_lora"]
    ff = w_down.shape[1]
    qd, vd = heads * dk, heads * dv
    d = w_up.shape[1]
    pw = {}
    w_up_b, w_down_b, w_m_in_b = w_up.astype(BF16), w_down.astype(BF16), w_m_in.astype(BF16)
    depth, n_a = w_up.shape[0], w_m_in.shape[0]
    pw["w_up_a"] = [WeightView(w_up_b, (None, d, ff), (l, 0, 0)) for l in range(depth)]
    pw["w_up_g"] = [WeightView(w_up_b, (None, d, ff), (l, 0, 1)) for l in range(depth)]
    pw["w_down"] = [WeightView(w_down_b, (None, ff, d), (l, 0, 0)) for l in range(depth)]
    assert (2 * qd) % vd == 0, "value / output-gate columns must start on a multiple of their width"
    pw["w_m_q"] = [WeightView(w_m_in_b, (None, d, qd), (l, 0, 0)) for l in range(n_a)]
    pw["w_m_k"] = [WeightView(w_m_in_b, (None, d, qd), (l, 0, 1)) for l in range(n_a)]
    pw["w_m_v"] = [WeightView(w_m_in_b, (None, d, vd), (l, 0, 2 * qd // vd)) for l in range(n_a)]
    pw["w_m_o"] = [WeightView(w_m_in_b, (None, d, vd), (l, 0, 2 * qd // vd + 1)) for l in range(n_a)]
    gates = jnp.pad(w_m_in[:, :, 2 * qd + 2 * vd:], ((0, 0), (0, 0), (0, LANES - 2 * heads))).astype(BF16)
    pw["w_m_g"] = [WeightView(gates, (None, d, LANES), (l, 0, 0)) for l in range(n_a)]
    pw["w_m_out"] = w_m_out.astype(BF16)
    pw["w_dkv"] = jnp.pad(w_dkv, ((0, 0), (0, LANES - rope))).astype(BF16)
    pw["w_uk_r"] = jnp.pad(w_uk, ((0, 0), (0, 0), (0, LANES - nope))).reshape(kv_lora, nh * LANES).astype(BF16)
    v_head = w_uv.shape[2]
    pw["w_uv_r"] = jnp.pad(w_uv, ((0, 0), (0, 0), (0, LANES - v_head))).reshape(kv_lora, nh * LANES).T.astype(BF16)
    nb = w_uq.shape[0]
    wq = w_uq.reshape(nb, w_uq.shape[1], nh, nope + rope)
    pw["w_uq_r"] = jnp.pad(wq, ((0, 0), (0, 0), (0, 0), (0, LANES - nope - rope))).reshape(
        nb, w_uq.shape[1], nh * LANES).astype(BF16)
    half = rope // 2
    partner = jnp.concatenate([-wq[..., nope + half:], wq[..., nope:nope + half]], axis=-1)
    pw["w_uq_rot"] = jnp.pad(partner, ((0, 0), (0, 0), (0, 0), (nope, LANES - nope - rope))).reshape(
        nb, w_uq.shape[1], nh * LANES).astype(BF16)
    pw["w_dq"] = w_dq.astype(BF16)
    pw["w_o"] = w_o.astype(BF16)
    wukt = jnp.transpose(w_uk, (1, 2, 0))
    pw["w_uk_t"] = jnp.pad(wukt, ((0, 0), (0, LANES - nope), (0, 0))).astype(BF16)
    pw["w_uv_t"] = jnp.transpose(w_uv, (1, 0, 2)).astype(BF16)
    return pw


def _trunk(x, mods, mods_kv, pos, conv_bufs, m_states, kv_past, pw, small, dims):
    heads, dk, dv = dims["m_heads"], dims["m_dk"], dims["m_dv"]
    nh, nope, rope, kv_lora, v_head = (dims["mla_heads"], dims["qk_nope"], dims["qk_rope"],
                                       dims["kv_lora"], dims["v_head"])
    depth, n_a = dims["depth"], dims["n_a"]
    is_prompt = kv_past is None
    gq, r, d = x.shape
    att_scale = (nope + rope) ** -0.5 * LOG2_E
    new_c, new_n, new_m, new_conv = [], [], [], []
    ckv = kpe = kcat = vv = ksq = None
    y = None
    for layer in range(depth):
        sh1, sc1, gt1, sh2, sc2, gt2 = mods[layer]
        g1 = small["g_norm1"][layer][None, :]
        g2 = small["g_norm2"][layer][None, :]
        if layer == n_a:
            sh_kv, sc_kv = mods_kv
            tabs = _rope_tables(pos, rope, 0, 1.0, 0.0)
            if is_prompt:
                ckv, kpe, kcat, vv, ksq = shared_latent(x, small["g_kv_in"][None, :], sh_kv, sc_kv, pw["w_dkv"],
                                                        small["g_kv"][None, :], tabs, kv_lora, rope,
                                                        pw["w_uk_r"], pw["w_uv_r"], v_head)
            else:
                ckv, kpe = shared_latent(x, small["g_kv_in"][None, :], sh_kv, sc_kv, pw["w_dkv"],
                                         small["g_kv"][None, :], tabs, kv_lora, rope)
        if layer < n_a:
            ws = [pw["w_m_q"][layer], pw["w_m_k"][layer], pw["w_m_v"][layer], pw["w_m_o"][layer], pw["w_m_g"][layer]]
            if is_prompt:
                q, k, v, o, gates = norm_mod_proj(x, g1, sh1, sc1, ws, [BF16, F32, BF16, F32, F32],
                                                  [dk ** -0.5, 1.0, 1.0, 1.0, 1.0], "mlstm_in_proj")
                hh, c_st, n_st, m_st = mlstm_chunkwise(q, k, v, o, gates[..., :2 * heads],
                                                       small["b_m_gates"][layer], small["g_m_head"][layer],
                                                       heads, dk, dv)
                n_st = n_st.reshape(gq, heads, dk)
                m_st = m_st.reshape(gq, heads)
            else:
                q, k, v, o, gates = norm_mod_proj(x, g1, sh1, sc1, ws, [F32] * 5,
                                                  [dk ** -0.5, 1.0, 1.0, 1.0, 1.0], "mlstm_in_proj_tok")
                hh, c_st, n_st, m_st = mlstm_step(q[0], k[0], v[0], o[0], gates[0, :, :2 * heads],
                                                  small["b_m_gates"][layer], small["g_m_head"][layer],
                                                  m_states[0][layer], m_states[1][layer], m_states[2][layer],
                                                  heads, dk, dv)
                hh = hh[None]
            new_c.append(c_st)
            new_n.append(n_st)
            new_m.append(m_st)
            mix, w_mix = hh, pw["w_m_out"][layer]
        else:
            j = layer - n_a
            q_cos, q_msin, q_sin = _rope_tables(pos, rope, nope, att_scale, 1.0)
            qh, qsq = mla_queries(x, g1, sh1, sc1, pw["w_dq"][j], small["g_q"][j][None, :], pw["w_uq_r"][j],
                                  pw["w_uq_rot"][j], q_cos, q_sin - q_msin)
            if is_prompt:
                att = prompt_attention(qh, kcat, vv, qsq, ksq[:, :, 0, :], v_head)
            else:
                bsz = r
                q2 = qh[0]
                q_lat = head_proj_lanes(q2, pw["w_uk_t"], BF16, "absorb_q")
                q_lat = jnp.swapaxes(q_lat, 0, 1)
                q_pe = q2.reshape(bsz, nh, LANES)[:, :, nope:nope + rope]
                o_lat = paged_attention(q_lat, q_pe, ckv[0][:, None, :], kpe[0][:, None, :],
                                        kv_past[0], kv_past[1], kv_past[2])
                o_lat = o_lat.reshape(bsz, nh * kv_lora)
                att = head_proj_lanes(o_lat, pw["w_uv_t"], F32, "unabsorb_o")
                att = jnp.swapaxes(att, 0, 1).reshape(1, bsz, nh * v_head)
            mix, w_mix = att, pw["w_o"][j]
        final = layer == depth - 1
        gf = small["g_final"][None, :]
        if is_prompt:
            x, a_last = conv_ffn_seq(x, mix, w_mix, gt1, conv_bufs[layer], g2, sh2, sc2, gt2,
                                     pw["w_up_a"][layer], pw["w_up_g"][layer], small["w_conv"][layer],
                                     small["b_conv"][layer][None, :], pw["w_down"][layer], gf, final, "conv_ffn_seq")
            new_conv.append(a_last[:, 6:8, :])
        else:
            x = resid_proj(mix, w_mix, x, gt1, "mixer_out_proj")
            buf = conv_bufs[layer]
            x2, a_new = conv_ffn_tok(x[0], buf[:, 0, :], buf[:, 1, :], g2, sh2[0], sc2[0], gt2[0],
                                     pw["w_up_a"][layer], pw["w_up_g"][layer], small["w_conv"][layer],
                                     small["b_conv"][layer][None, :], pw["w_down"][layer], gf, final, "conv_ffn_tok")
            x = x2[None]
            new_conv.append(jnp.stack([buf[:, 1, :], a_new], axis=1))
    return x, jnp.stack(new_c), jnp.stack(new_n), jnp.stack(new_m), jnp.stack(new_conv), ckv, kpe


def kernel(x_prompt, x_sample, state_mlstm_C, state_mlstm_n, state_mlstm_m, state_conv, cache_ckv, cache_kpe,
           page_table, c_prompt, c_sample, g_norm1, g_norm2, w_ada, b_ada, w_up, w_conv, b_conv, w_down,
           w_m_in, b_m_gates, g_m_head, w_m_out, g_kv_in, w_ada_kv, b_ada_kv, w_dkv, g_kv, w_uk, w_uv,
           w_dq, g_q, w_uq, w_o, g_final):
    bp, s, d = x_prompt.shape
    bs, t, _ = x_sample.shape
    assert t == 1, "the sample path handles one new token per sequence"
    depth = w_ada.shape[0]
    n_a = w_m_in.shape[0]
    heads, dv = g_m_head.shape[1], g_m_head.shape[2]
    dk = state_mlstm_C.shape[3]
    kv_lora, nh, nope = w_uk.shape
    v_head = w_uv.shape[2]
    rope = w_dkv.shape[1] - kv_lora
    ff = w_down.shape[1]
    dims = dict(m_heads=heads, m_dk=dk, m_dv=dv, mla_heads=nh, qk_nope=nope, qk_rope=rope, kv_lora=kv_lora,
                v_head=v_head, depth=depth, n_a=n_a)
    past_len = page_table.shape[1] * cache_ckv.shape[1]

    pw = _prep_weights(w_up, w_down, w_m_in, w_m_out, w_dkv, w_uk, w_uv, w_dq, w_uq, w_o, dims)
    small = dict(g_norm1=g_norm1, g_norm2=g_norm2, w_conv=w_conv, b_conv=b_conv, b_m_gates=b_m_gates,
                 g_m_head=g_m_head, g_kv_in=g_kv_in, g_kv=g_kv, g_q=g_q, g_final=g_final)

    c_all = jnp.concatenate([c_prompt, c_sample], axis=0)
    mod = ada_mod(c_all, w_ada, b_ada[:, None, :])
    mod_kv = ada_mod(c_all, w_ada_kv[None], b_ada_kv[None, None, :])[0]

    def split(m, n, lo, hi, per_row):
        parts = jnp.split(m[lo:hi], n, axis=-1)
        return [p[None] if per_row else p[:, None, :] for p in parts]

    mods_p = [split(mod[l], 6, 0, bp, False) for l in range(depth)]
    mods_s = [split(mod[l], 6, bp, bp + bs, True) for l in range(depth)]
    kv_p = split(mod_kv, 2, 0, bp, False)
    kv_s = split(mod_kv, 2, bp, bp + bs, True)

    conv0 = [jnp.zeros((bp, 8, ff), F32)] * depth
    y_p, c_p, n_p, m_p, conv_p, ckv_p, kpe_p = _trunk(
        x_prompt, mods_p, kv_p, jnp.arange(s), conv0, None, None, pw, small, dims)

    pos_s = jnp.full((bs,), past_len, jnp.int32)
    y_s, c_s, n_s, m_s, conv_s, ckv_s, kpe_s = _trunk(
        x_sample.reshape(1, bs, d), mods_s, kv_s, pos_s, state_conv,
        (state_mlstm_C, state_mlstm_n, state_mlstm_m), (cache_ckv, jnp.swapaxes(cache_kpe, 1, 2), page_table), pw, small, dims)

    return (y_p, y_s.reshape(bs, 1, d), c_p, n_p, m_p, conv_p, ckv_p, kpe_p,
            c_s, n_s, m_s, conv_s, ckv_s.reshape(bs, 1, kv_lora), kpe_s.reshape(bs, 1, rope))
```

```python
import functools
import math
from typing import NamedTuple

import jax
import jax.numpy as jnp
from jax import lax
from jax.experimental import pallas as pl
from jax.experimental.pallas import tpu as pltpu

F32 = jnp.float32
BF16 = jnp.bfloat16

NORM_EPS = 1e-6
ROPE_THETA = 10000.0
LANES = 128
BF16_ROWS = 16
CONV_HALO = 8
VMEM_LIMIT = 56 * 1024 * 1024

ROW_TILE = 512
FFN_ROW_TILE = 512
MLSTM_CHUNK = 256
ATTN_TILE = 2048
ATTN_BLOCK = 512
FFN_CHUNK = 256
PAGED_CHUNK = 512
LOG2_E = math.log2(math.e)
SOFTMAX_MIN_ROW_SUM = 2.0 ** -60
STEP_BATCH = 8

NT_DIMS = (((1,), (1,)), ((), ()))
TN_DIMS = (((0,), (0,)), ((), ()))


def _params(*sem):
    return pltpu.CompilerParams(dimension_semantics=sem, vmem_limit_bytes=VMEM_LIMIT)


def _const_spec(shape):
    nd = len(shape)
    return pl.BlockSpec(shape, lambda *_: (0,) * nd, pipeline_mode=pl.Buffered(1))


class WeightView(NamedTuple):
    array: jax.Array
    block: tuple
    index: tuple

    @property
    def shape(self):
        return tuple(d for d in self.block if d is not None)

    def spec(self):
        return pl.BlockSpec(self.block, lambda *_: self.index, pipeline_mode=pl.Buffered(1))

    def chunk_spec(self, axis, size):
        n = self.block[axis] // size
        block = tuple(size if a == axis else d for a, d in enumerate(self.block))
        index = lambda c: tuple(i * n + c if a == axis else i for a, i in enumerate(self.index))
        return pl.BlockSpec(block, index)


def _rms(x, g):
    return x * lax.rsqrt(jnp.mean(x * x, axis=-1, keepdims=True) + NORM_EPS) * g


def _norm_mod(x, g, sh, sc):
    return _rms(x, g) * (1.0 + sc) + sh


def _log_sigmoid(x):
    return jnp.minimum(x, 0.0) - jnp.log(1.0 + jnp.exp(-jnp.abs(x)))


def _rope3(t, c, sa, sb, half):
    return t * c + pltpu.roll(t, LANES - half, axis=1) * sa + pltpu.roll(t, half, axis=1) * sb


def _row_spec(tm, width):
    return pl.BlockSpec((None, tm, width), lambda g, i: (g, i, 0))


def _mod_spec(per_row, tm, width):
    if per_row:
        return pl.BlockSpec((None, tm, width), lambda g, i: (g, i, 0))
    return pl.BlockSpec((None, 1, width), lambda g, i: (g, 0, 0))


def _ada_kernel(c_ref, w_ref, b_ref, o_ref):
    c = c_ref[...]
    a = (c * jax.nn.sigmoid(c)).astype(BF16)
    o_ref[...] = jnp.dot(a, w_ref[...].astype(BF16), preferred_element_type=F32) + b_ref[...]


def ada_mod(c, w, b, tn=1024):
    m, d = c.shape
    nl, _, n = w.shape
    tn = min(tn, n)
    return pl.pallas_call(
        _ada_kernel,
        grid=(nl, n // tn),
        in_specs=[pl.BlockSpec((m, d), lambda l, j: (0, 0)),
                  pl.BlockSpec((None, d, tn), lambda l, j: (l, 0, j)),
                  pl.BlockSpec((None, 1, tn), lambda l, j: (l, 0, j))],
        out_specs=pl.BlockSpec((None, m, tn), lambda l, j: (l, 0, j)),
        out_shape=jax.ShapeDtypeStruct((nl, m, n), F32),
        compiler_params=_params("parallel", "parallel"),
        name="ada_mod",
    )(c, w, b)


def _proj_kernel(x_ref, g_ref, sh_ref, sc_ref, *refs, scales):
    n = len(scales)
    hn = _norm_mod(x_ref[...], g_ref[...], sh_ref[...], sc_ref[...]).astype(BF16)
    for w_ref, o_ref, s in zip(refs[:n], refs[n:], scales):
        acc = jnp.dot(hn, w_ref[...], preferred_element_type=F32)
        if s != 1.0:
            acc = acc * s
        o_ref[...] = acc.astype(o_ref.dtype)


def norm_mod_proj(x, g, sh, sc, ws, out_dtypes, scales, name):
    gq, r, d = x.shape
    tm = min(ROW_TILE, r)
    per_row = sh.shape[1] != 1
    in_specs = [_row_spec(tm, d), _const_spec((1, d)),
                _mod_spec(per_row, tm, d), _mod_spec(per_row, tm, d)]
    in_specs += [w.spec() for w in ws]
    return pl.pallas_call(
        functools.partial(_proj_kernel, scales=tuple(scales)),
        grid=(gq, r // tm),
        in_specs=in_specs,
        out_specs=[_row_spec(tm, w.shape[1]) for w in ws],
        out_shape=[jax.ShapeDtypeStruct((gq, r, w.shape[1]), dt) for w, dt in zip(ws, out_dtypes)],
        compiler_params=_params("parallel", "parallel"),
        name=name,
    )(x, g, sh, sc, *[w.array for w in ws])


def _resid_kernel(a_ref, w_ref, x_ref, gt_ref, o_ref):
    mix = jnp.dot(a_ref[...].astype(BF16), w_ref[...], preferred_element_type=F32)
    o_ref[...] = x_ref[...] + gt_ref[...] * mix


def resid_proj(a, w, x, gt, name):
    gq, r, d = x.shape
    k = a.shape[-1]
    tm = min(ROW_TILE, r)
    per_row = gt.shape[1] != 1
    return pl.pallas_call(
        _resid_kernel,
        grid=(gq, r // tm),
        in_specs=[_row_spec(tm, k), _const_spec(w.shape), _row_spec(tm, d), _mod_spec(per_row, tm, d)],
        out_specs=_row_spec(tm, d),
        out_shape=jax.ShapeDtypeStruct((gq, r, d), F32),
        compiler_params=_params("parallel", "parallel"),
        name=name,
    )(a, w, x, gt)


def _mlstm_chunk_kernel(q_ref, k_ref, v_ref, o_ref, gr_ref, gc_ref, bgc_ref, bgr_ref, gh_ref,
                        hh_ref, c_ref, n_ref, m_ref, *, heads, dk, dv):
    ci = pl.program_id(1)
    chunk = q_ref.shape[0]

    @pl.when(ci == 0)
    def _():
        c_ref[...] = jnp.zeros_like(c_ref)
        n_ref[...] = jnp.zeros_like(n_ref)
        m_ref[...] = jnp.zeros_like(m_ref)

    gates_r = gr_ref[...] + bgc_ref[...]
    gates_c = gc_ref[...] + bgr_ref[...]
    row = lax.broadcasted_iota(jnp.int32, (chunk, chunk), 0)
    col = lax.broadcasted_iota(jnp.int32, (chunk, chunk), 1)
    causal = col <= row

    s_raw, q_c = [], []
    for h in range(heads):
        q = q_ref[:, h * dk:(h + 1) * dk]
        s_raw.append(lax.dot_general(q, k_ref[:, h * dk:(h + 1) * dk].astype(BF16), NT_DIMS,
                                     preferred_element_type=F32))
        q_c.append(jnp.dot(q, c_ref[h].astype(BF16), preferred_element_type=F32))

    hr = range(heads)
    qs = [q_ref[:, h * dk:(h + 1) * dk] for h in hr]
    vs = [v_ref[:, h * dv:(h + 1) * dv] for h in hr]
    logf_r = _log_sigmoid(gates_r[heads:, :])
    logf_c = _log_sigmoid(gates_c[:, heads:])
    li_r = [gates_r[h:h + 1, :] for h in hr]
    lf_r = [logf_r[h:h + 1, :] for h in hr]
    li_c = [gates_c[:, h:h + 1] for h in hr]
    lf_c = [logf_c[:, h:h + 1] for h in hr]
    m_prev = [m_ref[h] for h in hr]
    n_prev = [n_ref[h] for h in hr]

    def lane_cumsum(x_r):
        low = jnp.where(causal, x_r, 0.0)
        if chunk % LANES == 0 and chunk > LANES:
            low = functools.reduce(jnp.add, [low[:, t:t + LANES] for t in range(0, chunk, LANES)])
        return jnp.sum(low, axis=1, keepdims=True)

    b_c = [lane_cumsum(lf_r[h]) for h in hr]
    b_r = [jnp.sum(jnp.where(row <= col, lf_c[h], 0.0), axis=0, keepdims=True) for h in hr]
    g = [jnp.sum(lf_r[h], axis=1, keepdims=True) for h in hr]

    a_c = [g[h] - b_c[h] + li_c[h] for h in hr]
    m_loc = [jnp.max(a_c[h], axis=0, keepdims=True) for h in hr]
    kw = [k_ref[:, h * dk:(h + 1) * dk] * jnp.exp(a_c[h] - m_loc[h]) for h in hr]
    c_loc = [lax.dot_general(kw[h].astype(BF16), vs[h], TN_DIMS, preferred_element_type=F32) for h in hr]
    n_loc = [jnp.sum(kw[h], axis=0, keepdims=True) for h in hr]

    dmat = [jnp.where(causal, b_c[h] - b_r[h] + li_r[h], -jnp.inf) for h in hr]
    w0 = [b_c[h] + m_prev[h] for h in hr]
    m_s = [jnp.maximum(w0[h], jnp.max(dmat[h], axis=1, keepdims=True)) for h in hr]
    w_inter = [jnp.exp(w0[h] - m_s[h]) for h in hr]
    s = [s_raw[h] * jnp.exp(dmat[h] - m_s[h]) for h in hr]
    num = [w_inter[h] * q_c[h] + jnp.dot(s[h].astype(BF16), vs[h], preferred_element_type=F32) for h in hr]
    def lane_fold(x):
        width = x.shape[1]
        if width % LANES or width == LANES:
            return x
        return functools.reduce(jnp.add, [x[:, t:t + LANES] for t in range(0, width, LANES)])

    def row_total(a, b):
        a, b = lane_fold(a), lane_fold(b)
        if a.shape == b.shape:
            return jnp.sum(a + b, axis=1, keepdims=True)
        return jnp.sum(a, axis=1, keepdims=True) + jnp.sum(b, axis=1, keepdims=True)

    den = [row_total(w_inter[h] * (qs[h].astype(F32) * n_prev[h]), s[h]) for h in hr]
    hval = [num[h] / jnp.maximum(jnp.abs(den[h]), jnp.exp(-m_s[h])) for h in hr]
    hn = [_rms(hval[h], gh_ref[h:h + 1, :]) for h in hr]
    for h in hr:
        gate = jax.nn.sigmoid(o_ref[:, h * dv:(h + 1) * dv])
        hh_ref[:, h * dv:(h + 1) * dv] = (hn[h] * gate).astype(hh_ref.dtype)

    for h in hr:
        m_new = jnp.maximum(g[h] + m_prev[h], m_loc[h])
        fw = jnp.exp(g[h] + m_prev[h] - m_new)
        lw = jnp.exp(m_loc[h] - m_new)
        c_ref[h] = fw * c_ref[h] + lw * c_loc[h]
        n_ref[h] = fw * n_prev[h] + lw * n_loc[h]
        m_ref[h] = m_new


def mlstm_chunkwise(q, k, v, o, gates, b_gates, g_head, heads, dk, dv):
    bsz, s, _ = q.shape
    chunk = min(MLSTM_CHUNK, s)
    gates_r = jnp.swapaxes(gates, 1, 2)
    g2 = 2 * heads
    return pl.pallas_call(
        functools.partial(_mlstm_chunk_kernel, heads=heads, dk=dk, dv=dv),
        grid=(bsz, s // chunk),
        in_specs=[_row_spec(chunk, heads * dk), _row_spec(chunk, heads * dk),
                  _row_spec(chunk, heads * dv), _row_spec(chunk, heads * dv),
                  pl.BlockSpec((None, g2, chunk), lambda b, c: (b, 0, c)),
                  _row_spec(chunk, g2),
                  _const_spec((g2, 1)), _const_spec((1, g2)), _const_spec((heads, dv))],
        out_specs=[_row_spec(chunk, heads * dv),
                   pl.BlockSpec((None, heads, dk, dv), lambda b, c: (b, 0, 0, 0)),
                   pl.BlockSpec((None, heads, 1, dk), lambda b, c: (b, 0, 0, 0)),
                   pl.BlockSpec((None, heads, 1, 1), lambda b, c: (b, 0, 0, 0))],
        out_shape=[jax.ShapeDtypeStruct((bsz, s, heads * dv), BF16),
                   jax.ShapeDtypeStruct((bsz, heads, dk, dv), F32),
                   jax.ShapeDtypeStruct((bsz, heads, 1, dk), F32),
                   jax.ShapeDtypeStruct((bsz, heads, 1, 1), F32)],
        compiler_params=_params("parallel", "arbitrary"),
        name="mlstm_chunkwise",
    )(q, k, v, o, gates_r, gates, b_gates.reshape(g2, 1), b_gates.reshape(1, g2), g_head)


def _mlstm_step_kernel(q_ref, k_ref, v_ref, o_ref, g_ref, bg_ref, gh_ref, c_ref, n_ref, m_ref,
                       hh_ref, co_ref, no_ref, mo_ref, *, heads, dk, dv):
    nb = q_ref.shape[0]
    gates = g_ref[...] + bg_ref[...]
    li = gates[:, :heads]
    lf = _log_sigmoid(gates[:, heads:])
    m_st = m_ref[...]
    m_new = jnp.maximum(lf + m_st, li)
    fw_all = jnp.exp(lf + m_st - m_new)
    iw_all = jnp.exp(li - m_new)
    floor_all = jnp.exp(-m_new)
    mo_ref[...] = m_new
    eye = lax.broadcasted_iota(jnp.int32, (dk, dk), 0) == lax.broadcasted_iota(jnp.int32, (dk, dk), 1)

    def to_col(r):
        return jnp.sum(jnp.where(eye, r, 0.0), axis=1, keepdims=True)

    hr = range(heads)
    for b in range(nb):
        q_r = [q_ref[b:b + 1, h * dk:(h + 1) * dk] for h in hr]
        k_r = [k_ref[b:b + 1, h * dk:(h + 1) * dk] for h in hr]
        k_c = [iw_all[b:b + 1, h:h + 1] * to_col(k_r[h]) for h in hr]
        c_new = [fw_all[b:b + 1, h:h + 1] * c_ref[b, h] + k_c[h] * v_ref[b:b + 1, h * dv:(h + 1) * dv] for h in hr]
        num = [jnp.dot(q_r[h].astype(BF16), c_new[h].astype(BF16), preferred_element_type=F32) for h in hr]
        n_new = [fw_all[b:b + 1, h:h + 1] * n_ref[b, h:h + 1, :] + iw_all[b:b + 1, h:h + 1] * k_r[h] for h in hr]
        den = [jnp.sum(q_r[h] * n_new[h], axis=1, keepdims=True) for h in hr]
        hval = [num[h] / jnp.maximum(jnp.abs(den[h]), floor_all[b:b + 1, h:h + 1]) for h in hr]
        hn = [_rms(hval[h], gh_ref[h:h + 1, :]) for h in hr]
        for h in hr:
            co_ref[b, h] = c_new[h]
            no_ref[b, h:h + 1, :] = n_new[h]
            hh_ref[b:b + 1, h * dv:(h + 1) * dv] = hn[h] * jax.nn.sigmoid(o_ref[b:b + 1, h * dv:(h + 1) * dv])


def mlstm_step(q, k, v, o, gates, b_gates, g_head, c_st, n_st, m_st, heads, dk, dv):
    bsz = q.shape[0]
    nb = min(STEP_BATCH, bsz)
    g2 = 2 * heads
    rows = lambda w: pl.BlockSpec((nb, w), lambda i: (i, 0))
    return pl.pallas_call(
        functools.partial(_mlstm_step_kernel, heads=heads, dk=dk, dv=dv),
        grid=(bsz // nb,),
        in_specs=[rows(heads * dk), rows(heads * dk), rows(heads * dv), rows(heads * dv), rows(g2),
                  _const_spec((1, g2)), _const_spec((heads, dv)),
                  pl.BlockSpec((nb, heads, dk, dv), lambda i: (i, 0, 0, 0)),
                  pl.BlockSpec((nb, heads, dk), lambda i: (i, 0, 0)),
                  rows(heads)],
        out_specs=[rows(heads * dv),
                   pl.BlockSpec((nb, heads, dk, dv), lambda i: (i, 0, 0, 0)),
                   pl.BlockSpec((nb, heads, dk), lambda i: (i, 0, 0)),
                   rows(heads)],
        out_shape=[jax.ShapeDtypeStruct((bsz, heads * dv), F32),
                   jax.ShapeDtypeStruct((bsz, heads, dk, dv), F32),
                   jax.ShapeDtypeStruct((bsz, heads, dk), F32),
                   jax.ShapeDtypeStruct((bsz, heads), F32)],
        compiler_params=_params("parallel"),
        name="mlstm_step",
    )(q, k, v, o, gates, b_gates.reshape(1, g2), g_head, c_st, n_st, m_st)


def _ffn_seq_kernel(x_ref, xh_ref, mix_ref, mixh_ref, wm_ref, gtm_ref, buf_ref, g_ref, sh_ref, sc_ref, gt_ref,
                    wa_ref, wg_ref, wc_ref, bc_ref, wd_ref, gf_ref, o_ref, alast_ref, *, fc, final_norm):
    i = pl.program_id(1)
    tm = x_ref.shape[0]
    ff = wa_ref.shape[1]
    halo = xh_ref.shape[0]
    taps = wc_ref.shape[0]
    gtm, wm = gtm_ref[...], wm_ref[...]
    x = x_ref[...] + gtm * jnp.dot(mix_ref[...], wm, preferred_element_type=F32)
    xh = xh_ref[...] + gtm * jnp.dot(mixh_ref[...], wm, preferred_element_type=F32)[mixh_ref.shape[0] - halo:]
    g, sh, sc = g_ref[...], sh_ref[...], sc_ref[...]
    hn = _norm_mod(x, g, sh, sc).astype(BF16)
    hh = _norm_mod(xh, g, sh, sc).astype(BF16)
    first = i == 0

    def up(c):
        cs = slice(c * fc, (c + 1) * fc)
        wa = wa_ref[:, cs]
        return (jnp.dot(hn, wa, preferred_element_type=F32),
                jnp.dot(hn, wg_ref[:, cs], preferred_element_type=F32),
                jnp.dot(hh, wa, preferred_element_type=F32))

    n_chunks = ff // fc
    acc = None
    nxt = up(0)
    for c in range(n_chunks):
        cs = slice(c * fc, (c + 1) * fc)
        a, gt, a_halo = nxt
        if c + 1 < n_chunks:
            nxt = up(c + 1)
        ext = jnp.concatenate([jnp.where(first, buf_ref[:, cs], a_halo), a], axis=0)
        conv = bc_ref[:, cs]
        for j in range(taps):
            lo = halo - (taps - 1) + j
            conv = conv + (a if lo == halo else ext[lo:lo + tm, :]) * wc_ref[j:j + 1, cs]
        act = (conv * jax.nn.sigmoid(conv) * gt).astype(BF16)
        part = jnp.dot(act, wd_ref[cs, :], preferred_element_type=F32)
        acc = part if acc is None else acc + part
        alast_ref[:, cs] = a[tm - halo:tm, :]

    y = x + gt_ref[...] * acc
    if final_norm:
        y = _rms(y, gf_ref[...])
    o_ref[...] = y


def conv_ffn_seq(x, mix, w_mix, gt_mix, buf, g, sh, sc, gt, w_up_a, w_up_g, w_conv, b_conv, w_down, g_final,
                 final_norm, name):
    bsz, s, d = x.shape
    ff = w_up_a.shape[1]
    kk = mix.shape[-1]
    tm = min(FFN_ROW_TILE, s)
    fc = min(FFN_CHUNK, ff)
    assert w_conv.shape[0] - 1 <= CONV_HALO
    halo = lambda b, i: (b, jnp.maximum(i * (tm // CONV_HALO) - 1, 0), 0)
    halo_bf16 = lambda b, i: (b, jnp.maximum(i * (tm // BF16_ROWS) - 1, 0), 0)
    return pl.pallas_call(
        functools.partial(_ffn_seq_kernel, fc=fc, final_norm=final_norm),
        grid=(bsz, s // tm),
        in_specs=[_row_spec(tm, d), pl.BlockSpec((None, CONV_HALO, d), halo),
                  _row_spec(tm, kk), pl.BlockSpec((None, BF16_ROWS, kk), halo_bf16),
                  _const_spec(w_mix.shape), _mod_spec(False, tm, d),
                  pl.BlockSpec((None, CONV_HALO, ff), lambda b, i: (b, 0, 0)),
                  _const_spec((1, d)), _mod_spec(False, tm, d), _mod_spec(False, tm, d), _mod_spec(False, tm, d),
                  w_up_a.spec(), w_up_g.spec(), _const_spec(w_conv.shape),
                  _const_spec(b_conv.shape), w_down.spec(), _const_spec((1, d))],
        out_specs=[_row_spec(tm, d), pl.BlockSpec((None, CONV_HALO, ff), lambda b, i: (b, 0, 0))],
        out_shape=[jax.ShapeDtypeStruct((bsz, s, d), F32), jax.ShapeDtypeStruct((bsz, CONV_HALO, ff), F32)],
        compiler_params=_params("parallel", "arbitrary"),
        name=name,
    )(x, x, mix, mix, w_mix, gt_mix, buf, g, sh, sc, gt, w_up_a.array, w_up_g.array, w_conv, b_conv,
      w_down.array, g_final)


def _ffn_tok_kernel(x_ref, b0_ref, b1_ref, g_ref, sh_ref, sc_ref, gt_ref, wa_ref, wg_ref, wc_ref,
                    bc_ref, wd_ref, gf_ref, o_ref, a_ref, acc_ref, *, final_norm):
    c = pl.program_id(0)
    x = x_ref[...]
    hn = _norm_mod(x, g_ref[...], sh_ref[...], sc_ref[...]).astype(BF16)
    a = jnp.dot(hn, wa_ref[...], preferred_element_type=F32)
    gt = jnp.dot(hn, wg_ref[...], preferred_element_type=F32)
    a_ref[...] = a
    conv = bc_ref[...] + b0_ref[...] * wc_ref[0:1, :] + b1_ref[...] * wc_ref[1:2, :] + a * wc_ref[2:3, :]
    act = (conv * jax.nn.sigmoid(conv) * gt).astype(BF16)
    part = jnp.dot(act, wd_ref[...], preferred_element_type=F32)

    @pl.when(c == 0)
    def _():
        acc_ref[...] = part

    @pl.when(c > 0)
    def _():
        acc_ref[...] += part

    y = x + gt_ref[...] * acc_ref[...]
    if final_norm:
        y = _rms(y, gf_ref[...])
    o_ref[...] = y


def conv_ffn_tok(x, buf0, buf1, g, sh, sc, gt, w_up_a, w_up_g, w_conv, b_conv, w_down, g_final, final_norm, name):
    bsz, d = x.shape
    ff = w_up_a.shape[1]
    fc = min(FFN_CHUNK, ff)
    full = lambda w: pl.BlockSpec((bsz, w), lambda c: (0, 0))
    cols = lambda r: pl.BlockSpec((r, fc), lambda c: (0, c))
    up_axis, down_axis = len(w_up_a.block) - 1, len(w_down.block) - 2
    return pl.pallas_call(
        functools.partial(_ffn_tok_kernel, final_norm=final_norm),
        grid=(ff // fc,),
        in_specs=[full(d), cols(bsz), cols(bsz), _const_spec((1, d)), full(d), full(d), full(d),
                  w_up_a.chunk_spec(up_axis, fc), w_up_g.chunk_spec(up_axis, fc), cols(w_conv.shape[0]), cols(1),
                  w_down.chunk_spec(down_axis, fc), _const_spec((1, d))],
        out_specs=[full(d), cols(bsz)],
        out_shape=[jax.ShapeDtypeStruct((bsz, d), F32), jax.ShapeDtypeStruct((bsz, ff), F32)],
        scratch_shapes=[pltpu.VMEM((bsz, d), F32)],
        compiler_params=_params("arbitrary"),
        name=name,
    )(x, buf0, buf1, g, sh, sc, gt, w_up_a.array, w_up_g.array, w_conv, b_conv, w_down.array, g_final)


def _latent_kernel(x_ref, g_ref, sh_ref, sc_ref, w_ref, gkv_ref, rc_ref, ra_ref, rb_ref, *refs,
                   kv_lora, rope, nope, with_kv):
    if with_kv:
        wuk_ref, wuv_ref, one_ref, ckv_ref, kpe_ref, kcat_ref, v_ref, ksq_ref = refs
    else:
        ckv_ref, kpe_ref = refs
    hn = _norm_mod(x_ref[...], g_ref[...], sh_ref[...], sc_ref[...]).astype(BF16)
    lat = jnp.dot(hn, w_ref[...], preferred_element_type=F32)
    ckv = _rms(lat[:, :kv_lora], gkv_ref[...])
    ckv_ref[...] = ckv
    kpe = _rope3(lat[:, kv_lora:kv_lora + LANES], rc_ref[...], ra_ref[...], rb_ref[...], rope // 2)
    kpe_ref[...] = kpe[:, :rope]
    if with_kv:
        cb = ckv.astype(BF16)
        kn = jnp.dot(cb, wuk_ref[...], preferred_element_type=F32)
        kpe_hi = pltpu.roll(kpe, nope, axis=1)
        for h in range(kn.shape[1] // LANES):
            hs = slice(h * LANES, (h + 1) * LANES)
            kb = (kn[:, hs] + kpe_hi).astype(BF16)
            kcat_ref[:, hs] = kb
            kb = kb.astype(F32)
            ksq_ref[:, h:h + 1] = jnp.max(jnp.sum(kb * kb, axis=1, keepdims=True), axis=0, keepdims=True)
        v_t = lax.dot_general(wuv_ref[...], cb, NT_DIMS, preferred_element_type=F32)
        v_ref[...] = (v_t + one_ref[...]).astype(BF16)


def shared_latent(x, g, sh, sc, w_dkv_p, g_kv, tabs, kv_lora, rope, nope, w_uk_r=None, w_uv_r=None, v_head=None):
    gq, r, d = x.shape
    tm = min(ROW_TILE, r)
    per_row = sh.shape[1] != 1
    with_kv = w_uk_r is not None
    tab_spec = pl.BlockSpec((tm, LANES), lambda g_, i: (i, 0))
    in_specs = [_row_spec(tm, d), _const_spec((1, d)), _mod_spec(per_row, tm, d), _mod_spec(per_row, tm, d),
                _const_spec(w_dkv_p.shape), _const_spec((1, kv_lora)), tab_spec, tab_spec, tab_spec]
    out_specs = [_row_spec(tm, kv_lora), _row_spec(tm, rope)]
    out_shape = [jax.ShapeDtypeStruct((gq, r, kv_lora), F32), jax.ShapeDtypeStruct((gq, r, rope), F32)]
    args = [x, g, sh, sc, w_dkv_p, g_kv, *tabs]
    if with_kv:
        vt_rows = w_uv_r.shape[0]
        ones_row = (jnp.arange(vt_rows) % LANES == v_head).astype(F32)[:, None]
        in_specs += [_const_spec(w_uk_r.shape), _const_spec(w_uv_r.shape), _const_spec(ones_row.shape)]
        n_heads = w_uk_r.shape[1] // LANES
        out_specs += [_row_spec(tm, w_uk_r.shape[1]), pl.BlockSpec((None, vt_rows, tm), lambda g_, i: (g_, 0, i)),
                      pl.BlockSpec((None, None, 1, n_heads), lambda g_, i: (g_, i, 0, 0))]
        out_shape += [jax.ShapeDtypeStruct((gq, r, w_uk_r.shape[1]), BF16),
                      jax.ShapeDtypeStruct((gq, vt_rows, r), BF16),
                      jax.ShapeDtypeStruct((gq, r // tm, 1, n_heads), F32)]
        args += [w_uk_r, w_uv_r, ones_row]
    return pl.pallas_call(
        functools.partial(_latent_kernel, kv_lora=kv_lora, rope=rope, nope=nope, with_kv=with_kv),
        grid=(gq, r // tm),
        in_specs=in_specs, out_specs=out_specs, out_shape=out_shape,
        compiler_params=_params("parallel", "parallel"),
        name="shared_latent_kv" if with_kv else "shared_latent",
    )(*args)


def _query_kernel(x_ref, g_ref, sh_ref, sc_ref, wdq_ref, gq_ref, wuq_ref, wrot_ref, rc_ref, rs_ref, q_ref, qsq_ref):
    hn = _norm_mod(x_ref[...], g_ref[...], sh_ref[...], sc_ref[...]).astype(BF16)
    qd = jnp.dot(hn, wdq_ref[...], preferred_element_type=F32)
    qn = _rms(qd, gq_ref[...]).astype(BF16)
    qf = jnp.dot(qn, wuq_ref[...], preferred_element_type=F32)
    qr = jnp.dot(qn, wrot_ref[...], preferred_element_type=F32)
    rc, rs = rc_ref[...], rs_ref[...]
    for h in range(qf.shape[1] // LANES):
        hs = slice(h * LANES, (h + 1) * LANES)
        qb = (qf[:, hs] * rc + qr[:, hs] * rs).astype(q_ref.dtype)
        q_ref[:, hs] = qb
        qb = qb.astype(F32)
        qsq_ref[:, h:h + 1] = jnp.sum(qb * qb, axis=1, keepdims=True)


def mla_queries(x, g, sh, sc, w_dq, g_q, w_uq_r, w_uq_rot, tab_cos, tab_sin):
    gq, r, d = x.shape
    tm = min(ROW_TILE, r)
    per_row = sh.shape[1] != 1
    n_heads = w_uq_r.shape[1] // LANES
    tab_spec = pl.BlockSpec((tm, LANES), lambda g_, i: (i, 0))
    return pl.pallas_call(
        _query_kernel,
        grid=(gq, r // tm),
        in_specs=[_row_spec(tm, d), _const_spec((1, d)), _mod_spec(per_row, tm, d), _mod_spec(per_row, tm, d),
                  _const_spec(w_dq.shape), _const_spec(g_q.shape), _const_spec(w_uq_r.shape),
                  _const_spec(w_uq_rot.shape), tab_spec, tab_spec],
        out_specs=[_row_spec(tm, w_uq_r.shape[1]), _row_spec(tm, n_heads)],
        out_shape=[jax.ShapeDtypeStruct((gq, r, w_uq_r.shape[1]), BF16),
                   jax.ShapeDtypeStruct((gq, r, n_heads), F32)],
        compiler_params=_params("parallel", "parallel"),
        name="mla_queries",
    )(x, g, sh, sc, w_dq, g_q, w_uq_r, w_uq_rot, tab_cos, tab_sin)


def _attn_kernel(q_ref, k_ref, vt_ref, qsq_ref, ksq_ref, o_ref, *, v_head, v_rows, blk, key_blocks_per_tile):
    qi = pl.program_id(2)
    tq = q_ref.shape[0]
    n_full = qi * (tq // blk)
    base = pl.multiple_of(qi * tq, tq)
    heads = range(2)
    diag = [(h, base + c * blk, c * blk, True) for c in range(tq // blk) for h in heads]

    def scores(h, kstart, q_lo):
        hs = slice(h * LANES, (h + 1) * LANES)
        return lax.dot_general(k_ref[pl.ds(kstart, blk), hs], q_ref[q_lo:tq, hs], NT_DIMS,
                               preferred_element_type=F32)

    def values(h, kstart):
        return vt_ref[h * LANES:h * LANES + v_rows, pl.ds(kstart, blk)]

    def cols_from(x, q_lo, new_cols):
        return new_cols if q_lo == 0 else jnp.concatenate([x[:, :q_lo], new_cols], axis=1)

    def run(tasks, carry, update):
        carry = list(carry)
        nxt = scores(*tasks[0][:3])
        for t, (h, kstart, q_lo, masked) in enumerate(tasks):
            s = nxt
            if t + 1 < len(tasks):
                nxt = scores(*tasks[t + 1][:3])
            if masked:
                key = lax.broadcasted_iota(jnp.int32, s.shape, 0)
                qry = lax.broadcasted_iota(jnp.int32, s.shape, 1)
                s = jnp.where(key <= qry, s, -jnp.inf)
            carry[h] = update(h, s, kstart, q_lo, carry[h])
        return tuple(carry)

    def sweep(update, init):
        per_trip = 2 if (tq // blk) % 2 == 0 else 1

        def full_chunks(trip, carry):
            kstart = pl.multiple_of(trip * (per_trip * blk), per_trip * blk)
            return run([(h, kstart + c * blk, 0, False) for c in range(per_trip) for h in heads], carry, update)
        return run(diag, lax.fori_loop(0, n_full // per_trip, full_chunks, init), update)

    def store(accs):
        outs = []
        for acc in accs:
            out_t = acc[:v_head] / acc[v_head:v_head + 1]
            outs.append(jnp.transpose(out_t))
        o_ref[...] = jnp.concatenate(outs, axis=1).astype(o_ref.dtype)

    ksq_blocks = ksq_ref[...]
    visible = lax.broadcasted_iota(jnp.int32, ksq_blocks.shape, 1) < (qi + 1) * key_blocks_per_tile
    ksq_max = jnp.max(jnp.where(visible, ksq_blocks, 0.0), axis=1, keepdims=True)
    bounds = [jnp.sqrt(qsq_ref[h:h + 1, :] * ksq_max[h:h + 1, :]) for h in heads]

    def update_bounded(h, s, kstart, q_lo, acc):
        p = jnp.exp2(s - bounds[h][:, q_lo:]).astype(BF16)
        pv = jnp.dot(values(h, kstart), p, preferred_element_type=F32)
        return cols_from(acc, q_lo, acc[:, q_lo:] + pv)

    accs = sweep(update_bounded, tuple(jnp.zeros((v_rows, tq), F32) for _ in heads))
    store(accs)
    smallest = jnp.minimum(accs[0][v_head:v_head + 1], accs[1][v_head:v_head + 1])
    row_sums_ok = jnp.min(smallest) >= SOFTMAX_MIN_ROW_SUM

    @pl.when(jnp.logical_not(row_sums_ok))
    def _():
        def update_online(h, s, kstart, q_lo, state):
            m, acc = state
            m_old = m[:, q_lo:]
            m_new = jnp.maximum(m_old, jnp.max(s, axis=0, keepdims=True))
            p = jnp.exp2(s - m_new).astype(BF16)
            pv = jnp.dot(values(h, kstart), p, preferred_element_type=F32)
            return (cols_from(m, q_lo, m_new),
                    cols_from(acc, q_lo, jnp.exp2(m_old - m_new) * acc[:, q_lo:] + pv))

        init = tuple((jnp.full((1, tq), -jnp.inf, F32), jnp.zeros((v_rows, tq), F32)) for _ in heads)
        store([acc for _, acc in sweep(update_online, init)])


def prompt_attention(q, kcat, vt, qsq, ksq_blocks, v_head):
    bsz, s, hw = q.shape
    pairs = hw // (2 * LANES)
    assert 2 * v_head == LANES, "two heads fill one 128-lane output block"
    tq = min(ATTN_TILE, s)
    blk = min(ATTN_BLOCK, tq)
    n_blocks = ksq_blocks.shape[1]
    assert n_blocks % (s // tq) == 0, "query tiles must cover whole key-norm blocks"
    v_rows = -(-(v_head + 1) // BF16_ROWS) * BF16_ROWS
    qsq_t = jnp.swapaxes(qsq, 1, 2).reshape(bsz, pairs, 2, s)
    ksq_t = jnp.swapaxes(ksq_blocks, 1, 2).reshape(bsz, pairs, 2, n_blocks)
    return pl.pallas_call(
        functools.partial(_attn_kernel, v_head=v_head, v_rows=v_rows, blk=blk,
                          key_blocks_per_tile=n_blocks // (s // tq)),
        grid=(bsz, pairs, s // tq),
        in_specs=[pl.BlockSpec((None, tq, 2 * LANES), lambda b, j, i: (b, i, j)),
                  pl.BlockSpec((None, s, 2 * LANES), lambda b, j, i: (b, 0, j)),
                  pl.BlockSpec((None, 2 * LANES, s), lambda b, j, i: (b, j, 0)),
                  pl.BlockSpec((None, None, 2, tq), lambda b, j, i: (b, j, 0, i)),
                  pl.BlockSpec((None, None, 2, n_blocks), lambda b, j, i: (b, j, 0, 0))],
        out_specs=pl.BlockSpec((None, tq, 2 * v_head), lambda b, j, i: (b, i, j)),
        out_shape=jax.ShapeDtypeStruct((bsz, s, pairs * 2 * v_head), BF16),
        compiler_params=_params("parallel", "parallel", "arbitrary"),
        name="prompt_attention",
    )(q, kcat, vt, qsq_t, ksq_t)


def _head_proj_kernel(a_ref, w_ref, o_ref):
    kk = w_ref.shape[1]
    for h in range(w_ref.shape[0]):
        o_ref[h] = jnp.dot(a_ref[:, h * kk:(h + 1) * kk].astype(BF16), w_ref[h],
                           preferred_element_type=F32).astype(o_ref.dtype)


def head_proj_lanes(a, w, out_dtype, name):
    bsz = a.shape[0]
    nh, kk, n = w.shape
    return pl.pallas_call(
        _head_proj_kernel,
        grid=(1,),
        in_specs=[pl.BlockSpec(a.shape, lambda i: (0, 0)), pl.BlockSpec(w.shape, lambda i: (0, 0, 0))],
        out_specs=pl.BlockSpec((nh, bsz, n), lambda i: (0, 0, 0)),
        out_shape=jax.ShapeDtypeStruct((nh, bsz, n), out_dtype),
        compiler_params=_params("arbitrary"),
        name=name,
    )(a, w)


def _paged_attn_kernel(pt_ref, ql_ref, qp_ref, cn_ref, kn_ref, ckv_hbm, kpe_hbm, o_ref,
                       ckv_buf, kpe_buf, s_ref, cb0, p0, w0, cb1, p1, w1, sem, *, n_seq, n_pages, ps, chunk):
    step = pl.program_id(0)
    slot = step % 2
    nh, c_lat = ql_ref.shape
    sets = ((cb0, p0, w0), (cb1, p1, w1))

    def page_copies(bi, sl):
        out = []
        for pg in range(n_pages):
            page = pt_ref[bi * n_pages + pg]
            rows = pl.ds(pg * ps, ps)
            out.append(pltpu.make_async_copy(ckv_hbm.at[page], ckv_buf.at[sl, rows, :], sem.at[0, sl]))
            out.append(pltpu.make_async_copy(kpe_hbm.at[page], kpe_buf.at[sl, :, rows], sem.at[1, sl]))
        return out

    @pl.when(step == 0)
    def _():
        for cp in page_copies(0, 0):
            cp.start()
        cb1[...] = jnp.zeros_like(cb1)
        p1[...] = jnp.zeros_like(p1)
        w1[...] = jnp.concatenate([jnp.zeros((nh, c_lat), F32), jnp.ones((nh, LANES), F32)], axis=1)

    @pl.when(step + 1 < n_seq)
    def _():
        for cp in page_copies(step + 1, 1 - slot):
            cp.start()

    @pl.when(step < n_seq)
    def _():
        for cp in page_copies(step, slot):
            cp.wait()

    n_chunks = n_pages * ps // chunk
    chunks = [slice(c * chunk, (c + 1) * chunk) for c in range(n_chunks)]

    def main(par):
        (cb_w, p_w, w_w), (cb_r, p_r, w_r) = sets[par], sets[1 - par]
        ql = ql_ref[...]
        qp = qp_ref[...]
        w_prev = w_r[...]
        accs = [w_prev[:, :c_lat], jnp.zeros((nh, c_lat), F32)]
        s_lat = []
        for c, cs in enumerate(chunks):
            ck = ckv_buf[par, cs, :].astype(BF16)
            cb_w[cs, :] = ck
            s_lat.append(lax.dot_general(ql, ck, NT_DIMS, preferred_element_type=F32))
            accs[c % 2] += jnp.dot(p_r[:, cs], cb_r[cs, :], preferred_element_type=F32)
        o_ref[...] = (accs[0] + accs[1]) / w_prev[:, c_lat:c_lat + 1]

        for cs, sl in zip(chunks, s_lat):
            s_ref[:, cs] = sl + jnp.dot(qp, kpe_buf[par, :, cs].astype(BF16), preferred_element_type=F32)

        cn = cn_ref[...].astype(BF16).astype(F32)
        kn = kn_ref[...].astype(BF16).astype(F32)
        s_new = (jnp.sum(ql.astype(F32) * cn, axis=1, keepdims=True)
                 + jnp.sum(qp.astype(F32) * kn, axis=1, keepdims=True))
        s = s_ref[...]
        m = jnp.maximum(jnp.max(s, axis=1, keepdims=True), s_new)
        p = jnp.exp2(s - m)
        p_new = jnp.exp2(s_new - m)
        l = jnp.sum(p, axis=1, keepdims=True) + p_new
        p_w[...] = p.astype(BF16)
        w_w[...] = jnp.concatenate([p_new.astype(BF16).astype(F32) * cn, jnp.broadcast_to(l, (nh, LANES))], axis=1)

    for par in range(2):
        pl.when(step % 2 == par)(functools.partial(main, par))


def paged_attention(q_lat, q_pe, ckv_new, kpe_new, cache_ckv, cache_kpe_t, page_table):
    bsz, nh, c = q_lat.shape
    r = q_pe.shape[-1]
    n_pages = page_table.shape[1]
    ps = cache_ckv.shape[1]
    past = n_pages * ps
    assert bsz >= 2, "the two-slot pipeline needs at least two sequences"
    chunk = min(PAGED_CHUNK, past)
    per_b = lambda rows, w: pl.BlockSpec((None, rows, w), lambda s_, pt: (jnp.minimum(s_, bsz - 1), 0, 0))
    return pl.pallas_call(
        functools.partial(_paged_attn_kernel, n_seq=bsz, n_pages=n_pages, ps=ps, chunk=chunk),
        grid_spec=pltpu.PrefetchScalarGridSpec(
            num_scalar_prefetch=1,
            grid=(bsz + 1,),
            in_specs=[per_b(nh, c), per_b(nh, r), per_b(1, c), per_b(1, r),
                      pl.BlockSpec(memory_space=pl.ANY), pl.BlockSpec(memory_space=pl.ANY)],
            out_specs=pl.BlockSpec((None, nh, c), lambda s_, pt: (jnp.maximum(s_ - 1, 0), 0, 0)),
            scratch_shapes=[pltpu.VMEM((2, past, c), F32), pltpu.VMEM((2, r, past), F32),
                            pltpu.VMEM((nh, past), F32)]
            + 2 * [pltpu.VMEM((past, c), BF16), pltpu.VMEM((nh, past), BF16), pltpu.VMEM((nh, c + LANES), F32)]
            + [pltpu.SemaphoreType.DMA((2, 2))]),
        out_shape=jax.ShapeDtypeStruct((bsz, nh, c), F32),
        compiler_params=_params("arbitrary"),
        name="paged_attention",
    )(page_table.reshape(-1), q_lat, q_pe, ckv_new, kpe_new, cache_ckv, cache_kpe_t)


def _rope_tables(pos, rope, lo, scale, passthrough):
    half = rope // 2
    freq = ROPE_THETA ** (-jnp.arange(half, dtype=F32) / half)
    ang = pos.astype(F32)[:, None] * freq[None, :]
    cos, sin = jnp.cos(ang), jnp.sin(ang)
    n = pos.shape[0]
    zeros = lambda w: jnp.zeros((n, w), F32)
    tail = LANES - lo - rope
    c = jnp.concatenate([jnp.full((n, lo), passthrough, F32), cos, cos, zeros(tail)], axis=1)
    sa = jnp.concatenate([zeros(lo), -sin, zeros(half + tail)], axis=1)
    sb = jnp.concatenate([zeros(lo + half), sin, zeros(tail)], axis=1)
    return c * scale, sa * scale, sb * scale


def _prep_weights(w_up, w_down, w_m_in, w_m_out, w_dkv, w_uk, w_uv, w_dq, w_uq, w_o, dims):
    heads, dk, dv = dims["m_heads"], dims["m_dk"], dims["m_dv"]
    nh, nope, rope, kv_lora = dims["mla_heads"], dims["qk_nope"], dims["qk_rope"], dims["kv_lora"]
    ff = w_down.shape[1]
    qd, vd = heads * dk, heads * dv
    d = w_up.shape[1]
    pw = {}
    w_up_b, w_down_b, w_m_in_b = w_up.astype(BF16), w_down.astype(BF16), w_m_in.astype(BF16)
    depth, n_a = w_up.shape[0], w_m_in.shape[0]
    pw["w_up_a"] = [WeightView(w_up_b, (None, d, ff), (l, 0, 0)) for l in range(depth)]
    pw["w_up_g"] = [WeightView(w_up_b, (None, d, ff), (l, 0, 1)) for l in range(depth)]
    pw["w_down"] = [WeightView(w_down_b, (None, ff, d), (l, 0, 0)) for l in range(depth)]
    assert (2 * qd) % vd == 0, "value / output-gate columns must start on a multiple of their width"
    pw["w_m_q"] = [WeightView(w_m_in_b, (None, d, qd), (l, 0, 0)) for l in range(n_a)]
    pw["w_m_k"] = [WeightView(w_m_in_b, (None, d, qd), (l, 0, 1)) for l in range(n_a)]
    pw["w_m_v"] = [WeightView(w_m_in_b, (None, d, vd), (l, 0, 2 * qd // vd)) for l in range(n_a)]
    pw["w_m_o"] = [WeightView(w_m_in_b, (None, d, vd), (l, 0, 2 * qd // vd + 1)) for l in range(n_a)]
    gates = jnp.pad(w_m_in[:, :, 2 * qd + 2 * vd:], ((0, 0), (0, 0), (0, LANES - 2 * heads))).astype(BF16)
    pw["w_m_g"] = [WeightView(gates, (None, d, LANES), (l, 0, 0)) for l in range(n_a)]
    pw["w_m_out"] = w_m_out.astype(BF16)
    pw["w_dkv"] = jnp.pad(w_dkv, ((0, 0), (0, LANES - rope))).astype(BF16)
    pw["w_uk_r"] = jnp.pad(w_uk, ((0, 0), (0, 0), (0, LANES - nope))).reshape(kv_lora, nh * LANES).astype(BF16)
    v_head = w_uv.shape[2]
    pw["w_uv_r"] = jnp.pad(w_uv, ((0, 0), (0, 0), (0, LANES - v_head))).reshape(kv_lora, nh * LANES).T.astype(BF16)
    nb = w_uq.shape[0]
    wq = w_uq.reshape(nb, w_uq.shape[1], nh, nope + rope)
    pw["w_uq_r"] = jnp.pad(wq, ((0, 0), (0, 0), (0, 0), (0, LANES - nope - rope))).reshape(
        nb, w_uq.shape[1], nh * LANES).astype(BF16)
    half = rope // 2
    partner = jnp.concatenate([-wq[..., nope + half:], wq[..., nope:nope + half]], axis=-1)
    pw["w_uq_rot"] = jnp.pad(partner, ((0, 0), (0, 0), (0, 0), (nope, LANES - nope - rope))).reshape(
        nb, w_uq.shape[1], nh * LANES).astype(BF16)
    pw["w_dq"] = w_dq.astype(BF16)
    pw["w_o"] = w_o.astype(BF16)
    wukt = jnp.transpose(w_uk, (1, 2, 0))
    pw["w_uk_t"] = jnp.pad(wukt, ((0, 0), (0, LANES - nope), (0, 0))).astype(BF16)
    pw["w_uv_t"] = jnp.transpose(w_uv, (1, 0, 2)).astype(BF16)
    return pw


def _trunk(x, mods, mods_kv, pos, conv_bufs, m_states, kv_past, pw, small, dims):
    heads, dk, dv = dims["m_heads"], dims["m_dk"], dims["m_dv"]
    nh, nope, rope, kv_lora, v_head = (dims["mla_heads"], dims["qk_nope"], dims["qk_rope"],
                                       dims["kv_lora"], dims["v_head"])
    depth, n_a = dims["depth"], dims["n_a"]
    is_prompt = kv_past is None
    gq, r, d = x.shape
    att_scale = (nope + rope) ** -0.5 * LOG2_E
    new_c, new_n, new_m, new_conv = [], [], [], []
    ckv = kpe = kcat = vv = ksq = None
    y = None
    for layer in range(depth):
        sh1, sc1, gt1, sh2, sc2, gt2 = mods[layer]
        g1 = small["g_norm1"][layer][None, :]
        g2 = small["g_norm2"][layer][None, :]
        if layer == n_a:
            sh_kv, sc_kv = mods_kv
            tabs = _rope_tables(pos, rope, 0, 1.0, 0.0)
            if is_prompt:
                ckv, kpe, kcat, vv, ksq = shared_latent(x, small["g_kv_in"][None, :], sh_kv, sc_kv, pw["w_dkv"],
                                                        small["g_kv"][None, :], tabs, kv_lora, rope, nope,
                                                        pw["w_uk_r"], pw["w_uv_r"], v_head)
            else:
                ckv, kpe = shared_latent(x, small["g_kv_in"][None, :], sh_kv, sc_kv, pw["w_dkv"],
                                         small["g_kv"][None, :], tabs, kv_lora, rope, nope)
        if layer < n_a:
            ws = [pw["w_m_q"][layer], pw["w_m_k"][layer], pw["w_m_v"][layer], pw["w_m_o"][layer], pw["w_m_g"][layer]]
            if is_prompt:
                q, k, v, o, gates = norm_mod_proj(x, g1, sh1, sc1, ws, [BF16, F32, BF16, F32, F32],
                                                  [dk ** -0.5, 1.0, 1.0, 1.0, 1.0], "mlstm_in_proj")
                hh, c_st, n_st, m_st = mlstm_chunkwise(q, k, v, o, gates[..., :2 * heads],
                                                       small["b_m_gates"][layer], small["g_m_head"][layer],
                                                       heads, dk, dv)
                n_st = n_st.reshape(gq, heads, dk)
                m_st = m_st.reshape(gq, heads)
            else:
                q, k, v, o, gates = norm_mod_proj(x, g1, sh1, sc1, ws, [F32] * 5,
                                                  [dk ** -0.5, 1.0, 1.0, 1.0, 1.0], "mlstm_in_proj_tok")
                hh, c_st, n_st, m_st = mlstm_step(q[0], k[0], v[0], o[0], gates[0, :, :2 * heads],
                                                  small["b_m_gates"][layer], small["g_m_head"][layer],
                                                  m_states[0][layer], m_states[1][layer], m_states[2][layer],
                                                  heads, dk, dv)
                hh = hh[None]
            new_c.append(c_st)
            new_n.append(n_st)
            new_m.append(m_st)
            mix, w_mix = hh, pw["w_m_out"][layer]
        else:
            j = layer - n_a
            q_cos, q_msin, q_sin = _rope_tables(pos, rope, nope, att_scale, 1.0)
            qh, qsq = mla_queries(x, g1, sh1, sc1, pw["w_dq"][j], small["g_q"][j][None, :], pw["w_uq_r"][j],
                                  pw["w_uq_rot"][j], q_cos, q_sin - q_msin)
            if is_prompt:
                att = prompt_attention(qh, kcat, vv, qsq, ksq[:, :, 0, :], v_head)
            else:
                bsz = r
                q2 = qh[0]
                q_lat = head_proj_lanes(q2, pw["w_uk_t"], BF16, "absorb_q")
                q_lat = jnp.swapaxes(q_lat, 0, 1)
                q_pe = q2.reshape(bsz, nh, LANES)[:, :, nope:nope + rope]
                o_lat = paged_attention(q_lat, q_pe, ckv[0][:, None, :], kpe[0][:, None, :],
                                        kv_past[0], kv_past[1], kv_past[2])
                o_lat = o_lat.reshape(bsz, nh * kv_lora)
                att = head_proj_lanes(o_lat, pw["w_uv_t"], F32, "unabsorb_o")
                att = jnp.swapaxes(att, 0, 1).reshape(1, bsz, nh * v_head)
            mix, w_mix = att, pw["w_o"][j]
        final = layer == depth - 1
        gf = small["g_final"][None, :]
        if is_prompt:
            x, a_last = conv_ffn_seq(x, mix, w_mix, gt1, conv_bufs[layer], g2, sh2, sc2, gt2,
                                     pw["w_up_a"][layer], pw["w_up_g"][layer], small["w_conv"][layer],
                                     small["b_conv"][layer][None, :], pw["w_down"][layer], gf, final, "conv_ffn_seq")
            new_conv.append(a_last[:, CONV_HALO - (small["w_conv"].shape[1] - 1):, :])
        else:
            x = resid_proj(mix, w_mix, x, gt1, "mixer_out_proj")
            buf = conv_bufs[layer]
            x2, a_new = conv_ffn_tok(x[0], buf[:, 0, :], buf[:, 1, :], g2, sh2[0], sc2[0], gt2[0],
                                     pw["w_up_a"][layer], pw["w_up_g"][layer], small["w_conv"][layer],
                                     small["b_conv"][layer][None, :], pw["w_down"][layer], gf, final, "conv_ffn_tok")
            x = x2[None]
            new_conv.append(jnp.stack([buf[:, 1, :], a_new], axis=1))
    return x, jnp.stack(new_c), jnp.stack(new_n), jnp.stack(new_m), jnp.stack(new_conv), ckv, kpe


def kernel(x_prompt, x_sample, state_mlstm_C, state_mlstm_n, state_mlstm_m, state_conv, cache_ckv, cache_kpe,
           page_table, c_prompt, c_sample, g_norm1, g_norm2, w_ada, b_ada, w_up, w_conv, b_conv, w_down,
           w_m_in, b_m_gates, g_m_head, w_m_out, g_kv_in, w_ada_kv, b_ada_kv, w_dkv, g_kv, w_uk, w_uv,
           w_dq, g_q, w_uq, w_o, g_final):
    bp, s, d = x_prompt.shape
    bs, t, _ = x_sample.shape
    assert t == 1, "the sample path handles one new token per sequence"
    assert w_conv.shape[1] == 3, "the single-token ConvFFN kernel is written for a width-3 conv"
    depth = w_ada.shape[0]
    n_a = w_m_in.shape[0]
    heads, dv = g_m_head.shape[1], g_m_head.shape[2]
    dk = state_mlstm_C.shape[3]
    kv_lora, nh, nope = w_uk.shape
    v_head = w_uv.shape[2]
    rope = w_dkv.shape[1] - kv_lora
    ff = w_down.shape[1]
    dims = dict(m_heads=heads, m_dk=dk, m_dv=dv, mla_heads=nh, qk_nope=nope, qk_rope=rope, kv_lora=kv_lora,
                v_head=v_head, depth=depth, n_a=n_a)
    past_len = page_table.shape[1] * cache_ckv.shape[1]

    pw = _prep_weights(w_up, w_down, w_m_in, w_m_out, w_dkv, w_uk, w_uv, w_dq, w_uq, w_o, dims)
    small = dict(g_norm1=g_norm1, g_norm2=g_norm2, w_conv=w_conv, b_conv=b_conv, b_m_gates=b_m_gates,
                 g_m_head=g_m_head, g_kv_in=g_kv_in, g_kv=g_kv, g_q=g_q, g_final=g_final)

    c_all = jnp.concatenate([c_prompt, c_sample], axis=0)
    mod = ada_mod(c_all, w_ada, b_ada[:, None, :])
    mod_kv = ada_mod(c_all, w_ada_kv[None], b_ada_kv[None, None, :])[0]

    def split(m, n, lo, hi, per_row):
        parts = jnp.split(m[lo:hi], n, axis=-1)
        return [p[None] if per_row else p[:, None, :] for p in parts]

    mods_p = [split(mod[l], 6, 0, bp, False) for l in range(depth)]
    mods_s = [split(mod[l], 6, bp, bp + bs, True) for l in range(depth)]
    kv_p = split(mod_kv, 2, 0, bp, False)
    kv_s = split(mod_kv, 2, bp, bp + bs, True)

    conv0 = [jnp.zeros((bp, CONV_HALO, ff), F32)] * depth
    y_p, c_p, n_p, m_p, conv_p, ckv_p, kpe_p = _trunk(
        x_prompt, mods_p, kv_p, jnp.arange(s), conv0, None, None, pw, small, dims)

    pos_s = jnp.full((bs,), past_len, jnp.int32)
    y_s, c_s, n_s, m_s, conv_s, ckv_s, kpe_s = _trunk(
        x_sample.reshape(1, bs, d), mods_s, kv_s, pos_s, state_conv,
        (state_mlstm_C, state_mlstm_n, state_mlstm_m), (cache_ckv, jnp.swapaxes(cache_kpe, 1, 2), page_table), pw, small, dims)

    return (y_p, y_s.reshape(bs, 1, d), c_p, n_p, m_p, conv_p, ckv_p, kpe_p,
            c_s, n_s, m_s, conv_s, ckv_s.reshape(bs, 1, kv_lora), kpe_s.reshape(bs, 1, rope))
```

```python
import functools
import math
from typing import NamedTuple

import jax
import jax.numpy as jnp
from jax import lax
from jax.experimental import pallas as pl
from jax.experimental.pallas import tpu as pltpu

F32 = jnp.float32
BF16 = jnp.bfloat16

NORM_EPS = 1e-6
ROPE_THETA = 10000.0
LANES = 128
BF16_ROWS = 16
CONV_HALO = 8
VMEM_LIMIT = 56 * 1024 * 1024

ROW_TILE = 512
FFN_ROW_TILE = 512
MLSTM_CHUNK = 256
ATTN_TILE = 2048
ATTN_BLOCK = 512
FFN_CHUNK = 256
FFN_TOK_CHUNK = 1408
PAGED_CHUNK = 512
LOG2_E = math.log2(math.e)
SOFTMAX_MIN_ROW_SUM = 2.0 ** -60
STEP_BATCH = 8

NT_DIMS = (((1,), (1,)), ((), ()))
TN_DIMS = (((0,), (0,)), ((), ()))


def _params(*sem):
    return pltpu.CompilerParams(dimension_semantics=sem, vmem_limit_bytes=VMEM_LIMIT)


def _const_spec(shape):
    nd = len(shape)
    return pl.BlockSpec(shape, lambda *_: (0,) * nd, pipeline_mode=pl.Buffered(1))


class WeightView(NamedTuple):
    array: jax.Array
    block: tuple
    index: tuple

    @property
    def shape(self):
        return tuple(d for d in self.block if d is not None)

    def spec(self):
        return pl.BlockSpec(self.block, lambda *_: self.index, pipeline_mode=pl.Buffered(1))

    def chunk_spec(self, axis, size):
        n = self.block[axis] // size
        block = tuple(size if a == axis else d for a, d in enumerate(self.block))
        index = lambda c: tuple(i * n + c if a == axis else i for a, i in enumerate(self.index))
        return pl.BlockSpec(block, index)


def _rms(x, g):
    return x * lax.rsqrt(jnp.mean(x * x, axis=-1, keepdims=True) + NORM_EPS) * g


def _norm_mod(x, g, sh, sc):
    return _rms(x, g) * (1.0 + sc) + sh


def _log_sigmoid(x):
    return jnp.minimum(x, 0.0) - jnp.log(1.0 + jnp.exp(-jnp.abs(x)))


def _rope3(t, c, sa, sb, half):
    return t * c + pltpu.roll(t, LANES - half, axis=1) * sa + pltpu.roll(t, half, axis=1) * sb


def _row_spec(tm, width):
    return pl.BlockSpec((None, tm, width), lambda g, i: (g, i, 0))


def _mod_spec(per_row, tm, width):
    if per_row:
        return pl.BlockSpec((None, tm, width), lambda g, i: (g, i, 0))
    return pl.BlockSpec((None, 1, width), lambda g, i: (g, 0, 0))


def _ada_kernel(c_ref, w_ref, b_ref, o_ref):
    c = c_ref[...]
    a = (c * jax.nn.sigmoid(c)).astype(BF16)
    o_ref[...] = jnp.dot(a, w_ref[...].astype(BF16), preferred_element_type=F32) + b_ref[...]


def ada_mod(c, w, b, tn=1024):
    m, d = c.shape
    nl, _, n = w.shape
    tn = min(tn, n)
    return pl.pallas_call(
        _ada_kernel,
        grid=(nl, n // tn),
        in_specs=[pl.BlockSpec((m, d), lambda l, j: (0, 0)),
                  pl.BlockSpec((None, d, tn), lambda l, j: (l, 0, j)),
                  pl.BlockSpec((None, 1, tn), lambda l, j: (l, 0, j))],
        out_specs=pl.BlockSpec((None, m, tn), lambda l, j: (l, 0, j)),
        out_shape=jax.ShapeDtypeStruct((nl, m, n), F32),
        compiler_params=_params("parallel", "parallel"),
        name="ada_mod",
    )(c, w, b)


def _proj_kernel(x_ref, g_ref, sh_ref, sc_ref, *refs, scales):
    n = len(scales)
    hn = _norm_mod(x_ref[...], g_ref[...], sh_ref[...], sc_ref[...]).astype(BF16)
    for w_ref, o_ref, s in zip(refs[:n], refs[n:], scales):
        acc = jnp.dot(hn, w_ref[...], preferred_element_type=F32)
        if s != 1.0:
            acc = acc * s
        o_ref[...] = acc.astype(o_ref.dtype)


def norm_mod_proj(x, g, sh, sc, ws, out_dtypes, scales, name):
    gq, r, d = x.shape
    tm = min(ROW_TILE, r)
    per_row = sh.shape[1] != 1
    in_specs = [_row_spec(tm, d), _const_spec((1, d)),
                _mod_spec(per_row, tm, d), _mod_spec(per_row, tm, d)]
    in_specs += [w.spec() for w in ws]
    return pl.pallas_call(
        functools.partial(_proj_kernel, scales=tuple(scales)),
        grid=(gq, r // tm),
        in_specs=in_specs,
        out_specs=[_row_spec(tm, w.shape[1]) for w in ws],
        out_shape=[jax.ShapeDtypeStruct((gq, r, w.shape[1]), dt) for w, dt in zip(ws, out_dtypes)],
        compiler_params=_params("parallel", "parallel"),
        name=name,
    )(x, g, sh, sc, *[w.array for w in ws])


def _resid_kernel(a_ref, w_ref, x_ref, gt_ref, o_ref):
    mix = jnp.dot(a_ref[...].astype(BF16), w_ref[...], preferred_element_type=F32)
    o_ref[...] = x_ref[...] + gt_ref[...] * mix


def resid_proj(a, w, x, gt, name):
    gq, r, d = x.shape
    k = a.shape[-1]
    tm = min(ROW_TILE, r)
    per_row = gt.shape[1] != 1
    return pl.pallas_call(
        _resid_kernel,
        grid=(gq, r // tm),
        in_specs=[_row_spec(tm, k), _const_spec(w.shape), _row_spec(tm, d), _mod_spec(per_row, tm, d)],
        out_specs=_row_spec(tm, d),
        out_shape=jax.ShapeDtypeStruct((gq, r, d), F32),
        compiler_params=_params("parallel", "parallel"),
        name=name,
    )(a, w, x, gt)


def _mlstm_chunk_kernel(q_ref, k_ref, v_ref, o_ref, g_ref, bgc_ref, bgr_ref, gh_ref,
                        hh_ref, c_ref, n_ref, m_ref, *, heads, dk, dv):
    ci = pl.program_id(1)
    chunk = q_ref.shape[0]

    @pl.when(ci == 0)
    def _():
        c_ref[...] = jnp.zeros_like(c_ref)
        n_ref[...] = jnp.zeros_like(n_ref)
        m_ref[...] = jnp.zeros_like(m_ref)

    gates = g_ref[...]
    gates_c = gates[:, :2 * heads] + bgr_ref[...]
    gates_r = jnp.transpose(gates)[:2 * heads, :] + bgc_ref[...]
    row = lax.broadcasted_iota(jnp.int32, (chunk, chunk), 0)
    col = lax.broadcasted_iota(jnp.int32, (chunk, chunk), 1)
    causal = col <= row

    s_raw, q_c = [], []
    for h in range(heads):
        q = q_ref[:, h * dk:(h + 1) * dk]
        s_raw.append(lax.dot_general(q, k_ref[:, h * dk:(h + 1) * dk].astype(BF16), NT_DIMS,
                                     preferred_element_type=F32))
        q_c.append(jnp.dot(q, c_ref[h].astype(BF16), preferred_element_type=F32))

    hr = range(heads)
    qs = [q_ref[:, h * dk:(h + 1) * dk] for h in hr]
    vs = [v_ref[:, h * dv:(h + 1) * dv] for h in hr]
    logf_r = _log_sigmoid(gates_r[heads:, :])
    logf_c = _log_sigmoid(gates_c[:, heads:])
    li_r = [gates_r[h:h + 1, :] for h in hr]
    lf_r = [logf_r[h:h + 1, :] for h in hr]
    li_c = [gates_c[:, h:h + 1] for h in hr]
    lf_c = [logf_c[:, h:h + 1] for h in hr]
    m_prev = [m_ref[h] for h in hr]
    n_prev = [n_ref[h] for h in hr]

    def lane_cumsum(x_r):
        low = jnp.where(causal, x_r, 0.0)
        if chunk % LANES == 0 and chunk > LANES:
            low = functools.reduce(jnp.add, [low[:, t:t + LANES] for t in range(0, chunk, LANES)])
        return jnp.sum(low, axis=1, keepdims=True)

    b_c = [lane_cumsum(lf_r[h]) for h in hr]
    b_r = [jnp.sum(jnp.where(row <= col, lf_c[h], 0.0), axis=0, keepdims=True) for h in hr]
    g = [jnp.sum(lf_r[h], axis=1, keepdims=True) for h in hr]

    a_c = [g[h] - b_c[h] + li_c[h] for h in hr]
    m_loc = [jnp.max(a_c[h], axis=0, keepdims=True) for h in hr]
    kw = [k_ref[:, h * dk:(h + 1) * dk] * jnp.exp(a_c[h] - m_loc[h]) for h in hr]
    c_loc = [lax.dot_general(kw[h].astype(BF16), vs[h], TN_DIMS, preferred_element_type=F32) for h in hr]
    n_loc = [jnp.sum(kw[h], axis=0, keepdims=True) for h in hr]

    dmat = [jnp.where(causal, b_c[h] - b_r[h] + li_r[h], -jnp.inf) for h in hr]
    w0 = [b_c[h] + m_prev[h] for h in hr]
    m_s = [jnp.maximum(w0[h], jnp.max(dmat[h], axis=1, keepdims=True)) for h in hr]
    w_inter = [jnp.exp(w0[h] - m_s[h]) for h in hr]
    s = [s_raw[h] * jnp.exp(dmat[h] - m_s[h]) for h in hr]
    num = [w_inter[h] * q_c[h] + jnp.dot(s[h].astype(BF16), vs[h], preferred_element_type=F32) for h in hr]
    def lane_fold(x):
        width = x.shape[1]
        if width % LANES or width == LANES:
            return x
        return functools.reduce(jnp.add, [x[:, t:t + LANES] for t in range(0, width, LANES)])

    def row_total(a, b):
        a, b = lane_fold(a), lane_fold(b)
        if a.shape == b.shape:
            return jnp.sum(a + b, axis=1, keepdims=True)
        return jnp.sum(a, axis=1, keepdims=True) + jnp.sum(b, axis=1, keepdims=True)

    den = [row_total(w_inter[h] * (qs[h].astype(F32) * n_prev[h]), s[h]) for h in hr]
    hval = [num[h] / jnp.maximum(jnp.abs(den[h]), jnp.exp(-m_s[h])) for h in hr]
    hn = [_rms(hval[h], gh_ref[h:h + 1, :]) for h in hr]
    for h in hr:
        gate = jax.nn.sigmoid(o_ref[:, h * dv:(h + 1) * dv])
        hh_ref[:, h * dv:(h + 1) * dv] = (hn[h] * gate).astype(hh_ref.dtype)

    for h in hr:
        m_new = jnp.maximum(g[h] + m_prev[h], m_loc[h])
        fw = jnp.exp(g[h] + m_prev[h] - m_new)
        lw = jnp.exp(m_loc[h] - m_new)
        c_ref[h] = fw * c_ref[h] + lw * c_loc[h]
        n_ref[h] = fw * n_prev[h] + lw * n_loc[h]
        m_ref[h] = m_new


def mlstm_chunkwise(q, k, v, o, gates, b_gates, g_head, heads, dk, dv):
    bsz, s, _ = q.shape
    chunk = min(MLSTM_CHUNK, s)
    g2 = 2 * heads
    return pl.pallas_call(
        functools.partial(_mlstm_chunk_kernel, heads=heads, dk=dk, dv=dv),
        grid=(bsz, s // chunk),
        in_specs=[_row_spec(chunk, heads * dk), _row_spec(chunk, heads * dk),
                  _row_spec(chunk, heads * dv), _row_spec(chunk, heads * dv),
                  _row_spec(chunk, gates.shape[-1]),
                  _const_spec((g2, 1)), _const_spec((1, g2)), _const_spec((heads, dv))],
        out_specs=[_row_spec(chunk, heads * dv),
                   pl.BlockSpec((None, heads, dk, dv), lambda b, c: (b, 0, 0, 0)),
                   pl.BlockSpec((None, heads, 1, dk), lambda b, c: (b, 0, 0, 0)),
                   pl.BlockSpec((None, heads, 1, 1), lambda b, c: (b, 0, 0, 0))],
        out_shape=[jax.ShapeDtypeStruct((bsz, s, heads * dv), BF16),
                   jax.ShapeDtypeStruct((bsz, heads, dk, dv), F32),
                   jax.ShapeDtypeStruct((bsz, heads, 1, dk), F32),
                   jax.ShapeDtypeStruct((bsz, heads, 1, 1), F32)],
        compiler_params=_params("parallel", "arbitrary"),
        name="mlstm_chunkwise",
    )(q, k, v, o, gates, b_gates.reshape(g2, 1), b_gates.reshape(1, g2), g_head)


def _mlstm_step_kernel(q_ref, k_ref, v_ref, o_ref, g_ref, bg_ref, gh_ref, c_ref, n_ref, m_ref,
                       hh_ref, co_ref, no_ref, mo_ref, *, heads, dk, dv):
    nb = q_ref.shape[0]
    gates = g_ref[...] + bg_ref[...]
    li = gates[:, :heads]
    lf = _log_sigmoid(gates[:, heads:])
    m_st = m_ref[...]
    m_new = jnp.maximum(lf + m_st, li)
    fw_all = jnp.exp(lf + m_st - m_new)
    iw_all = jnp.exp(li - m_new)
    floor_all = jnp.exp(-m_new)
    mo_ref[...] = m_new
    eye = lax.broadcasted_iota(jnp.int32, (dk, dk), 0) == lax.broadcasted_iota(jnp.int32, (dk, dk), 1)

    def to_col(r):
        return jnp.sum(jnp.where(eye, r, 0.0), axis=1, keepdims=True)

    hr = range(heads)
    for b in range(nb):
        q_r = [q_ref[b:b + 1, h * dk:(h + 1) * dk] for h in hr]
        k_r = [k_ref[b:b + 1, h * dk:(h + 1) * dk] for h in hr]
        k_c = [iw_all[b:b + 1, h:h + 1] * to_col(k_r[h]) for h in hr]
        c_new = [fw_all[b:b + 1, h:h + 1] * c_ref[b, h] + k_c[h] * v_ref[b:b + 1, h * dv:(h + 1) * dv] for h in hr]
        num = [jnp.dot(q_r[h].astype(BF16), c_new[h].astype(BF16), preferred_element_type=F32) for h in hr]
        n_new = [fw_all[b:b + 1, h:h + 1] * n_ref[b, h:h + 1, :] + iw_all[b:b + 1, h:h + 1] * k_r[h] for h in hr]
        den = [jnp.sum(q_r[h] * n_new[h], axis=1, keepdims=True) for h in hr]
        hval = [num[h] / jnp.maximum(jnp.abs(den[h]), floor_all[b:b + 1, h:h + 1]) for h in hr]
        hn = [_rms(hval[h], gh_ref[h:h + 1, :]) for h in hr]
        for h in hr:
            co_ref[b, h] = c_new[h]
            no_ref[b, h:h + 1, :] = n_new[h]
            hh_ref[b:b + 1, h * dv:(h + 1) * dv] = hn[h] * jax.nn.sigmoid(o_ref[b:b + 1, h * dv:(h + 1) * dv])


def mlstm_step(q, k, v, o, gates, b_gates, g_head, c_st, n_st, m_st, heads, dk, dv):
    bsz = q.shape[0]
    nb = min(STEP_BATCH, bsz)
    g2 = 2 * heads
    rows = lambda w: pl.BlockSpec((nb, w), lambda i: (i, 0))
    return pl.pallas_call(
        functools.partial(_mlstm_step_kernel, heads=heads, dk=dk, dv=dv),
        grid=(bsz // nb,),
        in_specs=[rows(heads * dk), rows(heads * dk), rows(heads * dv), rows(heads * dv), rows(g2),
                  _const_spec((1, g2)), _const_spec((heads, dv)),
                  pl.BlockSpec((nb, heads, dk, dv), lambda i: (i, 0, 0, 0)),
                  pl.BlockSpec((nb, heads, dk), lambda i: (i, 0, 0)),
                  rows(heads)],
        out_specs=[rows(heads * dv),
                   pl.BlockSpec((nb, heads, dk, dv), lambda i: (i, 0, 0, 0)),
                   pl.BlockSpec((nb, heads, dk), lambda i: (i, 0, 0)),
                   rows(heads)],
        out_shape=[jax.ShapeDtypeStruct((bsz, heads * dv), F32),
                   jax.ShapeDtypeStruct((bsz, heads, dk, dv), F32),
                   jax.ShapeDtypeStruct((bsz, heads, dk), F32),
                   jax.ShapeDtypeStruct((bsz, heads), F32)],
        compiler_params=_params("parallel"),
        name="mlstm_step",
    )(q, k, v, o, gates, b_gates.reshape(1, g2), g_head, c_st, n_st, m_st)


def _ffn_seq_kernel(x_ref, xh_ref, mix_ref, mixh_ref, wm_ref, gtm_ref, buf_ref, g_ref, sh_ref, sc_ref, gt_ref,
                    wa_ref, wg_ref, wc_ref, bc_ref, wd_ref, gf_ref, o_ref, alast_ref, *, fc, final_norm):
    i = pl.program_id(1)
    tm = x_ref.shape[0]
    ff = wa_ref.shape[1]
    halo = xh_ref.shape[0]
    taps = wc_ref.shape[0]
    gtm, wm = gtm_ref[...], wm_ref[...]
    x = x_ref[...] + gtm * jnp.dot(mix_ref[...], wm, preferred_element_type=F32)
    xh = xh_ref[...] + gtm * jnp.dot(mixh_ref[...], wm, preferred_element_type=F32)[mixh_ref.shape[0] - halo:]
    g, sh, sc = g_ref[...], sh_ref[...], sc_ref[...]
    hn = _norm_mod(x, g, sh, sc).astype(BF16)
    hh = _norm_mod(xh, g, sh, sc).astype(BF16)
    first = i == 0

    def up(c):
        cs = slice(c * fc, (c + 1) * fc)
        wa = wa_ref[:, cs]
        return (jnp.dot(hn, wa, preferred_element_type=F32),
                jnp.dot(hn, wg_ref[:, cs], preferred_element_type=F32),
                jnp.dot(hh, wa, preferred_element_type=F32))

    n_chunks = ff // fc
    acc = None
    nxt = up(0)
    for c in range(n_chunks):
        cs = slice(c * fc, (c + 1) * fc)
        a, gt, a_halo = nxt
        if c + 1 < n_chunks:
            nxt = up(c + 1)
        ext = jnp.concatenate([jnp.where(first, buf_ref[:, cs], a_halo), a], axis=0)
        conv = bc_ref[:, cs]
        for j in range(taps):
            lo = halo - (taps - 1) + j
            conv = conv + (a if lo == halo else ext[lo:lo + tm, :]) * wc_ref[j:j + 1, cs]
        act = (conv * jax.nn.sigmoid(conv) * gt).astype(BF16)
        part = jnp.dot(act, wd_ref[cs, :], preferred_element_type=F32)
        acc = part if acc is None else acc + part
        alast_ref[:, cs] = a[tm - halo:tm, :]

    y = x + gt_ref[...] * acc
    if final_norm:
        y = _rms(y, gf_ref[...])
    o_ref[...] = y


def conv_ffn_seq(x, mix, w_mix, gt_mix, buf, g, sh, sc, gt, w_up_a, w_up_g, w_conv, b_conv, w_down, g_final,
                 final_norm, name):
    bsz, s, d = x.shape
    ff = w_up_a.shape[1]
    kk = mix.shape[-1]
    tm = min(FFN_ROW_TILE, s)
    fc = min(FFN_CHUNK, ff)
    assert w_conv.shape[0] - 1 <= CONV_HALO
    halo = lambda b, i: (b, jnp.maximum(i * (tm // CONV_HALO) - 1, 0), 0)
    halo_bf16 = lambda b, i: (b, jnp.maximum(i * (tm // BF16_ROWS) - 1, 0), 0)
    return pl.pallas_call(
        functools.partial(_ffn_seq_kernel, fc=fc, final_norm=final_norm),
        grid=(bsz, s // tm),
        in_specs=[_row_spec(tm, d), pl.BlockSpec((None, CONV_HALO, d), halo),
                  _row_spec(tm, kk), pl.BlockSpec((None, BF16_ROWS, kk), halo_bf16),
                  _const_spec(w_mix.shape), _mod_spec(False, tm, d),
                  pl.BlockSpec((None, CONV_HALO, ff), lambda b, i: (b, 0, 0)),
                  _const_spec((1, d)), _mod_spec(False, tm, d), _mod_spec(False, tm, d), _mod_spec(False, tm, d),
                  w_up_a.spec(), w_up_g.spec(), _const_spec(w_conv.shape),
                  _const_spec(b_conv.shape), w_down.spec(), _const_spec((1, d))],
        out_specs=[_row_spec(tm, d), pl.BlockSpec((None, CONV_HALO, ff), lambda b, i: (b, 0, 0))],
        out_shape=[jax.ShapeDtypeStruct((bsz, s, d), F32), jax.ShapeDtypeStruct((bsz, CONV_HALO, ff), F32)],
        compiler_params=_params("parallel", "arbitrary"),
        name=name,
    )(x, x, mix, mix, w_mix, gt_mix, buf, g, sh, sc, gt, w_up_a.array, w_up_g.array, w_conv, b_conv,
      w_down.array, g_final)


def _ffn_tok_kernel(x_ref, b0_ref, b1_ref, g_ref, sh_ref, sc_ref, gt_ref, wa_ref, wg_ref, wc_ref,
                    bc_ref, wd_ref, gf_ref, o_ref, a_ref, acc_ref, *, final_norm):
    c = pl.program_id(0)
    x = x_ref[...]
    hn = _norm_mod(x, g_ref[...], sh_ref[...], sc_ref[...]).astype(BF16)
    a = jnp.dot(hn, wa_ref[...], preferred_element_type=F32)
    gt = jnp.dot(hn, wg_ref[...], preferred_element_type=F32)
    a_ref[...] = a
    conv = bc_ref[...] + b0_ref[...] * wc_ref[0:1, :] + b1_ref[...] * wc_ref[1:2, :] + a * wc_ref[2:3, :]
    act = (conv * jax.nn.sigmoid(conv) * gt).astype(BF16)
    part = jnp.dot(act, wd_ref[...], preferred_element_type=F32)

    @pl.when(c == 0)
    def _():
        acc_ref[...] = part

    @pl.when(c > 0)
    def _():
        acc_ref[...] += part

    y = x + gt_ref[...] * acc_ref[...]
    if final_norm:
        y = _rms(y, gf_ref[...])
    o_ref[...] = y


def conv_ffn_tok(x, buf0, buf1, g, sh, sc, gt, w_up_a, w_up_g, w_conv, b_conv, w_down, g_final, final_norm, name):
    bsz, d = x.shape
    ff = w_up_a.shape[1]
    fc = min(FFN_TOK_CHUNK, ff)
    assert ff % fc == 0
    full = lambda w: pl.BlockSpec((bsz, w), lambda c: (0, 0))
    cols = lambda r: pl.BlockSpec((r, fc), lambda c: (0, c))
    up_axis, down_axis = len(w_up_a.block) - 1, len(w_down.block) - 2
    return pl.pallas_call(
        functools.partial(_ffn_tok_kernel, final_norm=final_norm),
        grid=(ff // fc,),
        in_specs=[full(d), cols(bsz), cols(bsz), _const_spec((1, d)), full(d), full(d), full(d),
                  w_up_a.chunk_spec(up_axis, fc), w_up_g.chunk_spec(up_axis, fc), cols(w_conv.shape[0]), cols(1),
                  w_down.chunk_spec(down_axis, fc), _const_spec((1, d))],
        out_specs=[full(d), cols(bsz)],
        out_shape=[jax.ShapeDtypeStruct((bsz, d), F32), jax.ShapeDtypeStruct((bsz, ff), F32)],
        scratch_shapes=[pltpu.VMEM((bsz, d), F32)],
        compiler_params=_params("arbitrary"),
        name=name,
    )(x, buf0, buf1, g, sh, sc, gt, w_up_a.array, w_up_g.array, w_conv, b_conv, w_down.array, g_final)


def _latent_kernel(x_ref, g_ref, sh_ref, sc_ref, w_ref, gkv_ref, rc_ref, ra_ref, rb_ref, *refs,
                   kv_lora, rope, nope, with_kv):
    if with_kv:
        wuk_ref, wuv_ref, one_ref, ckv_ref, kpe_ref, kcat_ref, v_ref, ksq_ref = refs
    else:
        ckv_ref, kpe_ref = refs
    hn = _norm_mod(x_ref[...], g_ref[...], sh_ref[...], sc_ref[...]).astype(BF16)
    lat = jnp.dot(hn, w_ref[...], preferred_element_type=F32)
    ckv = _rms(lat[:, :kv_lora], gkv_ref[...])
    ckv_ref[...] = ckv
    kpe = _rope3(lat[:, kv_lora:kv_lora + LANES], rc_ref[...], ra_ref[...], rb_ref[...], rope // 2)
    kpe_ref[...] = kpe[:, :rope]
    if with_kv:
        cb = ckv.astype(BF16)
        kn = jnp.dot(cb, wuk_ref[...], preferred_element_type=F32)
        kpe_hi = pltpu.roll(kpe, nope, axis=1)
        for h in range(kn.shape[1] // LANES):
            hs = slice(h * LANES, (h + 1) * LANES)
            kb = (kn[:, hs] + kpe_hi).astype(BF16)
            kcat_ref[:, hs] = kb
            kb = kb.astype(F32)
            ksq_ref[:, h:h + 1] = jnp.max(jnp.sum(kb * kb, axis=1, keepdims=True), axis=0, keepdims=True)
        v_t = lax.dot_general(wuv_ref[...], cb, NT_DIMS, preferred_element_type=F32)
        v_ref[...] = (v_t + one_ref[...]).astype(BF16)


def shared_latent(x, g, sh, sc, w_dkv_p, g_kv, tabs, kv_lora, rope, nope, w_uk_r=None, w_uv_r=None, v_head=None):
    gq, r, d = x.shape
    tm = min(ROW_TILE, r)
    per_row = sh.shape[1] != 1
    with_kv = w_uk_r is not None
    tab_spec = pl.BlockSpec((tm, LANES), lambda g_, i: (i, 0))
    in_specs = [_row_spec(tm, d), _const_spec((1, d)), _mod_spec(per_row, tm, d), _mod_spec(per_row, tm, d),
                _const_spec(w_dkv_p.shape), _const_spec((1, kv_lora)), tab_spec, tab_spec, tab_spec]
    out_specs = [_row_spec(tm, kv_lora), _row_spec(tm, rope)]
    out_shape = [jax.ShapeDtypeStruct((gq, r, kv_lora), F32), jax.ShapeDtypeStruct((gq, r, rope), F32)]
    args = [x, g, sh, sc, w_dkv_p, g_kv, *tabs]
    if with_kv:
        vt_rows = w_uv_r.shape[0]
        ones_row = (jnp.arange(vt_rows) % LANES == v_head).astype(F32)[:, None]
        in_specs += [_const_spec(w_uk_r.shape), _const_spec(w_uv_r.shape), _const_spec(ones_row.shape)]
        n_heads = w_uk_r.shape[1] // LANES
        out_specs += [_row_spec(tm, w_uk_r.shape[1]), pl.BlockSpec((None, vt_rows, tm), lambda g_, i: (g_, 0, i)),
                      pl.BlockSpec((None, None, 1, n_heads), lambda g_, i: (g_, i, 0, 0))]
        out_shape += [jax.ShapeDtypeStruct((gq, r, w_uk_r.shape[1]), BF16),
                      jax.ShapeDtypeStruct((gq, vt_rows, r), BF16),
                      jax.ShapeDtypeStruct((gq, r // tm, 1, n_heads), F32)]
        args += [w_uk_r, w_uv_r, ones_row]
    return pl.pallas_call(
        functools.partial(_latent_kernel, kv_lora=kv_lora, rope=rope, nope=nope, with_kv=with_kv),
        grid=(gq, r // tm),
        in_specs=in_specs, out_specs=out_specs, out_shape=out_shape,
        compiler_params=_params("parallel", "parallel"),
        name="shared_latent_kv" if with_kv else "shared_latent",
    )(*args)


def _query_kernel(x_ref, g_ref, sh_ref, sc_ref, wdq_ref, gq_ref, wuq_ref, wrot_ref, rc_ref, rs_ref, q_ref, qsq_ref):
    hn = _norm_mod(x_ref[...], g_ref[...], sh_ref[...], sc_ref[...]).astype(BF16)
    qd = jnp.dot(hn, wdq_ref[...], preferred_element_type=F32)
    qn = _rms(qd, gq_ref[...]).astype(BF16)
    qf = jnp.dot(qn, wuq_ref[...], preferred_element_type=F32)
    qr = jnp.dot(qn, wrot_ref[...], preferred_element_type=F32)
    rc, rs = rc_ref[...], rs_ref[...]
    for h in range(qf.shape[1] // LANES):
        hs = slice(h * LANES, (h + 1) * LANES)
        qb = (qf[:, hs] * rc + qr[:, hs] * rs).astype(q_ref.dtype)
        q_ref[:, hs] = qb
        qb = qb.astype(F32)
        qsq_ref[:, h:h + 1] = jnp.sum(qb * qb, axis=1, keepdims=True)


def mla_queries(x, g, sh, sc, w_dq, g_q, w_uq_r, w_uq_rot, tab_cos, tab_sin):
    gq, r, d = x.shape
    tm = min(ROW_TILE, r)
    per_row = sh.shape[1] != 1
    n_heads = w_uq_r.shape[1] // LANES
    tab_spec = pl.BlockSpec((tm, LANES), lambda g_, i: (i, 0))
    return pl.pallas_call(
        _query_kernel,
        grid=(gq, r // tm),
        in_specs=[_row_spec(tm, d), _const_spec((1, d)), _mod_spec(per_row, tm, d), _mod_spec(per_row, tm, d),
                  _const_spec(w_dq.shape), _const_spec(g_q.shape), _const_spec(w_uq_r.shape),
                  _const_spec(w_uq_rot.shape), tab_spec, tab_spec],
        out_specs=[_row_spec(tm, w_uq_r.shape[1]), _row_spec(tm, n_heads)],
        out_shape=[jax.ShapeDtypeStruct((gq, r, w_uq_r.shape[1]), BF16),
                   jax.ShapeDtypeStruct((gq, r, n_heads), F32)],
        compiler_params=_params("parallel", "parallel"),
        name="mla_queries",
    )(x, g, sh, sc, w_dq, g_q, w_uq_r, w_uq_rot, tab_cos, tab_sin)


def _attn_kernel(q_ref, k_ref, vt_ref, qsq_ref, ksq_ref, o_ref, *, v_head, v_rows, blk, key_blocks_per_tile):
    qi = pl.program_id(2)
    tq = q_ref.shape[0]
    n_full = qi * (tq // blk)
    base = pl.multiple_of(qi * tq, tq)
    heads = range(2)
    diag = [(h, base + c * blk, c * blk, True) for c in range(tq // blk) for h in heads]

    def scores(h, kstart, q_lo):
        hs = slice(h * LANES, (h + 1) * LANES)
        return lax.dot_general(k_ref[pl.ds(kstart, blk), hs], q_ref[q_lo:tq, hs], NT_DIMS,
                               preferred_element_type=F32)

    def values(h, kstart):
        return vt_ref[h * LANES:h * LANES + v_rows, pl.ds(kstart, blk)]

    def cols_from(x, q_lo, new_cols):
        return new_cols if q_lo == 0 else jnp.concatenate([x[:, :q_lo], new_cols], axis=1)

    def run(tasks, carry, update):
        carry = list(carry)
        nxt = scores(*tasks[0][:3])
        for t, (h, kstart, q_lo, masked) in enumerate(tasks):
            s = nxt
            if t + 1 < len(tasks):
                nxt = scores(*tasks[t + 1][:3])
            if masked:
                key = lax.broadcasted_iota(jnp.int32, s.shape, 0)
                qry = lax.broadcasted_iota(jnp.int32, s.shape, 1)
                s = jnp.where(key <= qry, s, -jnp.inf)
            carry[h] = update(h, s, kstart, q_lo, carry[h])
        return tuple(carry)

    def sweep(update, init):
        per_trip = 2 if (tq // blk) % 2 == 0 else 1

        def full_chunks(trip, carry):
            kstart = pl.multiple_of(trip * (per_trip * blk), per_trip * blk)
            return run([(h, kstart + c * blk, 0, False) for c in range(per_trip) for h in heads], carry, update)
        return run(diag, lax.fori_loop(0, n_full // per_trip, full_chunks, init), update)

    def store(accs):
        outs = []
        for acc in accs:
            out_t = acc[:v_head] / acc[v_head:v_head + 1]
            outs.append(jnp.transpose(out_t))
        o_ref[...] = jnp.concatenate(outs, axis=1).astype(o_ref.dtype)

    ksq_blocks = ksq_ref[...]
    visible = lax.broadcasted_iota(jnp.int32, ksq_blocks.shape, 1) < (qi + 1) * key_blocks_per_tile
    ksq_max = jnp.max(jnp.where(visible, ksq_blocks, 0.0), axis=1, keepdims=True)
    bounds = [jnp.sqrt(qsq_ref[h:h + 1, :] * ksq_max[h:h + 1, :]) for h in heads]

    def update_bounded(h, s, kstart, q_lo, acc):
        p = jnp.exp2(s - bounds[h][:, q_lo:]).astype(BF16)
        pv = jnp.dot(values(h, kstart), p, preferred_element_type=F32)
        return cols_from(acc, q_lo, acc[:, q_lo:] + pv)

    accs = sweep(update_bounded, tuple(jnp.zeros((v_rows, tq), F32) for _ in heads))
    store(accs)
    smallest = jnp.minimum(accs[0][v_head:v_head + 1], accs[1][v_head:v_head + 1])
    row_sums_ok = jnp.min(smallest) >= SOFTMAX_MIN_ROW_SUM

    @pl.when(jnp.logical_not(row_sums_ok))
    def _():
        def update_online(h, s, kstart, q_lo, state):
            m, acc = state
            m_old = m[:, q_lo:]
            m_new = jnp.maximum(m_old, jnp.max(s, axis=0, keepdims=True))
            p = jnp.exp2(s - m_new).astype(BF16)
            pv = jnp.dot(values(h, kstart), p, preferred_element_type=F32)
            return (cols_from(m, q_lo, m_new),
                    cols_from(acc, q_lo, jnp.exp2(m_old - m_new) * acc[:, q_lo:] + pv))

        init = tuple((jnp.full((1, tq), -jnp.inf, F32), jnp.zeros((v_rows, tq), F32)) for _ in heads)
        store([acc for _, acc in sweep(update_online, init)])


def prompt_attention(q, kcat, vt, qsq, ksq_blocks, v_head):
    bsz, s, hw = q.shape
    pairs = hw // (2 * LANES)
    assert 2 * v_head == LANES, "two heads fill one 128-lane output block"
    tq = min(ATTN_TILE, s)
    blk = min(ATTN_BLOCK, tq)
    n_blocks = ksq_blocks.shape[1]
    assert n_blocks % (s // tq) == 0, "query tiles must cover whole key-norm blocks"
    v_rows = -(-(v_head + 1) // BF16_ROWS) * BF16_ROWS
    qsq_t = jnp.swapaxes(qsq, 1, 2).reshape(bsz, pairs, 2, s)
    ksq_t = jnp.swapaxes(ksq_blocks, 1, 2).reshape(bsz, pairs, 2, n_blocks)
    return pl.pallas_call(
        functools.partial(_attn_kernel, v_head=v_head, v_rows=v_rows, blk=blk,
                          key_blocks_per_tile=n_blocks // (s // tq)),
        grid=(bsz, pairs, s // tq),
        in_specs=[pl.BlockSpec((None, tq, 2 * LANES), lambda b, j, i: (b, i, j)),
                  pl.BlockSpec((None, s, 2 * LANES), lambda b, j, i: (b, 0, j)),
                  pl.BlockSpec((None, 2 * LANES, s), lambda b, j, i: (b, j, 0)),
                  pl.BlockSpec((None, None, 2, tq), lambda b, j, i: (b, j, 0, i)),
                  pl.BlockSpec((None, None, 2, n_blocks), lambda b, j, i: (b, j, 0, 0))],
        out_specs=pl.BlockSpec((None, tq, 2 * v_head), lambda b, j, i: (b, i, j)),
        out_shape=jax.ShapeDtypeStruct((bsz, s, pairs * 2 * v_head), BF16),
        compiler_params=_params("parallel", "parallel", "arbitrary"),
        name="prompt_attention",
    )(q, kcat, vt, qsq_t, ksq_t)


def _head_proj_kernel(a_ref, w_ref, o_ref):
    kk = w_ref.shape[1]
    for h in range(w_ref.shape[0]):
        o_ref[h] = jnp.dot(a_ref[:, h * kk:(h + 1) * kk].astype(BF16), w_ref[h],
                           preferred_element_type=F32).astype(o_ref.dtype)


def head_proj_lanes(a, w, out_dtype, name):
    bsz = a.shape[0]
    nh, kk, n = w.shape
    return pl.pallas_call(
        _head_proj_kernel,
        grid=(1,),
        in_specs=[pl.BlockSpec(a.shape, lambda i: (0, 0)), pl.BlockSpec(w.shape, lambda i: (0, 0, 0))],
        out_specs=pl.BlockSpec((nh, bsz, n), lambda i: (0, 0, 0)),
        out_shape=jax.ShapeDtypeStruct((nh, bsz, n), out_dtype),
        compiler_params=_params("arbitrary"),
        name=name,
    )(a, w)


def _paged_attn_kernel(pt_ref, ql_ref, qp_ref, cn_ref, kn_ref, ckv_hbm, kpe_hbm, o_ref,
                       ckv_buf, kpe_buf, s_ref, cb0, p0, w0, cb1, p1, w1, sem, *, n_seq, n_pages, ps, chunk):
    step = pl.program_id(0)
    slot = step % 2
    nh, c_lat = ql_ref.shape
    sets = ((cb0, p0, w0), (cb1, p1, w1))

    def page_copies(bi, sl):
        out = []
        for pg in range(n_pages):
            page = pt_ref[bi * n_pages + pg]
            rows = pl.ds(pg * ps, ps)
            out.append(pltpu.make_async_copy(ckv_hbm.at[page], ckv_buf.at[sl, rows, :], sem.at[0, sl]))
            out.append(pltpu.make_async_copy(kpe_hbm.at[page], kpe_buf.at[sl, :, rows], sem.at[1, sl]))
        return out

    @pl.when(step == 0)
    def _():
        for cp in page_copies(0, 0):
            cp.start()
        cb1[...] = jnp.zeros_like(cb1)
        p1[...] = jnp.zeros_like(p1)
        w1[...] = jnp.concatenate([jnp.zeros((nh, c_lat), F32), jnp.ones((nh, LANES), F32)], axis=1)

    @pl.when(step + 1 < n_seq)
    def _():
        for cp in page_copies(step + 1, 1 - slot):
            cp.start()

    @pl.when(step < n_seq)
    def _():
        for cp in page_copies(step, slot):
            cp.wait()

    n_chunks = n_pages * ps // chunk
    chunks = [slice(c * chunk, (c + 1) * chunk) for c in range(n_chunks)]

    def main(par):
        (cb_w, p_w, w_w), (cb_r, p_r, w_r) = sets[par], sets[1 - par]
        ql = ql_ref[...]
        qp = qp_ref[...]
        w_prev = w_r[...]
        accs = [w_prev[:, :c_lat], jnp.zeros((nh, c_lat), F32)]
        s_lat = []
        for c, cs in enumerate(chunks):
            ck = ckv_buf[par, cs, :].astype(BF16)
            cb_w[cs, :] = ck
            s_lat.append(lax.dot_general(ql, ck, NT_DIMS, preferred_element_type=F32))
            accs[c % 2] += jnp.dot(p_r[:, cs], cb_r[cs, :], preferred_element_type=F32)
        o_ref[...] = (accs[0] + accs[1]) / w_prev[:, c_lat:c_lat + 1]

        for cs, sl in zip(chunks, s_lat):
            s_ref[:, cs] = sl + jnp.dot(qp, kpe_buf[par, :, cs].astype(BF16), preferred_element_type=F32)

        cn = cn_ref[...].astype(BF16).astype(F32)
        kn = kn_ref[...].astype(BF16).astype(F32)
        s_new = (jnp.sum(ql.astype(F32) * cn, axis=1, keepdims=True)
                 + jnp.sum(qp.astype(F32) * kn, axis=1, keepdims=True))
        s = s_ref[...]
        m = jnp.maximum(jnp.max(s, axis=1, keepdims=True), s_new)
        p = jnp.exp2(s - m)
        p_new = jnp.exp2(s_new - m)
        l = jnp.sum(p, axis=1, keepdims=True) + p_new
        p_w[...] = p.astype(BF16)
        w_w[...] = jnp.concatenate([p_new.astype(BF16).astype(F32) * cn, jnp.broadcast_to(l, (nh, LANES))], axis=1)

    for par in range(2):
        pl.when(step % 2 == par)(functools.partial(main, par))


def paged_attention(q_lat, q_pe, ckv_new, kpe_new, cache_ckv, cache_kpe_t, page_table):
    bsz, nh, c = q_lat.shape
    r = q_pe.shape[-1]
    n_pages = page_table.shape[1]
    ps = cache_ckv.shape[1]
    past = n_pages * ps
    assert bsz >= 2, "the two-slot pipeline needs at least two sequences"
    chunk = min(PAGED_CHUNK, past)
    per_b = lambda rows, w: pl.BlockSpec((None, rows, w), lambda s_, pt: (jnp.minimum(s_, bsz - 1), 0, 0))
    return pl.pallas_call(
        functools.partial(_paged_attn_kernel, n_seq=bsz, n_pages=n_pages, ps=ps, chunk=chunk),
        grid_spec=pltpu.PrefetchScalarGridSpec(
            num_scalar_prefetch=1,
            grid=(bsz + 1,),
            in_specs=[per_b(nh, c), per_b(nh, r), per_b(1, c), per_b(1, r),
                      pl.BlockSpec(memory_space=pl.ANY), pl.BlockSpec(memory_space=pl.ANY)],
            out_specs=pl.BlockSpec((None, nh, c), lambda s_, pt: (jnp.maximum(s_ - 1, 0), 0, 0)),
            scratch_shapes=[pltpu.VMEM((2, past, c), F32), pltpu.VMEM((2, r, past), F32),
                            pltpu.VMEM((nh, past), F32)]
            + 2 * [pltpu.VMEM((past, c), BF16), pltpu.VMEM((nh, past), BF16), pltpu.VMEM((nh, c + LANES), F32)]
            + [pltpu.SemaphoreType.DMA((2, 2))]),
        out_shape=jax.ShapeDtypeStruct((bsz, nh, c), F32),
        compiler_params=_params("arbitrary"),
        name="paged_attention",
    )(page_table.reshape(-1), q_lat, q_pe, ckv_new, kpe_new, cache_ckv, cache_kpe_t)


def _rope_tables(pos, rope, lo, scale, passthrough):
    half = rope // 2
    freq = ROPE_THETA ** (-jnp.arange(half, dtype=F32) / half)
    ang = pos.astype(F32)[:, None] * freq[None, :]
    cos, sin = jnp.cos(ang), jnp.sin(ang)
    n = pos.shape[0]
    zeros = lambda w: jnp.zeros((n, w), F32)
    tail = LANES - lo - rope
    c = jnp.concatenate([jnp.full((n, lo), passthrough, F32), cos, cos, zeros(tail)], axis=1)
    sa = jnp.concatenate([zeros(lo), -sin, zeros(half + tail)], axis=1)
    sb = jnp.concatenate([zeros(lo + half), sin, zeros(tail)], axis=1)
    return c * scale, sa * scale, sb * scale


def _prep_weights(w_up, w_down, w_m_in, w_m_out, w_dkv, w_uk, w_uv, w_dq, w_uq, w_o, dims):
    heads, dk, dv = dims["m_heads"], dims["m_dk"], dims["m_dv"]
    nh, nope, rope, kv_lora = dims["mla_heads"], dims["qk_nope"], dims["qk_rope"], dims["kv_lora"]
    ff = w_down.shape[1]
    qd, vd = heads * dk, heads * dv
    d = w_up.shape[1]
    pw = {}
    w_up_b, w_down_b, w_m_in_b = w_up.astype(BF16), w_down.astype(BF16), w_m_in.astype(BF16)
    depth, n_a = w_up.shape[0], w_m_in.shape[0]
    pw["w_up_a"] = [WeightView(w_up_b, (None, d, ff), (l, 0, 0)) for l in range(depth)]
    pw["w_up_g"] = [WeightView(w_up_b, (None, d, ff), (l, 0, 1)) for l in range(depth)]
    pw["w_down"] = [WeightView(w_down_b, (None, ff, d), (l, 0, 0)) for l in range(depth)]
    assert (2 * qd) % vd == 0, "value / output-gate columns must start on a multiple of their width"
    pw["w_m_q"] = [WeightView(w_m_in_b, (None, d, qd), (l, 0, 0)) for l in range(n_a)]
    pw["w_m_k"] = [WeightView(w_m_in_b, (None, d, qd), (l, 0, 1)) for l in range(n_a)]
    pw["w_m_v"] = [WeightView(w_m_in_b, (None, d, vd), (l, 0, 2 * qd // vd)) for l in range(n_a)]
    pw["w_m_o"] = [WeightView(w_m_in_b, (None, d, vd), (l, 0, 2 * qd // vd + 1)) for l in range(n_a)]
    gates = jnp.pad(w_m_in[:, :, 2 * qd + 2 * vd:], ((0, 0), (0, 0), (0, LANES - 2 * heads))).astype(BF16)
    pw["w_m_g"] = [WeightView(gates, (None, d, LANES), (l, 0, 0)) for l in range(n_a)]
    pw["w_m_out"] = w_m_out.astype(BF16)
    pw["w_dkv"] = jnp.pad(w_dkv, ((0, 0), (0, LANES - rope))).astype(BF16)
    pw["w_uk_r"] = jnp.pad(w_uk, ((0, 0), (0, 0), (0, LANES - nope))).reshape(kv_lora, nh * LANES).astype(BF16)
    v_head = w_uv.shape[2]
    pw["w_uv_r"] = jnp.pad(w_uv, ((0, 0), (0, 0), (0, LANES - v_head))).reshape(kv_lora, nh * LANES).T.astype(BF16)
    nb = w_uq.shape[0]
    wq = w_uq.reshape(nb, w_uq.shape[1], nh, nope + rope)
    pw["w_uq_r"] = jnp.pad(wq, ((0, 0), (0, 0), (0, 0), (0, LANES - nope - rope))).reshape(
        nb, w_uq.shape[1], nh * LANES).astype(BF16)
    half = rope // 2
    partner = jnp.concatenate([-wq[..., nope + half:], wq[..., nope:nope + half]], axis=-1)
    pw["w_uq_rot"] = jnp.pad(partner, ((0, 0), (0, 0), (0, 0), (nope, LANES - nope - rope))).reshape(
        nb, w_uq.shape[1], nh * LANES).astype(BF16)
    pw["w_dq"] = w_dq.astype(BF16)
    pw["w_o"] = w_o.astype(BF16)
    wukt = jnp.transpose(w_uk, (1, 2, 0))
    pw["w_uk_t"] = jnp.pad(wukt, ((0, 0), (0, LANES - nope), (0, 0))).astype(BF16)
    pw["w_uv_t"] = jnp.transpose(w_uv, (1, 0, 2)).astype(BF16)
    return pw


def _trunk(x, mods, mods_kv, pos, conv_bufs, m_states, kv_past, pw, small, dims):
    heads, dk, dv = dims["m_heads"], dims["m_dk"], dims["m_dv"]
    nh, nope, rope, kv_lora, v_head = (dims["mla_heads"], dims["qk_nope"], dims["qk_rope"],
                                       dims["kv_lora"], dims["v_head"])
    depth, n_a = dims["depth"], dims["n_a"]
    is_prompt = kv_past is None
    gq, r, d = x.shape
    att_scale = (nope + rope) ** -0.5 * LOG2_E
    new_c, new_n, new_m, new_conv = [], [], [], []
    ckv = kpe = kcat = vv = ksq = None
    y = None
    for layer in range(depth):
        sh1, sc1, gt1, sh2, sc2, gt2 = mods[layer]
        g1 = small["g_norm1"][layer][None, :]
        g2 = small["g_norm2"][layer][None, :]
        if layer == n_a:
            sh_kv, sc_kv = mods_kv
            tabs = _rope_tables(pos, rope, 0, 1.0, 0.0)
            if is_prompt:
                ckv, kpe, kcat, vv, ksq = shared_latent(x, small["g_kv_in"][None, :], sh_kv, sc_kv, pw["w_dkv"],
                                                        small["g_kv"][None, :], tabs, kv_lora, rope, nope,
                                                        pw["w_uk_r"], pw["w_uv_r"], v_head)
            else:
                ckv, kpe = shared_latent(x, small["g_kv_in"][None, :], sh_kv, sc_kv, pw["w_dkv"],
                                         small["g_kv"][None, :], tabs, kv_lora, rope, nope)
        if layer < n_a:
            ws = [pw["w_m_q"][layer], pw["w_m_k"][layer], pw["w_m_v"][layer], pw["w_m_o"][layer], pw["w_m_g"][layer]]
            if is_prompt:
                q, k, v, o, gates = norm_mod_proj(x, g1, sh1, sc1, ws, [BF16, F32, BF16, F32, F32],
                                                  [dk ** -0.5, 1.0, 1.0, 1.0, 1.0], "mlstm_in_proj")
                hh, c_st, n_st, m_st = mlstm_chunkwise(q, k, v, o, gates,
                                                       small["b_m_gates"][layer], small["g_m_head"][layer],
                                                       heads, dk, dv)
                n_st = n_st.reshape(gq, heads, dk)
                m_st = m_st.reshape(gq, heads)
            else:
                q, k, v, o, gates = norm_mod_proj(x, g1, sh1, sc1, ws, [F32] * 5,
                                                  [dk ** -0.5, 1.0, 1.0, 1.0, 1.0], "mlstm_in_proj_tok")
                hh, c_st, n_st, m_st = mlstm_step(q[0], k[0], v[0], o[0], gates[0, :, :2 * heads],
                                                  small["b_m_gates"][layer], small["g_m_head"][layer],
                                                  m_states[0][layer], m_states[1][layer], m_states[2][layer],
                                                  heads, dk, dv)
                hh = hh[None]
            new_c.append(c_st)
            new_n.append(n_st)
            new_m.append(m_st)
            mix, w_mix = hh, pw["w_m_out"][layer]
        else:
            j = layer - n_a
            q_cos, q_msin, q_sin = _rope_tables(pos, rope, nope, att_scale, 1.0)
            qh, qsq = mla_queries(x, g1, sh1, sc1, pw["w_dq"][j], small["g_q"][j][None, :], pw["w_uq_r"][j],
                                  pw["w_uq_rot"][j], q_cos, q_sin - q_msin)
            if is_prompt:
                att = prompt_attention(qh, kcat, vv, qsq, ksq[:, :, 0, :], v_head)
            else:
                bsz = r
                q2 = qh[0]
                q_lat = head_proj_lanes(q2, pw["w_uk_t"], BF16, "absorb_q")
                q_lat = jnp.swapaxes(q_lat, 0, 1)
                q_pe = q2.reshape(bsz, nh, LANES)[:, :, nope:nope + rope]
                o_lat = paged_attention(q_lat, q_pe, ckv[0][:, None, :], kpe[0][:, None, :],
                                        kv_past[0], kv_past[1], kv_past[2])
                o_lat = o_lat.reshape(bsz, nh * kv_lora)
                att = head_proj_lanes(o_lat, pw["w_uv_t"], F32, "unabsorb_o")
                att = jnp.swapaxes(att, 0, 1).reshape(1, bsz, nh * v_head)
            mix, w_mix = att, pw["w_o"][j]
        final = layer == depth - 1
        gf = small["g_final"][None, :]
        if is_prompt:
            x, a_last = conv_ffn_seq(x, mix, w_mix, gt1, conv_bufs[layer], g2, sh2, sc2, gt2,
                                     pw["w_up_a"][layer], pw["w_up_g"][layer], small["w_conv"][layer],
                                     small["b_conv"][layer][None, :], pw["w_down"][layer], gf, final, "conv_ffn_seq")
            new_conv.append(a_last[:, CONV_HALO - (small["w_conv"].shape[1] - 1):, :])
        else:
            x = resid_proj(mix, w_mix, x, gt1, "mixer_out_proj")
            buf = conv_bufs[layer]
            x2, a_new = conv_ffn_tok(x[0], buf[:, 0, :], buf[:, 1, :], g2, sh2[0], sc2[0], gt2[0],
                                     pw["w_up_a"][layer], pw["w_up_g"][layer], small["w_conv"][layer],
                                     small["b_conv"][layer][None, :], pw["w_down"][layer], gf, final, "conv_ffn_tok")
            x = x2[None]
            new_conv.append(jnp.stack([buf[:, 1, :], a_new], axis=1))
    return x, jnp.stack(new_c), jnp.stack(new_n), jnp.stack(new_m), jnp.stack(new_conv), ckv, kpe


def kernel(x_prompt, x_sample, state_mlstm_C, state_mlstm_n, state_mlstm_m, state_conv, cache_ckv, cache_kpe,
           page_table, c_prompt, c_sample, g_norm1, g_norm2, w_ada, b_ada, w_up, w_conv, b_conv, w_down,
           w_m_in, b_m_gates, g_m_head, w_m_out, g_kv_in, w_ada_kv, b_ada_kv, w_dkv, g_kv, w_uk, w_uv,
           w_dq, g_q, w_uq, w_o, g_final):
    bp, s, d = x_prompt.shape
    bs, t, _ = x_sample.shape
    assert t == 1, "the sample path handles one new token per sequence"
    assert w_conv.shape[1] == 3, "the single-token ConvFFN kernel is written for a width-3 conv"
    depth = w_ada.shape[0]
    n_a = w_m_in.shape[0]
    heads, dv = g_m_head.shape[1], g_m_head.shape[2]
    dk = state_mlstm_C.shape[3]
    kv_lora, nh, nope = w_uk.shape
    v_head = w_uv.shape[2]
    rope = w_dkv.shape[1] - kv_lora
    ff = w_down.shape[1]
    dims = dict(m_heads=heads, m_dk=dk, m_dv=dv, mla_heads=nh, qk_nope=nope, qk_rope=rope, kv_lora=kv_lora,
                v_head=v_head, depth=depth, n_a=n_a)
    past_len = page_table.shape[1] * cache_ckv.shape[1]

    pw = _prep_weights(w_up, w_down, w_m_in, w_m_out, w_dkv, w_uk, w_uv, w_dq, w_uq, w_o, dims)
    small = dict(g_norm1=g_norm1, g_norm2=g_norm2, w_conv=w_conv, b_conv=b_conv, b_m_gates=b_m_gates,
                 g_m_head=g_m_head, g_kv_in=g_kv_in, g_kv=g_kv, g_q=g_q, g_final=g_final)

    c_all = jnp.concatenate([c_prompt, c_sample], axis=0)
    mod = ada_mod(c_all, w_ada, b_ada[:, None, :])
    mod_kv = ada_mod(c_all, w_ada_kv[None], b_ada_kv[None, None, :])[0]

    def split(m, n, lo, hi, per_row):
        parts = jnp.split(m[lo:hi], n, axis=-1)
        return [p[None] if per_row else p[:, None, :] for p in parts]

    mods_p = [split(mod[l], 6, 0, bp, False) for l in range(depth)]
    mods_s = [split(mod[l], 6, bp, bp + bs, True) for l in range(depth)]
    kv_p = split(mod_kv, 2, 0, bp, False)
    kv_s = split(mod_kv, 2, bp, bp + bs, True)

    conv0 = [jnp.zeros((bp, CONV_HALO, ff), F32)] * depth
    y_p, c_p, n_p, m_p, conv_p, ckv_p, kpe_p = _trunk(
        x_prompt, mods_p, kv_p, jnp.arange(s), conv0, None, None, pw, small, dims)

    pos_s = jnp.full((bs,), past_len, jnp.int32)
    y_s, c_s, n_s, m_s, conv_s, ckv_s, kpe_s = _trunk(
        x_sample.reshape(1, bs, d), mods_s, kv_s, pos_s, state_conv,
        (state_mlstm_C, state_mlstm_n, state_mlstm_m), (cache_ckv, jnp.swapaxes(cache_kpe, 1, 2), page_table), pw, small, dims)

    return (y_p, y_s.reshape(bs, 1, d), c_p, n_p, m_p, conv_p, ckv_p, kpe_p,
            c_s, n_s, m_s, conv_s, ckv_s.reshape(bs, 1, kv_lora), kpe_s.reshape(bs, 1, rope))
```

```python
import functools
import math
from typing import NamedTuple

import jax
import jax.numpy as jnp
from jax import lax
from jax.experimental import pallas as pl
from jax.experimental.pallas import tpu as pltpu

F32 = jnp.float32
BF16 = jnp.bfloat16

NORM_EPS = 1e-6
ROPE_THETA = 10000.0
LANES = 128
BF16_ROWS = 16
CONV_HALO = 8
VMEM_LIMIT = 56 * 1024 * 1024

ROW_TILE = 1024
FFN_ROW_TILE = 512
MLSTM_CHUNK = 256
ATTN_TILE = 2048
ATTN_BLOCK = 512
FFN_CHUNK = 256
FFN_TOK_CHUNK = 1408
PAGED_CHUNK = 512
LOG2_E = math.log2(math.e)
SOFTMAX_MIN_ROW_SUM = 2.0 ** -60
STEP_BATCH = 8

NT_DIMS = (((1,), (1,)), ((), ()))
TN_DIMS = (((0,), (0,)), ((), ()))


def _params(*sem):
    return pltpu.CompilerParams(dimension_semantics=sem, vmem_limit_bytes=VMEM_LIMIT)


def _const_spec(shape):
    nd = len(shape)
    return pl.BlockSpec(shape, lambda *_: (0,) * nd, pipeline_mode=pl.Buffered(1))


class WeightView(NamedTuple):
    array: jax.Array
    block: tuple
    index: tuple

    @property
    def shape(self):
        return tuple(d for d in self.block if d is not None)

    def spec(self):
        return pl.BlockSpec(self.block, lambda *_: self.index, pipeline_mode=pl.Buffered(1))

    def chunk_spec(self, axis, size):
        n = self.block[axis] // size
        block = tuple(size if a == axis else d for a, d in enumerate(self.block))
        index = lambda c: tuple(i * n + c if a == axis else i for a, i in enumerate(self.index))
        return pl.BlockSpec(block, index)


def _rms(x, g):
    return x * lax.rsqrt(jnp.mean(x * x, axis=-1, keepdims=True) + NORM_EPS) * g


def _norm_mod(x, g, sh, sc):
    return _rms(x, g) * (1.0 + sc) + sh


def _log_sigmoid(x):
    return jnp.minimum(x, 0.0) - jnp.log(1.0 + jnp.exp(-jnp.abs(x)))


def _rope3(t, c, sa, sb, half):
    return t * c + pltpu.roll(t, LANES - half, axis=1) * sa + pltpu.roll(t, half, axis=1) * sb


def _row_spec(tm, width):
    return pl.BlockSpec((None, tm, width), lambda g, i: (g, i, 0))


def _mod_spec(per_row, tm, width):
    if per_row:
        return pl.BlockSpec((None, tm, width), lambda g, i: (g, i, 0))
    return pl.BlockSpec((None, 1, width), lambda g, i: (g, 0, 0))


def _ada_kernel(c_ref, w_ref, b_ref, o_ref):
    c = c_ref[...]
    a = (c * jax.nn.sigmoid(c)).astype(BF16)
    o_ref[...] = jnp.dot(a, w_ref[...].astype(BF16), preferred_element_type=F32) + b_ref[...]


def ada_mod(c, w, b, tn=1024):
    m, d = c.shape
    nl, _, n = w.shape
    tn = min(tn, n)
    return pl.pallas_call(
        _ada_kernel,
        grid=(nl, n // tn),
        in_specs=[pl.BlockSpec((m, d), lambda l, j: (0, 0)),
                  pl.BlockSpec((None, d, tn), lambda l, j: (l, 0, j)),
                  pl.BlockSpec((None, 1, tn), lambda l, j: (l, 0, j))],
        out_specs=pl.BlockSpec((None, m, tn), lambda l, j: (l, 0, j)),
        out_shape=jax.ShapeDtypeStruct((nl, m, n), F32),
        compiler_params=_params("parallel", "parallel"),
        name="ada_mod",
    )(c, w, b)


def _proj_kernel(x_ref, g_ref, sh_ref, sc_ref, *refs, scales):
    n = len(scales)
    hn = _norm_mod(x_ref[...], g_ref[...], sh_ref[...], sc_ref[...]).astype(BF16)
    for w_ref, o_ref, s in zip(refs[:n], refs[n:], scales):
        acc = jnp.dot(hn, w_ref[...], preferred_element_type=F32)
        if s != 1.0:
            acc = acc * s
        o_ref[...] = acc.astype(o_ref.dtype)


def norm_mod_proj(x, g, sh, sc, ws, out_dtypes, scales, name):
    gq, r, d = x.shape
    tm = min(ROW_TILE, r)
    per_row = sh.shape[1] != 1
    in_specs = [_row_spec(tm, d), _const_spec((1, d)),
                _mod_spec(per_row, tm, d), _mod_spec(per_row, tm, d)]
    in_specs += [w.spec() for w in ws]
    return pl.pallas_call(
        functools.partial(_proj_kernel, scales=tuple(scales)),
        grid=(gq, r // tm),
        in_specs=in_specs,
        out_specs=[_row_spec(tm, w.shape[1]) for w in ws],
        out_shape=[jax.ShapeDtypeStruct((gq, r, w.shape[1]), dt) for w, dt in zip(ws, out_dtypes)],
        compiler_params=_params("parallel", "parallel"),
        name=name,
    )(x, g, sh, sc, *[w.array for w in ws])


def _resid_kernel(a_ref, w_ref, x_ref, gt_ref, o_ref):
    mix = jnp.dot(a_ref[...].astype(BF16), w_ref[...], preferred_element_type=F32)
    o_ref[...] = x_ref[...] + gt_ref[...] * mix


def resid_proj(a, w, x, gt, name):
    gq, r, d = x.shape
    k = a.shape[-1]
    tm = min(ROW_TILE, r)
    per_row = gt.shape[1] != 1
    return pl.pallas_call(
        _resid_kernel,
        grid=(gq, r // tm),
        in_specs=[_row_spec(tm, k), _const_spec(w.shape), _row_spec(tm, d), _mod_spec(per_row, tm, d)],
        out_specs=_row_spec(tm, d),
        out_shape=jax.ShapeDtypeStruct((gq, r, d), F32),
        compiler_params=_params("parallel", "parallel"),
        name=name,
    )(a, w, x, gt)


def _mlstm_chunk_kernel(q_ref, k_ref, v_ref, o_ref, g_ref, bgc_ref, bgr_ref, gh_ref,
                        hh_ref, c_ref, n_ref, m_ref, *, heads, dk, dv):
    ci = pl.program_id(1)
    chunk = q_ref.shape[0]

    @pl.when(ci == 0)
    def _():
        c_ref[...] = jnp.zeros_like(c_ref)
        n_ref[...] = jnp.zeros_like(n_ref)
        m_ref[...] = jnp.zeros_like(m_ref)

    gates = g_ref[...]
    gates_c = gates[:, :2 * heads] + bgr_ref[...]
    gates_r = jnp.transpose(gates)[:2 * heads, :] + bgc_ref[...]
    row = lax.broadcasted_iota(jnp.int32, (chunk, chunk), 0)
    col = lax.broadcasted_iota(jnp.int32, (chunk, chunk), 1)
    causal = col <= row

    s_raw, q_c = [], []
    for h in range(heads):
        q = q_ref[:, h * dk:(h + 1) * dk]
        s_raw.append(lax.dot_general(q, k_ref[:, h * dk:(h + 1) * dk].astype(BF16), NT_DIMS,
                                     preferred_element_type=F32))
        q_c.append(jnp.dot(q, c_ref[h].astype(BF16), preferred_element_type=F32))

    hr = range(heads)
    qs = [q_ref[:, h * dk:(h + 1) * dk] for h in hr]
    vs = [v_ref[:, h * dv:(h + 1) * dv] for h in hr]
    logf_r = _log_sigmoid(gates_r[heads:, :])
    logf_c = _log_sigmoid(gates_c[:, heads:])
    li_r = [gates_r[h:h + 1, :] for h in hr]
    lf_r = [logf_r[h:h + 1, :] for h in hr]
    li_c = [gates_c[:, h:h + 1] for h in hr]
    lf_c = [logf_c[:, h:h + 1] for h in hr]
    m_prev = [m_ref[h] for h in hr]
    n_prev = [n_ref[h] for h in hr]

    def lane_cumsum(x_r):
        low = jnp.where(causal, x_r, 0.0)
        if chunk % LANES == 0 and chunk > LANES:
            low = functools.reduce(jnp.add, [low[:, t:t + LANES] for t in range(0, chunk, LANES)])
        return jnp.sum(low, axis=1, keepdims=True)

    b_c = [lane_cumsum(lf_r[h]) for h in hr]
    b_r = [jnp.sum(jnp.where(row <= col, lf_c[h], 0.0), axis=0, keepdims=True) for h in hr]
    g = [jnp.sum(lf_r[h], axis=1, keepdims=True) for h in hr]

    a_c = [g[h] - b_c[h] + li_c[h] for h in hr]
    m_loc = [jnp.max(a_c[h], axis=0, keepdims=True) for h in hr]
    kw = [k_ref[:, h * dk:(h + 1) * dk] * jnp.exp(a_c[h] - m_loc[h]) for h in hr]
    c_loc = [lax.dot_general(kw[h].astype(BF16), vs[h], TN_DIMS, preferred_element_type=F32) for h in hr]
    n_loc = [jnp.sum(kw[h], axis=0, keepdims=True) for h in hr]

    dmat = [jnp.where(causal, b_c[h] - b_r[h] + li_r[h], -jnp.inf) for h in hr]
    w0 = [b_c[h] + m_prev[h] for h in hr]
    m_s = [jnp.maximum(w0[h], jnp.max(dmat[h], axis=1, keepdims=True)) for h in hr]
    w_inter = [jnp.exp(w0[h] - m_s[h]) for h in hr]
    s = [s_raw[h] * jnp.exp(dmat[h] - m_s[h]) for h in hr]
    num = [w_inter[h] * q_c[h] + jnp.dot(s[h].astype(BF16), vs[h], preferred_element_type=F32) for h in hr]
    def lane_fold(x):
        width = x.shape[1]
        if width % LANES or width == LANES:
            return x
        return functools.reduce(jnp.add, [x[:, t:t + LANES] for t in range(0, width, LANES)])

    def row_total(a, b):
        a, b = lane_fold(a), lane_fold(b)
        if a.shape == b.shape:
            return jnp.sum(a + b, axis=1, keepdims=True)
        return jnp.sum(a, axis=1, keepdims=True) + jnp.sum(b, axis=1, keepdims=True)

    den = [row_total(w_inter[h] * (qs[h].astype(F32) * n_prev[h]), s[h]) for h in hr]
    hval = [num[h] / jnp.maximum(jnp.abs(den[h]), jnp.exp(-m_s[h])) for h in hr]
    hn = [_rms(hval[h], gh_ref[h:h + 1, :]) for h in hr]
    for h in hr:
        gate = jax.nn.sigmoid(o_ref[:, h * dv:(h + 1) * dv])
        hh_ref[:, h * dv:(h + 1) * dv] = (hn[h] * gate).astype(hh_ref.dtype)

    for h in hr:
        m_new = jnp.maximum(g[h] + m_prev[h], m_loc[h])
        fw = jnp.exp(g[h] + m_prev[h] - m_new)
        lw = jnp.exp(m_loc[h] - m_new)
        c_ref[h] = fw * c_ref[h] + lw * c_loc[h]
        n_ref[h] = fw * n_prev[h] + lw * n_loc[h]
        m_ref[h] = m_new


def mlstm_chunkwise(q, k, v, o, gates, b_gates, g_head, heads, dk, dv):
    bsz, s, _ = q.shape
    chunk = min(MLSTM_CHUNK, s)
    g2 = 2 * heads
    return pl.pallas_call(
        functools.partial(_mlstm_chunk_kernel, heads=heads, dk=dk, dv=dv),
        grid=(bsz, s // chunk),
        in_specs=[_row_spec(chunk, heads * dk), _row_spec(chunk, heads * dk),
                  _row_spec(chunk, heads * dv), _row_spec(chunk, heads * dv),
                  _row_spec(chunk, gates.shape[-1]),
                  _const_spec((g2, 1)), _const_spec((1, g2)), _const_spec((heads, dv))],
        out_specs=[_row_spec(chunk, heads * dv),
                   pl.BlockSpec((None, heads, dk, dv), lambda b, c: (b, 0, 0, 0)),
                   pl.BlockSpec((None, heads, 1, dk), lambda b, c: (b, 0, 0, 0)),
                   pl.BlockSpec((None, heads, 1, 1), lambda b, c: (b, 0, 0, 0))],
        out_shape=[jax.ShapeDtypeStruct((bsz, s, heads * dv), BF16),
                   jax.ShapeDtypeStruct((bsz, heads, dk, dv), F32),
                   jax.ShapeDtypeStruct((bsz, heads, 1, dk), F32),
                   jax.ShapeDtypeStruct((bsz, heads, 1, 1), F32)],
        compiler_params=_params("parallel", "arbitrary"),
        name="mlstm_chunkwise",
    )(q, k, v, o, gates, b_gates.reshape(g2, 1), b_gates.reshape(1, g2), g_head)


def _mlstm_step_kernel(q_ref, k_ref, v_ref, o_ref, g_ref, bg_ref, gh_ref, c_ref, n_ref, m_ref,
                       hh_ref, co_ref, no_ref, mo_ref, *, heads, dk, dv):
    nb = q_ref.shape[0]
    gates = g_ref[...] + bg_ref[...]
    li = gates[:, :heads]
    lf = _log_sigmoid(gates[:, heads:])
    m_st = m_ref[...]
    m_new = jnp.maximum(lf + m_st, li)
    fw_all = jnp.exp(lf + m_st - m_new)
    iw_all = jnp.exp(li - m_new)
    floor_all = jnp.exp(-m_new)
    mo_ref[...] = m_new
    eye = lax.broadcasted_iota(jnp.int32, (dk, dk), 0) == lax.broadcasted_iota(jnp.int32, (dk, dk), 1)

    def to_col(r):
        return jnp.sum(jnp.where(eye, r, 0.0), axis=1, keepdims=True)

    hr = range(heads)
    for b in range(nb):
        q_r = [q_ref[b:b + 1, h * dk:(h + 1) * dk] for h in hr]
        k_r = [k_ref[b:b + 1, h * dk:(h + 1) * dk] for h in hr]
        k_c = [iw_all[b:b + 1, h:h + 1] * to_col(k_r[h]) for h in hr]
        c_new = [fw_all[b:b + 1, h:h + 1] * c_ref[b, h] + k_c[h] * v_ref[b:b + 1, h * dv:(h + 1) * dv] for h in hr]
        num = [jnp.dot(q_r[h].astype(BF16), c_new[h].astype(BF16), preferred_element_type=F32) for h in hr]
        n_new = [fw_all[b:b + 1, h:h + 1] * n_ref[b, h:h + 1, :] + iw_all[b:b + 1, h:h + 1] * k_r[h] for h in hr]
        den = [jnp.sum(q_r[h] * n_new[h], axis=1, keepdims=True) for h in hr]
        hval = [num[h] / jnp.maximum(jnp.abs(den[h]), floor_all[b:b + 1, h:h + 1]) for h in hr]
        hn = [_rms(hval[h], gh_ref[h:h + 1, :]) for h in hr]
        for h in hr:
            co_ref[b, h] = c_new[h]
            no_ref[b, h:h + 1, :] = n_new[h]
            hh_ref[b:b + 1, h * dv:(h + 1) * dv] = hn[h] * jax.nn.sigmoid(o_ref[b:b + 1, h * dv:(h + 1) * dv])


def mlstm_step(q, k, v, o, gates, b_gates, g_head, c_st, n_st, m_st, heads, dk, dv):
    bsz = q.shape[0]
    nb = min(STEP_BATCH, bsz)
    g2 = 2 * heads
    rows = lambda w: pl.BlockSpec((nb, w), lambda i: (i, 0))
    return pl.pallas_call(
        functools.partial(_mlstm_step_kernel, heads=heads, dk=dk, dv=dv),
        grid=(bsz // nb,),
        in_specs=[rows(heads * dk), rows(heads * dk), rows(heads * dv), rows(heads * dv), rows(g2),
                  _const_spec((1, g2)), _const_spec((heads, dv)),
                  pl.BlockSpec((nb, heads, dk, dv), lambda i: (i, 0, 0, 0)),
                  pl.BlockSpec((nb, heads, dk), lambda i: (i, 0, 0)),
                  rows(heads)],
        out_specs=[rows(heads * dv),
                   pl.BlockSpec((nb, heads, dk, dv), lambda i: (i, 0, 0, 0)),
                   pl.BlockSpec((nb, heads, dk), lambda i: (i, 0, 0)),
                   rows(heads)],
        out_shape=[jax.ShapeDtypeStruct((bsz, heads * dv), F32),
                   jax.ShapeDtypeStruct((bsz, heads, dk, dv), F32),
                   jax.ShapeDtypeStruct((bsz, heads, dk), F32),
                   jax.ShapeDtypeStruct((bsz, heads), F32)],
        compiler_params=_params("parallel"),
        name="mlstm_step",
    )(q, k, v, o, gates, b_gates.reshape(1, g2), g_head, c_st, n_st, m_st)


def _ffn_seq_kernel(x_ref, xh_ref, mix_ref, mixh_ref, wm_ref, gtm_ref, buf_ref, g_ref, sh_ref, sc_ref, gt_ref,
                    wa_ref, wg_ref, wc_ref, bc_ref, wd_ref, gf_ref, o_ref, alast_ref, *, fc, final_norm):
    i = pl.program_id(1)
    tm = x_ref.shape[0]
    ff = wa_ref.shape[1]
    halo = xh_ref.shape[0]
    taps = wc_ref.shape[0]
    gtm, wm = gtm_ref[...], wm_ref[...]
    x = x_ref[...] + gtm * jnp.dot(mix_ref[...], wm, preferred_element_type=F32)
    xh = xh_ref[...] + gtm * jnp.dot(mixh_ref[...], wm, preferred_element_type=F32)[mixh_ref.shape[0] - halo:]
    g, sh, sc = g_ref[...], sh_ref[...], sc_ref[...]
    hn = _norm_mod(x, g, sh, sc).astype(BF16)
    hh = _norm_mod(xh, g, sh, sc).astype(BF16)
    first = i == 0

    def up(c):
        cs = slice(c * fc, (c + 1) * fc)
        wa = wa_ref[:, cs]
        return (jnp.dot(hn, wa, preferred_element_type=F32),
                jnp.dot(hn, wg_ref[:, cs], preferred_element_type=F32),
                jnp.dot(hh, wa, preferred_element_type=F32))

    n_chunks = ff // fc
    acc = None
    nxt = up(0)
    for c in range(n_chunks):
        cs = slice(c * fc, (c + 1) * fc)
        a, gt, a_halo = nxt
        if c + 1 < n_chunks:
            nxt = up(c + 1)
        ext = jnp.concatenate([jnp.where(first, buf_ref[:, cs], a_halo), a], axis=0)
        conv = bc_ref[:, cs]
        for j in range(taps):
            lo = halo - (taps - 1) + j
            conv = conv + (a if lo == halo else ext[lo:lo + tm, :]) * wc_ref[j:j + 1, cs]
        act = (conv * jax.nn.sigmoid(conv) * gt).astype(BF16)
        part = jnp.dot(act, wd_ref[cs, :], preferred_element_type=F32)
        acc = part if acc is None else acc + part
        alast_ref[:, cs] = a[tm - halo:tm, :]

    y = x + gt_ref[...] * acc
    if final_norm:
        y = _rms(y, gf_ref[...])
    o_ref[...] = y


def conv_ffn_seq(x, mix, w_mix, gt_mix, buf, g, sh, sc, gt, w_up_a, w_up_g, w_conv, b_conv, w_down, g_final,
                 final_norm, name):
    bsz, s, d = x.shape
    ff = w_up_a.shape[1]
    kk = mix.shape[-1]
    tm = min(FFN_ROW_TILE, s)
    fc = min(FFN_CHUNK, ff)
    assert w_conv.shape[0] - 1 <= CONV_HALO
    halo = lambda b, i: (b, jnp.maximum(i * (tm // CONV_HALO) - 1, 0), 0)
    halo_bf16 = lambda b, i: (b, jnp.maximum(i * (tm // BF16_ROWS) - 1, 0), 0)
    return pl.pallas_call(
        functools.partial(_ffn_seq_kernel, fc=fc, final_norm=final_norm),
        grid=(bsz, s // tm),
        in_specs=[_row_spec(tm, d), pl.BlockSpec((None, CONV_HALO, d), halo),
                  _row_spec(tm, kk), pl.BlockSpec((None, BF16_ROWS, kk), halo_bf16),
                  _const_spec(w_mix.shape), _mod_spec(False, tm, d),
                  pl.BlockSpec((None, CONV_HALO, ff), lambda b, i: (b, 0, 0)),
                  _const_spec((1, d)), _mod_spec(False, tm, d), _mod_spec(False, tm, d), _mod_spec(False, tm, d),
                  w_up_a.spec(), w_up_g.spec(), _const_spec(w_conv.shape),
                  _const_spec(b_conv.shape), w_down.spec(), _const_spec((1, d))],
        out_specs=[_row_spec(tm, d), pl.BlockSpec((None, CONV_HALO, ff), lambda b, i: (b, 0, 0))],
        out_shape=[jax.ShapeDtypeStruct((bsz, s, d), F32), jax.ShapeDtypeStruct((bsz, CONV_HALO, ff), F32)],
        compiler_params=_params("parallel", "arbitrary"),
        name=name,
    )(x, x, mix, mix, w_mix, gt_mix, buf, g, sh, sc, gt, w_up_a.array, w_up_g.array, w_conv, b_conv,
      w_down.array, g_final)


def _ffn_tok_kernel(x_ref, b0_ref, b1_ref, g_ref, sh_ref, sc_ref, gt_ref, wa_ref, wg_ref, wc_ref,
                    bc_ref, wd_ref, gf_ref, o_ref, a_ref, acc_ref, *, final_norm):
    c = pl.program_id(0)
    x = x_ref[...]
    hn = _norm_mod(x, g_ref[...], sh_ref[...], sc_ref[...]).astype(BF16)
    a = jnp.dot(hn, wa_ref[...], preferred_element_type=F32)
    gt = jnp.dot(hn, wg_ref[...], preferred_element_type=F32)
    a_ref[...] = a
    conv = bc_ref[...] + b0_ref[...] * wc_ref[0:1, :] + b1_ref[...] * wc_ref[1:2, :] + a * wc_ref[2:3, :]
    act = (conv * jax.nn.sigmoid(conv) * gt).astype(BF16)
    part = jnp.dot(act, wd_ref[...], preferred_element_type=F32)

    @pl.when(c == 0)
    def _():
        acc_ref[...] = part

    @pl.when(c > 0)
    def _():
        acc_ref[...] += part

    y = x + gt_ref[...] * acc_ref[...]
    if final_norm:
        y = _rms(y, gf_ref[...])
    o_ref[...] = y


def conv_ffn_tok(x, buf0, buf1, g, sh, sc, gt, w_up_a, w_up_g, w_conv, b_conv, w_down, g_final, final_norm, name):
    bsz, d = x.shape
    ff = w_up_a.shape[1]
    fc = min(FFN_TOK_CHUNK, ff)
    assert ff % fc == 0
    full = lambda w: pl.BlockSpec((bsz, w), lambda c: (0, 0))
    cols = lambda r: pl.BlockSpec((r, fc), lambda c: (0, c))
    up_axis, down_axis = len(w_up_a.block) - 1, len(w_down.block) - 2
    return pl.pallas_call(
        functools.partial(_ffn_tok_kernel, final_norm=final_norm),
        grid=(ff // fc,),
        in_specs=[full(d), cols(bsz), cols(bsz), _const_spec((1, d)), full(d), full(d), full(d),
                  w_up_a.chunk_spec(up_axis, fc), w_up_g.chunk_spec(up_axis, fc), cols(w_conv.shape[0]), cols(1),
                  w_down.chunk_spec(down_axis, fc), _const_spec((1, d))],
        out_specs=[full(d), cols(bsz)],
        out_shape=[jax.ShapeDtypeStruct((bsz, d), F32), jax.ShapeDtypeStruct((bsz, ff), F32)],
        scratch_shapes=[pltpu.VMEM((bsz, d), F32)],
        compiler_params=_params("arbitrary"),
        name=name,
    )(x, buf0, buf1, g, sh, sc, gt, w_up_a.array, w_up_g.array, w_conv, b_conv, w_down.array, g_final)


def _latent_kernel(x_ref, g_ref, sh_ref, sc_ref, w_ref, gkv_ref, rc_ref, ra_ref, rb_ref, *refs,
                   kv_lora, rope, nope, with_kv):
    if with_kv:
        wuk_ref, wuv_ref, one_ref, ckv_ref, kpe_ref, kcat_ref, v_ref, ksq_ref = refs
    else:
        ckv_ref, kpe_ref = refs
    hn = _norm_mod(x_ref[...], g_ref[...], sh_ref[...], sc_ref[...]).astype(BF16)
    lat = jnp.dot(hn, w_ref[...], preferred_element_type=F32)
    ckv = _rms(lat[:, :kv_lora], gkv_ref[...])
    ckv_ref[...] = ckv
    kpe = _rope3(lat[:, kv_lora:kv_lora + LANES], rc_ref[...], ra_ref[...], rb_ref[...], rope // 2)
    kpe_ref[...] = kpe[:, :rope]
    if with_kv:
        cb = ckv.astype(BF16)
        kn = jnp.dot(cb, wuk_ref[...], preferred_element_type=F32)
        kpe_hi = pltpu.roll(kpe, nope, axis=1)
        for h in range(kn.shape[1] // LANES):
            hs = slice(h * LANES, (h + 1) * LANES)
            kb = (kn[:, hs] + kpe_hi).astype(BF16)
            kcat_ref[:, hs] = kb
            kb = kb.astype(F32)
            ksq_ref[:, h:h + 1] = jnp.max(jnp.sum(kb * kb, axis=1, keepdims=True), axis=0, keepdims=True)
        v_t = lax.dot_general(wuv_ref[...], cb, NT_DIMS, preferred_element_type=F32)
        v_ref[...] = (v_t + one_ref[...]).astype(BF16)


def shared_latent(x, g, sh, sc, w_dkv_p, g_kv, tabs, kv_lora, rope, nope, w_uk_r=None, w_uv_r=None, v_head=None):
    gq, r, d = x.shape
    tm = min(ROW_TILE, r)
    per_row = sh.shape[1] != 1
    with_kv = w_uk_r is not None
    tab_spec = pl.BlockSpec((tm, LANES), lambda g_, i: (i, 0))
    in_specs = [_row_spec(tm, d), _const_spec((1, d)), _mod_spec(per_row, tm, d), _mod_spec(per_row, tm, d),
                _const_spec(w_dkv_p.shape), _const_spec((1, kv_lora)), tab_spec, tab_spec, tab_spec]
    out_specs = [_row_spec(tm, kv_lora), _row_spec(tm, rope)]
    out_shape = [jax.ShapeDtypeStruct((gq, r, kv_lora), F32), jax.ShapeDtypeStruct((gq, r, rope), F32)]
    args = [x, g, sh, sc, w_dkv_p, g_kv, *tabs]
    if with_kv:
        vt_rows = w_uv_r.shape[0]
        ones_row = (jnp.arange(vt_rows) % LANES == v_head).astype(F32)[:, None]
        in_specs += [_const_spec(w_uk_r.shape), _const_spec(w_uv_r.shape), _const_spec(ones_row.shape)]
        n_heads = w_uk_r.shape[1] // LANES
        out_specs += [_row_spec(tm, w_uk_r.shape[1]), pl.BlockSpec((None, vt_rows, tm), lambda g_, i: (g_, 0, i)),
                      pl.BlockSpec((None, None, 1, n_heads), lambda g_, i: (g_, i, 0, 0))]
        out_shape += [jax.ShapeDtypeStruct((gq, r, w_uk_r.shape[1]), BF16),
                      jax.ShapeDtypeStruct((gq, vt_rows, r), BF16),
                      jax.ShapeDtypeStruct((gq, r // tm, 1, n_heads), F32)]
        args += [w_uk_r, w_uv_r, ones_row]
    return pl.pallas_call(
        functools.partial(_latent_kernel, kv_lora=kv_lora, rope=rope, nope=nope, with_kv=with_kv),
        grid=(gq, r // tm),
        in_specs=in_specs, out_specs=out_specs, out_shape=out_shape,
        compiler_params=_params("parallel", "parallel"),
        name="shared_latent_kv" if with_kv else "shared_latent",
    )(*args)


def _query_kernel(x_ref, g_ref, sh_ref, sc_ref, wdq_ref, gq_ref, wuq_ref, wrot_ref, rc_ref, rs_ref, q_ref, qsq_ref):
    hn = _norm_mod(x_ref[...], g_ref[...], sh_ref[...], sc_ref[...]).astype(BF16)
    qd = jnp.dot(hn, wdq_ref[...], preferred_element_type=F32)
    qn = _rms(qd, gq_ref[...]).astype(BF16)
    qf = jnp.dot(qn, wuq_ref[...], preferred_element_type=F32)
    qr = jnp.dot(qn, wrot_ref[...], preferred_element_type=F32)
    rc, rs = rc_ref[...], rs_ref[...]
    for h in range(qf.shape[1] // LANES):
        hs = slice(h * LANES, (h + 1) * LANES)
        qb = (qf[:, hs] * rc + qr[:, hs] * rs).astype(q_ref.dtype)
        q_ref[:, hs] = qb
        qb = qb.astype(F32)
        qsq_ref[:, h:h + 1] = jnp.sum(qb * qb, axis=1, keepdims=True)


def mla_queries(x, g, sh, sc, w_dq, g_q, w_uq_r, w_uq_rot, tab_cos, tab_sin):
    gq, r, d = x.shape
    tm = min(ROW_TILE, r)
    per_row = sh.shape[1] != 1
    n_heads = w_uq_r.shape[1] // LANES
    tab_spec = pl.BlockSpec((tm, LANES), lambda g_, i: (i, 0))
    return pl.pallas_call(
        _query_kernel,
        grid=(gq, r // tm),
        in_specs=[_row_spec(tm, d), _const_spec((1, d)), _mod_spec(per_row, tm, d), _mod_spec(per_row, tm, d),
                  _const_spec(w_dq.shape), _const_spec(g_q.shape), _const_spec(w_uq_r.shape),
                  _const_spec(w_uq_rot.shape), tab_spec, tab_spec],
        out_specs=[_row_spec(tm, w_uq_r.shape[1]), _row_spec(tm, n_heads)],
        out_shape=[jax.ShapeDtypeStruct((gq, r, w_uq_r.shape[1]), BF16),
                   jax.ShapeDtypeStruct((gq, r, n_heads), F32)],
        compiler_params=_params("parallel", "parallel"),
        name="mla_queries",
    )(x, g, sh, sc, w_dq, g_q, w_uq_r, w_uq_rot, tab_cos, tab_sin)


def _attn_kernel(q_ref, k_ref, vt_ref, qsq_ref, ksq_ref, o_ref, *, v_head, v_rows, blk, key_blocks_per_tile):
    qi = pl.program_id(2)
    tq = q_ref.shape[0]
    n_full = qi * (tq // blk)
    base = pl.multiple_of(qi * tq, tq)
    heads = range(2)
    diag = [(h, base + c * blk, c * blk, True) for c in range(tq // blk) for h in heads]

    def scores(h, kstart, q_lo):
        hs = slice(h * LANES, (h + 1) * LANES)
        return lax.dot_general(k_ref[pl.ds(kstart, blk), hs], q_ref[q_lo:tq, hs], NT_DIMS,
                               preferred_element_type=F32)

    def values(h, kstart):
        return vt_ref[h * LANES:h * LANES + v_rows, pl.ds(kstart, blk)]

    def cols_from(x, q_lo, new_cols):
        return new_cols if q_lo == 0 else jnp.concatenate([x[:, :q_lo], new_cols], axis=1)

    def run(tasks, carry, update):
        carry = list(carry)
        nxt = scores(*tasks[0][:3])
        for t, (h, kstart, q_lo, masked) in enumerate(tasks):
            s = nxt
            if t + 1 < len(tasks):
                nxt = scores(*tasks[t + 1][:3])
            if masked:
                key = lax.broadcasted_iota(jnp.int32, s.shape, 0)
                qry = lax.broadcasted_iota(jnp.int32, s.shape, 1)
                s = jnp.where(key <= qry, s, -jnp.inf)
            carry[h] = update(h, s, kstart, q_lo, carry[h])
        return tuple(carry)

    def sweep(update, init):
        per_trip = 2 if (tq // blk) % 2 == 0 else 1

        def full_chunks(trip, carry):
            kstart = pl.multiple_of(trip * (per_trip * blk), per_trip * blk)
            return run([(h, kstart + c * blk, 0, False) for c in range(per_trip) for h in heads], carry, update)
        return run(diag, lax.fori_loop(0, n_full // per_trip, full_chunks, init), update)

    def store(accs):
        outs = []
        for acc in accs:
            out_t = acc[:v_head] / acc[v_head:v_head + 1]
            outs.append(jnp.transpose(out_t))
        o_ref[...] = jnp.concatenate(outs, axis=1).astype(o_ref.dtype)

    ksq_blocks = ksq_ref[...]
    visible = lax.broadcasted_iota(jnp.int32, ksq_blocks.shape, 1) < (qi + 1) * key_blocks_per_tile
    ksq_max = jnp.max(jnp.where(visible, ksq_blocks, 0.0), axis=1, keepdims=True)
    bounds = [jnp.sqrt(qsq_ref[h:h + 1, :] * ksq_max[h:h + 1, :]) for h in heads]

    def update_bounded(h, s, kstart, q_lo, acc):
        p = jnp.exp2(s - bounds[h][:, q_lo:]).astype(BF16)
        pv = jnp.dot(values(h, kstart), p, preferred_element_type=F32)
        return cols_from(acc, q_lo, acc[:, q_lo:] + pv)

    accs = sweep(update_bounded, tuple(jnp.zeros((v_rows, tq), F32) for _ in heads))
    store(accs)
    smallest = jnp.minimum(accs[0][v_head:v_head + 1], accs[1][v_head:v_head + 1])
    row_sums_ok = jnp.min(smallest) >= SOFTMAX_MIN_ROW_SUM

    @pl.when(jnp.logical_not(row_sums_ok))
    def _():
        def update_online(h, s, kstart, q_lo, state):
            m, acc = state
            m_old = m[:, q_lo:]
            m_new = jnp.maximum(m_old, jnp.max(s, axis=0, keepdims=True))
            p = jnp.exp2(s - m_new).astype(BF16)
            pv = jnp.dot(values(h, kstart), p, preferred_element_type=F32)
            return (cols_from(m, q_lo, m_new),
                    cols_from(acc, q_lo, jnp.exp2(m_old - m_new) * acc[:, q_lo:] + pv))

        init = tuple((jnp.full((1, tq), -jnp.inf, F32), jnp.zeros((v_rows, tq), F32)) for _ in heads)
        store([acc for _, acc in sweep(update_online, init)])


def prompt_attention(q, kcat, vt, qsq, ksq_blocks, v_head):
    bsz, s, hw = q.shape
    pairs = hw // (2 * LANES)
    assert 2 * v_head == LANES, "two heads fill one 128-lane output block"
    tq = min(ATTN_TILE, s)
    blk = min(ATTN_BLOCK, tq)
    n_blocks = ksq_blocks.shape[1]
    assert n_blocks % (s // tq) == 0, "query tiles must cover whole key-norm blocks"
    v_rows = -(-(v_head + 1) // BF16_ROWS) * BF16_ROWS
    qsq_t = jnp.swapaxes(qsq, 1, 2).reshape(bsz, pairs, 2, s)
    ksq_t = jnp.swapaxes(ksq_blocks, 1, 2).reshape(bsz, pairs, 2, n_blocks)
    return pl.pallas_call(
        functools.partial(_attn_kernel, v_head=v_head, v_rows=v_rows, blk=blk,
                          key_blocks_per_tile=n_blocks // (s // tq)),
        grid=(bsz, pairs, s // tq),
        in_specs=[pl.BlockSpec((None, tq, 2 * LANES), lambda b, j, i: (b, i, j)),
                  pl.BlockSpec((None, s, 2 * LANES), lambda b, j, i: (b, 0, j)),
                  pl.BlockSpec((None, 2 * LANES, s), lambda b, j, i: (b, j, 0)),
                  pl.BlockSpec((None, None, 2, tq), lambda b, j, i: (b, j, 0, i)),
                  pl.BlockSpec((None, None, 2, n_blocks), lambda b, j, i: (b, j, 0, 0))],
        out_specs=pl.BlockSpec((None, tq, 2 * v_head), lambda b, j, i: (b, i, j)),
        out_shape=jax.ShapeDtypeStruct((bsz, s, pairs * 2 * v_head), BF16),
        compiler_params=_params("parallel", "parallel", "arbitrary"),
        name="prompt_attention",
    )(q, kcat, vt, qsq_t, ksq_t)


def _head_proj_kernel(a_ref, w_ref, o_ref):
    kk = w_ref.shape[1]
    for h in range(w_ref.shape[0]):
        o_ref[h] = jnp.dot(a_ref[:, h * kk:(h + 1) * kk].astype(BF16), w_ref[h],
                           preferred_element_type=F32).astype(o_ref.dtype)


def head_proj_lanes(a, w, out_dtype, name):
    bsz = a.shape[0]
    nh, kk, n = w.shape
    return pl.pallas_call(
        _head_proj_kernel,
        grid=(1,),
        in_specs=[pl.BlockSpec(a.shape, lambda i: (0, 0)), pl.BlockSpec(w.shape, lambda i: (0, 0, 0))],
        out_specs=pl.BlockSpec((nh, bsz, n), lambda i: (0, 0, 0)),
        out_shape=jax.ShapeDtypeStruct((nh, bsz, n), out_dtype),
        compiler_params=_params("arbitrary"),
        name=name,
    )(a, w)


def _paged_attn_kernel(pt_ref, ql_ref, qp_ref, cn_ref, kn_ref, ckv_hbm, kpe_hbm, o_ref,
                       ckv_buf, kpe_buf, s_ref, cb0, p0, w0, cb1, p1, w1, sem, *, n_seq, n_pages, ps, chunk):
    step = pl.program_id(0)
    slot = step % 2
    nh, c_lat = ql_ref.shape
    sets = ((cb0, p0, w0), (cb1, p1, w1))

    def page_copies(bi, sl):
        out = []
        for pg in range(n_pages):
            page = pt_ref[bi * n_pages + pg]
            rows = pl.ds(pg * ps, ps)
            out.append(pltpu.make_async_copy(ckv_hbm.at[page], ckv_buf.at[sl, rows, :], sem.at[0, sl]))
            out.append(pltpu.make_async_copy(kpe_hbm.at[page], kpe_buf.at[sl, :, rows], sem.at[1, sl]))
        return out

    @pl.when(step == 0)
    def _():
        for cp in page_copies(0, 0):
            cp.start()
        cb1[...] = jnp.zeros_like(cb1)
        p1[...] = jnp.zeros_like(p1)
        w1[...] = jnp.concatenate([jnp.zeros((nh, c_lat), F32), jnp.ones((nh, LANES), F32)], axis=1)

    @pl.when(step + 1 < n_seq)
    def _():
        for cp in page_copies(step + 1, 1 - slot):
            cp.start()

    @pl.when(step < n_seq)
    def _():
        for cp in page_copies(step, slot):
            cp.wait()

    n_chunks = n_pages * ps // chunk
    chunks = [slice(c * chunk, (c + 1) * chunk) for c in range(n_chunks)]

    def main(par):
        (cb_w, p_w, w_w), (cb_r, p_r, w_r) = sets[par], sets[1 - par]
        ql = ql_ref[...]
        qp = qp_ref[...]
        w_prev = w_r[...]
        accs = [w_prev[:, :c_lat], jnp.zeros((nh, c_lat), F32)]
        s_lat = []
        for c, cs in enumerate(chunks):
            ck = ckv_buf[par, cs, :].astype(BF16)
            cb_w[cs, :] = ck
            s_lat.append(lax.dot_general(ql, ck, NT_DIMS, preferred_element_type=F32))
            accs[c % 2] += jnp.dot(p_r[:, cs], cb_r[cs, :], preferred_element_type=F32)
        o_ref[...] = (accs[0] + accs[1]) / w_prev[:, c_lat:c_lat + 1]

        for cs, sl in zip(chunks, s_lat):
            s_ref[:, cs] = sl + jnp.dot(qp, kpe_buf[par, :, cs].astype(BF16), preferred_element_type=F32)

        cn = cn_ref[...].astype(BF16).astype(F32)
        kn = kn_ref[...].astype(BF16).astype(F32)
        s_new = (jnp.sum(ql.astype(F32) * cn, axis=1, keepdims=True)
                 + jnp.sum(qp.astype(F32) * kn, axis=1, keepdims=True))
        s = s_ref[...]
        m = jnp.maximum(jnp.max(s, axis=1, keepdims=True), s_new)
        p = jnp.exp2(s - m)
        p_new = jnp.exp2(s_new - m)
        l = jnp.sum(p, axis=1, keepdims=True) + p_new
        p_w[...] = p.astype(BF16)
        w_w[...] = jnp.concatenate([p_new.astype(BF16).astype(F32) * cn, jnp.broadcast_to(l, (nh, LANES))], axis=1)

    for par in range(2):
        pl.when(step % 2 == par)(functools.partial(main, par))


def paged_attention(q_lat, q_pe, ckv_new, kpe_new, cache_ckv, cache_kpe_t, page_table):
    bsz, nh, c = q_lat.shape
    r = q_pe.shape[-1]
    n_pages = page_table.shape[1]
    ps = cache_ckv.shape[1]
    past = n_pages * ps
    assert bsz >= 2, "the two-slot pipeline needs at least two sequences"
    chunk = min(PAGED_CHUNK, past)
    per_b = lambda rows, w: pl.BlockSpec((None, rows, w), lambda s_, pt: (jnp.minimum(s_, bsz - 1), 0, 0))
    return pl.pallas_call(
        functools.partial(_paged_attn_kernel, n_seq=bsz, n_pages=n_pages, ps=ps, chunk=chunk),
        grid_spec=pltpu.PrefetchScalarGridSpec(
            num_scalar_prefetch=1,
            grid=(bsz + 1,),
            in_specs=[per_b(nh, c), per_b(nh, r), per_b(1, c), per_b(1, r),
                      pl.BlockSpec(memory_space=pl.ANY), pl.BlockSpec(memory_space=pl.ANY)],
            out_specs=pl.BlockSpec((None, nh, c), lambda s_, pt: (jnp.maximum(s_ - 1, 0), 0, 0)),
            scratch_shapes=[pltpu.VMEM((2, past, c), F32), pltpu.VMEM((2, r, past), F32),
                            pltpu.VMEM((nh, past), F32)]
            + 2 * [pltpu.VMEM((past, c), BF16), pltpu.VMEM((nh, past), BF16), pltpu.VMEM((nh, c + LANES), F32)]
            + [pltpu.SemaphoreType.DMA((2, 2))]),
        out_shape=jax.ShapeDtypeStruct((bsz, nh, c), F32),
        compiler_params=_params("arbitrary"),
        name="paged_attention",
    )(page_table.reshape(-1), q_lat, q_pe, ckv_new, kpe_new, cache_ckv, cache_kpe_t)


def _rope_tables(pos, rope, lo, scale, passthrough):
    half = rope // 2
    freq = ROPE_THETA ** (-jnp.arange(half, dtype=F32) / half)
    ang = pos.astype(F32)[:, None] * freq[None, :]
    cos, sin = jnp.cos(ang), jnp.sin(ang)
    n = pos.shape[0]
    zeros = lambda w: jnp.zeros((n, w), F32)
    tail = LANES - lo - rope
    c = jnp.concatenate([jnp.full((n, lo), passthrough, F32), cos, cos, zeros(tail)], axis=1)
    sa = jnp.concatenate([zeros(lo), -sin, zeros(half + tail)], axis=1)
    sb = jnp.concatenate([zeros(lo + half), sin, zeros(tail)], axis=1)
    return c * scale, sa * scale, sb * scale


def _prep_weights(w_up, w_down, w_m_in, w_m_out, w_dkv, w_uk, w_uv, w_dq, w_uq, w_o, dims):
    heads, dk, dv = dims["m_heads"], dims["m_dk"], dims["m_dv"]
    nh, nope, rope, kv_lora = dims["mla_heads"], dims["qk_nope"], dims["qk_rope"], dims["kv_lora"]
    ff = w_down.shape[1]
    qd, vd = heads * dk, heads * dv
    d = w_up.shape[1]
    pw = {}
    w_up_b, w_down_b, w_m_in_b = w_up.astype(BF16), w_down.astype(BF16), w_m_in.astype(BF16)
    depth, n_a = w_up.shape[0], w_m_in.shape[0]
    pw["w_up_a"] = [WeightView(w_up_b, (None, d, ff), (l, 0, 0)) for l in range(depth)]
    pw["w_up_g"] = [WeightView(w_up_b, (None, d, ff), (l, 0, 1)) for l in range(depth)]
    pw["w_down"] = [WeightView(w_down_b, (None, ff, d), (l, 0, 0)) for l in range(depth)]
    assert (2 * qd) % vd == 0, "value / output-gate columns must start on a multiple of their width"
    pw["w_m_q"] = [WeightView(w_m_in_b, (None, d, qd), (l, 0, 0)) for l in range(n_a)]
    pw["w_m_k"] = [WeightView(w_m_in_b, (None, d, qd), (l, 0, 1)) for l in range(n_a)]
    pw["w_m_v"] = [WeightView(w_m_in_b, (None, d, vd), (l, 0, 2 * qd // vd)) for l in range(n_a)]
    pw["w_m_o"] = [WeightView(w_m_in_b, (None, d, vd), (l, 0, 2 * qd // vd + 1)) for l in range(n_a)]
    gates = jnp.pad(w_m_in[:, :, 2 * qd + 2 * vd:], ((0, 0), (0, 0), (0, LANES - 2 * heads))).astype(BF16)
    pw["w_m_g"] = [WeightView(gates, (None, d, LANES), (l, 0, 0)) for l in range(n_a)]
    pw["w_m_out"] = w_m_out.astype(BF16)
    pw["w_dkv"] = jnp.pad(w_dkv, ((0, 0), (0, LANES - rope))).astype(BF16)
    pw["w_uk_r"] = jnp.pad(w_uk, ((0, 0), (0, 0), (0, LANES - nope))).reshape(kv_lora, nh * LANES).astype(BF16)
    v_head = w_uv.shape[2]
    pw["w_uv_r"] = jnp.pad(w_uv, ((0, 0), (0, 0), (0, LANES - v_head))).reshape(kv_lora, nh * LANES).T.astype(BF16)
    nb = w_uq.shape[0]
    wq = w_uq.reshape(nb, w_uq.shape[1], nh, nope + rope)
    pw["w_uq_r"] = jnp.pad(wq, ((0, 0), (0, 0), (0, 0), (0, LANES - nope - rope))).reshape(
        nb, w_uq.shape[1], nh * LANES).astype(BF16)
    half = rope // 2
    partner = jnp.concatenate([-wq[..., nope + half:], wq[..., nope:nope + half]], axis=-1)
    pw["w_uq_rot"] = jnp.pad(partner, ((0, 0), (0, 0), (0, 0), (nope, LANES - nope - rope))).reshape(
        nb, w_uq.shape[1], nh * LANES).astype(BF16)
    pw["w_dq"] = w_dq.astype(BF16)
    pw["w_o"] = w_o.astype(BF16)
    wukt = jnp.transpose(w_uk, (1, 2, 0))
    pw["w_uk_t"] = jnp.pad(wukt, ((0, 0), (0, LANES - nope), (0, 0))).astype(BF16)
    pw["w_uv_t"] = jnp.transpose(w_uv, (1, 0, 2)).astype(BF16)
    return pw


def _trunk(x, mods, mods_kv, pos, conv_bufs, m_states, kv_past, pw, small, dims):
    heads, dk, dv = dims["m_heads"], dims["m_dk"], dims["m_dv"]
    nh, nope, rope, kv_lora, v_head = (dims["mla_heads"], dims["qk_nope"], dims["qk_rope"],
                                       dims["kv_lora"], dims["v_head"])
    depth, n_a = dims["depth"], dims["n_a"]
    is_prompt = kv_past is None
    gq, r, d = x.shape
    att_scale = (nope + rope) ** -0.5 * LOG2_E
    new_c, new_n, new_m, new_conv = [], [], [], []
    ckv = kpe = kcat = vv = ksq = None
    y = None
    for layer in range(depth):
        sh1, sc1, gt1, sh2, sc2, gt2 = mods[layer]
        g1 = small["g_norm1"][layer][None, :]
        g2 = small["g_norm2"][layer][None, :]
        if layer == n_a:
            sh_kv, sc_kv = mods_kv
            tabs = _rope_tables(pos, rope, 0, 1.0, 0.0)
            if is_prompt:
                ckv, kpe, kcat, vv, ksq = shared_latent(x, small["g_kv_in"][None, :], sh_kv, sc_kv, pw["w_dkv"],
                                                        small["g_kv"][None, :], tabs, kv_lora, rope, nope,
                                                        pw["w_uk_r"], pw["w_uv_r"], v_head)
            else:
                ckv, kpe = shared_latent(x, small["g_kv_in"][None, :], sh_kv, sc_kv, pw["w_dkv"],
                                         small["g_kv"][None, :], tabs, kv_lora, rope, nope)
        if layer < n_a:
            ws = [pw["w_m_q"][layer], pw["w_m_k"][layer], pw["w_m_v"][layer], pw["w_m_o"][layer], pw["w_m_g"][layer]]
            if is_prompt:
                q, k, v, o, gates = norm_mod_proj(x, g1, sh1, sc1, ws, [BF16, F32, BF16, F32, F32],
                                                  [dk ** -0.5, 1.0, 1.0, 1.0, 1.0], "mlstm_in_proj")
                hh, c_st, n_st, m_st = mlstm_chunkwise(q, k, v, o, gates,
                                                       small["b_m_gates"][layer], small["g_m_head"][layer],
                                                       heads, dk, dv)
                n_st = n_st.reshape(gq, heads, dk)
                m_st = m_st.reshape(gq, heads)
            else:
                q, k, v, o, gates = norm_mod_proj(x, g1, sh1, sc1, ws, [F32] * 5,
                                                  [dk ** -0.5, 1.0, 1.0, 1.0, 1.0], "mlstm_in_proj_tok")
                hh, c_st, n_st, m_st = mlstm_step(q[0], k[0], v[0], o[0], gates[0, :, :2 * heads],
                                                  small["b_m_gates"][layer], small["g_m_head"][layer],
                                                  m_states[0][layer], m_states[1][layer], m_states[2][layer],
                                                  heads, dk, dv)
                hh = hh[None]
            new_c.append(c_st)
            new_n.append(n_st)
            new_m.append(m_st)
            mix, w_mix = hh, pw["w_m_out"][layer]
        else:
            j = layer - n_a
            q_cos, q_msin, q_sin = _rope_tables(pos, rope, nope, att_scale, 1.0)
            qh, qsq = mla_queries(x, g1, sh1, sc1, pw["w_dq"][j], small["g_q"][j][None, :], pw["w_uq_r"][j],
                                  pw["w_uq_rot"][j], q_cos, q_sin - q_msin)
            if is_prompt:
                att = prompt_attention(qh, kcat, vv, qsq, ksq[:, :, 0, :], v_head)
            else:
                bsz = r
                q2 = qh[0]
                q_lat = head_proj_lanes(q2, pw["w_uk_t"], BF16, "absorb_q")
                q_lat = jnp.swapaxes(q_lat, 0, 1)
                q_pe = q2.reshape(bsz, nh, LANES)[:, :, nope:nope + rope]
                o_lat = paged_attention(q_lat, q_pe, ckv[0][:, None, :], kpe[0][:, None, :],
                                        kv_past[0], kv_past[1], kv_past[2])
                o_lat = o_lat.reshape(bsz, nh * kv_lora)
                att = head_proj_lanes(o_lat, pw["w_uv_t"], F32, "unabsorb_o")
                att = jnp.swapaxes(att, 0, 1).reshape(1, bsz, nh * v_head)
            mix, w_mix = att, pw["w_o"][j]
        final = layer == depth - 1
        gf = small["g_final"][None, :]
        if is_prompt:
            x, a_last = conv_ffn_seq(x, mix, w_mix, gt1, conv_bufs[layer], g2, sh2, sc2, gt2,
                                     pw["w_up_a"][layer], pw["w_up_g"][layer], small["w_conv"][layer],
                                     small["b_conv"][layer][None, :], pw["w_down"][layer], gf, final, "conv_ffn_seq")
            new_conv.append(a_last[:, CONV_HALO - (small["w_conv"].shape[1] - 1):, :])
        else:
            x = resid_proj(mix, w_mix, x, gt1, "mixer_out_proj")
            buf = conv_bufs[layer]
            x2, a_new = conv_ffn_tok(x[0], buf[:, 0, :], buf[:, 1, :], g2, sh2[0], sc2[0], gt2[0],
                                     pw["w_up_a"][layer], pw["w_up_g"][layer], small["w_conv"][layer],
                                     small["b_conv"][layer][None, :], pw["w_down"][layer], gf, final, "conv_ffn_tok")
            x = x2[None]
            new_conv.append(jnp.stack([buf[:, 1, :], a_new], axis=1))
    return x, jnp.stack(new_c), jnp.stack(new_n), jnp.stack(new_m), jnp.stack(new_conv), ckv, kpe


def kernel(x_prompt, x_sample, state_mlstm_C, state_mlstm_n, state_mlstm_m, state_conv, cache_ckv, cache_kpe,
           page_table, c_prompt, c_sample, g_norm1, g_norm2, w_ada, b_ada, w_up, w_conv, b_conv, w_down,
           w_m_in, b_m_gates, g_m_head, w_m_out, g_kv_in, w_ada_kv, b_ada_kv, w_dkv, g_kv, w_uk, w_uv,
           w_dq, g_q, w_uq, w_o, g_final):
    bp, s, d = x_prompt.shape
    bs, t, _ = x_sample.shape
    assert t == 1, "the sample path handles one new token per sequence"
    assert w_conv.shape[1] == 3, "the single-token ConvFFN kernel is written for a width-3 conv"
    depth = w_ada.shape[0]
    n_a = w_m_in.shape[0]
    heads, dv = g_m_head.shape[1], g_m_head.shape[2]
    dk = state_mlstm_C.shape[3]
    kv_lora, nh, nope = w_uk.shape
    v_head = w_uv.shape[2]
    rope = w_dkv.shape[1] - kv_lora
    ff = w_down.shape[1]
    dims = dict(m_heads=heads, m_dk=dk, m_dv=dv, mla_heads=nh, qk_nope=nope, qk_rope=rope, kv_lora=kv_lora,
                v_head=v_head, depth=depth, n_a=n_a)
    past_len = page_table.shape[1] * cache_ckv.shape[1]

    pw = _prep_weights(w_up, w_down, w_m_in, w_m_out, w_dkv, w_uk, w_uv, w_dq, w_uq, w_o, dims)
    small = dict(g_norm1=g_norm1, g_norm2=g_norm2, w_conv=w_conv, b_conv=b_conv, b_m_gates=b_m_gates,
                 g_m_head=g_m_head, g_kv_in=g_kv_in, g_kv=g_kv, g_q=g_q, g_final=g_final)

    c_all = jnp.concatenate([c_prompt, c_sample], axis=0)
    mod = ada_mod(c_all, w_ada, b_ada[:, None, :])
    mod_kv = ada_mod(c_all, w_ada_kv[None], b_ada_kv[None, None, :])[0]

    def split(m, n, lo, hi, per_row):
        parts = jnp.split(m[lo:hi], n, axis=-1)
        return [p[None] if per_row else p[:, None, :] for p in parts]

    mods_p = [split(mod[l], 6, 0, bp, False) for l in range(depth)]
    mods_s = [split(mod[l], 6, bp, bp + bs, True) for l in range(depth)]
    kv_p = split(mod_kv, 2, 0, bp, False)
    kv_s = split(mod_kv, 2, bp, bp + bs, True)

    conv0 = [jnp.zeros((bp, CONV_HALO, ff), F32)] * depth
    y_p, c_p, n_p, m_p, conv_p, ckv_p, kpe_p = _trunk(
        x_prompt, mods_p, kv_p, jnp.arange(s), conv0, None, None, pw, small, dims)

    pos_s = jnp.full((bs,), past_len, jnp.int32)
    y_s, c_s, n_s, m_s, conv_s, ckv_s, kpe_s = _trunk(
        x_sample.reshape(1, bs, d), mods_s, kv_s, pos_s, state_conv,
        (state_mlstm_C, state_mlstm_n, state_mlstm_m), (cache_ckv, jnp.swapaxes(cache_kpe, 1, 2), page_table), pw, small, dims)

    return (y_p, y_s.reshape(bs, 1, d), c_p, n_p, m_p, conv_p, ckv_p, kpe_p,
            c_s, n_s, m_s, conv_s, ckv_s.reshape(bs, 1, kv_lora), kpe_s.reshape(bs, 1, rope))
```

```python
import functools
import math
from typing import NamedTuple

import jax
import jax.numpy as jnp
from jax import lax
from jax.experimental import pallas as pl
from jax.experimental.pallas import tpu as pltpu

F32 = jnp.float32
BF16 = jnp.bfloat16

NORM_EPS = 1e-6
ROPE_THETA = 10000.0
LANES = 128
BF16_ROWS = 16
CONV_HALO = 8
VMEM_LIMIT = 56 * 1024 * 1024

ROW_TILE = 1024
FFN_ROW_TILE = 1024
MLSTM_CHUNK = 256
ATTN_TILE = 2048
ATTN_BLOCK = 512
FFN_CHUNK = 256
FFN_TOK_CHUNK = 1408
PAGED_CHUNK = 512
LOG2_E = math.log2(math.e)
SOFTMAX_MIN_ROW_SUM = 2.0 ** -60
STEP_BATCH = 8

NT_DIMS = (((1,), (1,)), ((), ()))
TN_DIMS = (((0,), (0,)), ((), ()))


def _params(*sem):
    return pltpu.CompilerParams(dimension_semantics=sem, vmem_limit_bytes=VMEM_LIMIT)


def _const_spec(shape):
    nd = len(shape)
    return pl.BlockSpec(shape, lambda *_: (0,) * nd, pipeline_mode=pl.Buffered(1))


class WeightView(NamedTuple):
    array: jax.Array
    block: tuple
    index: tuple

    @property
    def shape(self):
        return tuple(d for d in self.block if d is not None)

    def spec(self):
        return pl.BlockSpec(self.block, lambda *_: self.index, pipeline_mode=pl.Buffered(1))

    def chunk_spec(self, axis, size):
        n = self.block[axis] // size
        block = tuple(size if a == axis else d for a, d in enumerate(self.block))
        index = lambda c: tuple(i * n + c if a == axis else i for a, i in enumerate(self.index))
        return pl.BlockSpec(block, index)


def _rms(x, g):
    return x * lax.rsqrt(jnp.mean(x * x, axis=-1, keepdims=True) + NORM_EPS) * g


def _norm_mod(x, g, sh, sc):
    return _rms(x, g) * (1.0 + sc) + sh


def _log_sigmoid(x):
    return jnp.minimum(x, 0.0) - jnp.log(1.0 + jnp.exp(-jnp.abs(x)))


def _rope3(t, c, sa, sb, half):
    return t * c + pltpu.roll(t, LANES - half, axis=1) * sa + pltpu.roll(t, half, axis=1) * sb


def _row_spec(tm, width):
    return pl.BlockSpec((None, tm, width), lambda g, i: (g, i, 0))


def _mod_spec(per_row, tm, width):
    if per_row:
        return pl.BlockSpec((None, tm, width), lambda g, i: (g, i, 0))
    return pl.BlockSpec((None, 1, width), lambda g, i: (g, 0, 0))


def _ada_kernel(c_ref, w_ref, b_ref, o_ref):
    c = c_ref[...]
    a = (c * jax.nn.sigmoid(c)).astype(BF16)
    o_ref[...] = jnp.dot(a, w_ref[...].astype(BF16), preferred_element_type=F32) + b_ref[...]


def ada_mod(c, w, b, tn=1024):
    m, d = c.shape
    nl, _, n = w.shape
    tn = min(tn, n)
    return pl.pallas_call(
        _ada_kernel,
        grid=(nl, n // tn),
        in_specs=[pl.BlockSpec((m, d), lambda l, j: (0, 0)),
                  pl.BlockSpec((None, d, tn), lambda l, j: (l, 0, j)),
                  pl.BlockSpec((None, 1, tn), lambda l, j: (l, 0, j))],
        out_specs=pl.BlockSpec((None, m, tn), lambda l, j: (l, 0, j)),
        out_shape=jax.ShapeDtypeStruct((nl, m, n), F32),
        compiler_params=_params("parallel", "parallel"),
        name="ada_mod",
    )(c, w, b)


def _proj_kernel(x_ref, g_ref, sh_ref, sc_ref, *refs, scales):
    n = len(scales)
    hn = _norm_mod(x_ref[...], g_ref[...], sh_ref[...], sc_ref[...]).astype(BF16)
    for w_ref, o_ref, s in zip(refs[:n], refs[n:], scales):
        acc = jnp.dot(hn, w_ref[...], preferred_element_type=F32)
        if s != 1.0:
            acc = acc * s
        o_ref[...] = acc.astype(o_ref.dtype)


def norm_mod_proj(x, g, sh, sc, ws, out_dtypes, scales, name):
    gq, r, d = x.shape
    tm = min(ROW_TILE, r)
    per_row = sh.shape[1] != 1
    in_specs = [_row_spec(tm, d), _const_spec((1, d)),
                _mod_spec(per_row, tm, d), _mod_spec(per_row, tm, d)]
    in_specs += [w.spec() for w in ws]
    return pl.pallas_call(
        functools.partial(_proj_kernel, scales=tuple(scales)),
        grid=(gq, r // tm),
        in_specs=in_specs,
        out_specs=[_row_spec(tm, w.shape[1]) for w in ws],
        out_shape=[jax.ShapeDtypeStruct((gq, r, w.shape[1]), dt) for w, dt in zip(ws, out_dtypes)],
        compiler_params=_params("parallel", "parallel"),
        name=name,
    )(x, g, sh, sc, *[w.array for w in ws])


def _resid_kernel(a_ref, w_ref, x_ref, gt_ref, o_ref):
    mix = jnp.dot(a_ref[...].astype(BF16), w_ref[...], preferred_element_type=F32)
    o_ref[...] = x_ref[...] + gt_ref[...] * mix


def resid_proj(a, w, x, gt, name):
    gq, r, d = x.shape
    k = a.shape[-1]
    tm = min(ROW_TILE, r)
    per_row = gt.shape[1] != 1
    return pl.pallas_call(
        _resid_kernel,
        grid=(gq, r // tm),
        in_specs=[_row_spec(tm, k), _const_spec(w.shape), _row_spec(tm, d), _mod_spec(per_row, tm, d)],
        out_specs=_row_spec(tm, d),
        out_shape=jax.ShapeDtypeStruct((gq, r, d), F32),
        compiler_params=_params("parallel", "parallel"),
        name=name,
    )(a, w, x, gt)


def _mlstm_chunk_kernel(q_ref, k_ref, v_ref, o_ref, g_ref, bgc_ref, bgr_ref, gh_ref,
                        hh_ref, c_ref, n_ref, m_ref, *, heads, dk, dv):
    ci = pl.program_id(1)
    chunk = q_ref.shape[0]

    @pl.when(ci == 0)
    def _():
        c_ref[...] = jnp.zeros_like(c_ref)
        n_ref[...] = jnp.zeros_like(n_ref)
        m_ref[...] = jnp.zeros_like(m_ref)

    gates = g_ref[...]
    gates_c = gates[:, :2 * heads] + bgr_ref[...]
    gates_r = jnp.transpose(gates)[:2 * heads, :] + bgc_ref[...]
    row = lax.broadcasted_iota(jnp.int32, (chunk, chunk), 0)
    col = lax.broadcasted_iota(jnp.int32, (chunk, chunk), 1)
    causal = col <= row

    s_raw, q_c = [], []
    for h in range(heads):
        q = q_ref[:, h * dk:(h + 1) * dk]
        s_raw.append(lax.dot_general(q, k_ref[:, h * dk:(h + 1) * dk].astype(BF16), NT_DIMS,
                                     preferred_element_type=F32))
        q_c.append(jnp.dot(q, c_ref[h].astype(BF16), preferred_element_type=F32))

    hr = range(heads)
    qs = [q_ref[:, h * dk:(h + 1) * dk] for h in hr]
    vs = [v_ref[:, h * dv:(h + 1) * dv] for h in hr]
    logf_r = _log_sigmoid(gates_r[heads:, :])
    logf_c = _log_sigmoid(gates_c[:, heads:])
    li_r = [gates_r[h:h + 1, :] for h in hr]
    lf_r = [logf_r[h:h + 1, :] for h in hr]
    li_c = [gates_c[:, h:h + 1] for h in hr]
    lf_c = [logf_c[:, h:h + 1] for h in hr]
    m_prev = [m_ref[h] for h in hr]
    n_prev = [n_ref[h] for h in hr]

    def lane_cumsum(x_r):
        low = jnp.where(causal, x_r, 0.0)
        if chunk % LANES == 0 and chunk > LANES:
            low = functools.reduce(jnp.add, [low[:, t:t + LANES] for t in range(0, chunk, LANES)])
        return jnp.sum(low, axis=1, keepdims=True)

    b_c = [lane_cumsum(lf_r[h]) for h in hr]
    b_r = [jnp.sum(jnp.where(row <= col, lf_c[h], 0.0), axis=0, keepdims=True) for h in hr]
    g = [jnp.sum(lf_r[h], axis=1, keepdims=True) for h in hr]

    a_c = [g[h] - b_c[h] + li_c[h] for h in hr]
    m_loc = [jnp.max(a_c[h], axis=0, keepdims=True) for h in hr]
    kw = [k_ref[:, h * dk:(h + 1) * dk] * jnp.exp(a_c[h] - m_loc[h]) for h in hr]
    c_loc = [lax.dot_general(kw[h].astype(BF16), vs[h], TN_DIMS, preferred_element_type=F32) for h in hr]
    n_loc = [jnp.sum(kw[h], axis=0, keepdims=True) for h in hr]

    dmat = [jnp.where(causal, b_c[h] - b_r[h] + li_r[h], -jnp.inf) for h in hr]
    w0 = [b_c[h] + m_prev[h] for h in hr]
    m_s = [jnp.maximum(w0[h], jnp.max(dmat[h], axis=1, keepdims=True)) for h in hr]
    w_inter = [jnp.exp(w0[h] - m_s[h]) for h in hr]
    s = [s_raw[h] * jnp.exp(dmat[h] - m_s[h]) for h in hr]
    num = [w_inter[h] * q_c[h] + jnp.dot(s[h].astype(BF16), vs[h], preferred_element_type=F32) for h in hr]
    def lane_fold(x):
        width = x.shape[1]
        if width % LANES or width == LANES:
            return x
        return functools.reduce(jnp.add, [x[:, t:t + LANES] for t in range(0, width, LANES)])

    def row_total(a, b):
        a, b = lane_fold(a), lane_fold(b)
        if a.shape == b.shape:
            return jnp.sum(a + b, axis=1, keepdims=True)
        return jnp.sum(a, axis=1, keepdims=True) + jnp.sum(b, axis=1, keepdims=True)

    den = [row_total(w_inter[h] * (qs[h].astype(F32) * n_prev[h]), s[h]) for h in hr]
    hval = [num[h] / jnp.maximum(jnp.abs(den[h]), jnp.exp(-m_s[h])) for h in hr]
    hn = [_rms(hval[h], gh_ref[h:h + 1, :]) for h in hr]
    for h in hr:
        gate = jax.nn.sigmoid(o_ref[:, h * dv:(h + 1) * dv])
        hh_ref[:, h * dv:(h + 1) * dv] = (hn[h] * gate).astype(hh_ref.dtype)

    for h in hr:
        m_new = jnp.maximum(g[h] + m_prev[h], m_loc[h])
        fw = jnp.exp(g[h] + m_prev[h] - m_new)
        lw = jnp.exp(m_loc[h] - m_new)
        c_ref[h] = fw * c_ref[h] + lw * c_loc[h]
        n_ref[h] = fw * n_prev[h] + lw * n_loc[h]
        m_ref[h] = m_new


def mlstm_chunkwise(q, k, v, o, gates, b_gates, g_head, heads, dk, dv):
    bsz, s, _ = q.shape
    chunk = min(MLSTM_CHUNK, s)
    g2 = 2 * heads
    return pl.pallas_call(
        functools.partial(_mlstm_chunk_kernel, heads=heads, dk=dk, dv=dv),
        grid=(bsz, s // chunk),
        in_specs=[_row_spec(chunk, heads * dk), _row_spec(chunk, heads * dk),
                  _row_spec(chunk, heads * dv), _row_spec(chunk, heads * dv),
                  _row_spec(chunk, gates.shape[-1]),
                  _const_spec((g2, 1)), _const_spec((1, g2)), _const_spec((heads, dv))],
        out_specs=[_row_spec(chunk, heads * dv),
                   pl.BlockSpec((None, heads, dk, dv), lambda b, c: (b, 0, 0, 0)),
                   pl.BlockSpec((None, heads, 1, dk), lambda b, c: (b, 0, 0, 0)),
                   pl.BlockSpec((None, heads, 1, 1), lambda b, c: (b, 0, 0, 0))],
        out_shape=[jax.ShapeDtypeStruct((bsz, s, heads * dv), BF16),
                   jax.ShapeDtypeStruct((bsz, heads, dk, dv), F32),
                   jax.ShapeDtypeStruct((bsz, heads, 1, dk), F32),
                   jax.ShapeDtypeStruct((bsz, heads, 1, 1), F32)],
        compiler_params=_params("parallel", "arbitrary"),
        name="mlstm_chunkwise",
    )(q, k, v, o, gates, b_gates.reshape(g2, 1), b_gates.reshape(1, g2), g_head)


def _mlstm_step_kernel(q_ref, k_ref, v_ref, o_ref, g_ref, bg_ref, gh_ref, c_ref, n_ref, m_ref,
                       hh_ref, co_ref, no_ref, mo_ref, *, heads, dk, dv):
    nb = q_ref.shape[0]
    gates = g_ref[...] + bg_ref[...]
    li = gates[:, :heads]
    lf = _log_sigmoid(gates[:, heads:])
    m_st = m_ref[...]
    m_new = jnp.maximum(lf + m_st, li)
    fw_all = jnp.exp(lf + m_st - m_new)
    iw_all = jnp.exp(li - m_new)
    floor_all = jnp.exp(-m_new)
    mo_ref[...] = m_new
    eye = lax.broadcasted_iota(jnp.int32, (dk, dk), 0) == lax.broadcasted_iota(jnp.int32, (dk, dk), 1)

    def to_col(r):
        return jnp.sum(jnp.where(eye, r, 0.0), axis=1, keepdims=True)

    hr = range(heads)
    for b in range(nb):
        q_r = [q_ref[b:b + 1, h * dk:(h + 1) * dk] for h in hr]
        k_r = [k_ref[b:b + 1, h * dk:(h + 1) * dk] for h in hr]
        k_c = [iw_all[b:b + 1, h:h + 1] * to_col(k_r[h]) for h in hr]
        c_new = [fw_all[b:b + 1, h:h + 1] * c_ref[b, h] + k_c[h] * v_ref[b:b + 1, h * dv:(h + 1) * dv] for h in hr]
        num = [jnp.dot(q_r[h].astype(BF16), c_new[h].astype(BF16), preferred_element_type=F32) for h in hr]
        n_new = [fw_all[b:b + 1, h:h + 1] * n_ref[b, h:h + 1, :] + iw_all[b:b + 1, h:h + 1] * k_r[h] for h in hr]
        den = [jnp.sum(q_r[h] * n_new[h], axis=1, keepdims=True) for h in hr]
        hval = [num[h] / jnp.maximum(jnp.abs(den[h]), floor_all[b:b + 1, h:h + 1]) for h in hr]
        hn = [_rms(hval[h], gh_ref[h:h + 1, :]) for h in hr]
        for h in hr:
            co_ref[b, h] = c_new[h]
            no_ref[b, h:h + 1, :] = n_new[h]
            hh_ref[b:b + 1, h * dv:(h + 1) * dv] = hn[h] * jax.nn.sigmoid(o_ref[b:b + 1, h * dv:(h + 1) * dv])


def mlstm_step(q, k, v, o, gates, b_gates, g_head, c_st, n_st, m_st, heads, dk, dv):
    bsz = q.shape[0]
    nb = min(STEP_BATCH, bsz)
    g2 = 2 * heads
    rows = lambda w: pl.BlockSpec((nb, w), lambda i: (i, 0))
    return pl.pallas_call(
        functools.partial(_mlstm_step_kernel, heads=heads, dk=dk, dv=dv),
        grid=(bsz // nb,),
        in_specs=[rows(heads * dk), rows(heads * dk), rows(heads * dv), rows(heads * dv), rows(g2),
                  _const_spec((1, g2)), _const_spec((heads, dv)),
                  pl.BlockSpec((nb, heads, dk, dv), lambda i: (i, 0, 0, 0)),
                  pl.BlockSpec((nb, heads, dk), lambda i: (i, 0, 0)),
                  rows(heads)],
        out_specs=[rows(heads * dv),
                   pl.BlockSpec((nb, heads, dk, dv), lambda i: (i, 0, 0, 0)),
                   pl.BlockSpec((nb, heads, dk), lambda i: (i, 0, 0)),
                   rows(heads)],
        out_shape=[jax.ShapeDtypeStruct((bsz, heads * dv), F32),
                   jax.ShapeDtypeStruct((bsz, heads, dk, dv), F32),
                   jax.ShapeDtypeStruct((bsz, heads, dk), F32),
                   jax.ShapeDtypeStruct((bsz, heads), F32)],
        compiler_params=_params("parallel"),
        name="mlstm_step",
    )(q, k, v, o, gates, b_gates.reshape(1, g2), g_head, c_st, n_st, m_st)


def _ffn_seq_kernel(x_ref, xh_ref, mix_ref, mixh_ref, wm_ref, gtm_ref, buf_ref, g_ref, sh_ref, sc_ref, gt_ref,
                    wa_ref, wg_ref, wc_ref, bc_ref, wd_ref, gf_ref, o_ref, alast_ref, *, fc, final_norm):
    i = pl.program_id(1)
    tm = x_ref.shape[0]
    ff = wa_ref.shape[1]
    halo = xh_ref.shape[0]
    taps = wc_ref.shape[0]
    gtm, wm = gtm_ref[...], wm_ref[...]
    x = x_ref[...] + gtm * jnp.dot(mix_ref[...], wm, preferred_element_type=F32)
    xh = xh_ref[...] + gtm * jnp.dot(mixh_ref[...], wm, preferred_element_type=F32)[mixh_ref.shape[0] - halo:]
    g, sh, sc = g_ref[...], sh_ref[...], sc_ref[...]
    hn = _norm_mod(x, g, sh, sc).astype(BF16)
    hh = _norm_mod(xh, g, sh, sc).astype(BF16)
    first = i == 0

    def up(c):
        cs = slice(c * fc, (c + 1) * fc)
        wa = wa_ref[:, cs]
        return (jnp.dot(hn, wa, preferred_element_type=F32),
                jnp.dot(hn, wg_ref[:, cs], preferred_element_type=F32),
                jnp.dot(hh, wa, preferred_element_type=F32))

    n_chunks = ff // fc
    acc = None
    nxt = up(0)
    for c in range(n_chunks):
        cs = slice(c * fc, (c + 1) * fc)
        a, gt, a_halo = nxt
        if c + 1 < n_chunks:
            nxt = up(c + 1)
        ext = jnp.concatenate([jnp.where(first, buf_ref[:, cs], a_halo), a], axis=0)
        conv = bc_ref[:, cs]
        for j in range(taps):
            lo = halo - (taps - 1) + j
            conv = conv + (a if lo == halo else ext[lo:lo + tm, :]) * wc_ref[j:j + 1, cs]
        act = (conv * jax.nn.sigmoid(conv) * gt).astype(BF16)
        part = jnp.dot(act, wd_ref[cs, :], preferred_element_type=F32)
        acc = part if acc is None else acc + part
        alast_ref[:, cs] = a[tm - halo:tm, :]

    y = x + gt_ref[...] * acc
    if final_norm:
        y = _rms(y, gf_ref[...])
    o_ref[...] = y


def conv_ffn_seq(x, mix, w_mix, gt_mix, buf, g, sh, sc, gt, w_up_a, w_up_g, w_conv, b_conv, w_down, g_final,
                 final_norm, name):
    bsz, s, d = x.shape
    ff = w_up_a.shape[1]
    kk = mix.shape[-1]
    tm = min(FFN_ROW_TILE, s)
    fc = min(FFN_CHUNK, ff)
    assert w_conv.shape[0] - 1 <= CONV_HALO
    halo = lambda b, i: (b, jnp.maximum(i * (tm // CONV_HALO) - 1, 0), 0)
    halo_bf16 = lambda b, i: (b, jnp.maximum(i * (tm // BF16_ROWS) - 1, 0), 0)
    return pl.pallas_call(
        functools.partial(_ffn_seq_kernel, fc=fc, final_norm=final_norm),
        grid=(bsz, s // tm),
        in_specs=[_row_spec(tm, d), pl.BlockSpec((None, CONV_HALO, d), halo),
                  _row_spec(tm, kk), pl.BlockSpec((None, BF16_ROWS, kk), halo_bf16),
                  _const_spec(w_mix.shape), _mod_spec(False, tm, d),
                  pl.BlockSpec((None, CONV_HALO, ff), lambda b, i: (b, 0, 0)),
                  _const_spec((1, d)), _mod_spec(False, tm, d), _mod_spec(False, tm, d), _mod_spec(False, tm, d),
                  w_up_a.spec(), w_up_g.spec(), _const_spec(w_conv.shape),
                  _const_spec(b_conv.shape), w_down.spec(), _const_spec((1, d))],
        out_specs=[_row_spec(tm, d), pl.BlockSpec((None, CONV_HALO, ff), lambda b, i: (b, 0, 0))],
        out_shape=[jax.ShapeDtypeStruct((bsz, s, d), F32), jax.ShapeDtypeStruct((bsz, CONV_HALO, ff), F32)],
        compiler_params=_params("parallel", "arbitrary"),
        name=name,
    )(x, x, mix, mix, w_mix, gt_mix, buf, g, sh, sc, gt, w_up_a.array, w_up_g.array, w_conv, b_conv,
      w_down.array, g_final)


def _ffn_tok_kernel(x_ref, b0_ref, b1_ref, g_ref, sh_ref, sc_ref, gt_ref, wa_ref, wg_ref, wc_ref,
                    bc_ref, wd_ref, gf_ref, o_ref, a_ref, acc_ref, *, final_norm):
    c = pl.program_id(0)
    x = x_ref[...]
    hn = _norm_mod(x, g_ref[...], sh_ref[...], sc_ref[...]).astype(BF16)
    a = jnp.dot(hn, wa_ref[...], preferred_element_type=F32)
    gt = jnp.dot(hn, wg_ref[...], preferred_element_type=F32)
    a_ref[...] = a
    conv = bc_ref[...] + b0_ref[...] * wc_ref[0:1, :] + b1_ref[...] * wc_ref[1:2, :] + a * wc_ref[2:3, :]
    act = (conv * jax.nn.sigmoid(conv) * gt).astype(BF16)
    part = jnp.dot(act, wd_ref[...], preferred_element_type=F32)

    @pl.when(c == 0)
    def _():
        acc_ref[...] = part

    @pl.when(c > 0)
    def _():
        acc_ref[...] += part

    y = x + gt_ref[...] * acc_ref[...]
    if final_norm:
        y = _rms(y, gf_ref[...])
    o_ref[...] = y


def conv_ffn_tok(x, buf0, buf1, g, sh, sc, gt, w_up_a, w_up_g, w_conv, b_conv, w_down, g_final, final_norm, name):
    bsz, d = x.shape
    ff = w_up_a.shape[1]
    fc = min(FFN_TOK_CHUNK, ff)
    assert ff % fc == 0
    full = lambda w: pl.BlockSpec((bsz, w), lambda c: (0, 0))
    cols = lambda r: pl.BlockSpec((r, fc), lambda c: (0, c))
    up_axis, down_axis = len(w_up_a.block) - 1, len(w_down.block) - 2
    return pl.pallas_call(
        functools.partial(_ffn_tok_kernel, final_norm=final_norm),
        grid=(ff // fc,),
        in_specs=[full(d), cols(bsz), cols(bsz), _const_spec((1, d)), full(d), full(d), full(d),
                  w_up_a.chunk_spec(up_axis, fc), w_up_g.chunk_spec(up_axis, fc), cols(w_conv.shape[0]), cols(1),
                  w_down.chunk_spec(down_axis, fc), _const_spec((1, d))],
        out_specs=[full(d), cols(bsz)],
        out_shape=[jax.ShapeDtypeStruct((bsz, d), F32), jax.ShapeDtypeStruct((bsz, ff), F32)],
        scratch_shapes=[pltpu.VMEM((bsz, d), F32)],
        compiler_params=_params("arbitrary"),
        name=name,
    )(x, buf0, buf1, g, sh, sc, gt, w_up_a.array, w_up_g.array, w_conv, b_conv, w_down.array, g_final)


def _latent_kernel(x_ref, g_ref, sh_ref, sc_ref, w_ref, gkv_ref, rc_ref, ra_ref, rb_ref, *refs,
                   kv_lora, rope, nope, with_kv):
    if with_kv:
        wuk_ref, wuv_ref, one_ref, ckv_ref, kpe_ref, kcat_ref, v_ref, ksq_ref = refs
    else:
        ckv_ref, kpe_ref = refs
    hn = _norm_mod(x_ref[...], g_ref[...], sh_ref[...], sc_ref[...]).astype(BF16)
    lat = jnp.dot(hn, w_ref[...], preferred_element_type=F32)
    ckv = _rms(lat[:, :kv_lora], gkv_ref[...])
    ckv_ref[...] = ckv
    kpe = _rope3(lat[:, kv_lora:kv_lora + LANES], rc_ref[...], ra_ref[...], rb_ref[...], rope // 2)
    kpe_ref[...] = kpe[:, :rope]
    if with_kv:
        cb = ckv.astype(BF16)
        kn = jnp.dot(cb, wuk_ref[...], preferred_element_type=F32)
        kpe_hi = pltpu.roll(kpe, nope, axis=1)
        for h in range(kn.shape[1] // LANES):
            hs = slice(h * LANES, (h + 1) * LANES)
            kb = (kn[:, hs] + kpe_hi).astype(BF16)
            kcat_ref[:, hs] = kb
            kb = kb.astype(F32)
            ksq_ref[:, h:h + 1] = jnp.max(jnp.sum(kb * kb, axis=1, keepdims=True), axis=0, keepdims=True)
        v_t = lax.dot_general(wuv_ref[...], cb, NT_DIMS, preferred_element_type=F32)
        v_ref[...] = (v_t + one_ref[...]).astype(BF16)


def shared_latent(x, g, sh, sc, w_dkv_p, g_kv, tabs, kv_lora, rope, nope, w_uk_r=None, w_uv_r=None, v_head=None):
    gq, r, d = x.shape
    tm = min(ROW_TILE, r)
    per_row = sh.shape[1] != 1
    with_kv = w_uk_r is not None
    tab_spec = pl.BlockSpec((tm, LANES), lambda g_, i: (i, 0))
    in_specs = [_row_spec(tm, d), _const_spec((1, d)), _mod_spec(per_row, tm, d), _mod_spec(per_row, tm, d),
                _const_spec(w_dkv_p.shape), _const_spec((1, kv_lora)), tab_spec, tab_spec, tab_spec]
    out_specs = [_row_spec(tm, kv_lora), _row_spec(tm, rope)]
    out_shape = [jax.ShapeDtypeStruct((gq, r, kv_lora), F32), jax.ShapeDtypeStruct((gq, r, rope), F32)]
    args = [x, g, sh, sc, w_dkv_p, g_kv, *tabs]
    if with_kv:
        vt_rows = w_uv_r.shape[0]
        ones_row = (jnp.arange(vt_rows) % LANES == v_head).astype(F32)[:, None]
        in_specs += [_const_spec(w_uk_r.shape), _const_spec(w_uv_r.shape), _const_spec(ones_row.shape)]
        n_heads = w_uk_r.shape[1] // LANES
        out_specs += [_row_spec(tm, w_uk_r.shape[1]), pl.BlockSpec((None, vt_rows, tm), lambda g_, i: (g_, 0, i)),
                      pl.BlockSpec((None, None, 1, n_heads), lambda g_, i: (g_, i, 0, 0))]
        out_shape += [jax.ShapeDtypeStruct((gq, r, w_uk_r.shape[1]), BF16),
                      jax.ShapeDtypeStruct((gq, vt_rows, r), BF16),
                      jax.ShapeDtypeStruct((gq, r // tm, 1, n_heads), F32)]
        args += [w_uk_r, w_uv_r, ones_row]
    return pl.pallas_call(
        functools.partial(_latent_kernel, kv_lora=kv_lora, rope=rope, nope=nope, with_kv=with_kv),
        grid=(gq, r // tm),
        in_specs=in_specs, out_specs=out_specs, out_shape=out_shape,
        compiler_params=_params("parallel", "parallel"),
        name="shared_latent_kv" if with_kv else "shared_latent",
    )(*args)


def _query_kernel(x_ref, g_ref, sh_ref, sc_ref, wdq_ref, gq_ref, wuq_ref, wrot_ref, rc_ref, rs_ref, q_ref, qsq_ref):
    hn = _norm_mod(x_ref[...], g_ref[...], sh_ref[...], sc_ref[...]).astype(BF16)
    qd = jnp.dot(hn, wdq_ref[...], preferred_element_type=F32)
    qn = _rms(qd, gq_ref[...]).astype(BF16)
    qf = jnp.dot(qn, wuq_ref[...], preferred_element_type=F32)
    qr = jnp.dot(qn, wrot_ref[...], preferred_element_type=F32)
    rc, rs = rc_ref[...], rs_ref[...]
    for h in range(qf.shape[1] // LANES):
        hs = slice(h * LANES, (h + 1) * LANES)
        qb = (qf[:, hs] * rc + qr[:, hs] * rs).astype(q_ref.dtype)
        q_ref[:, hs] = qb
        qb = qb.astype(F32)
        qsq_ref[:, h:h + 1] = jnp.sum(qb * qb, axis=1, keepdims=True)


def mla_queries(x, g, sh, sc, w_dq, g_q, w_uq_r, w_uq_rot, tab_cos, tab_sin):
    gq, r, d = x.shape
    tm = min(ROW_TILE, r)
    per_row = sh.shape[1] != 1
    n_heads = w_uq_r.shape[1] // LANES
    tab_spec = pl.BlockSpec((tm, LANES), lambda g_, i: (i, 0))
    return pl.pallas_call(
        _query_kernel,
        grid=(gq, r // tm),
        in_specs=[_row_spec(tm, d), _const_spec((1, d)), _mod_spec(per_row, tm, d), _mod_spec(per_row, tm, d),
                  _const_spec(w_dq.shape), _const_spec(g_q.shape), _const_spec(w_uq_r.shape),
                  _const_spec(w_uq_rot.shape), tab_spec, tab_spec],
        out_specs=[_row_spec(tm, w_uq_r.shape[1]), _row_spec(tm, n_heads)],
        out_shape=[jax.ShapeDtypeStruct((gq, r, w_uq_r.shape[1]), BF16),
                   jax.ShapeDtypeStruct((gq, r, n_heads), F32)],
        compiler_params=_params("parallel", "parallel"),
        name="mla_queries",
    )(x, g, sh, sc, w_dq, g_q, w_uq_r, w_uq_rot, tab_cos, tab_sin)


def _attn_kernel(q_ref, k_ref, vt_ref, qsq_ref, ksq_ref, o_ref, *, v_head, v_rows, blk, key_blocks_per_tile):
    qi = pl.program_id(2)
    tq = q_ref.shape[0]
    n_full = qi * (tq // blk)
    base = pl.multiple_of(qi * tq, tq)
    heads = range(2)
    diag = [(h, base + c * blk, c * blk, True) for c in range(tq // blk) for h in heads]

    def scores(h, kstart, q_lo):
        hs = slice(h * LANES, (h + 1) * LANES)
        return lax.dot_general(k_ref[pl.ds(kstart, blk), hs], q_ref[q_lo:tq, hs], NT_DIMS,
                               preferred_element_type=F32)

    def values(h, kstart):
        return vt_ref[h * LANES:h * LANES + v_rows, pl.ds(kstart, blk)]

    def cols_from(x, q_lo, new_cols):
        return new_cols if q_lo == 0 else jnp.concatenate([x[:, :q_lo], new_cols], axis=1)

    def run(tasks, carry, update):
        carry = list(carry)
        nxt = scores(*tasks[0][:3])
        for t, (h, kstart, q_lo, masked) in enumerate(tasks):
            s = nxt
            if t + 1 < len(tasks):
                nxt = scores(*tasks[t + 1][:3])
            if masked:
                key = lax.broadcasted_iota(jnp.int32, s.shape, 0)
                qry = lax.broadcasted_iota(jnp.int32, s.shape, 1)
                s = jnp.where(key <= qry, s, -jnp.inf)
            carry[h] = update(h, s, kstart, q_lo, carry[h])
        return tuple(carry)

    def sweep(update, init):
        per_trip = 2 if (tq // blk) % 2 == 0 else 1

        def full_chunks(trip, carry):
            kstart = pl.multiple_of(trip * (per_trip * blk), per_trip * blk)
            return run([(h, kstart + c * blk, 0, False) for c in range(per_trip) for h in heads], carry, update)
        return run(diag, lax.fori_loop(0, n_full // per_trip, full_chunks, init), update)

    def store(accs):
        outs = []
        for acc in accs:
            out_t = acc[:v_head] / acc[v_head:v_head + 1]
            outs.append(jnp.transpose(out_t))
        o_ref[...] = jnp.concatenate(outs, axis=1).astype(o_ref.dtype)

    ksq_blocks = ksq_ref[...]
    visible = lax.broadcasted_iota(jnp.int32, ksq_blocks.shape, 1) < (qi + 1) * key_blocks_per_tile
    ksq_max = jnp.max(jnp.where(visible, ksq_blocks, 0.0), axis=1, keepdims=True)
    bounds = [jnp.sqrt(qsq_ref[h:h + 1, :] * ksq_max[h:h + 1, :]) for h in heads]

    def update_bounded(h, s, kstart, q_lo, acc):
        p = jnp.exp2(s - bounds[h][:, q_lo:]).astype(BF16)
        pv = jnp.dot(values(h, kstart), p, preferred_element_type=F32)
        return cols_from(acc, q_lo, acc[:, q_lo:] + pv)

    accs = sweep(update_bounded, tuple(jnp.zeros((v_rows, tq), F32) for _ in heads))
    store(accs)
    smallest = jnp.minimum(accs[0][v_head:v_head + 1], accs[1][v_head:v_head + 1])
    row_sums_ok = jnp.min(smallest) >= SOFTMAX_MIN_ROW_SUM

    @pl.when(jnp.logical_not(row_sums_ok))
    def _():
        def update_online(h, s, kstart, q_lo, state):
            m, acc = state
            m_old = m[:, q_lo:]
            m_new = jnp.maximum(m_old, jnp.max(s, axis=0, keepdims=True))
            p = jnp.exp2(s - m_new).astype(BF16)
            pv = jnp.dot(values(h, kstart), p, preferred_element_type=F32)
            return (cols_from(m, q_lo, m_new),
                    cols_from(acc, q_lo, jnp.exp2(m_old - m_new) * acc[:, q_lo:] + pv))

        init = tuple((jnp.full((1, tq), -jnp.inf, F32), jnp.zeros((v_rows, tq), F32)) for _ in heads)
        store([acc for _, acc in sweep(update_online, init)])


def prompt_attention(q, kcat, vt, qsq, ksq_blocks, v_head):
    bsz, s, hw = q.shape
    pairs = hw // (2 * LANES)
    assert 2 * v_head == LANES, "two heads fill one 128-lane output block"
    tq = min(ATTN_TILE, s)
    blk = min(ATTN_BLOCK, tq)
    n_blocks = ksq_blocks.shape[1]
    assert n_blocks % (s // tq) == 0, "query tiles must cover whole key-norm blocks"
    v_rows = -(-(v_head + 1) // BF16_ROWS) * BF16_ROWS
    qsq_t = jnp.swapaxes(qsq, 1, 2).reshape(bsz, pairs, 2, s)
    ksq_t = jnp.swapaxes(ksq_blocks, 1, 2).reshape(bsz, pairs, 2, n_blocks)
    return pl.pallas_call(
        functools.partial(_attn_kernel, v_head=v_head, v_rows=v_rows, blk=blk,
                          key_blocks_per_tile=n_blocks // (s // tq)),
        grid=(bsz, pairs, s // tq),
        in_specs=[pl.BlockSpec((None, tq, 2 * LANES), lambda b, j, i: (b, i, j)),
                  pl.BlockSpec((None, s, 2 * LANES), lambda b, j, i: (b, 0, j)),
                  pl.BlockSpec((None, 2 * LANES, s), lambda b, j, i: (b, j, 0)),
                  pl.BlockSpec((None, None, 2, tq), lambda b, j, i: (b, j, 0, i)),
                  pl.BlockSpec((None, None, 2, n_blocks), lambda b, j, i: (b, j, 0, 0))],
        out_specs=pl.BlockSpec((None, tq, 2 * v_head), lambda b, j, i: (b, i, j)),
        out_shape=jax.ShapeDtypeStruct((bsz, s, pairs * 2 * v_head), BF16),
        compiler_params=_params("parallel", "parallel", "arbitrary"),
        name="prompt_attention",
    )(q, kcat, vt, qsq_t, ksq_t)


def _head_proj_kernel(a_ref, w_ref, o_ref):
    kk = w_ref.shape[1]
    for h in range(w_ref.shape[0]):
        o_ref[h] = jnp.dot(a_ref[:, h * kk:(h + 1) * kk].astype(BF16), w_ref[h],
                           preferred_element_type=F32).astype(o_ref.dtype)


def head_proj_lanes(a, w, out_dtype, name):
    bsz = a.shape[0]
    nh, kk, n = w.shape
    return pl.pallas_call(
        _head_proj_kernel,
        grid=(1,),
        in_specs=[pl.BlockSpec(a.shape, lambda i: (0, 0)), pl.BlockSpec(w.shape, lambda i: (0, 0, 0))],
        out_specs=pl.BlockSpec((nh, bsz, n), lambda i: (0, 0, 0)),
        out_shape=jax.ShapeDtypeStruct((nh, bsz, n), out_dtype),
        compiler_params=_params("arbitrary"),
        name=name,
    )(a, w)


def _paged_attn_kernel(pt_ref, ql_ref, qp_ref, cn_ref, kn_ref, ckv_hbm, kpe_hbm, o_ref,
                       ckv_buf, kpe_buf, s_ref, cb0, p0, w0, cb1, p1, w1, sem, *, n_seq, n_pages, ps, chunk):
    step = pl.program_id(0)
    slot = step % 2
    nh, c_lat = ql_ref.shape
    sets = ((cb0, p0, w0), (cb1, p1, w1))

    def page_copies(bi, sl):
        out = []
        for pg in range(n_pages):
            page = pt_ref[bi * n_pages + pg]
            rows = pl.ds(pg * ps, ps)
            out.append(pltpu.make_async_copy(ckv_hbm.at[page], ckv_buf.at[sl, rows, :], sem.at[0, sl]))
            out.append(pltpu.make_async_copy(kpe_hbm.at[page], kpe_buf.at[sl, :, rows], sem.at[1, sl]))
        return out

    @pl.when(step == 0)
    def _():
        for cp in page_copies(0, 0):
            cp.start()
        cb1[...] = jnp.zeros_like(cb1)
        p1[...] = jnp.zeros_like(p1)
        w1[...] = jnp.concatenate([jnp.zeros((nh, c_lat), F32), jnp.ones((nh, LANES), F32)], axis=1)

    @pl.when(step + 1 < n_seq)
    def _():
        for cp in page_copies(step + 1, 1 - slot):
            cp.start()

    @pl.when(step < n_seq)
    def _():
        for cp in page_copies(step, slot):
            cp.wait()

    n_chunks = n_pages * ps // chunk
    chunks = [slice(c * chunk, (c + 1) * chunk) for c in range(n_chunks)]

    def main(par):
        (cb_w, p_w, w_w), (cb_r, p_r, w_r) = sets[par], sets[1 - par]
        ql = ql_ref[...]
        qp = qp_ref[...]
        w_prev = w_r[...]
        accs = [w_prev[:, :c_lat], jnp.zeros((nh, c_lat), F32)]
        s_lat = []
        for c, cs in enumerate(chunks):
            ck = ckv_buf[par, cs, :].astype(BF16)
            cb_w[cs, :] = ck
            s_lat.append(lax.dot_general(ql, ck, NT_DIMS, preferred_element_type=F32))
            accs[c % 2] += jnp.dot(p_r[:, cs], cb_r[cs, :], preferred_element_type=F32)
        o_ref[...] = (accs[0] + accs[1]) / w_prev[:, c_lat:c_lat + 1]

        for cs, sl in zip(chunks, s_lat):
            s_ref[:, cs] = sl + jnp.dot(qp, kpe_buf[par, :, cs].astype(BF16), preferred_element_type=F32)

        cn = cn_ref[...].astype(BF16).astype(F32)
        kn = kn_ref[...].astype(BF16).astype(F32)
        s_new = (jnp.sum(ql.astype(F32) * cn, axis=1, keepdims=True)
                 + jnp.sum(qp.astype(F32) * kn, axis=1, keepdims=True))
        s = s_ref[...]
        m = jnp.maximum(jnp.max(s, axis=1, keepdims=True), s_new)
        p = jnp.exp2(s - m)
        p_new = jnp.exp2(s_new - m)
        l = jnp.sum(p, axis=1, keepdims=True) + p_new
        p_w[...] = p.astype(BF16)
        w_w[...] = jnp.concatenate([p_new.astype(BF16).astype(F32) * cn, jnp.broadcast_to(l, (nh, LANES))], axis=1)

    for par in range(2):
        pl.when(step % 2 == par)(functools.partial(main, par))


def paged_attention(q_lat, q_pe, ckv_new, kpe_new, cache_ckv, cache_kpe_t, page_table):
    bsz, nh, c = q_lat.shape
    r = q_pe.shape[-1]
    n_pages = page_table.shape[1]
    ps = cache_ckv.shape[1]
    past = n_pages * ps
    assert bsz >= 2, "the two-slot pipeline needs at least two sequences"
    chunk = min(PAGED_CHUNK, past)
    per_b = lambda rows, w: pl.BlockSpec((None, rows, w), lambda s_, pt: (jnp.minimum(s_, bsz - 1), 0, 0))
    return pl.pallas_call(
        functools.partial(_paged_attn_kernel, n_seq=bsz, n_pages=n_pages, ps=ps, chunk=chunk),
        grid_spec=pltpu.PrefetchScalarGridSpec(
            num_scalar_prefetch=1,
            grid=(bsz + 1,),
            in_specs=[per_b(nh, c), per_b(nh, r), per_b(1, c), per_b(1, r),
                      pl.BlockSpec(memory_space=pl.ANY), pl.BlockSpec(memory_space=pl.ANY)],
            out_specs=pl.BlockSpec((None, nh, c), lambda s_, pt: (jnp.maximum(s_ - 1, 0), 0, 0)),
            scratch_shapes=[pltpu.VMEM((2, past, c), F32), pltpu.VMEM((2, r, past), F32),
                            pltpu.VMEM((nh, past), F32)]
            + 2 * [pltpu.VMEM((past, c), BF16), pltpu.VMEM((nh, past), BF16), pltpu.VMEM((nh, c + LANES), F32)]
            + [pltpu.SemaphoreType.DMA((2, 2))]),
        out_shape=jax.ShapeDtypeStruct((bsz, nh, c), F32),
        compiler_params=_params("arbitrary"),
        name="paged_attention",
    )(page_table.reshape(-1), q_lat, q_pe, ckv_new, kpe_new, cache_ckv, cache_kpe_t)


def _rope_tables(pos, rope, lo, scale, passthrough):
    half = rope // 2
    freq = ROPE_THETA ** (-jnp.arange(half, dtype=F32) / half)
    ang = pos.astype(F32)[:, None] * freq[None, :]
    cos, sin = jnp.cos(ang), jnp.sin(ang)
    n = pos.shape[0]
    zeros = lambda w: jnp.zeros((n, w), F32)
    tail = LANES - lo - rope
    c = jnp.concatenate([jnp.full((n, lo), passthrough, F32), cos, cos, zeros(tail)], axis=1)
    sa = jnp.concatenate([zeros(lo), -sin, zeros(half + tail)], axis=1)
    sb = jnp.concatenate([zeros(lo + half), sin, zeros(tail)], axis=1)
    return c * scale, sa * scale, sb * scale


def _prep_weights(w_up, w_down, w_m_in, w_m_out, w_dkv, w_uk, w_uv, w_dq, w_uq, w_o, dims):
    heads, dk, dv = dims["m_heads"], dims["m_dk"], dims["m_dv"]
    nh, nope, rope, kv_lora = dims["mla_heads"], dims["qk_nope"], dims["qk_rope"], dims["kv_lora"]
    ff = w_down.shape[1]
    qd, vd = heads * dk, heads * dv
    d = w_up.shape[1]
    pw = {}
    w_up_b, w_down_b, w_m_in_b = w_up.astype(BF16), w_down.astype(BF16), w_m_in.astype(BF16)
    depth, n_a = w_up.shape[0], w_m_in.shape[0]
    pw["w_up_a"] = [WeightView(w_up_b, (None, d, ff), (l, 0, 0)) for l in range(depth)]
    pw["w_up_g"] = [WeightView(w_up_b, (None, d, ff), (l, 0, 1)) for l in range(depth)]
    pw["w_down"] = [WeightView(w_down_b, (None, ff, d), (l, 0, 0)) for l in range(depth)]
    assert (2 * qd) % vd == 0, "value / output-gate columns must start on a multiple of their width"
    pw["w_m_q"] = [WeightView(w_m_in_b, (None, d, qd), (l, 0, 0)) for l in range(n_a)]
    pw["w_m_k"] = [WeightView(w_m_in_b, (None, d, qd), (l, 0, 1)) for l in range(n_a)]
    pw["w_m_v"] = [WeightView(w_m_in_b, (None, d, vd), (l, 0, 2 * qd // vd)) for l in range(n_a)]
    pw["w_m_o"] = [WeightView(w_m_in_b, (None, d, vd), (l, 0, 2 * qd // vd + 1)) for l in range(n_a)]
    gates = jnp.pad(w_m_in[:, :, 2 * qd + 2 * vd:], ((0, 0), (0, 0), (0, LANES - 2 * heads))).astype(BF16)
    pw["w_m_g"] = [WeightView(gates, (None, d, LANES), (l, 0, 0)) for l in range(n_a)]
    pw["w_m_out"] = w_m_out.astype(BF16)
    pw["w_dkv"] = jnp.pad(w_dkv, ((0, 0), (0, LANES - rope))).astype(BF16)
    pw["w_uk_r"] = jnp.pad(w_uk, ((0, 0), (0, 0), (0, LANES - nope))).reshape(kv_lora, nh * LANES).astype(BF16)
    v_head = w_uv.shape[2]
    pw["w_uv_r"] = jnp.pad(w_uv, ((0, 0), (0, 0), (0, LANES - v_head))).reshape(kv_lora, nh * LANES).T.astype(BF16)
    nb = w_uq.shape[0]
    wq = w_uq.reshape(nb, w_uq.shape[1], nh, nope + rope)
    pw["w_uq_r"] = jnp.pad(wq, ((0, 0), (0, 0), (0, 0), (0, LANES - nope - rope))).reshape(
        nb, w_uq.shape[1], nh * LANES).astype(BF16)
    half = rope // 2
    partner = jnp.concatenate([-wq[..., nope + half:], wq[..., nope:nope + half]], axis=-1)
    pw["w_uq_rot"] = jnp.pad(partner, ((0, 0), (0, 0), (0, 0), (nope, LANES - nope - rope))).reshape(
        nb, w_uq.shape[1], nh * LANES).astype(BF16)
    pw["w_dq"] = w_dq.astype(BF16)
    pw["w_o"] = w_o.astype(BF16)
    wukt = jnp.transpose(w_uk, (1, 2, 0))
    pw["w_uk_t"] = jnp.pad(wukt, ((0, 0), (0, LANES - nope), (0, 0))).astype(BF16)
    pw["w_uv_t"] = jnp.transpose(w_uv, (1, 0, 2)).astype(BF16)
    return pw


def _trunk(x, mods, mods_kv, pos, conv_bufs, m_states, kv_past, pw, small, dims):
    heads, dk, dv = dims["m_heads"], dims["m_dk"], dims["m_dv"]
    nh, nope, rope, kv_lora, v_head = (dims["mla_heads"], dims["qk_nope"], dims["qk_rope"],
                                       dims["kv_lora"], dims["v_head"])
    depth, n_a = dims["depth"], dims["n_a"]
    is_prompt = kv_past is None
    gq, r, d = x.shape
    att_scale = (nope + rope) ** -0.5 * LOG2_E
    new_c, new_n, new_m, new_conv = [], [], [], []
    ckv = kpe = kcat = vv = ksq = None
    y = None
    for layer in range(depth):
        sh1, sc1, gt1, sh2, sc2, gt2 = mods[layer]
        g1 = small["g_norm1"][layer][None, :]
        g2 = small["g_norm2"][layer][None, :]
        if layer == n_a:
            sh_kv, sc_kv = mods_kv
            tabs = _rope_tables(pos, rope, 0, 1.0, 0.0)
            if is_prompt:
                ckv, kpe, kcat, vv, ksq = shared_latent(x, small["g_kv_in"][None, :], sh_kv, sc_kv, pw["w_dkv"],
                                                        small["g_kv"][None, :], tabs, kv_lora, rope, nope,
                                                        pw["w_uk_r"], pw["w_uv_r"], v_head)
            else:
                ckv, kpe = shared_latent(x, small["g_kv_in"][None, :], sh_kv, sc_kv, pw["w_dkv"],
                                         small["g_kv"][None, :], tabs, kv_lora, rope, nope)
        if layer < n_a:
            ws = [pw["w_m_q"][layer], pw["w_m_k"][layer], pw["w_m_v"][layer], pw["w_m_o"][layer], pw["w_m_g"][layer]]
            if is_prompt:
                q, k, v, o, gates = norm_mod_proj(x, g1, sh1, sc1, ws, [BF16, F32, BF16, F32, F32],
                                                  [dk ** -0.5, 1.0, 1.0, 1.0, 1.0], "mlstm_in_proj")
                hh, c_st, n_st, m_st = mlstm_chunkwise(q, k, v, o, gates,
                                                       small["b_m_gates"][layer], small["g_m_head"][layer],
                                                       heads, dk, dv)
                n_st = n_st.reshape(gq, heads, dk)
                m_st = m_st.reshape(gq, heads)
            else:
                q, k, v, o, gates = norm_mod_proj(x, g1, sh1, sc1, ws, [F32] * 5,
                                                  [dk ** -0.5, 1.0, 1.0, 1.0, 1.0], "mlstm_in_proj_tok")
                hh, c_st, n_st, m_st = mlstm_step(q[0], k[0], v[0], o[0], gates[0, :, :2 * heads],
                                                  small["b_m_gates"][layer], small["g_m_head"][layer],
                                                  m_states[0][layer], m_states[1][layer], m_states[2][layer],
                                                  heads, dk, dv)
                hh = hh[None]
            new_c.append(c_st)
            new_n.append(n_st)
            new_m.append(m_st)
            mix, w_mix = hh, pw["w_m_out"][layer]
        else:
            j = layer - n_a
            q_cos, q_msin, q_sin = _rope_tables(pos, rope, nope, att_scale, 1.0)
            qh, qsq = mla_queries(x, g1, sh1, sc1, pw["w_dq"][j], small["g_q"][j][None, :], pw["w_uq_r"][j],
                                  pw["w_uq_rot"][j], q_cos, q_sin - q_msin)
            if is_prompt:
                att = prompt_attention(qh, kcat, vv, qsq, ksq[:, :, 0, :], v_head)
            else:
                bsz = r
                q2 = qh[0]
                q_lat = head_proj_lanes(q2, pw["w_uk_t"], BF16, "absorb_q")
                q_lat = jnp.swapaxes(q_lat, 0, 1)
                q_pe = q2.reshape(bsz, nh, LANES)[:, :, nope:nope + rope]
                o_lat = paged_attention(q_lat, q_pe, ckv[0][:, None, :], kpe[0][:, None, :],
                                        kv_past[0], kv_past[1], kv_past[2])
                o_lat = o_lat.reshape(bsz, nh * kv_lora)
                att = head_proj_lanes(o_lat, pw["w_uv_t"], F32, "unabsorb_o")
                att = jnp.swapaxes(att, 0, 1).reshape(1, bsz, nh * v_head)
            mix, w_mix = att, pw["w_o"][j]
        final = layer == depth - 1
        gf = small["g_final"][None, :]
        if is_prompt:
            x, a_last = conv_ffn_seq(x, mix, w_mix, gt1, conv_bufs[layer], g2, sh2, sc2, gt2,
                                     pw["w_up_a"][layer], pw["w_up_g"][layer], small["w_conv"][layer],
                                     small["b_conv"][layer][None, :], pw["w_down"][layer], gf, final, "conv_ffn_seq")
            new_conv.append(a_last[:, CONV_HALO - (small["w_conv"].shape[1] - 1):, :])
        else:
            x = resid_proj(mix, w_mix, x, gt1, "mixer_out_proj")
            buf = conv_bufs[layer]
            x2, a_new = conv_ffn_tok(x[0], buf[:, 0, :], buf[:, 1, :], g2, sh2[0], sc2[0], gt2[0],
                                     pw["w_up_a"][layer], pw["w_up_g"][layer], small["w_conv"][layer],
                                     small["b_conv"][layer][None, :], pw["w_down"][layer], gf, final, "conv_ffn_tok")
            x = x2[None]
            new_conv.append(jnp.stack([buf[:, 1, :], a_new], axis=1))
    return x, jnp.stack(new_c), jnp.stack(new_n), jnp.stack(new_m), jnp.stack(new_conv), ckv, kpe


def kernel(x_prompt, x_sample, state_mlstm_C, state_mlstm_n, state_mlstm_m, state_conv, cache_ckv, cache_kpe,
           page_table, c_prompt, c_sample, g_norm1, g_norm2, w_ada, b_ada, w_up, w_conv, b_conv, w_down,
           w_m_in, b_m_gates, g_m_head, w_m_out, g_kv_in, w_ada_kv, b_ada_kv, w_dkv, g_kv, w_uk, w_uv,
           w_dq, g_q, w_uq, w_o, g_final):
    bp, s, d = x_prompt.shape
    bs, t, _ = x_sample.shape
    assert t == 1, "the sample path handles one new token per sequence"
    assert w_conv.shape[1] == 3, "the single-token ConvFFN kernel is written for a width-3 conv"
    depth = w_ada.shape[0]
    n_a = w_m_in.shape[0]
    heads, dv = g_m_head.shape[1], g_m_head.shape[2]
    dk = state_mlstm_C.shape[3]
    kv_lora, nh, nope = w_uk.shape
    v_head = w_uv.shape[2]
    rope = w_dkv.shape[1] - kv_lora
    ff = w_down.shape[1]
    dims = dict(m_heads=heads, m_dk=dk, m_dv=dv, mla_heads=nh, qk_nope=nope, qk_rope=rope, kv_lora=kv_lora,
                v_head=v_head, depth=depth, n_a=n_a)
    past_len = page_table.shape[1] * cache_ckv.shape[1]

    pw = _prep_weights(w_up, w_down, w_m_in, w_m_out, w_dkv, w_uk, w_uv, w_dq, w_uq, w_o, dims)
    small = dict(g_norm1=g_norm1, g_norm2=g_norm2, w_conv=w_conv, b_conv=b_conv, b_m_gates=b_m_gates,
                 g_m_head=g_m_head, g_kv_in=g_kv_in, g_kv=g_kv, g_q=g_q, g_final=g_final)

    c_all = jnp.concatenate([c_prompt, c_sample], axis=0)
    mod = ada_mod(c_all, w_ada, b_ada[:, None, :])
    mod_kv = ada_mod(c_all, w_ada_kv[None], b_ada_kv[None, None, :])[0]

    def split(m, n, lo, hi, per_row):
        parts = jnp.split(m[lo:hi], n, axis=-1)
        return [p[None] if per_row else p[:, None, :] for p in parts]

    mods_p = [split(mod[l], 6, 0, bp, False) for l in range(depth)]
    mods_s = [split(mod[l], 6, bp, bp + bs, True) for l in range(depth)]
    kv_p = split(mod_kv, 2, 0, bp, False)
    kv_s = split(mod_kv, 2, bp, bp + bs, True)

    conv0 = [jnp.zeros((bp, CONV_HALO, ff), F32)] * depth
    y_p, c_p, n_p, m_p, conv_p, ckv_p, kpe_p = _trunk(
        x_prompt, mods_p, kv_p, jnp.arange(s), conv0, None, None, pw, small, dims)

    pos_s = jnp.full((bs,), past_len, jnp.int32)
    y_s, c_s, n_s, m_s, conv_s, ckv_s, kpe_s = _trunk(
        x_sample.reshape(1, bs, d), mods_s, kv_s, pos_s, state_conv,
        (state_mlstm_C, state_mlstm_n, state_mlstm_m), (cache_ckv, jnp.swapaxes(cache_kpe, 1, 2), page_table), pw, small, dims)

    return (y_p, y_s.reshape(bs, 1, d), c_p, n_p, m_p, conv_p, ckv_p, kpe_p,
            c_s, n_s, m_s, conv_s, ckv_s.reshape(bs, 1, kv_lora), kpe_s.reshape(bs, 1, rope))
```
